```python
import jax
import jax.numpy as jnp
from jax import lax
import numpy as np

D_MODEL = 1024
BATCH = 4
SEQ = 4096
DEPTH = 4
DEC_BATCH = 32
DEC_SEQ = 4
PAST_LEN = 8192
PAGE_SIZE = 128

HEAD_DIM = 64
RET_HEADS = 4
SWA_HEADS = 4
CONV_CH = 256
SGU_GROUPS = 4
SGU_GDIM = 64
RET_W = RET_HEADS * HEAD_DIM
SWA_W = SWA_HEADS * HEAD_DIM
SGU_W = SGU_GROUPS * SGU_GDIM
MIX_W = RET_W + SWA_W + CONV_CH + SGU_W
SPLIT_SIZES = (RET_W, RET_W, RET_W, RET_W, SWA_W, SWA_W, SWA_W, 2 * CONV_CH, 2 * SGU_W)
IN_W = 4 * RET_W + 3 * SWA_W + 2 * CONV_CH + 2 * SGU_W
RET_CHUNK = 128
RET_THETA = 10000.0
ROPE_THETA = 500000.0
ROPE_DIM = HEAD_DIM // 4
DILATED_CONFIGS = ((128, 1), (512, 4), (2048, 16))
SWA_WINDOW = 2048
DIL_BLOCK = 128
CONV_WIDTH = 31
SGU_CHUNK = 128
N_GROUPS = 4
EXPERTS_PER_GROUP = 8
N_EXPERTS = N_GROUPS * EXPERTS_PER_GROUP
TOP_K = 2
D_EXPERT = 512
MOE_BLOCK = 128
DN_ALPHA = (2 * DEPTH) ** 0.25
DN_BETA = (8 * DEPTH) ** -0.25
LN_EPS = 1e-5
NEG_INF = -1e30
F32 = jnp.float32

kernel_name = 'hybrid_retention_dilated_conv_sgu_hmoe_step'


def layer_norm(x, g, b=None):
    xf = x.astype(F32)
    mu = xf.mean(-1, keepdims=True)
    var = jnp.square(xf - mu).mean(-1, keepdims=True)
    y = (xf - mu) * lax.rsqrt(var + LN_EPS) * g.astype(F32)
    return y if b is None else y + b.astype(F32)


def rotary(x, pos, rot_dim, theta):
    half = rot_dim // 2
    inv = jnp.float32(theta) ** (-jnp.arange(half, dtype=F32) * (2.0 / rot_dim))
    ang = pos.astype(F32)[:, None] * inv[None, :]
    c = jnp.cos(ang)[None, :, None, :]
    s = jnp.sin(ang)[None, :, None, :]
    xf = x.astype(F32)
    x1, x2 = xf[..., :half], xf[..., half:rot_dim]
    return jnp.concatenate([x1 * c - x2 * s, x2 * c + x1 * s, xf[..., rot_dim:]], axis=-1)


def retention(q, k, v, s0):
    B, T, H, Dk = q.shape
    L = RET_CHUNK if T % RET_CHUNK == 0 else T
    nc = T // L
    log_g = jnp.log1p(-jnp.exp2(-5.0 - jnp.arange(H, dtype=F32)))
    i = jnp.arange(L, dtype=F32)
    rel = i[:, None] - i[None, :]
    decay = jnp.where(rel >= 0, jnp.exp(log_g[:, None, None] * jnp.maximum(rel, 0.0)), 0.0)
    q_dec = jnp.exp(log_g[:, None] * (i + 1.0))[None, :, :, None]
    k_dec = jnp.exp(log_g[:, None] * (L - 1.0 - i))[None, :, :, None]
    c_dec = jnp.exp(log_g * L)[None, :, None, None]

    def chunks(x):
        return x.astype(F32).reshape(B, nc, L, H, x.shape[-1]).transpose(1, 0, 3, 2, 4)

    def step(S, qkv):
        qc, kc, vc = qkv
        a = jnp.einsum('bhid,bhjd->bhij', qc, kc) * decay
        o = jnp.einsum('bhij,bhjd->bhid', a, vc) + jnp.einsum('bhid,bhde->bhie', qc, S) * q_dec
        S = S * c_dec + jnp.einsum('bhjd,bhje->bhde', kc * k_dec, vc)
        return S, o

    S, o = lax.scan(step, s0.astype(F32), (chunks(q), chunks(k), chunks(v)))
    o = o.transpose(1, 0, 3, 2, 4).reshape(B, T, H, -1)
    return o, S


def _band_attend(q, k, v, dil, span):
    B, T, H, Dh = q.shape
    L = T // dil
    nb = -(-L // DIL_BLOCK)
    Lp = nb * DIL_BLOCK

    def sub(x):
        x = x.reshape(B, L, dil, H, Dh)
        x = jnp.pad(x, ((0, 0), (0, Lp - L), (0, 0), (0, 0), (0, 0)))
        return x.reshape(B, nb, DIL_BLOCK, dil, H, Dh)

    def band(x):
        prev = jnp.pad(x, ((0, 0), (1, 0), (0, 0), (0, 0), (0, 0), (0, 0)))[:, :-1]
        return jnp.concatenate([prev, x], axis=2)

    qs = sub(q)
    kb = band(sub(k))
    vb = band(sub(v))
    s = jnp.einsum('bnqrhd,bnkrhd->bnrhqk', qs, kb) * (HEAD_DIM ** -0.5)
    qi = jnp.arange(DIL_BLOCK)
    ki = jnp.arange(2 * DIL_BLOCK)
    blk = jnp.arange(nb)
    rel = DIL_BLOCK + qi[:, None] - ki[None, :]
    valid = (rel >= 0) & (rel <= span)
    valid = valid[None] & ((blk[:, None, None] * DIL_BLOCK - DIL_BLOCK + ki[None, None, :]) >= 0)
    s = jnp.where(valid[None, :, None, None], s, NEG_INF)
    m = s.max(-1, keepdims=True)
    p = jnp.exp(s - m)
    l = p.sum(-1)
    o = jnp.einsum('bnrhqk,bnkrhd->bnqrhd', p, vb) / jnp.transpose(l, (0, 1, 4, 2, 3))[..., None]
    lse = jnp.transpose(m[..., 0] + jnp.log(l), (0, 1, 4, 2, 3))
    o = o.reshape(B, Lp, dil, H, Dh)[:, :L].reshape(B, T, H, Dh)
    lse = lse.reshape(B, Lp, dil, H)[:, :L].reshape(B, T, H)
    return o, lse


def _gather_attend(q, kc, vc, past, dil, span):
    Tn = q.shape[1]
    idx = past + jnp.arange(Tn)[:, None] - dil * jnp.arange(span + 1)[None, :]
    valid = idx >= 0
    idx = jnp.maximum(idx, 0)
    kg = kc[:, idx]
    vg = vc[:, idx]
    s = jnp.einsum('bthd,btjhd->bthj', q, kg) * (HEAD_DIM ** -0.5)
    s = jnp.where(valid[None, :, None, :], s, NEG_INF)
    m = s.max(-1, keepdims=True)
    p = jnp.exp(s - m)
    l = p.sum(-1)
    o = jnp.einsum('bthj,btjhd->bthd', p, vg) / l[..., None]
    return o, m[..., 0] + jnp.log(l)


def dilated_attention(q, k, v, k_buf, v_buf):
    outs, lses = [], []
    if k_buf is not None:
        past = k_buf.shape[1]
        kc = jnp.concatenate([k_buf.astype(F32), k], axis=1)
        vc = jnp.concatenate([v_buf.astype(F32), v], axis=1)
    for window, dil in DILATED_CONFIGS:
        span = window // dil
        if k_buf is None:
            o, lse = _band_attend(q, k, v, dil, span)
        else:
            o, lse = _gather_attend(q, kc, vc, past, dil, span)
        outs.append(o)
        lses.append(lse)
    w = jax.nn.softmax(jnp.stack(lses), axis=0)
    return jnp.einsum('cbth,cbthd->bthd', w, jnp.stack(outs))


def conv_module(a, buf, conv_w, conv_b, ln_g, ln_b):
    a = a.astype(F32)
    val, gate = jnp.split(a, 2, axis=-1)
    glu = val * jax.nn.sigmoid(gate)
    if buf is None:
        ctx = jnp.pad(glu, ((0, 0), (CONV_WIDTH - 1, 0), (0, 0)))
    else:
        ctx = jnp.concatenate([buf.astype(F32), glu], axis=1)
    y = lax.conv_general_dilated(ctx, conv_w.astype(F32)[:, None, :], (1,), 'VALID',
                                 dimension_numbers=('NWC', 'WIO', 'NWC'),
                                 feature_group_count=CONV_CH) + conv_b.astype(F32)
    return jax.nn.silu(layer_norm(y, ln_g, ln_b)), glu


def spatial_gating(uv, ln_g, ln_b, ws, bs):
    B, T, _ = uv.shape
    u, v = jnp.split(jax.nn.gelu(uv.astype(F32)), 2, axis=-1)
    vn = layer_norm(v.reshape(B, T, SGU_GROUPS, SGU_GDIM),
                    ln_g.reshape(SGU_GROUPS, SGU_GDIM), ln_b.reshape(SGU_GROUPS, SGU_GDIM))
    nc = -(-T // SGU_CHUNK)
    Tp = nc * SGU_CHUNK
    vp = jnp.pad(vn, ((0, 0), (0, Tp - T), (0, 0), (0, 0))).reshape(B, nc, SGU_CHUNK, SGU_GROUPS, SGU_GDIM)
    wm = ws.astype(F32) * jnp.tril(jnp.ones((SGU_CHUNK, SGU_CHUNK), F32))
    z = jnp.einsum('gts,bnsgc->bntgc', wm, vp) + bs.astype(F32).T[None, None, :, :, None]
    z = z.reshape(B, Tp, SGU_W)[:, :T]
    return u * z, vn.reshape(B, T, SGU_W)


def hier_moe(x, rg_w, rg_b, re_w, re_b, w1, w3, w2):
    B, T, D = x.shape
    n_tok = B * T
    xt = x.reshape(n_tok, D)
    xf = xt.astype(F32)
    g_logits = xf @ rg_w.astype(F32) + rg_b.astype(F32)
    g_sel = jnp.argmax(g_logits, axis=-1)
    g_gate = jax.nn.softmax(g_logits, axis=-1).max(-1)
    e_all = jnp.einsum('td,gde->tge', xf, re_w.astype(F32)) + re_b.astype(F32)
    e_logits = e_all[jnp.arange(n_tok), g_sel]
    e_val, e_idx = lax.top_k(e_logits, TOP_K)
    gate = jax.nn.softmax(e_val, axis=-1) * g_gate[:, None]
    expert = (g_sel[:, None] * EXPERTS_PER_GROUP + e_idx).astype(jnp.int32)
    n_asg = n_tok * TOP_K
    flat_e = expert.reshape(-1)
    flat_tok = jnp.repeat(jnp.arange(n_tok, dtype=jnp.int32), TOP_K)
    order = jnp.argsort(flat_e)
    se = flat_e[order]
    counts = jnp.bincount(flat_e, length=N_EXPERTS).astype(jnp.int32)
    start = jnp.cumsum(counts) - counts
    pcounts = (counts + MOE_BLOCK - 1) // MOE_BLOCK * MOE_BLOCK
    pstart = jnp.cumsum(pcounts) - pcounts
    dest_sorted = pstart[se] + jnp.arange(n_asg, dtype=jnp.int32) - start[se]
    dest = jnp.zeros((n_asg,), jnp.int32).at[order].set(dest_sorted)
    n_blk = -(-n_asg // MOE_BLOCK) + N_EXPERTS
    slot_tok = jnp.full((n_blk * MOE_BLOCK,), n_tok, jnp.int32).at[dest].set(flat_tok)
    blk_e = jnp.clip(jnp.searchsorted(pstart, jnp.arange(n_blk, dtype=jnp.int32) * MOE_BLOCK, side='right') - 1,
                     0, N_EXPERTS - 1)
    xpad = jnp.concatenate([xt, jnp.zeros((1, D), xt.dtype)], axis=0)

    def expert_block(args):
        tok, e = args
        xb = xpad[tok]
        return (jax.nn.silu(xb @ w1[e]) * (xb @ w3[e])) @ w2[e]

    y_slots = lax.map(expert_block, (slot_tok.reshape(n_blk, MOE_BLOCK), blk_e)).reshape(-1, D)
    y = jnp.einsum('tk,tkd->td', gate, y_slots[dest].reshape(n_tok, TOP_K, D).astype(F32))
    return y.reshape(B, T, D)


def token_mixers(x, pos, w_in, w_out, ret_gn_g, conv_w, conv_b, conv_ln_g, conv_ln_b,
                 sgu_ln_g, sgu_ln_b, sgu_w, sgu_b, layer_cache):
    B, T, _ = x.shape
    h = x @ w_in
    cuts, acc = [], 0
    for size in SPLIT_SIZES[:-1]:
        acc += size
        cuts.append(acc)
    rq, rk, rv, rg, sq, sk, sv, conv_in, sgu_in = jnp.split(h, cuts, axis=-1)

    def heads(t):
        return t.reshape(B, T, -1, HEAD_DIM)

    if layer_cache is None:
        s0 = jnp.zeros((B, RET_HEADS, HEAD_DIM, HEAD_DIM), F32)
    else:
        s0 = layer_cache[0]
    q_r = rotary(heads(rq), pos, HEAD_DIM, RET_THETA)
    k_r = rotary(heads(rk), pos, HEAD_DIM, RET_THETA) * (HEAD_DIM ** -0.5)
    ro, ret_new = retention(q_r, k_r, heads(rv), s0)
    y_ret = layer_norm(ro, ret_gn_g.reshape(RET_HEADS, HEAD_DIM)).reshape(B, T, RET_W) * jax.nn.silu(rg.astype(F32))

    q_b = rotary(heads(sq), pos, ROPE_DIM, ROPE_THETA)
    k_b = rotary(heads(sk), pos, ROPE_DIM, ROPE_THETA)
    v_b = heads(sv).astype(F32)
    if layer_cache is None:
        y_swa = dilated_attention(q_b, k_b, v_b, None, None)
        keep = min(SWA_WINDOW, T)
        k_new, v_new = k_b[:, T - keep:], v_b[:, T - keep:]
    else:
        y_swa = dilated_attention(q_b, k_b, v_b, layer_cache[1], layer_cache[2])
        k_new, v_new = k_b, v_b

    conv_buf = None if layer_cache is None else layer_cache[3]
    y_conv, glu = conv_module(conv_in, conv_buf, conv_w, conv_b, conv_ln_g, conv_ln_b)
    conv_new = glu[:, T - (CONV_WIDTH - 1):] if layer_cache is None else glu

    y_sgu, v_rows = spatial_gating(sgu_in, sgu_ln_g, sgu_ln_b, sgu_w, sgu_b)

    y = jnp.concatenate([y_ret, y_swa.reshape(B, T, SWA_W), y_conv, y_sgu], axis=-1) @ w_out.astype(F32)
    if layer_cache is None:
        return y, (ret_new, k_new, v_new, conv_new)
    return y, (ret_new, k_new, v_new, conv_new, v_rows)


def trunk(x, pos, weights, cache):
    (w_in, w_out, ret_gn_g, conv_w, conv_b, conv_ln_g, conv_ln_b, sgu_ln_g, sgu_ln_b, sgu_w, sgu_b,
     ln1_g, ln1_b, ln2_g, ln2_b, rg_w, rg_b, re_w, re_b, w1, w3, w2) = weights
    collected = None
    for l in range(DEPTH):
        layer_cache = None if cache is None else (cache[0][l], cache[1][l], cache[2][l], cache[3][l])
        y, st = token_mixers(x, pos, w_in[l], w_out[l], ret_gn_g[l], conv_w[l], conv_b[l], conv_ln_g[l],
                             conv_ln_b[l], sgu_ln_g[l], sgu_ln_b[l], sgu_w[l], sgu_b[l], layer_cache)
        x = layer_norm(DN_ALPHA * x.astype(F32) + y, ln1_g[l], ln1_b[l]).astype(x.dtype)
        f = hier_moe(x, rg_w[l], rg_b[l], re_w[l], re_b[l], w1[l], w3[l], w2[l])
        x = layer_norm(DN_ALPHA * x.astype(F32) + f, ln2_g[l], ln2_b[l]).astype(x.dtype)
        if collected is None:
            collected = [[] for _ in st]
        for lst, s in zip(collected, st):
            lst.append(s)
    return x, [jnp.stack(lst) for lst in collected]


def setup_inputs(seed: int = 0) -> dict:
    key = jax.random.key(seed)
    k = jax.random.split(key, 28)

    def nrm(i, shape, scale):
        return jax.random.normal(k[i], shape, F32) * scale

    win = min(SWA_WINDOW, PAST_LEN)
    return {
        'x_prompt': nrm(0, (BATCH, SEQ, D_MODEL), 1.0),
        'x_sample': nrm(1, (DEC_BATCH, DEC_SEQ, D_MODEL), 1.0),
        'state_ret': nrm(2, (DEPTH, DEC_BATCH, RET_HEADS, HEAD_DIM, HEAD_DIM), 0.5),
        'cache_swa_k': nrm(3, (DEPTH, DEC_BATCH, win, SWA_HEADS, HEAD_DIM), 1.0),
        'cache_swa_v': nrm(4, (DEPTH, DEC_BATCH, win, SWA_HEADS, HEAD_DIM), 1.0),
        'state_conv': nrm(5, (DEPTH, DEC_BATCH, CONV_WIDTH - 1, CONV_CH), 0.5),
        'w_in': nrm(6, (DEPTH, D_MODEL, IN_W), D_MODEL ** -0.5),
        'w_out': nrm(7, (DEPTH, MIX_W, D_MODEL), DN_BETA * MIX_W ** -0.5),
        'ret_gn_g': 1.0 + nrm(8, (DEPTH, RET_W), 0.02),
        'conv_w': nrm(9, (DEPTH, CONV_WIDTH, CONV_CH), CONV_WIDTH ** -0.5),
        'conv_b': nrm(10, (DEPTH, CONV_CH), 0.02),
        'conv_ln_g': 1.0 + nrm(11, (DEPTH, CONV_CH), 0.02),
        'conv_ln_b': nrm(12, (DEPTH, CONV_CH), 0.02),
        'sgu_ln_g': 1.0 + nrm(13, (DEPTH, SGU_W), 0.02),
        'sgu_ln_b': nrm(14, (DEPTH, SGU_W), 0.02),
        'sgu_w': nrm(15, (DEPTH, SGU_GROUPS, SGU_CHUNK, SGU_CHUNK), SGU_CHUNK ** -0.5),
        'sgu_b': 1.0 + nrm(16, (DEPTH, SGU_GROUPS, SGU_CHUNK), 0.02),
        'ln1_g': 1.0 + nrm(17, (DEPTH, D_MODEL), 0.02),
        'ln1_b': nrm(18, (DEPTH, D_MODEL), 0.02),
        'ln2_g': 1.0 + nrm(19, (DEPTH, D_MODEL), 0.02),
        'ln2_b': nrm(20, (DEPTH, D_MODEL), 0.02),
        'router_g_w': nrm(21, (DEPTH, D_MODEL, N_GROUPS), D_MODEL ** -0.5),
        'router_g_b': nrm(22, (DEPTH, N_GROUPS), 0.01),
        'router_e_w': nrm(23, (DEPTH, N_GROUPS, D_MODEL, EXPERTS_PER_GROUP), D_MODEL ** -0.5),
        'router_e_b': nrm(24, (DEPTH, N_GROUPS, EXPERTS_PER_GROUP), 0.01),
        'moe_w1': nrm(25, (DEPTH, N_EXPERTS, D_MODEL, D_EXPERT), D_MODEL ** -0.5),
        'moe_w3': nrm(26, (DEPTH, N_EXPERTS, D_MODEL, D_EXPERT), D_MODEL ** -0.5),
        'moe_w2': nrm(27, (DEPTH, N_EXPERTS, D_EXPERT, D_MODEL), DN_BETA * D_EXPERT ** -0.5),
    }


def reference(x_prompt, x_sample, state_ret, cache_swa_k, cache_swa_v, state_conv,
              w_in, w_out, ret_gn_g, conv_w, conv_b, conv_ln_g, conv_ln_b,
              sgu_ln_g, sgu_ln_b, sgu_w, sgu_b, ln1_g, ln1_b, ln2_g, ln2_b,
              router_g_w, router_g_b, router_e_w, router_e_b, moe_w1, moe_w3, moe_w2):
    weights = (w_in, w_out, ret_gn_g, conv_w, conv_b, conv_ln_g, conv_ln_b, sgu_ln_g, sgu_ln_b, sgu_w, sgu_b,
               ln1_g, ln1_b, ln2_g, ln2_b, router_g_w, router_g_b, router_e_w, router_e_b, moe_w1, moe_w3, moe_w2)
    pos_prompt = jnp.arange(x_prompt.shape[1], dtype=jnp.int32)
    pos_sample = PAST_LEN + jnp.arange(x_sample.shape[1], dtype=jnp.int32)
    y_prompt, st_p = trunk(x_prompt, pos_prompt, weights, None)
    y_sample, st_s = trunk(x_sample, pos_sample, weights, (state_ret, cache_swa_k, cache_swa_v, state_conv))
    p_ret, p_k, p_v, p_conv = st_p
    s_ret, s_k, s_v, s_conv, s_sgu = st_s
    return (y_prompt, y_sample, p_ret, p_k, p_v, p_conv, s_ret, s_k, s_v, s_conv, s_sgu)
```

```python
import functools
import math

import jax
import jax.numpy as jnp
from jax import lax
from jax.experimental import pallas as pl
from jax.experimental.pallas import tpu as pltpu

F32 = jnp.float32
BF16 = jnp.bfloat16
I32 = jnp.int32

HEAD_DIM = 64
N_HEADS = 4
MIX = N_HEADS * HEAD_DIM
(COL_RQ, COL_RK, COL_RV, COL_RG, COL_SQ, COL_SK, COL_SV,
 COL_CV, COL_CG, COL_GU, COL_GV) = range(11)
N_COLBLK = 11
RET_CHUNK = 128
RET_THETA = 10000.0
ROPE_THETA = 500000.0
ROPE_DIM = HEAD_DIM // 4
DILATED_CONFIGS = ((128, 1), (512, 4), (2048, 16))
DIL_BLOCK = 128
CONV_WIDTH = 31
SGU_CHUNK = 128
N_GROUPS = 4
EXPERTS_PER_GROUP = 8
N_EXPERTS = N_GROUPS * EXPERTS_PER_GROUP
TOP_K = 2
PAST_LEN = 8192
LN_EPS = 1e-5
NEG_INF = -1e30

LANE = 128
SUBLANE = 8
VMEM_CAP_BYTES = 60000 * 1024
COMPILER_TEMP_BYTES = 12 * 1024 * 1024

TM = 256
MOE_BLK = 256
CONV_TT = 512
CONV_CTX = 32
SWA_SB = DIL_BLOCK * 16


def _vmem(nbytes):
    return int(min(VMEM_CAP_BYTES, nbytes + COMPILER_TEMP_BYTES))


def _params(sem, nbytes):
    return pltpu.CompilerParams(dimension_semantics=sem, vmem_limit_bytes=_vmem(nbytes))


def _dot(a, b):
    return jnp.dot(a, b, preferred_element_type=F32)


def _dot_nt(a, b):
    return lax.dot_general(a, b, (((1,), (1,)), ((), ())), preferred_element_type=F32)


def _dot_tn(a, b):
    return lax.dot_general(a, b, (((0,), (0,)), ((), ())), preferred_element_type=F32)


def _split_dot(x, m):
    hi = x.astype(BF16)
    lo = (x - hi.astype(F32)).astype(BF16)
    return _dot(hi, m) + _dot(lo, m)


def _sigmoid(x):
    return 1.0 / (1.0 + jnp.exp(-x))


def _layer_norm(z, g, b):
    mu = jnp.mean(z, axis=-1, keepdims=True)
    d = z - mu
    var = jnp.mean(d * d, axis=-1, keepdims=True)
    return d * lax.rsqrt(var + LN_EPS) * g + b


def _group_norm(z, avg):
    mu = _split_dot(z, avg)
    d = z - mu
    var = _split_dot(d * d, avg)
    return d * lax.rsqrt(var + LN_EPS)


def _lane_head(width):
    return lax.broadcasted_iota(I32, (1, width), 1) // HEAD_DIM


def _rotate(x, cos, sin_signed, half):
    first = (lax.broadcasted_iota(I32, (1, LANE), 1) % HEAD_DIM) < half
    parts = []
    for p in range(x.shape[1] // LANE):
        t = x[:, p * LANE:(p + 1) * LANE]
        up = pltpu.roll(t, half, 1)
        down = pltpu.roll(t, LANE - half, 1)
        parts.append(jnp.where(first, down, up))
    return x * cos + jnp.concatenate(parts, axis=1) * sin_signed


def _in_proj_body(x_ref, w_ref, cr_ref, sr_ref, cs_ref, ss_ref, h_ref, wbf_ref):
    @pl.when(pl.program_id(0) == 0)
    def _():
        for j in range(N_COLBLK):
            wbf_ref[:, j * MIX:(j + 1) * MIX] = w_ref[:, j * MIX:(j + 1) * MIX].astype(BF16)

    xb = x_ref[...].astype(BF16)
    for j in range(N_COLBLK):
        hj = _dot(xb, wbf_ref[:, j * MIX:(j + 1) * MIX])
        if j in (COL_RQ, COL_RK):
            hj = _rotate(hj, cr_ref[...], sr_ref[...], HEAD_DIM // 2)
        if j == COL_RK:
            hj = hj * (HEAD_DIM ** -0.5)
        if j in (COL_SQ, COL_SK):
            hj = _rotate(hj, cs_ref[...], ss_ref[...], ROPE_DIM // 2)
        h_ref[:, j * MIX:(j + 1) * MIX] = hj


def _in_proj(x, w, tabs, n_prompt_tiles, tiles_per_seq):
    nt, d = x.shape
    width = w.shape[1]

    def tab_map(i):
        return (jnp.where(i < n_prompt_tiles, i % tiles_per_seq, tiles_per_seq + i - n_prompt_tiles), 0)

    tab_spec = pl.BlockSpec((TM, MIX), tab_map)
    nbytes = 2 * d * width * 4 + d * width * 2 + 2 * TM * d * 4 + 8 * TM * MIX * 4 + 2 * TM * width * 4
    return pl.pallas_call(
        _in_proj_body,
        grid=(nt // TM,),
        in_specs=[pl.BlockSpec((TM, d), lambda i: (i, 0)),
                  pl.BlockSpec((d, width), lambda i: (0, 0)),
                  tab_spec, tab_spec, tab_spec, tab_spec],
        out_specs=pl.BlockSpec((TM, width), lambda i: (i, 0)),
        out_shape=jax.ShapeDtypeStruct((nt, width), F32),
        scratch_shapes=[pltpu.VMEM((d, width), BF16)],
        compiler_params=_params(("arbitrary",), nbytes),
        name="in_proj",
    )(x, w, *tabs)


def _retention_body(q_ref, k_ref, v_ref, g_ref, s0_ref, dec_ref, qdec_ref, kdec_ref,
                    cmat_ref, bdm_ref, avg_ref, gn_ref, mix_ref, y_ref, sout_ref, s_scr, *, n_chunks):
    del mix_ref
    c = pl.program_id(1)

    @pl.when(c == 0)
    def _():
        s_scr[...] = s0_ref[0]

    q = q_ref[...]
    k = k_ref[...]
    vb = v_ref[...].astype(BF16)
    kb = k.astype(BF16)
    head = _lane_head(MIX)
    o = _dot(q.astype(BF16), s_scr[...].astype(BF16)) * qdec_ref[...]
    for h in range(N_HEADS):
        mh = head == h
        a = _dot_nt(jnp.where(mh, q, 0.0).astype(BF16), kb) * dec_ref[h]
        o = o + jnp.where(mh, _dot(a.astype(BF16), vb), 0.0)
    upd = _dot_tn((k * kdec_ref[...]).astype(BF16), vb)
    s_scr[...] = s_scr[...] * cmat_ref[...] + upd * bdm_ref[...]

    @pl.when(c == n_chunks - 1)
    def _():
        sout_ref[0] = s_scr[...]

    g = g_ref[...]
    y_ref[...] = _group_norm(o, avg_ref[...]) * gn_ref[...] * (g * _sigmoid(g))


def _retention_tables(l_real, l_pad):
    hh = jnp.arange(N_HEADS, dtype=F32)
    log_g = jnp.log1p(-jnp.exp2(-5.0 - hh))
    i = jnp.arange(l_pad, dtype=F32)
    rel = i[:, None] - i[None, :]
    dec = jnp.where(rel >= 0, jnp.exp(log_g[:, None, None] * jnp.maximum(rel, 0.0)), 0.0)
    qd = jnp.exp(log_g[:, None] * (i + 1.0))
    kd = jnp.where(i < l_real, jnp.exp(log_g[:, None] * (l_real - 1.0 - i)), 0.0)
    cd = jnp.exp(log_g * l_real)
    expand = lambda t: jnp.repeat(t.T, HEAD_DIM, axis=1)
    row_head = jnp.arange(MIX) // HEAD_DIM
    bdm = (row_head[:, None] == row_head[None, :]).astype(F32)
    cmat = cd[row_head][:, None] * bdm
    return dec, expand(qd), expand(kd), cmat, bdm


def _retention(h, mix, s0, tabs, avg, gn_g, *, rows, n_seq, n_chunks, row0):
    rb0 = row0 // rows
    dec, qdec, kdec, cmat, bdm = tabs

    def hspec(col):
        return pl.BlockSpec((rows, MIX), lambda b, c: (rb0 + b * n_chunks + c, col))

    const2 = lambda shape: pl.BlockSpec(shape, lambda b, c: (0, 0))
    in_specs = [hspec(COL_RQ), hspec(COL_RK), hspec(COL_RV), hspec(COL_RG),
                pl.BlockSpec((1, MIX, MIX), lambda b, c: (b, 0, 0)),
                pl.BlockSpec((N_HEADS, rows, rows), lambda b, c: (0, 0, 0)),
                const2((rows, MIX)), const2((rows, MIX)),
                const2((MIX, MIX)), const2((MIX, MIX)), const2((MIX, MIX)), const2((1, MIX)),
                pl.BlockSpec(memory_space=pl.ANY)]
    args = [h, h, h, h, s0, dec, qdec, kdec, cmat, bdm, avg, gn_g, mix]
    nbytes = 2 * (7 * rows * MIX * 4 + 5 * MIX * MIX * 4 + N_HEADS * rows * rows * 4) + MIX * MIX * 4
    return pl.pallas_call(
        functools.partial(_retention_body, n_chunks=n_chunks),
        grid=(n_seq, n_chunks),
        in_specs=in_specs,
        out_specs=[pl.BlockSpec((rows, MIX), lambda b, c: (rb0 + b * n_chunks + c, 0)),
                   pl.BlockSpec((1, MIX, MIX), lambda b, c: (b, 0, 0))],
        out_shape=[jax.ShapeDtypeStruct(mix.shape, F32),
                   jax.ShapeDtypeStruct((n_seq, MIX, MIX), F32)],
        scratch_shapes=[pltpu.VMEM((MIX, MIX), F32)],
        input_output_aliases={12: 0},
        compiler_params=_params(("arbitrary", "arbitrary"), nbytes),
        name="retention",
    )(*args)


def _swa_prompt_body(q_ref, k_ref, v_ref, mix_ref, y_ref, o0, o1, o2, l0, l1, l2, *, seq):
    del mix_ref
    o_scr = (o0, o1, o2)
    l_scr = (l0, l1, l2)
    head = _lane_head(LANE)
    qi = lax.broadcasted_iota(I32, (DIL_BLOCK, 1), 0)
    ki = lax.broadcasted_iota(I32, (1, 2 * DIL_BLOCK), 1)
    rel = DIL_BLOCK + qi - ki
    scale = HEAD_DIM ** -0.5
    for sb in range(seq // SWA_SB):
        for ci, (window, dil) in enumerate(DILATED_CONFIGS):
            span = window // dil
            band = (rel >= 0) & (rel <= span)

            def block(i, carry, sb=sb, ci=ci, dil=dil, band=band):
                nloc = i // dil
                off = nloc * (DIL_BLOCK * dil) + i % dil
                start = sb * SWA_SB + off
                if sb == 0:
                    first = nloc == 0
                    pstart = jnp.where(first, start, start - DIL_BLOCK * dil)
                    valid = band & (ki >= jnp.where(first, DIL_BLOCK, 0))
                else:
                    pstart = start - DIL_BLOCK * dil
                    valid = band
                stride = dil if dil > 1 else None
                cur = pl.ds(start, DIL_BLOCK, stride=stride)
                prev = pl.ds(pstart, DIL_BLOCK, stride=stride)
                qb = q_ref[cur, :]
                kb = jnp.concatenate([k_ref[prev, :], k_ref[cur, :]], axis=0).astype(BF16)
                vb = jnp.concatenate([v_ref[prev, :], v_ref[cur, :]], axis=0).astype(BF16)
                o = jnp.zeros((DIL_BLOCK, LANE), F32)
                ls = jnp.zeros((DIL_BLOCK, LANE), F32)
                for hh in range(LANE // HEAD_DIM):
                    mh = head == hh
                    s = _dot_nt(jnp.where(mh, qb, 0.0).astype(BF16), kb) * scale
                    s = jnp.where(valid, s, NEG_INF)
                    m = jnp.max(s, axis=-1, keepdims=True)
                    p = jnp.exp(s - m)
                    l = jnp.sum(p, axis=-1, keepdims=True)
                    o = jnp.where(mh, _dot(p.astype(BF16), vb) / l, o)
                    ls = jnp.where(mh, m + jnp.log(l), ls)
                dst = pl.ds(off, DIL_BLOCK, stride=stride)
                o_scr[ci][dst, :] = o
                l_scr[ci][dst, :] = ls
                return carry

            lax.fori_loop(0, SWA_SB // DIL_BLOCK, block, 0)

        def combine(t, carry, sb=sb):
            rows = pl.ds(t * TM, TM)
            la, lb, lc = l_scr[0][rows, :], l_scr[1][rows, :], l_scr[2][rows, :]
            mx = jnp.maximum(jnp.maximum(la, lb), lc)
            ea, eb, ec = jnp.exp(la - mx), jnp.exp(lb - mx), jnp.exp(lc - mx)
            num = ea * o_scr[0][rows, :] + eb * o_scr[1][rows, :] + ec * o_scr[2][rows, :]
            y_ref[pl.ds(sb * SWA_SB + t * TM, TM), :] = num / (ea + eb + ec)
            return carry

        lax.fori_loop(0, SWA_SB // TM, combine, 0)


def _swa_prompt(h, mix, *, seq, n_seq):
    halves = MIX // LANE

    def hspec(col):
        return pl.BlockSpec((seq, LANE), lambda b, p: (b, col * halves + p))

    nbytes = 2 * 4 * seq * LANE * 4 + 6 * SWA_SB * LANE * 4
    return pl.pallas_call(
        functools.partial(_swa_prompt_body, seq=seq),
        grid=(n_seq, halves),
        in_specs=[hspec(COL_SQ), hspec(COL_SK), hspec(COL_SV), pl.BlockSpec(memory_space=pl.ANY)],
        out_specs=pl.BlockSpec((seq, LANE), lambda b, p: (b, halves + p)),
        out_shape=jax.ShapeDtypeStruct(mix.shape, F32),
        scratch_shapes=[pltpu.VMEM((SWA_SB, LANE), F32)] * 6,
        input_output_aliases={3: 0},
        compiler_params=_params(("arbitrary", "arbitrary"), nbytes),
        name="swa_prompt",
    )(h, h, h, mix)


def _swa_sample_body(q_ref, k_ref, v_ref, ck_ref, cv_ref, mix_ref, y_ref, *, past, rows):
    del mix_ref
    q = q_ref[...]
    head = _lane_head(MIX)
    qm = jnp.concatenate([jnp.where(head == h, q, 0.0) for h in range(N_HEADS)], axis=0).astype(BF16)
    ck = ck_ref[0].astype(BF16)
    cv = cv_ref[0].astype(BF16)
    zpad = jnp.zeros((LANE - rows, MIX), F32)
    kn = jnp.concatenate([k_ref[...], zpad], axis=0).astype(BF16)
    vn = jnp.concatenate([v_ref[...], zpad], axis=0).astype(BF16)
    scale = HEAD_DIM ** -0.5
    s_c = _dot_nt(qm, ck) * scale
    s_n = _dot_nt(qm, kn) * scale
    t_row = lax.broadcasted_iota(I32, (N_HEADS * rows, 1), 0) % rows
    rel_c = past + t_row - lax.broadcasted_iota(I32, (1, past), 1)
    rel_n = t_row - lax.broadcasted_iota(I32, (1, LANE), 1)
    outs, lses = [], []
    for window, dil in DILATED_CONFIGS:
        span = window // dil
        shift = int(math.log2(dil))

        def valid(rel, dil=dil, shift=shift, span=span):
            return (rel >= 0) & ((rel & (dil - 1)) == 0) & ((rel >> shift) <= span)

        sc = jnp.where(valid(rel_c), s_c, NEG_INF)
        sn = jnp.where(valid(rel_n), s_n, NEG_INF)
        m = jnp.maximum(jnp.max(sc, axis=-1, keepdims=True), jnp.max(sn, axis=-1, keepdims=True))
        pc = jnp.exp(sc - m)
        pn = jnp.exp(sn - m)
        l = jnp.sum(pc, axis=-1, keepdims=True) + jnp.sum(pn, axis=-1, keepdims=True)
        outs.append((_dot(pc.astype(BF16), cv) + _dot(pn.astype(BF16), vn)) / l)
        lses.append(m + jnp.log(l))
    mx = jnp.maximum(jnp.maximum(lses[0], lses[1]), lses[2])
    es = [jnp.exp(ls - mx) for ls in lses]
    y_all = (es[0] * outs[0] + es[1] * outs[1] + es[2] * outs[2]) / (es[0] + es[1] + es[2])
    y = jnp.zeros((rows, MIX), F32)
    for h in range(N_HEADS):
        y = jnp.where(head == h, y_all[h * rows:(h + 1) * rows, :], y)
    y_ref[...] = y


def _swa_sample(h, mix, cache_k, cache_v, *, rows, n_seq, row0):
    past = cache_k.shape[1]
    rb0 = row0 // rows

    def hspec(col):
        return pl.BlockSpec((rows, MIX), lambda b: (rb0 + b, col))

    cspec = pl.BlockSpec((1, past, MIX), lambda b: (b, 0, 0))
    nbytes = 2 * (2 * past * MIX * 4 + 4 * rows * MIX * 4) + 12 * N_HEADS * rows * past * 4
    return pl.pallas_call(
        functools.partial(_swa_sample_body, past=past, rows=rows),
        grid=(n_seq,),
        in_specs=[hspec(COL_SQ), hspec(COL_SK), hspec(COL_SV), cspec, cspec,
                  pl.BlockSpec(memory_space=pl.ANY)],
        out_specs=pl.BlockSpec((rows, MIX), lambda b: (rb0 + b, 1)),
        out_shape=jax.ShapeDtypeStruct(mix.shape, F32),
        input_output_aliases={5: 0},
        compiler_params=_params(("arbitrary",), nbytes),
        name="swa_sample",
    )(h, h, h, cache_k, cache_v, mix)


def _conv_body(val_ref, gate_ref, buf_ref, w_ref, cb_ref, g_ref, b_ref, mix_ref,
               y_ref, glu_ref, ctx, *, rows, n_tiles):
    del mix_ref
    t = pl.program_id(1)

    @pl.when(t == 0)
    def _():
        ctx[0:CONV_CTX, :] = buf_ref[0]

    glu = val_ref[...] * _sigmoid(gate_ref[...])
    glu_ref[...] = glu
    ctx[CONV_CTX:CONV_CTX + rows, :] = glu
    lead = CONV_CTX - (CONV_WIDTH - 1)
    acc = jnp.zeros((rows, MIX), F32)
    for j in range(CONV_WIDTH):
        acc = acc + ctx[pl.ds(lead + j, rows), :] * w_ref[j:j + 1, :]
    z = _layer_norm(acc + cb_ref[...], g_ref[...], b_ref[...])
    y_ref[...] = z * _sigmoid(z)
    if n_tiles > 1:
        ctx[0:CONV_CTX, :] = ctx[rows:rows + CONV_CTX, :]


def _conv(h, mix, buf, w, cb, g, b, *, rows, n_seq, n_tiles, row0):
    rb0 = row0 // rows
    rmap = lambda col: (lambda s, t: (rb0 + s * n_tiles + t, col))
    const = lambda shape: pl.BlockSpec(shape, lambda s, t: (0, 0))
    nbytes = 2 * 4 * rows * MIX * 4 + (CONV_CTX + rows) * MIX * 4 + 8 * rows * MIX * 4
    return pl.pallas_call(
        functools.partial(_conv_body, rows=rows, n_tiles=n_tiles),
        grid=(n_seq, n_tiles),
        in_specs=[pl.BlockSpec((rows, MIX), rmap(COL_CV)), pl.BlockSpec((rows, MIX), rmap(COL_CG)),
                  pl.BlockSpec((1, CONV_CTX, MIX), lambda s, t: (s, 0, 0)),
                  const((CONV_CTX, MIX)), const((1, MIX)), const((1, MIX)), const((1, MIX)),
                  pl.BlockSpec(memory_space=pl.ANY)],
        out_specs=[pl.BlockSpec((rows, MIX), rmap(2)),
                   pl.BlockSpec((rows, MIX), lambda s, t: (s * n_tiles + t, 0))],
        out_shape=[jax.ShapeDtypeStruct(mix.shape, F32),
                   jax.ShapeDtypeStruct((n_seq * n_tiles * rows, MIX), F32)],
        scratch_shapes=[pltpu.VMEM((CONV_CTX + rows, MIX), F32)],
        input_output_aliases={7: 0},
        compiler_params=_params(("arbitrary", "arbitrary"), nbytes),
        name="conv",
    )(h, h, buf, w, cb, g, b, mix)


def _gelu_tanh(x):
    c = math.sqrt(2.0 / math.pi)
    return x * (0.5 * (1.0 + jnp.tanh(c * (x + 0.044715 * (x * x * x)))))


def _sgu_body(u_ref, v_ref, g_ref, b_ref, ws_ref, bias_ref, avg_ref, mix_ref, y_ref, vn_ref, *, rows):
    del mix_ref
    u = _gelu_tanh(u_ref[...])
    v = _gelu_tanh(v_ref[...])
    vn = _group_norm(v, avg_ref[...]) * g_ref[...] + b_ref[...]
    vn_ref[...] = vn
    if rows < SGU_CHUNK:
        vn = jnp.concatenate([vn, jnp.zeros((SGU_CHUNK - rows, MIX), F32)], axis=0)
    vnb = vn.astype(BF16)
    tri = (lax.broadcasted_iota(I32, (SGU_CHUNK, SGU_CHUNK), 0)
           >= lax.broadcasted_iota(I32, (SGU_CHUNK, SGU_CHUNK), 1))
    head = _lane_head(MIX)
    z = jnp.zeros((SGU_CHUNK, MIX), F32)
    for gi in range(N_HEADS):
        wm = jnp.where(tri, ws_ref[gi], 0.0).astype(BF16)
        z = jnp.where(head == gi, _dot(wm, vnb), z)
    z = z + bias_ref[...]
    y_ref[...] = u * z[0:rows, :]


def _sgu(h, mix, g, b, ws, bias, avg, *, rows, n_chunks, row0):
    rb0 = row0 // rows
    rmap = lambda col: (lambda c: (rb0 + c, col))
    const = lambda shape: pl.BlockSpec(shape, lambda c: (0,) * len(shape))
    nbytes = 2 * (4 * rows * MIX * 4 + N_HEADS * SGU_CHUNK * SGU_CHUNK * 4 + SGU_CHUNK * MIX * 4) + 8 * SGU_CHUNK * MIX * 4
    return pl.pallas_call(
        functools.partial(_sgu_body, rows=rows),
        grid=(n_chunks,),
        in_specs=[pl.BlockSpec((rows, MIX), rmap(COL_GU)), pl.BlockSpec((rows, MIX), rmap(COL_GV)),
                  const((1, MIX)), const((1, MIX)), const((N_HEADS, SGU_CHUNK, SGU_CHUNK)),
                  const((SGU_CHUNK, MIX)), const((MIX, MIX)), pl.BlockSpec(memory_space=pl.ANY)],
        out_specs=[pl.BlockSpec((rows, MIX), rmap(3)), pl.BlockSpec((rows, MIX), lambda c: (c, 0))],
        out_shape=[jax.ShapeDtypeStruct(mix.shape, F32), jax.ShapeDtypeStruct((n_chunks * rows, MIX), F32)],
        input_output_aliases={7: 0},
        compiler_params=_params(("arbitrary",), nbytes),
        name="sgu",
    )(h, h, g, b, ws, bias, avg, mix)


def _out_router_body(mix_ref, w_ref, x_ref, g_ref, b_ref, wr_ref, br_ref,
                     x1_ref, route_ref, counts_ref, wbf, wrbf, carry, *, alpha):
    i = pl.program_id(0)

    @pl.when(i == 0)
    def _():
        wbf[...] = w_ref[...].astype(BF16)
        wrbf[...] = wr_ref[...].astype(BF16)
        carry[...] = jnp.zeros_like(carry)

    y = _dot(mix_ref[...].astype(BF16), wbf[...])
    x1 = _layer_norm(alpha * x_ref[...] + y, g_ref[...], b_ref[...])
    x1_ref[...] = x1
    logits = _dot(x1.astype(BF16), wrbf[...]) + br_ref[...]
    lane = lax.broadcasted_iota(I32, (1, LANE), 1)

    def first_max(mask, vals):
        masked = jnp.where(mask, vals, NEG_INF)
        top = jnp.max(masked, axis=-1, keepdims=True)
        idx = jnp.min(jnp.where(mask & (masked == top), lane, LANE), axis=-1, keepdims=True)
        return top, idx

    gmask = lane < N_GROUPS
    gmax, gsel = first_max(gmask, logits)
    g_gate = 1.0 / jnp.sum(jnp.where(gmask, jnp.exp(logits - gmax), 0.0), axis=-1, keepdims=True)
    lo = N_GROUPS + EXPERTS_PER_GROUP * gsel
    emask = (lane >= lo) & (lane < lo + EXPERTS_PER_GROUP)
    v1, i1 = first_max(emask, logits)
    v2, i2 = first_max(emask & (lane != i1), logits)
    e12 = jnp.exp(v2 - v1)
    gate_a = g_gate * (1.0 / (1.0 + e12))
    gate_b = g_gate * (e12 / (1.0 + e12))
    e_a = i1 - N_GROUPS
    e_b = i2 - N_GROUPS

    oh_a = (lane == e_a).astype(F32)
    oh_b = (lane == e_b).astype(F32)
    both = oh_a + oh_b
    strict = (lax.broadcasted_iota(I32, (TM, TM), 0) > lax.broadcasted_iota(I32, (TM, TM), 1))
    before = _dot(strict.astype(BF16), both.astype(BF16)) + carry[...]
    rank_a = jnp.sum(oh_a * before, axis=-1, keepdims=True)
    rank_b = jnp.sum(oh_b * before, axis=-1, keepdims=True)
    carry[...] = carry[...] + jnp.sum(both, axis=0, keepdims=True)
    counts_ref[...] = carry[...]

    route = jnp.zeros((TM, LANE), F32)
    for ln, val in enumerate((e_a.astype(F32), e_b.astype(F32), gate_a, gate_b, rank_a, rank_b)):
        route = jnp.where(lane == ln, val, route)
    route_ref[...] = route


def _out_router(mix, w_out, x, g, b, wr, br, *, alpha):
    nt, d = x.shape
    kdim = mix.shape[1]
    const = lambda shape: pl.BlockSpec(shape, lambda i: (0, 0))
    row = lambda width: pl.BlockSpec((TM, width), lambda i: (i, 0))
    nbytes = 2 * (kdim * d * 4 + d * LANE * 4 + TM * kdim * 4 + 2 * TM * d * 4 + TM * LANE * 4) + kdim * d * 2
    return pl.pallas_call(
        functools.partial(_out_router_body, alpha=alpha),
        grid=(nt // TM,),
        in_specs=[row(kdim), const((kdim, d)), row(d), const((1, d)), const((1, d)),
                  const((d, LANE)), const((1, LANE))],
        out_specs=[row(d), row(LANE), const((1, LANE))],
        out_shape=[jax.ShapeDtypeStruct((nt, d), F32), jax.ShapeDtypeStruct((nt, LANE), F32),
                   jax.ShapeDtypeStruct((1, LANE), F32)],
        scratch_shapes=[pltpu.VMEM((kdim, d), BF16), pltpu.VMEM((d, LANE), BF16), pltpu.VMEM((1, LANE), F32)],
        compiler_params=_params(("arbitrary",), nbytes),
        name="out_router",
    )(mix, w_out, x, g, b, wr, br)


def _row_copy(src_hbm, row, dst, slot, sem):
    return pltpu.make_async_copy(src_hbm.at[pl.ds(row, 1), :], dst.at[pl.ds(slot, 1), :], sem)


def _ffn_body(blk_e, n_act, slot_tok, x_hbm, w1_ref, w3_ref, w2_ref, y_ref,
              xbuf, w1b, w3b, w2b, sem):
    j = pl.program_id(0)

    @pl.when(j >= n_act[0])
    def _():
        y_ref[...] = jnp.zeros_like(y_ref)

    @pl.when(j < n_act[0])
    def _():
        def start(r, carry):
            _row_copy(x_hbm, slot_tok[j * MOE_BLK + r], xbuf, r, sem).start()
            return carry

        lax.fori_loop(0, MOE_BLK, start, 0)

        @pl.when((j == 0) | (blk_e[j] != blk_e[jnp.maximum(j - 1, 0)]))
        def _():
            w1b[...] = w1_ref[0].astype(BF16)
            w3b[...] = w3_ref[0].astype(BF16)
            w2b[...] = w2_ref[0].astype(BF16)

        def wait(r, carry):
            _row_copy(x_hbm, 0, xbuf, r, sem).wait()
            return carry

        lax.fori_loop(0, MOE_BLK, wait, 0)
        xb = xbuf[...].astype(BF16)
        h1 = _dot(xb, w1b[...])
        h3 = _dot(xb, w3b[...])
        act = (h1 * _sigmoid(h1)) * h3
        y_ref[...] = _dot(act.astype(BF16), w2b[...])


def _ffn(x1, w1, w3, w2, blk_e, n_act, slot_tok, n_blk):
    d = x1.shape[1]
    de = w1.shape[2]
    wmap = lambda j, be, na, st: (be[j], 0, 0)
    nbytes = 2 * (3 * d * de * 4 + MOE_BLK * d * 4) + MOE_BLK * d * 4 + 3 * d * de * 2 + 6 * MOE_BLK * de * 4
    return pl.pallas_call(
        _ffn_body,
        grid_spec=pltpu.PrefetchScalarGridSpec(
            num_scalar_prefetch=3,
            grid=(n_blk,),
            in_specs=[pl.BlockSpec(memory_space=pl.ANY),
                      pl.BlockSpec((1, d, de), wmap), pl.BlockSpec((1, d, de), wmap),
                      pl.BlockSpec((1, de, d), wmap)],
            out_specs=pl.BlockSpec((MOE_BLK, d), lambda j, be, na, st: (j, 0)),
            scratch_shapes=[pltpu.VMEM((MOE_BLK, d), F32), pltpu.VMEM((d, de), BF16),
                            pltpu.VMEM((d, de), BF16), pltpu.VMEM((de, d), BF16),
                            pltpu.SemaphoreType.DMA(())]),
        out_shape=jax.ShapeDtypeStruct((n_blk * MOE_BLK, d), F32),
        compiler_params=_params(("arbitrary",), nbytes),
        name="expert_ffn",
    )(blk_e, n_act, slot_tok, x1, w1, w3, w2)


def _combine_body(dest, x_ref, route_ref, g_ref, b_ref, ys_hbm, x2_ref, ybuf, sem, *, alpha):
    i = pl.program_id(0)

    def start(r, carry):
        for kk in range(TOP_K):
            _row_copy(ys_hbm, dest[(i * TM + r) * TOP_K + kk], ybuf.at[kk], r, sem).start()
        return carry

    lax.fori_loop(0, TM, start, 0)

    def wait(r, carry):
        for kk in range(TOP_K):
            _row_copy(ys_hbm, 0, ybuf.at[kk], r, sem).wait()
        return carry

    lax.fori_loop(0, TM, wait, 0)
    route = route_ref[...]
    f = route[:, 2:3] * ybuf[0] + route[:, 3:4] * ybuf[1]
    x2_ref[...] = _layer_norm(alpha * x_ref[...] + f, g_ref[...], b_ref[...])


def _combine(dest, x1, route, g, b, ys, *, alpha):
    nt, d = x1.shape
    row = lambda width: pl.BlockSpec((TM, width), lambda i, ds: (i, 0))
    const = lambda shape: pl.BlockSpec(shape, lambda i, ds: (0, 0))
    nbytes = 2 * (2 * TM * d * 4 + TM * LANE * 4) + TOP_K * TM * d * 4 + 4 * TM * d * 4
    return pl.pallas_call(
        functools.partial(_combine_body, alpha=alpha),
        grid_spec=pltpu.PrefetchScalarGridSpec(
            num_scalar_prefetch=1,
            grid=(nt // TM,),
            in_specs=[row(d), row(LANE), const((1, d)), const((1, d)), pl.BlockSpec(memory_space=pl.ANY)],
            out_specs=row(d),
            scratch_shapes=[pltpu.VMEM((TOP_K, TM, d), F32), pltpu.SemaphoreType.DMA(())]),
        out_shape=jax.ShapeDtypeStruct((nt, d), F32),
        compiler_params=_params(("arbitrary",), nbytes),
        name="combine_ln",
    )(dest, x1, route, g, b, ys)


def _rope_tables(pos, rot_dim, theta):
    half = rot_dim // 2
    inv = jnp.float32(theta) ** (-jnp.arange(half, dtype=F32) * (2.0 / rot_dim))
    ang = pos.astype(F32)[:, None] * inv[None, :]
    c, s = jnp.cos(ang), jnp.sin(ang)
    rest = HEAD_DIM - rot_dim
    ones = jnp.ones((pos.shape[0], rest), F32)
    zeros = jnp.zeros((pos.shape[0], rest), F32)
    cos_h = jnp.concatenate([c, c, ones], axis=1)
    sin_h = jnp.concatenate([-s, s, zeros], axis=1)
    return jnp.tile(cos_h, (1, N_HEADS)), jnp.tile(sin_h, (1, N_HEADS))


def _block_diag(s):
    n = s.shape[0]
    eye = jnp.eye(N_HEADS, dtype=s.dtype)
    return (s[:, :, :, None, :] * eye[None, :, None, :, None]).reshape(n, MIX, MIX)


def _diag_blocks(s):
    return jnp.stack([s[:, h * HEAD_DIM:(h + 1) * HEAD_DIM, h * HEAD_DIM:(h + 1) * HEAD_DIM]
                      for h in range(N_HEADS)], axis=1)


def _routing_plan(route, counts, nt, n_blk):
    cnt = counts[0, :N_EXPERTS].astype(I32)
    pcnt = (cnt + MOE_BLK - 1) // MOE_BLK * MOE_BLK
    pstart = jnp.cumsum(pcnt) - pcnt
    n_act = (jnp.sum(pcnt) // MOE_BLK).astype(I32)
    expert = route[:, 0:TOP_K].astype(I32)
    rank = route[:, 4:4 + TOP_K].astype(I32)
    dest = (pstart[expert] + rank).reshape(-1)
    blk = jnp.arange(n_blk, dtype=I32)
    blk_e = jnp.clip(jnp.searchsorted(pstart, jnp.minimum(blk, n_act - 1) * MOE_BLK, side='right') - 1,
                     0, N_EXPERTS - 1).astype(I32)
    tok = jnp.repeat(jnp.arange(nt, dtype=I32), TOP_K)
    slot_tok = (jnp.arange(n_blk * MOE_BLK, dtype=I32) % nt).at[dest].set(tok)
    return dest, blk_e, n_act.reshape(1), slot_tok


def kernel(x_prompt, x_sample, state_ret, cache_swa_k, cache_swa_v, state_conv, w_in, w_out, ret_gn_g,
           conv_w, conv_b, conv_ln_g, conv_ln_b, sgu_ln_g, sgu_ln_b, sgu_w, sgu_b, ln1_g, ln1_b,
           ln2_g, ln2_b, router_g_w, router_g_b, router_e_w, router_e_b, moe_w1, moe_w3, moe_w2):
    depth = w_in.shape[0]
    n_seq, seq, d = x_prompt.shape
    n_dec, t_new, _ = x_sample.shape
    past = cache_swa_k.shape[2]
    rows_s = -(-t_new // SUBLANE) * SUBLANE
    n_p = n_seq * seq
    n_s = n_dec * rows_s
    nt = n_p + n_s
    assert seq % SWA_SB == 0 and seq % CONV_TT == 0 and seq % TM == 0 and n_s % TM == 0
    assert t_new % RET_CHUNK != 0 and t_new <= SGU_CHUNK and w_in.shape[2] == N_COLBLK * MIX
    alpha = (2 * depth) ** 0.25
    keep = min(DILATED_CONFIGS[-1][0], seq)
    n_blk = nt * TOP_K // MOE_BLK + N_EXPERTS

    x = jnp.concatenate([x_prompt.reshape(n_p, d),
                         jnp.pad(x_sample, ((0, 0), (0, rows_s - t_new), (0, 0))).reshape(n_s, d)], axis=0)
    pos = jnp.concatenate([jnp.arange(seq, dtype=I32),
                           PAST_LEN + jnp.arange(n_s, dtype=I32) % rows_s])
    rope = _rope_tables(pos, HEAD_DIM, RET_THETA) + _rope_tables(pos, ROPE_DIM, ROPE_THETA)
    ret_tabs_p = _retention_tables(RET_CHUNK, RET_CHUNK)
    ret_tabs_s = _retention_tables(t_new, rows_s)
    lane_head = jnp.arange(MIX) // HEAD_DIM
    avg = ((lane_head[:, None] == lane_head[None, :]).astype(F32) / HEAD_DIM).astype(BF16)
    zero_state = jnp.zeros((n_seq, MIX, MIX), F32)
    zero_ctx = jnp.zeros((n_seq, CONV_CTX, MIX), F32)
    row2 = lambda v: v.reshape(1, -1)
    mix = jnp.zeros((nt, 4 * MIX), F32)

    outs = [[] for _ in range(9)]
    for l in range(depth):
        h = _in_proj(x, w_in[l], rope, n_p // TM, seq // TM)
        gn = row2(ret_gn_g[l])
        mix, s_p = _retention(h, mix, zero_state, ret_tabs_p, avg, gn,
                              rows=RET_CHUNK, n_seq=n_seq, n_chunks=seq // RET_CHUNK, row0=0)
        mix, s_s = _retention(h, mix, _block_diag(state_ret[l]), ret_tabs_s, avg, gn,
                              rows=rows_s, n_seq=n_dec, n_chunks=1, row0=n_p)
        mix = _swa_prompt(h, mix, seq=seq, n_seq=n_seq)
        mix = _swa_sample(h, mix, cache_swa_k[l].reshape(n_dec, past, MIX),
                          cache_swa_v[l].reshape(n_dec, past, MIX), rows=rows_s, n_seq=n_dec, row0=n_p)
        cw = jnp.pad(conv_w[l], ((0, CONV_CTX - CONV_WIDTH), (0, 0)))
        cargs = (cw, row2(conv_b[l]), row2(conv_ln_g[l]), row2(conv_ln_b[l]))
        mix, glu_p = _conv(h, mix, zero_ctx, *cargs, rows=CONV_TT, n_seq=n_seq, n_tiles=seq // CONV_TT, row0=0)
        ctx_s = jnp.pad(state_conv[l], ((0, 0), (CONV_CTX - (CONV_WIDTH - 1), 0), (0, 0)))
        mix, glu_s = _conv(h, mix, ctx_s, *cargs, rows=rows_s, n_seq=n_dec, n_tiles=1, row0=n_p)
        sbias = jnp.repeat(sgu_b[l].T, HEAD_DIM, axis=1)
        sargs = (row2(sgu_ln_g[l]), row2(sgu_ln_b[l]), sgu_w[l], sbias, avg)
        mix, _ = _sgu(h, mix, *sargs, rows=SGU_CHUNK, n_chunks=n_p // SGU_CHUNK, row0=0)
        mix, vn_s = _sgu(h, mix, *sargs, rows=rows_s, n_chunks=n_dec, row0=n_p)

        wr = jnp.concatenate([router_g_w[l], jnp.transpose(router_e_w[l], (1, 0, 2)).reshape(d, N_EXPERTS),
                              jnp.zeros((d, LANE - N_GROUPS - N_EXPERTS), F32)], axis=1)
        br = jnp.concatenate([router_g_b[l], router_e_b[l].reshape(-1),
                              jnp.zeros((LANE - N_GROUPS - N_EXPERTS,), F32)]).reshape(1, LANE)
        x1, route, counts = _out_router(mix, w_out[l], x, row2(ln1_g[l]), row2(ln1_b[l]), wr, br, alpha=alpha)
        dest, blk_e, n_act, slot_tok = _routing_plan(route, counts, nt, n_blk)
        ys = _ffn(x1, moe_w1[l], moe_w3[l], moe_w2[l], blk_e, n_act, slot_tok, n_blk)
        x = _combine(dest, x1, route, row2(ln2_g[l]), row2(ln2_b[l]), ys, alpha=alpha)

        hp = h[:n_p].reshape(n_seq, seq, -1)
        hs = h[n_p:].reshape(n_dec, rows_s, -1)
        col = lambda t, c: t[..., c * MIX:(c + 1) * MIX]
        heads = lambda t: t.reshape(t.shape[0], t.shape[1], N_HEADS, HEAD_DIM)
        outs[0].append(_diag_blocks(s_p))
        outs[1].append(heads(col(hp, COL_SK)[:, seq - keep:]))
        outs[2].append(heads(col(hp, COL_SV)[:, seq - keep:]))
        outs[3].append(glu_p.reshape(n_seq, seq, MIX)[:, seq - (CONV_WIDTH - 1):])
        outs[4].append(_diag_blocks(s_s))
        outs[5].append(heads(col(hs, COL_SK)[:, :t_new]))
        outs[6].append(heads(col(hs, COL_SV)[:, :t_new]))
        outs[7].append(glu_s.reshape(n_dec, rows_s, MIX)[:, :t_new])
        outs[8].append(vn_s.reshape(n_dec, rows_s, MIX)[:, :t_new])

    y_prompt = x[:n_p].reshape(n_seq, seq, d)
    y_sample = x[n_p:].reshape(n_dec, rows_s, d)[:, :t_new]
    return (y_prompt, y_sample) + tuple(jnp.stack(o) for o in outs)
```

```python
import functools
import math

import jax
import jax.numpy as jnp
from jax import lax
from jax.experimental import pallas as pl
from jax.experimental.pallas import tpu as pltpu

F32 = jnp.float32
BF16 = jnp.bfloat16
I32 = jnp.int32

HEAD_DIM = 64
N_HEADS = 4
MIX = N_HEADS * HEAD_DIM
(COL_RQ, COL_RK, COL_RV, COL_RG, COL_SQ, COL_SK, COL_SV,
 COL_CV, COL_CG, COL_GU, COL_GV) = range(11)
N_COLBLK = 11
RET_CHUNK = 128
RET_THETA = 10000.0
ROPE_THETA = 500000.0
ROPE_DIM = HEAD_DIM // 4
DILATED_CONFIGS = ((128, 1), (512, 4), (2048, 16))
DIL_BLOCK = 128
CONV_WIDTH = 31
SGU_CHUNK = 128
N_GROUPS = 4
EXPERTS_PER_GROUP = 8
N_EXPERTS = N_GROUPS * EXPERTS_PER_GROUP
TOP_K = 2
PAST_LEN = 8192
LN_EPS = 1e-5
NEG_INF = -1e30

LANE = 128
SUBLANE = 8
VMEM_CAP_BYTES = 60000 * 1024
COMPILER_TEMP_BYTES = 12 * 1024 * 1024

TM = 256
MOE_BLK = 256
MOE_CHUNK = SUBLANE
LOC_ROWS = -(-(TM * TOP_K + N_EXPERTS * (MOE_CHUNK - 1)) // LANE) * LANE
MAX_CHUNKS = LOC_ROWS // MOE_CHUNK
CONV_TT = 512
CONV_CTX = 32
SWA_SB = DIL_BLOCK * 16


def _vmem(nbytes):
    return int(min(VMEM_CAP_BYTES, nbytes + COMPILER_TEMP_BYTES))


def _params(sem, nbytes):
    return pltpu.CompilerParams(dimension_semantics=sem, vmem_limit_bytes=_vmem(nbytes))


def _dot(a, b):
    return jnp.dot(a, b, preferred_element_type=F32)


def _dot_nt(a, b):
    return lax.dot_general(a, b, (((1,), (1,)), ((), ())), preferred_element_type=F32)


def _dot_tn(a, b):
    return lax.dot_general(a, b, (((0,), (0,)), ((), ())), preferred_element_type=F32)


def _split_dot(x, m):
    hi = x.astype(BF16)
    lo = (x - hi.astype(F32)).astype(BF16)
    return _dot(hi, m) + _dot(lo, m)


def _sigmoid(x):
    return 1.0 / (1.0 + jnp.exp(-x))


def _layer_norm(z, g, b):
    mu = jnp.mean(z, axis=-1, keepdims=True)
    d = z - mu
    var = jnp.mean(d * d, axis=-1, keepdims=True)
    return d * lax.rsqrt(var + LN_EPS) * g + b


def _group_norm(z, avg):
    mu = _split_dot(z, avg)
    d = z - mu
    var = _split_dot(d * d, avg)
    return d * lax.rsqrt(var + LN_EPS)


def _lane_head(width):
    return lax.broadcasted_iota(I32, (1, width), 1) // HEAD_DIM


def _rotate(x, cos, sin_signed, half):
    first = (lax.broadcasted_iota(I32, (1, LANE), 1) % HEAD_DIM) < half
    parts = []
    for p in range(x.shape[1] // LANE):
        t = x[:, p * LANE:(p + 1) * LANE]
        up = pltpu.roll(t, half, 1)
        down = pltpu.roll(t, LANE - half, 1)
        parts.append(jnp.where(first, down, up))
    return x * cos + jnp.concatenate(parts, axis=1) * sin_signed


def _in_proj_body(x_ref, w_ref, cr_ref, sr_ref, cs_ref, ss_ref, h_ref, wbf_ref):
    @pl.when(pl.program_id(0) == 0)
    def _():
        for j in range(N_COLBLK):
            wbf_ref[:, j * MIX:(j + 1) * MIX] = w_ref[0, :, j * MIX:(j + 1) * MIX].astype(BF16)

    xb = x_ref[...].astype(BF16)
    for j in range(N_COLBLK):
        hj = _dot(xb, wbf_ref[:, j * MIX:(j + 1) * MIX])
        if j in (COL_RQ, COL_RK):
            hj = _rotate(hj, cr_ref[...], sr_ref[...], HEAD_DIM // 2)
        if j == COL_RK:
            hj = hj * (HEAD_DIM ** -0.5)
        if j in (COL_SQ, COL_SK):
            hj = _rotate(hj, cs_ref[...], ss_ref[...], ROPE_DIM // 2)
        h_ref[:, j * MIX:(j + 1) * MIX] = hj


def _in_proj(x, w, layer, tabs, n_prompt_tiles, tiles_per_seq):
    nt, d = x.shape
    width = w.shape[2]

    def tab_map(i):
        return (jnp.where(i < n_prompt_tiles, i % tiles_per_seq, tiles_per_seq + i - n_prompt_tiles), 0)

    tab_spec = pl.BlockSpec((TM, MIX), tab_map)
    nbytes = 2 * d * width * 4 + d * width * 2 + 2 * TM * d * 4 + 8 * TM * MIX * 4 + 2 * TM * width * 4
    return pl.pallas_call(
        _in_proj_body,
        grid=(nt // TM,),
        in_specs=[pl.BlockSpec((TM, d), lambda i: (i, 0)),
                  pl.BlockSpec((1, d, width), lambda i: (layer, 0, 0)),
                  tab_spec, tab_spec, tab_spec, tab_spec],
        out_specs=pl.BlockSpec((TM, width), lambda i: (i, 0)),
        out_shape=jax.ShapeDtypeStruct((nt, width), F32),
        scratch_shapes=[pltpu.VMEM((d, width), BF16)],
        compiler_params=_params(("arbitrary",), nbytes),
        name="in_proj",
    )(x, w, *tabs)


def _retention_body(q_ref, k_ref, v_ref, g_ref, s0_ref, dec_ref, qdec_ref, kdec_ref,
                    cmat_ref, bdm_ref, avg_ref, gn_ref, mix_ref, y_ref, sout_ref, s_scr, *, n_chunks):
    del mix_ref
    c = pl.program_id(1)

    @pl.when(c == 0)
    def _():
        s_scr[...] = s0_ref[0]

    q = q_ref[...]
    k = k_ref[...]
    vb = v_ref[...].astype(BF16)
    kb = k.astype(BF16)
    head = _lane_head(MIX)
    o = _dot(q.astype(BF16), s_scr[...].astype(BF16)) * qdec_ref[...]
    for h in range(N_HEADS):
        mh = head == h
        a = _dot_nt(jnp.where(mh, q, 0.0).astype(BF16), kb) * dec_ref[h]
        o = o + jnp.where(mh, _dot(a.astype(BF16), vb), 0.0)
    upd = _dot_tn((k * kdec_ref[...]).astype(BF16), vb)
    s_scr[...] = s_scr[...] * cmat_ref[...] + upd * bdm_ref[...]

    @pl.when(c == n_chunks - 1)
    def _():
        sout_ref[0] = s_scr[...]

    g = g_ref[...]
    y_ref[...] = _group_norm(o, avg_ref[...]) * gn_ref[...] * (g * _sigmoid(g))


def _retention_tables(l_real, l_pad):
    hh = jnp.arange(N_HEADS, dtype=F32)
    log_g = jnp.log1p(-jnp.exp2(-5.0 - hh))
    i = jnp.arange(l_pad, dtype=F32)
    rel = i[:, None] - i[None, :]
    dec = jnp.where(rel >= 0, jnp.exp(log_g[:, None, None] * jnp.maximum(rel, 0.0)), 0.0)
    qd = jnp.exp(log_g[:, None] * (i + 1.0))
    kd = jnp.where(i < l_real, jnp.exp(log_g[:, None] * (l_real - 1.0 - i)), 0.0)
    cd = jnp.exp(log_g * l_real)
    expand = lambda t: jnp.repeat(t.T, HEAD_DIM, axis=1)
    row_head = jnp.arange(MIX) // HEAD_DIM
    bdm = (row_head[:, None] == row_head[None, :]).astype(F32)
    cmat = cd[row_head][:, None] * bdm
    return dec, expand(qd), expand(kd), cmat, bdm


def _retention(h, mix, s0, tabs, avg, gn_g, *, rows, n_seq, n_chunks, row0):
    rb0 = row0 // rows
    dec, qdec, kdec, cmat, bdm = tabs

    def hspec(col):
        return pl.BlockSpec((rows, MIX), lambda b, c: (rb0 + b * n_chunks + c, col))

    const2 = lambda shape: pl.BlockSpec(shape, lambda b, c: (0, 0))
    in_specs = [hspec(COL_RQ), hspec(COL_RK), hspec(COL_RV), hspec(COL_RG),
                pl.BlockSpec((1, MIX, MIX), lambda b, c: (b, 0, 0)),
                pl.BlockSpec((N_HEADS, rows, rows), lambda b, c: (0, 0, 0)),
                const2((rows, MIX)), const2((rows, MIX)),
                const2((MIX, MIX)), const2((MIX, MIX)), const2((MIX, MIX)), const2((1, MIX)),
                pl.BlockSpec(memory_space=pl.ANY)]
    args = [h, h, h, h, s0, dec, qdec, kdec, cmat, bdm, avg, gn_g, mix]
    nbytes = 2 * (7 * rows * MIX * 4 + 5 * MIX * MIX * 4 + N_HEADS * rows * rows * 4) + MIX * MIX * 4
    return pl.pallas_call(
        functools.partial(_retention_body, n_chunks=n_chunks),
        grid=(n_seq, n_chunks),
        in_specs=in_specs,
        out_specs=[pl.BlockSpec((rows, MIX), lambda b, c: (rb0 + b * n_chunks + c, 0)),
                   pl.BlockSpec((1, MIX, MIX), lambda b, c: (b, 0, 0))],
        out_shape=[jax.ShapeDtypeStruct(mix.shape, F32),
                   jax.ShapeDtypeStruct((n_seq, MIX, MIX), F32)],
        scratch_shapes=[pltpu.VMEM((MIX, MIX), F32)],
        input_output_aliases={12: 0},
        compiler_params=_params(("arbitrary", "arbitrary"), nbytes),
        name="retention",
    )(*args)


def _swa_prompt_body(q_ref, k_ref, v_ref, mix_ref, y_ref, o0, o1, o2, l0, l1, l2, *, seq):
    del mix_ref
    o_scr = (o0, o1, o2)
    l_scr = (l0, l1, l2)
    head = _lane_head(LANE)
    qi = lax.broadcasted_iota(I32, (DIL_BLOCK, 1), 0)
    ki = lax.broadcasted_iota(I32, (1, 2 * DIL_BLOCK), 1)
    rel = DIL_BLOCK + qi - ki
    scale = HEAD_DIM ** -0.5
    for sb in range(seq // SWA_SB):
        for ci, (window, dil) in enumerate(DILATED_CONFIGS):
            span = window // dil
            band = (rel >= 0) & (rel <= span)

            def block(i, carry, sb=sb, ci=ci, dil=dil, band=band):
                nloc = i // dil
                off = nloc * (DIL_BLOCK * dil) + i % dil
                start = sb * SWA_SB + off
                if sb == 0:
                    first = nloc == 0
                    pstart = jnp.where(first, start, start - DIL_BLOCK * dil)
                    valid = band & (ki >= jnp.where(first, DIL_BLOCK, 0))
                else:
                    pstart = start - DIL_BLOCK * dil
                    valid = band
                stride = dil if dil > 1 else None
                cur = pl.ds(start, DIL_BLOCK, stride=stride)
                prev = pl.ds(pstart, DIL_BLOCK, stride=stride)
                qb = q_ref[cur, :]
                kb = jnp.concatenate([k_ref[prev, :], k_ref[cur, :]], axis=0).astype(BF16)
                vb = jnp.concatenate([v_ref[prev, :], v_ref[cur, :]], axis=0).astype(BF16)
                o = jnp.zeros((DIL_BLOCK, LANE), F32)
                ls = jnp.zeros((DIL_BLOCK, LANE), F32)
                for hh in range(LANE // HEAD_DIM):
                    mh = head == hh
                    s = _dot_nt(jnp.where(mh, qb, 0.0).astype(BF16), kb) * scale
                    s = jnp.where(valid, s, NEG_INF)
                    m = jnp.max(s, axis=-1, keepdims=True)
                    p = jnp.exp(s - m)
                    l = jnp.sum(p, axis=-1, keepdims=True)
                    o = jnp.where(mh, _dot(p.astype(BF16), vb) / l, o)
                    ls = jnp.where(mh, m + jnp.log(l), ls)
                dst = pl.ds(off, DIL_BLOCK, stride=stride)
                o_scr[ci][dst, :] = o
                l_scr[ci][dst, :] = ls
                return carry

            lax.fori_loop(0, SWA_SB // DIL_BLOCK, block, 0)

        def combine(t, carry, sb=sb):
            rows = pl.ds(t * TM, TM)
            la, lb, lc = l_scr[0][rows, :], l_scr[1][rows, :], l_scr[2][rows, :]
            mx = jnp.maximum(jnp.maximum(la, lb), lc)
            ea, eb, ec = jnp.exp(la - mx), jnp.exp(lb - mx), jnp.exp(lc - mx)
            num = ea * o_scr[0][rows, :] + eb * o_scr[1][rows, :] + ec * o_scr[2][rows, :]
            y_ref[pl.ds(sb * SWA_SB + t * TM, TM), :] = num / (ea + eb + ec)
            return carry

        lax.fori_loop(0, SWA_SB // TM, combine, 0)


def _swa_prompt(h, mix, *, seq, n_seq):
    halves = MIX // LANE

    def hspec(col):
        return pl.BlockSpec((seq, LANE), lambda b, p: (b, col * halves + p))

    nbytes = 2 * 4 * seq * LANE * 4 + 6 * SWA_SB * LANE * 4
    return pl.pallas_call(
        functools.partial(_swa_prompt_body, seq=seq),
        grid=(n_seq, halves),
        in_specs=[hspec(COL_SQ), hspec(COL_SK), hspec(COL_SV), pl.BlockSpec(memory_space=pl.ANY)],
        out_specs=pl.BlockSpec((seq, LANE), lambda b, p: (b, halves + p)),
        out_shape=jax.ShapeDtypeStruct(mix.shape, F32),
        scratch_shapes=[pltpu.VMEM((SWA_SB, LANE), F32)] * 6,
        input_output_aliases={3: 0},
        compiler_params=_params(("arbitrary", "arbitrary"), nbytes),
        name="swa_prompt",
    )(h, h, h, mix)


def _swa_sample_body(q_ref, k_ref, v_ref, ck_ref, cv_ref, mix_ref, y_ref, *, past, rows):
    del mix_ref
    q = q_ref[...]
    head = _lane_head(MIX)
    qm = jnp.concatenate([jnp.where(head == h, q, 0.0) for h in range(N_HEADS)], axis=0).astype(BF16)
    ck = ck_ref[0, 0].astype(BF16)
    cv = cv_ref[0, 0].astype(BF16)
    zpad = jnp.zeros((LANE - rows, MIX), F32)
    kn = jnp.concatenate([k_ref[...], zpad], axis=0).astype(BF16)
    vn = jnp.concatenate([v_ref[...], zpad], axis=0).astype(BF16)
    scale = HEAD_DIM ** -0.5
    s_c = _dot(qm, ck) * scale
    s_n = _dot_nt(qm, kn) * scale
    t_row = lax.broadcasted_iota(I32, (N_HEADS * rows, 1), 0) % rows
    rel_c = past + t_row - lax.broadcasted_iota(I32, (1, past), 1)
    rel_n = t_row - lax.broadcasted_iota(I32, (1, LANE), 1)
    outs, lses = [], []
    for window, dil in DILATED_CONFIGS:
        span = window // dil
        shift = int(math.log2(dil))

        def valid(rel, dil=dil, shift=shift, span=span):
            return (rel >= 0) & ((rel & (dil - 1)) == 0) & ((rel >> shift) <= span)

        sc = jnp.where(valid(rel_c), s_c, NEG_INF)
        sn = jnp.where(valid(rel_n), s_n, NEG_INF)
        m = jnp.maximum(jnp.max(sc, axis=-1, keepdims=True), jnp.max(sn, axis=-1, keepdims=True))
        pc = jnp.exp(sc - m)
        pn = jnp.exp(sn - m)
        l = jnp.sum(pc, axis=-1, keepdims=True) + jnp.sum(pn, axis=-1, keepdims=True)
        outs.append((_dot_nt(pc.astype(BF16), cv) + _dot(pn.astype(BF16), vn)) / l)
        lses.append(m + jnp.log(l))
    mx = jnp.maximum(jnp.maximum(lses[0], lses[1]), lses[2])
    es = [jnp.exp(ls - mx) for ls in lses]
    y_all = (es[0] * outs[0] + es[1] * outs[1] + es[2] * outs[2]) / (es[0] + es[1] + es[2])
    y = jnp.zeros((rows, MIX), F32)
    for h in range(N_HEADS):
        y = jnp.where(head == h, y_all[h * rows:(h + 1) * rows, :], y)
    y_ref[...] = y


def _swa_sample(h, mix, cache_k, cache_v, layer, *, rows, n_seq, row0):
    past = cache_k.shape[3]
    rb0 = row0 // rows

    def hspec(col):
        return pl.BlockSpec((rows, MIX), lambda b: (rb0 + b, col))

    cspec = pl.BlockSpec((1, 1, MIX, past), lambda b: (layer, b, 0, 0))
    nbytes = 2 * (2 * past * MIX * 4 + 4 * rows * MIX * 4) + 12 * N_HEADS * rows * past * 4
    return pl.pallas_call(
        functools.partial(_swa_sample_body, past=past, rows=rows),
        grid=(n_seq,),
        in_specs=[hspec(COL_SQ), hspec(COL_SK), hspec(COL_SV), cspec, cspec,
                  pl.BlockSpec(memory_space=pl.ANY)],
        out_specs=pl.BlockSpec((rows, MIX), lambda b: (rb0 + b, 1)),
        out_shape=jax.ShapeDtypeStruct(mix.shape, F32),
        input_output_aliases={5: 0},
        compiler_params=_params(("arbitrary",), nbytes),
        name="swa_sample",
    )(h, h, h, cache_k, cache_v, mix)


def _conv_body(val_ref, gate_ref, buf_ref, w_ref, cb_ref, g_ref, b_ref, mix_ref,
               y_ref, glu_ref, ctx, *, rows, n_tiles):
    del mix_ref
    t = pl.program_id(1)

    @pl.when(t == 0)
    def _():
        ctx[0:CONV_CTX, :] = buf_ref[0]

    glu = val_ref[...] * _sigmoid(gate_ref[...])
    glu_ref[...] = glu
    ctx[CONV_CTX:CONV_CTX + rows, :] = glu
    lead = CONV_CTX - (CONV_WIDTH - 1)
    acc = jnp.zeros((rows, MIX), F32)
    for j in range(CONV_WIDTH):
        acc = acc + ctx[pl.ds(lead + j, rows), :] * w_ref[j:j + 1, :]
    z = _layer_norm(acc + cb_ref[...], g_ref[...], b_ref[...])
    y_ref[...] = z * _sigmoid(z)
    if n_tiles > 1:
        ctx[0:CONV_CTX, :] = ctx[rows:rows + CONV_CTX, :]


def _conv(h, mix, buf, w, cb, g, b, *, rows, n_seq, n_tiles, row0):
    rb0 = row0 // rows
    rmap = lambda col: (lambda s, t: (rb0 + s * n_tiles + t, col))
    const = lambda shape: pl.BlockSpec(shape, lambda s, t: (0, 0))
    nbytes = 2 * 4 * rows * MIX * 4 + (CONV_CTX + rows) * MIX * 4 + 8 * rows * MIX * 4
    return pl.pallas_call(
        functools.partial(_conv_body, rows=rows, n_tiles=n_tiles),
        grid=(n_seq, n_tiles),
        in_specs=[pl.BlockSpec((rows, MIX), rmap(COL_CV)), pl.BlockSpec((rows, MIX), rmap(COL_CG)),
                  pl.BlockSpec((1, CONV_CTX, MIX), lambda s, t: (s, 0, 0)),
                  const((CONV_CTX, MIX)), const((1, MIX)), const((1, MIX)), const((1, MIX)),
                  pl.BlockSpec(memory_space=pl.ANY)],
        out_specs=[pl.BlockSpec((rows, MIX), rmap(2)),
                   pl.BlockSpec((rows, MIX), lambda s, t: (s * n_tiles + t, 0))],
        out_shape=[jax.ShapeDtypeStruct(mix.shape, F32),
                   jax.ShapeDtypeStruct((n_seq * n_tiles * rows, MIX), F32)],
        scratch_shapes=[pltpu.VMEM((CONV_CTX + rows, MIX), F32)],
        input_output_aliases={7: 0},
        compiler_params=_params(("arbitrary", "arbitrary"), nbytes),
        name="conv",
    )(h, h, buf, w, cb, g, b, mix)


def _gelu_tanh(x):
    c = math.sqrt(2.0 / math.pi)
    return x * (0.5 * (1.0 + jnp.tanh(c * (x + 0.044715 * (x * x * x)))))


def _sgu_body(u_ref, v_ref, g_ref, b_ref, ws_ref, bias_ref, avg_ref, mix_ref, y_ref, vn_ref, *, rows):
    del mix_ref
    u = _gelu_tanh(u_ref[...])
    v = _gelu_tanh(v_ref[...])
    vn = _group_norm(v, avg_ref[...]) * g_ref[...] + b_ref[...]
    vn_ref[...] = vn
    if rows < SGU_CHUNK:
        vn = jnp.concatenate([vn, jnp.zeros((SGU_CHUNK - rows, MIX), F32)], axis=0)
    vnb = vn.astype(BF16)
    tri = (lax.broadcasted_iota(I32, (SGU_CHUNK, SGU_CHUNK), 0)
           >= lax.broadcasted_iota(I32, (SGU_CHUNK, SGU_CHUNK), 1))
    head = _lane_head(MIX)
    z = jnp.zeros((SGU_CHUNK, MIX), F32)
    for gi in range(N_HEADS):
        wm = jnp.where(tri, ws_ref[gi], 0.0).astype(BF16)
        z = jnp.where(head == gi, _dot(wm, vnb), z)
    z = z + bias_ref[...]
    y_ref[...] = u * z[0:rows, :]


def _sgu(h, mix, g, b, ws, bias, avg, *, rows, n_chunks, row0):
    rb0 = row0 // rows
    rmap = lambda col: (lambda c: (rb0 + c, col))
    const = lambda shape: pl.BlockSpec(shape, lambda c: (0,) * len(shape))
    nbytes = 2 * (4 * rows * MIX * 4 + N_HEADS * SGU_CHUNK * SGU_CHUNK * 4 + SGU_CHUNK * MIX * 4) + 8 * SGU_CHUNK * MIX * 4
    return pl.pallas_call(
        functools.partial(_sgu_body, rows=rows),
        grid=(n_chunks,),
        in_specs=[pl.BlockSpec((rows, MIX), rmap(COL_GU)), pl.BlockSpec((rows, MIX), rmap(COL_GV)),
                  const((1, MIX)), const((1, MIX)), const((N_HEADS, SGU_CHUNK, SGU_CHUNK)),
                  const((SGU_CHUNK, MIX)), const((MIX, MIX)), pl.BlockSpec(memory_space=pl.ANY)],
        out_specs=[pl.BlockSpec((rows, MIX), rmap(3)), pl.BlockSpec((rows, MIX), lambda c: (c, 0))],
        out_shape=[jax.ShapeDtypeStruct(mix.shape, F32), jax.ShapeDtypeStruct((n_chunks * rows, MIX), F32)],
        input_output_aliases={7: 0},
        compiler_params=_params(("arbitrary",), nbytes),
        name="sgu",
    )(h, h, g, b, ws, bias, avg, mix)


def _out_router_body(mix_ref, w_ref, x_ref, g_ref, b_ref, wr_ref, br_ref,
                     x1_ref, route_ref, counts_ref, wbf, wrbf, *, alpha):
    @pl.when(pl.program_id(0) == 0)
    def _():
        wbf[...] = w_ref[0].astype(BF16)
        wrbf[...] = wr_ref[...].astype(BF16)

    y = _dot(mix_ref[...].astype(BF16), wbf[...])
    x1 = _layer_norm(alpha * x_ref[...] + y, g_ref[...], b_ref[...])
    x1_ref[...] = x1
    logits = _dot(x1.astype(BF16), wrbf[...]) + br_ref[...]
    lane = lax.broadcasted_iota(I32, (1, LANE), 1)

    def first_max(mask, vals):
        masked = jnp.where(mask, vals, NEG_INF)
        top = jnp.max(masked, axis=-1, keepdims=True)
        idx = jnp.min(jnp.where(mask & (masked == top), lane, LANE), axis=-1, keepdims=True)
        return top, idx

    gmask = lane < N_GROUPS
    gmax, gsel = first_max(gmask, logits)
    g_gate = 1.0 / jnp.sum(jnp.where(gmask, jnp.exp(logits - gmax), 0.0), axis=-1, keepdims=True)
    lo = N_GROUPS + EXPERTS_PER_GROUP * gsel
    emask = (lane >= lo) & (lane < lo + EXPERTS_PER_GROUP)
    v1, i1 = first_max(emask, logits)
    v2, i2 = first_max(emask & (lane != i1), logits)
    e12 = jnp.exp(v2 - v1)
    gate_a = g_gate * (1.0 / (1.0 + e12))
    gate_b = g_gate * (e12 / (1.0 + e12))
    e_a = i1 - N_GROUPS
    e_b = i2 - N_GROUPS

    oh_a = (lane == e_a).astype(F32)
    oh_b = (lane == e_b).astype(F32)
    both = oh_a + oh_b
    strict = (lax.broadcasted_iota(I32, (TM, TM), 0) > lax.broadcasted_iota(I32, (TM, TM), 1))
    before = _dot(strict.astype(BF16), both.astype(BF16))
    rank_a = jnp.sum(oh_a * before, axis=-1, keepdims=True)
    rank_b = jnp.sum(oh_b * before, axis=-1, keepdims=True)
    counts_ref[0] = jnp.sum(both, axis=0, keepdims=True)

    route = jnp.zeros((TM, LANE), F32)
    for ln, val in enumerate((e_a.astype(F32), e_b.astype(F32), gate_a, gate_b, rank_a, rank_b)):
        route = jnp.where(lane == ln, val, route)
    route_ref[...] = route


def _out_router(mix, w_out, layer, x, g, b, wr, br, *, alpha):
    nt, d = x.shape
    kdim = mix.shape[1]
    const = lambda shape: pl.BlockSpec(shape, lambda i: (0, 0))
    row = lambda width: pl.BlockSpec((TM, width), lambda i: (i, 0))
    nbytes = 2 * (kdim * d * 4 + d * LANE * 4 + TM * kdim * 4 + 2 * TM * d * 4 + TM * LANE * 4) + kdim * d * 2
    return pl.pallas_call(
        functools.partial(_out_router_body, alpha=alpha),
        grid=(nt // TM,),
        in_specs=[row(kdim), pl.BlockSpec((1, kdim, d), lambda i: (layer, 0, 0)), row(d),
                  const((1, d)), const((1, d)), const((d, LANE)), const((1, LANE))],
        out_specs=[row(d), row(LANE), pl.BlockSpec((1, 1, LANE), lambda i: (i, 0, 0))],
        out_shape=[jax.ShapeDtypeStruct((nt, d), F32), jax.ShapeDtypeStruct((nt, LANE), F32),
                   jax.ShapeDtypeStruct((nt // TM, 1, LANE), F32)],
        scratch_shapes=[pltpu.VMEM((kdim, d), BF16), pltpu.VMEM((d, LANE), BF16)],
        compiler_params=_params(("arbitrary",), nbytes),
        name="out_router",
    )(mix, w_out, x, g, b, wr, br)


def _local_slots(route, off_row):
    lane = lax.broadcasted_iota(I32, (1, LANE), 1).astype(F32)
    slots = []
    for kk in range(TOP_K):
        onehot = (lane == route[:, kk:kk + 1]).astype(F32)
        slots.append(jnp.sum(onehot * off_row, axis=-1, keepdims=True) + route[:, 4 + kk:5 + kk])
    return slots


def _chunk_copy(hbm, hbm_row, loc, chunk, sem, *, to_hbm):
    vm = loc.at[pl.ds(pl.multiple_of(chunk * MOE_CHUNK, MOE_CHUNK), MOE_CHUNK), :]
    hb = hbm.at[pl.ds(pl.multiple_of(hbm_row, MOE_CHUNK), MOE_CHUNK), :]
    return pltpu.make_async_copy(vm, hb, sem) if to_hbm else pltpu.make_async_copy(hb, vm, sem)


def _dispatch_body(dst_tab, n_chunks, x_ref, route_ref, off_ref, xs_in, xs_hbm, loc, sem):
    del xs_in
    i = pl.program_id(0)
    slot_a, slot_b = _local_slots(route_ref[...], off_ref[0])
    col = lax.broadcasted_iota(I32, (1, LOC_ROWS), 1).astype(F32)
    onehot = ((col == slot_a) | (col == slot_b)).astype(BF16)
    loc[...] = _dot_tn(onehot, x_ref[...].astype(BF16))

    def start(c, carry):
        _chunk_copy(xs_hbm, dst_tab[i * MAX_CHUNKS + c], loc, c, sem, to_hbm=True).start()
        return carry

    def wait(c, carry):
        _chunk_copy(xs_hbm, 0, loc, 0, sem, to_hbm=True).wait()
        return carry

    lax.fori_loop(0, n_chunks[i], start, 0)
    lax.fori_loop(0, n_chunks[i], wait, 0)


def _dispatch(dst_tab, n_chunks, x1, route, off, xs0):
    nt, d = x1.shape
    row = lambda width: pl.BlockSpec((TM, width), lambda i, dt, nc: (i, 0))
    nbytes = 2 * (TM * d * 4 + TM * LANE * 4) + LOC_ROWS * d * 4 + 4 * TM * LOC_ROWS * 4 + LOC_ROWS * d * 4
    return pl.pallas_call(
        _dispatch_body,
        grid_spec=pltpu.PrefetchScalarGridSpec(
            num_scalar_prefetch=2,
            grid=(nt // TM,),
            in_specs=[row(d), row(LANE), pl.BlockSpec((1, 1, LANE), lambda i, dt, nc: (i, 0, 0)),
                      pl.BlockSpec(memory_space=pl.ANY)],
            out_specs=pl.BlockSpec(memory_space=pl.ANY),
            scratch_shapes=[pltpu.VMEM((LOC_ROWS, d), F32), pltpu.SemaphoreType.DMA(())]),
        out_shape=jax.ShapeDtypeStruct(xs0.shape, F32),
        input_output_aliases={5: 0},
        compiler_params=_params(("arbitrary",), nbytes),
        name="moe_dispatch",
    )(dst_tab, n_chunks, x1, route, off, xs0)


def _ffn_body(blk_e, blk_valid, x_ref, w1_ref, w3_ref, w2_ref, y_ref, w1b, w3b, w2b):
    j = pl.program_id(0)
    n_valid = blk_valid[j]

    @pl.when(n_valid == 0)
    def _():
        y_ref[...] = jnp.zeros_like(y_ref)

    @pl.when(n_valid > 0)
    def _():
        @pl.when((j == 0) | (blk_e[j] != blk_e[jnp.maximum(j - 1, 0)]))
        def _():
            w1b[...] = w1_ref[0, 0].astype(BF16)
            w3b[...] = w3_ref[0, 0].astype(BF16)
            w2b[...] = w2_ref[0, 0].astype(BF16)

        xb = x_ref[...].astype(BF16)
        h1 = _dot(xb, w1b[...])
        h3 = _dot(xb, w3b[...])
        act = (h1 * _sigmoid(h1)) * h3
        y_ref[...] = _dot(act.astype(BF16), w2b[...])


def _ffn(xs, w1, w3, w2, layer, blk_e, blk_valid):
    d = xs.shape[1]
    de = w1.shape[3]
    n_blk = xs.shape[0] // MOE_BLK
    wmap = lambda j, be, bv: (layer, be[j], 0, 0)
    rows = pl.BlockSpec((MOE_BLK, d), lambda j, be, bv: (j, 0))
    nbytes = 2 * (3 * d * de * 4 + 2 * MOE_BLK * d * 4) + 3 * d * de * 2 + 6 * MOE_BLK * de * 4
    return pl.pallas_call(
        _ffn_body,
        grid_spec=pltpu.PrefetchScalarGridSpec(
            num_scalar_prefetch=2,
            grid=(n_blk,),
            in_specs=[rows, pl.BlockSpec((1, 1, d, de), wmap), pl.BlockSpec((1, 1, d, de), wmap),
                      pl.BlockSpec((1, 1, de, d), wmap)],
            out_specs=rows,
            scratch_shapes=[pltpu.VMEM((d, de), BF16), pltpu.VMEM((d, de), BF16), pltpu.VMEM((de, d), BF16)]),
        out_shape=jax.ShapeDtypeStruct(xs.shape, F32),
        compiler_params=_params(("arbitrary",), nbytes),
        name="expert_ffn",
    )(blk_e, blk_valid, xs, w1, w3, w2)


def _combine_body(dst_tab, n_chunks, x_ref, route_ref, off_ref, g_ref, b_ref, ys_hbm, x2_ref,
                  loc, sem, *, alpha):
    i = pl.program_id(0)

    @pl.when(i == 0)
    def _():
        loc[...] = jnp.zeros_like(loc)

    def start(c, carry):
        _chunk_copy(ys_hbm, dst_tab[i * MAX_CHUNKS + c], loc, c, sem, to_hbm=False).start()
        return carry

    def wait(c, carry):
        _chunk_copy(ys_hbm, 0, loc, 0, sem, to_hbm=False).wait()
        return carry

    lax.fori_loop(0, n_chunks[i], start, 0)
    route = route_ref[...]
    slot_a, slot_b = _local_slots(route, off_ref[0])
    col = lax.broadcasted_iota(I32, (1, LOC_ROWS), 1).astype(F32)
    gates = jnp.where(col == slot_a, route[:, 2:3], 0.0) + jnp.where(col == slot_b, route[:, 3:4], 0.0)
    lax.fori_loop(0, n_chunks[i], wait, 0)
    f = _dot(gates.astype(BF16), loc[...].astype(BF16))
    x2_ref[...] = _layer_norm(alpha * x_ref[...] + f, g_ref[...], b_ref[...])


def _combine(dst_tab, n_chunks, x1, route, off, g, b, ys, *, alpha):
    nt, d = x1.shape
    row = lambda width: pl.BlockSpec((TM, width), lambda i, dt, nc: (i, 0))
    const = lambda shape: pl.BlockSpec(shape, lambda i, dt, nc: (0, 0))
    nbytes = 2 * (2 * TM * d * 4 + TM * LANE * 4) + LOC_ROWS * d * 4 + 4 * TM * LOC_ROWS * 4 + LOC_ROWS * d * 2
    return pl.pallas_call(
        functools.partial(_combine_body, alpha=alpha),
        grid_spec=pltpu.PrefetchScalarGridSpec(
            num_scalar_prefetch=2,
            grid=(nt // TM,),
            in_specs=[row(d), row(LANE), pl.BlockSpec((1, 1, LANE), lambda i, dt, nc: (i, 0, 0)),
                      const((1, d)), const((1, d)), pl.BlockSpec(memory_space=pl.ANY)],
            out_specs=row(d),
            scratch_shapes=[pltpu.VMEM((LOC_ROWS, d), F32), pltpu.SemaphoreType.DMA(())]),
        out_shape=jax.ShapeDtypeStruct((nt, d), F32),
        compiler_params=_params(("arbitrary",), nbytes),
        name="combine_ln",
    )(dst_tab, n_chunks, x1, route, off, g, b, ys)


def _rope_tables(pos, rot_dim, theta):
    half = rot_dim // 2
    inv = jnp.float32(theta) ** (-jnp.arange(half, dtype=F32) * (2.0 / rot_dim))
    ang = pos.astype(F32)[:, None] * inv[None, :]
    c, s = jnp.cos(ang), jnp.sin(ang)
    rest = HEAD_DIM - rot_dim
    ones = jnp.ones((pos.shape[0], rest), F32)
    zeros = jnp.zeros((pos.shape[0], rest), F32)
    cos_h = jnp.concatenate([c, c, ones], axis=1)
    sin_h = jnp.concatenate([-s, s, zeros], axis=1)
    return jnp.tile(cos_h, (1, N_HEADS)), jnp.tile(sin_h, (1, N_HEADS))


def _block_diag(s):
    n = s.shape[0]
    eye = jnp.eye(N_HEADS, dtype=s.dtype)
    return (s[:, :, :, None, :] * eye[None, :, None, :, None]).reshape(n, MIX, MIX)


def _diag_blocks(s):
    return jnp.stack([s[:, h * HEAD_DIM:(h + 1) * HEAD_DIM, h * HEAD_DIM:(h + 1) * HEAD_DIM]
                      for h in range(N_HEADS)], axis=1)


def _routing_plan(tile_counts, n_blk):
    cnt_t = tile_counts[:, 0, :N_EXPERTS].astype(I32)
    run_chunks = (cnt_t + MOE_CHUNK - 1) // MOE_CHUNK
    run_rows = run_chunks * MOE_CHUNK
    cnt = jnp.sum(run_rows, axis=0)
    pcnt = (cnt + MOE_BLK - 1) // MOE_BLK * MOE_BLK
    pstart = jnp.cumsum(pcnt) - pcnt
    n_act = jnp.sum(pcnt) // MOE_BLK
    gstart = pstart[None, :] + jnp.cumsum(run_rows, axis=0) - run_rows
    chunk0 = jnp.cumsum(run_chunks, axis=1) - run_chunks
    n_chunks = jnp.sum(run_chunks, axis=1).astype(I32)
    c = jnp.arange(MAX_CHUNKS, dtype=I32)
    owner = jnp.clip(jnp.sum(chunk0[:, None, :] <= c[None, :, None], axis=-1) - 1, 0, N_EXPERTS - 1)
    dst_tab = (jnp.take_along_axis(gstart, owner, axis=1)
               + (c[None, :] - jnp.take_along_axis(chunk0, owner, axis=1)) * MOE_CHUNK)
    off = jnp.zeros((cnt_t.shape[0], 1, LANE), F32).at[:, 0, :N_EXPERTS].set((chunk0 * MOE_CHUNK).astype(F32))
    pos = jnp.arange(n_blk, dtype=I32) * MOE_BLK
    owner_b = jnp.clip(jnp.sum(pstart[None, :] <= pos[:, None], axis=-1) - 1, 0, N_EXPERTS - 1)
    active = pos < n_act * MOE_BLK
    blk_valid = jnp.where(active, jnp.clip(pstart[owner_b] + cnt[owner_b] - pos, 0, MOE_BLK), 0)
    blk_e = jnp.where(active, owner_b, owner_b[jnp.maximum(n_act - 1, 0)])
    return dst_tab.reshape(-1).astype(I32), n_chunks, off, blk_e.astype(I32), blk_valid.astype(I32)


def kernel(x_prompt, x_sample, state_ret, cache_swa_k, cache_swa_v, state_conv, w_in, w_out, ret_gn_g,
           conv_w, conv_b, conv_ln_g, conv_ln_b, sgu_ln_g, sgu_ln_b, sgu_w, sgu_b, ln1_g, ln1_b,
           ln2_g, ln2_b, router_g_w, router_g_b, router_e_w, router_e_b, moe_w1, moe_w3, moe_w2):
    depth = w_in.shape[0]
    n_seq, seq, d = x_prompt.shape
    n_dec, t_new, _ = x_sample.shape
    past = cache_swa_k.shape[2]
    rows_s = -(-t_new // SUBLANE) * SUBLANE
    n_p = n_seq * seq
    n_s = n_dec * rows_s
    nt = n_p + n_s
    assert seq % SWA_SB == 0 and seq % CONV_TT == 0 and seq % TM == 0 and n_s % TM == 0
    assert t_new % RET_CHUNK != 0 and t_new <= SGU_CHUNK and w_in.shape[2] == N_COLBLK * MIX
    alpha = (2 * depth) ** 0.25
    keep = min(DILATED_CONFIGS[-1][0], seq)
    n_blk = -(-(nt * TOP_K + (nt // TM) * N_EXPERTS * (MOE_CHUNK - 1) + N_EXPERTS * (MOE_BLK - 1)) // MOE_BLK)
    cache_k = jnp.transpose(cache_swa_k, (0, 1, 3, 4, 2)).reshape(depth, n_dec, MIX, past)
    cache_v = jnp.transpose(cache_swa_v, (0, 1, 3, 4, 2)).reshape(depth, n_dec, MIX, past)

    x = jnp.concatenate([x_prompt.reshape(n_p, d),
                         jnp.pad(x_sample, ((0, 0), (0, rows_s - t_new), (0, 0))).reshape(n_s, d)], axis=0)
    pos = jnp.concatenate([jnp.arange(seq, dtype=I32),
                           PAST_LEN + jnp.arange(n_s, dtype=I32) % rows_s])
    rope = _rope_tables(pos, HEAD_DIM, RET_THETA) + _rope_tables(pos, ROPE_DIM, ROPE_THETA)
    ret_tabs_p = _retention_tables(RET_CHUNK, RET_CHUNK)
    ret_tabs_s = _retention_tables(t_new, rows_s)
    lane_head = jnp.arange(MIX) // HEAD_DIM
    avg = ((lane_head[:, None] == lane_head[None, :]).astype(F32) / HEAD_DIM).astype(BF16)
    zero_state = jnp.zeros((n_seq, MIX, MIX), F32)
    zero_ctx = jnp.zeros((n_seq, CONV_CTX, MIX), F32)
    row2 = lambda v: v.reshape(1, -1)
    mix = jnp.zeros((nt, 4 * MIX), F32)

    outs = [[] for _ in range(9)]
    for l in range(depth):
        h = _in_proj(x, w_in, l, rope, n_p // TM, seq // TM)
        gn = row2(ret_gn_g[l])
        mix, s_p = _retention(h, mix, zero_state, ret_tabs_p, avg, gn,
                              rows=RET_CHUNK, n_seq=n_seq, n_chunks=seq // RET_CHUNK, row0=0)
        mix, s_s = _retention(h, mix, _block_diag(state_ret[l]), ret_tabs_s, avg, gn,
                              rows=rows_s, n_seq=n_dec, n_chunks=1, row0=n_p)
        mix = _swa_prompt(h, mix, seq=seq, n_seq=n_seq)
        mix = _swa_sample(h, mix, cache_k, cache_v, l, rows=rows_s, n_seq=n_dec, row0=n_p)
        cw = jnp.pad(conv_w[l], ((0, CONV_CTX - CONV_WIDTH), (0, 0)))
        cargs = (cw, row2(conv_b[l]), row2(conv_ln_g[l]), row2(conv_ln_b[l]))
        mix, glu_p = _conv(h, mix, zero_ctx, *cargs, rows=CONV_TT, n_seq=n_seq, n_tiles=seq // CONV_TT, row0=0)
        ctx_s = jnp.pad(state_conv[l], ((0, 0), (CONV_CTX - (CONV_WIDTH - 1), 0), (0, 0)))
        mix, glu_s = _conv(h, mix, ctx_s, *cargs, rows=rows_s, n_seq=n_dec, n_tiles=1, row0=n_p)
        sbias = jnp.repeat(sgu_b[l].T, HEAD_DIM, axis=1)
        sargs = (row2(sgu_ln_g[l]), row2(sgu_ln_b[l]), sgu_w[l], sbias, avg)
        mix, _ = _sgu(h, mix, *sargs, rows=SGU_CHUNK, n_chunks=n_p // SGU_CHUNK, row0=0)
        mix, vn_s = _sgu(h, mix, *sargs, rows=rows_s, n_chunks=n_dec, row0=n_p)

        wr = jnp.concatenate([router_g_w[l], jnp.transpose(router_e_w[l], (1, 0, 2)).reshape(d, N_EXPERTS),
                              jnp.zeros((d, LANE - N_GROUPS - N_EXPERTS), F32)], axis=1)
        br = jnp.concatenate([router_g_b[l], router_e_b[l].reshape(-1),
                              jnp.zeros((LANE - N_GROUPS - N_EXPERTS,), F32)]).reshape(1, LANE)
        x1, route, counts = _out_router(mix, w_out, l, x, row2(ln1_g[l]), row2(ln1_b[l]), wr, br, alpha=alpha)
        dst_tab, n_chunks, off, blk_e, blk_valid = _routing_plan(counts, n_blk)
        xs = _dispatch(dst_tab, n_chunks, x1, route, off, jnp.zeros((n_blk * MOE_BLK, d), F32))
        ys = _ffn(xs, moe_w1, moe_w3, moe_w2, l, blk_e, blk_valid)
        x = _combine(dst_tab, n_chunks, x1, route, off, row2(ln2_g[l]), row2(ln2_b[l]), ys, alpha=alpha)

        def tail(col, n_keep):
            return jnp.stack([lax.slice(h, ((b + 1) * seq - n_keep, col * MIX), ((b + 1) * seq, (col + 1) * MIX))
                              for b in range(n_seq)])

        def new_rows(col):
            blk = lax.slice(h, (n_p, col * MIX), (nt, (col + 1) * MIX))
            return blk.reshape(n_dec, rows_s, MIX)[:, :t_new]

        heads = lambda t: t.reshape(t.shape[0], t.shape[1], N_HEADS, HEAD_DIM)
        outs[0].append(_diag_blocks(s_p))
        outs[1].append(heads(tail(COL_SK, keep)))
        outs[2].append(heads(tail(COL_SV, keep)))
        outs[3].append(glu_p.reshape(n_seq, seq, MIX)[:, seq - (CONV_WIDTH - 1):])
        outs[4].append(_diag_blocks(s_s))
        outs[5].append(heads(new_rows(COL_SK)))
        outs[6].append(heads(new_rows(COL_SV)))
        outs[7].append(glu_s.reshape(n_dec, rows_s, MIX)[:, :t_new])
        outs[8].append(vn_s.reshape(n_dec, rows_s, MIX)[:, :t_new])

    y_prompt = x[:n_p].reshape(n_seq, seq, d)
    y_sample = x[n_p:].reshape(n_dec, rows_s, d)[:, :t_new]
    return (y_prompt, y_sample) + tuple(jnp.stack(o) for o in outs)
```

```python
import functools
import math

import jax
import jax.numpy as jnp
from jax import lax
from jax.experimental import pallas as pl
from jax.experimental.pallas import tpu as pltpu

F32 = jnp.float32
BF16 = jnp.bfloat16
I32 = jnp.int32

HEAD_DIM = 64
N_HEADS = 4
MIX = N_HEADS * HEAD_DIM
(COL_RQ, COL_RK, COL_RV, COL_RG, COL_SQ, COL_SK, COL_SV,
 COL_CV, COL_CG, COL_GU, COL_GV) = range(11)
N_COLBLK = 11
RET_CHUNK = 128
RET_THETA = 10000.0
ROPE_THETA = 500000.0
ROPE_DIM = HEAD_DIM // 4
DILATED_CONFIGS = ((128, 1), (512, 4), (2048, 16))
DIL_BLOCK = 128
CONV_WIDTH = 31
SGU_CHUNK = 128
N_GROUPS = 4
EXPERTS_PER_GROUP = 8
N_EXPERTS = N_GROUPS * EXPERTS_PER_GROUP
TOP_K = 2
PAST_LEN = 8192
LN_EPS = 1e-5
NEG_INF = -1e30

LANE = 128
SUBLANE = 8
VMEM_CAP_BYTES = 60000 * 1024
COMPILER_TEMP_BYTES = 12 * 1024 * 1024

TM = 256
ROUTE_SUB = 64
MOE_BLK = 256
MOE_CHUNK = SUBLANE
LOC_ROWS = -(-(TM * TOP_K + N_EXPERTS * (MOE_CHUNK - 1)) // LANE) * LANE
MAX_CHUNKS = LOC_ROWS // MOE_CHUNK
CONV_TT = 512
CONV_CTX = 32
SWA_SB = DIL_BLOCK * 16
SWA_UNROLL = 4
SGU_ROWS = 1024
RET_ROWS = 1024
SAMPLE_SEQS = 8


def _vmem(nbytes):
    return int(min(VMEM_CAP_BYTES, nbytes + COMPILER_TEMP_BYTES))


def _params(sem, nbytes):
    return pltpu.CompilerParams(dimension_semantics=sem, vmem_limit_bytes=_vmem(nbytes))


def _dot(a, b):
    return jnp.dot(a, b, preferred_element_type=F32)


def _dot_nt(a, b):
    return lax.dot_general(a, b, (((1,), (1,)), ((), ())), preferred_element_type=F32)


def _dot_tn(a, b):
    return lax.dot_general(a, b, (((0,), (0,)), ((), ())), preferred_element_type=F32)


def _split_dot(x, m):
    hi = x.astype(BF16)
    lo = (x - hi.astype(F32)).astype(BF16)
    return _dot(hi, m) + _dot(lo, m)


def _sigmoid(x):
    return 1.0 / (1.0 + jnp.exp(-x))


def _layer_norm(z, g, b):
    mu = jnp.mean(z, axis=-1, keepdims=True)
    d = z - mu
    var = jnp.mean(d * d, axis=-1, keepdims=True)
    return d * lax.rsqrt(var + LN_EPS) * g + b


def _group_norm(z, avg):
    mu = _split_dot(z, avg)
    d = z - mu
    var = _split_dot(d * d, avg)
    return d * lax.rsqrt(var + LN_EPS)


def _lane_head(width):
    return lax.broadcasted_iota(I32, (1, width), 1) // HEAD_DIM


def _rotate(x, cos, sin_signed, half):
    first = (lax.broadcasted_iota(I32, (1, LANE), 1) % HEAD_DIM) < half
    parts = []
    for p in range(x.shape[1] // LANE):
        t = x[:, p * LANE:(p + 1) * LANE]
        up = pltpu.roll(t, half, 1)
        down = pltpu.roll(t, LANE - half, 1)
        parts.append(jnp.where(first, down, up))
    return x * cos + jnp.concatenate(parts, axis=1) * sin_signed


def _in_proj_body(x_ref, w_ref, cr_ref, sr_ref, cs_ref, ss_ref, h_ref, wbf_ref):
    @pl.when(pl.program_id(0) == 0)
    def _():
        for j in range(N_COLBLK):
            wbf_ref[:, j * MIX:(j + 1) * MIX] = w_ref[0, :, j * MIX:(j + 1) * MIX].astype(BF16)

    xb = x_ref[...].astype(BF16)
    for j in range(N_COLBLK):
        hj = _dot(xb, wbf_ref[:, j * MIX:(j + 1) * MIX])
        if j in (COL_RQ, COL_RK):
            hj = _rotate(hj, cr_ref[...], sr_ref[...], HEAD_DIM // 2)
        if j == COL_RK:
            hj = hj * (HEAD_DIM ** -0.5)
        if j in (COL_SQ, COL_SK):
            hj = _rotate(hj, cs_ref[...], ss_ref[...], ROPE_DIM // 2)
        h_ref[:, j * MIX:(j + 1) * MIX] = hj


def _in_proj(x, w, layer, tabs, n_prompt_tiles, tiles_per_seq):
    nt, d = x.shape
    width = w.shape[2]

    def tab_map(i):
        return (jnp.where(i < n_prompt_tiles, i % tiles_per_seq, tiles_per_seq + i - n_prompt_tiles), 0)

    tab_spec = pl.BlockSpec((TM, MIX), tab_map)
    nbytes = 2 * d * width * 4 + d * width * 2 + 2 * TM * d * 4 + 8 * TM * MIX * 4 + 2 * TM * width * 4
    return pl.pallas_call(
        _in_proj_body,
        grid=(nt // TM,),
        in_specs=[pl.BlockSpec((TM, d), lambda i: (i, 0)),
                  pl.BlockSpec((1, d, width), lambda i: (layer, 0, 0)),
                  tab_spec, tab_spec, tab_spec, tab_spec],
        out_specs=pl.BlockSpec((TM, width), lambda i: (i, 0)),
        out_shape=jax.ShapeDtypeStruct((nt, width), F32),
        scratch_shapes=[pltpu.VMEM((d, width), BF16)],
        compiler_params=_params(("arbitrary",), nbytes),
        name="in_proj",
    )(x, w, *tabs)


def _retention_body(q_ref, k_ref, v_ref, g_ref, s0_ref, dec_ref, qdec_ref, kdec_ref,
                    cmat_ref, bdm_ref, avg_ref, gn_ref, mix_ref, y_ref, sout_ref, s_scr,
                    *, n_steps, seqs, chunks, rows):
    del mix_ref
    c = pl.program_id(1)

    @pl.when(c == 0)
    def _():
        s_scr[...] = s0_ref[...]

    head = _lane_head(MIX)
    for s in range(seqs):
        state = s_scr[s]
        for cc in range(chunks):
            rs = slice((s * chunks + cc) * rows, (s * chunks + cc + 1) * rows)
            q = q_ref[rs, :]
            k = k_ref[rs, :]
            vb = v_ref[rs, :].astype(BF16)
            kb = k.astype(BF16)
            o = _dot(q.astype(BF16), state.astype(BF16)) * qdec_ref[...]
            for h in range(N_HEADS):
                mh = head == h
                a = _dot_nt(jnp.where(mh, q, 0.0).astype(BF16), kb) * dec_ref[h]
                o = o + jnp.where(mh, _dot(a.astype(BF16), vb), 0.0)
            upd = _dot_tn((k * kdec_ref[...]).astype(BF16), vb)
            state = state * cmat_ref[...] + upd * bdm_ref[...]
            g = g_ref[rs, :]
            y_ref[rs, :] = _group_norm(o, avg_ref[...]) * gn_ref[...] * (g * _sigmoid(g))
        s_scr[s] = state

    @pl.when(c == n_steps - 1)
    def _():
        sout_ref[...] = s_scr[...]


def _retention_tables(l_real, l_pad):
    hh = jnp.arange(N_HEADS, dtype=F32)
    log_g = jnp.log1p(-jnp.exp2(-5.0 - hh))
    i = jnp.arange(l_pad, dtype=F32)
    rel = i[:, None] - i[None, :]
    dec = jnp.where(rel >= 0, jnp.exp(log_g[:, None, None] * jnp.maximum(rel, 0.0)), 0.0)
    qd = jnp.exp(log_g[:, None] * (i + 1.0))
    kd = jnp.where(i < l_real, jnp.exp(log_g[:, None] * (l_real - 1.0 - i)), 0.0)
    cd = jnp.exp(log_g * l_real)
    expand = lambda t: jnp.repeat(t.T, HEAD_DIM, axis=1)
    row_head = jnp.arange(MIX) // HEAD_DIM
    bdm = (row_head[:, None] == row_head[None, :]).astype(F32)
    cmat = cd[row_head][:, None] * bdm
    return dec, expand(qd), expand(kd), cmat, bdm


def _retention(h, mix, s0, tabs, avg, gn_g, *, rows, n_seq, n_chunks, seqs, chunks, row0):
    assert (seqs == 1 or chunks == n_chunks) and n_seq % seqs == 0 and n_chunks % chunks == 0
    blk = seqs * chunks * rows
    rb0 = row0 // blk
    n_steps = n_chunks // chunks
    dec, qdec, kdec, cmat, bdm = tabs

    def hspec(col):
        return pl.BlockSpec((blk, MIX), lambda b, c: (rb0 + b * n_steps + c, col))

    const2 = lambda shape: pl.BlockSpec(shape, lambda b, c: (0, 0))
    state_spec = pl.BlockSpec((seqs, MIX, MIX), lambda b, c: (b, 0, 0))
    in_specs = [hspec(COL_RQ), hspec(COL_RK), hspec(COL_RV), hspec(COL_RG), state_spec,
                pl.BlockSpec((N_HEADS, rows, rows), lambda b, c: (0, 0, 0)),
                const2((rows, MIX)), const2((rows, MIX)),
                const2((MIX, MIX)), const2((MIX, MIX)), const2((MIX, MIX)), const2((1, MIX)),
                pl.BlockSpec(memory_space=pl.ANY)]
    args = [h, h, h, h, s0, dec, qdec, kdec, cmat, bdm, avg, gn_g, mix]
    nbytes = (2 * (5 * blk * MIX * 4 + 2 * rows * MIX * 4 + (3 + 2 * seqs) * MIX * MIX * 4 + N_HEADS * rows * rows * 4)
              + seqs * MIX * MIX * 4)
    return pl.pallas_call(
        functools.partial(_retention_body, n_steps=n_steps, seqs=seqs, chunks=chunks, rows=rows),
        grid=(n_seq // seqs, n_steps),
        in_specs=in_specs,
        out_specs=[pl.BlockSpec((blk, MIX), lambda b, c: (rb0 + b * n_steps + c, 0)), state_spec],
        out_shape=[jax.ShapeDtypeStruct(mix.shape, F32),
                   jax.ShapeDtypeStruct((n_seq, MIX, MIX), F32)],
        scratch_shapes=[pltpu.VMEM((seqs, MIX, MIX), F32)],
        input_output_aliases={12: 0},
        compiler_params=_params(("arbitrary", "arbitrary"), nbytes),
        name="retention",
    )(*args)


def _swa_prompt_body(q_ref, k_ref, v_ref, mix_ref, y_ref, o0, o1, o2, l0, l1, l2, *, seq):
    del mix_ref
    o_scr = (o0, o1, o2)
    l_scr = (l0, l1, l2)
    head = _lane_head(LANE)
    qi = lax.broadcasted_iota(I32, (DIL_BLOCK, 1), 0)
    ki = lax.broadcasted_iota(I32, (1, 2 * DIL_BLOCK), 1)
    rel = DIL_BLOCK + qi - ki
    scale = HEAD_DIM ** -0.5
    for sb in range(seq // SWA_SB):
        for ci, (window, dil) in enumerate(DILATED_CONFIGS):
            span = window // dil
            band = (rel >= 0) & (rel <= span)

            def block(i, carry, sb=sb, ci=ci, dil=dil, band=band):
                nloc = i // dil
                off = nloc * (DIL_BLOCK * dil) + i % dil
                start = sb * SWA_SB + off
                if sb == 0:
                    first = nloc == 0
                    pstart = jnp.where(first, start, start - DIL_BLOCK * dil)
                    valid = band & (ki >= jnp.where(first, DIL_BLOCK, 0))
                else:
                    pstart = start - DIL_BLOCK * dil
                    valid = band
                stride = dil if dil > 1 else None
                cur = pl.ds(start, DIL_BLOCK, stride=stride)
                prev = pl.ds(pstart, DIL_BLOCK, stride=stride)
                qb = q_ref[cur, :]
                kb = jnp.concatenate([k_ref[prev, :], k_ref[cur, :]], axis=0).astype(BF16)
                vb = jnp.concatenate([v_ref[prev, :], v_ref[cur, :]], axis=0).astype(BF16)
                o = jnp.zeros((DIL_BLOCK, LANE), F32)
                ls = jnp.zeros((DIL_BLOCK, LANE), F32)
                for hh in range(LANE // HEAD_DIM):
                    mh = head == hh
                    s = _dot_nt(jnp.where(mh, qb, 0.0).astype(BF16), kb) * scale
                    s = jnp.where(valid, s, NEG_INF)
                    m = jnp.max(s, axis=-1, keepdims=True)
                    p = jnp.exp(s - m)
                    l = jnp.sum(p, axis=-1, keepdims=True)
                    o = jnp.where(mh, _dot(p.astype(BF16), vb) / l, o)
                    ls = jnp.where(mh, m + jnp.log(l), ls)
                dst = pl.ds(off, DIL_BLOCK, stride=stride)
                o_scr[ci][dst, :] = o
                l_scr[ci][dst, :] = ls
                return carry

            lax.fori_loop(0, SWA_SB // DIL_BLOCK, block, 0, unroll=SWA_UNROLL)

        def combine(t, carry, sb=sb):
            rows = pl.ds(t * TM, TM)
            la, lb, lc = l_scr[0][rows, :], l_scr[1][rows, :], l_scr[2][rows, :]
            mx = jnp.maximum(jnp.maximum(la, lb), lc)
            ea, eb, ec = jnp.exp(la - mx), jnp.exp(lb - mx), jnp.exp(lc - mx)
            num = ea * o_scr[0][rows, :] + eb * o_scr[1][rows, :] + ec * o_scr[2][rows, :]
            y_ref[pl.ds(sb * SWA_SB + t * TM, TM), :] = num / (ea + eb + ec)
            return carry

        lax.fori_loop(0, SWA_SB // TM, combine, 0)


def _swa_prompt(h, mix, *, seq, n_seq):
    halves = MIX // LANE

    def hspec(col):
        return pl.BlockSpec((seq, LANE), lambda b, p: (b, col * halves + p))

    nbytes = 2 * 4 * seq * LANE * 4 + 6 * SWA_SB * LANE * 4
    return pl.pallas_call(
        functools.partial(_swa_prompt_body, seq=seq),
        grid=(n_seq, halves),
        in_specs=[hspec(COL_SQ), hspec(COL_SK), hspec(COL_SV), pl.BlockSpec(memory_space=pl.ANY)],
        out_specs=pl.BlockSpec((seq, LANE), lambda b, p: (b, halves + p)),
        out_shape=jax.ShapeDtypeStruct(mix.shape, F32),
        scratch_shapes=[pltpu.VMEM((SWA_SB, LANE), F32)] * 6,
        input_output_aliases={3: 0},
        compiler_params=_params(("arbitrary", "arbitrary"), nbytes),
        name="swa_prompt",
    )(h, h, h, mix)


def _swa_sample_body(q_ref, k_ref, v_ref, ck_ref, cv_ref, mix_ref, y_ref, *, past, rows):
    del mix_ref
    q = q_ref[...]
    head = _lane_head(MIX)
    qm = jnp.concatenate([jnp.where(head == h, q, 0.0) for h in range(N_HEADS)], axis=0).astype(BF16)
    ck = ck_ref[0, 0].astype(BF16)
    cv = cv_ref[0, 0].astype(BF16)
    zpad = jnp.zeros((LANE - rows, MIX), F32)
    kn = jnp.concatenate([k_ref[...], zpad], axis=0).astype(BF16)
    vn = jnp.concatenate([v_ref[...], zpad], axis=0).astype(BF16)
    scale = HEAD_DIM ** -0.5
    s_c = _dot(qm, ck) * scale
    s_n = _dot_nt(qm, kn) * scale
    t_row = lax.broadcasted_iota(I32, (N_HEADS * rows, 1), 0) % rows
    rel_c = past + t_row - lax.broadcasted_iota(I32, (1, past), 1)
    rel_n = t_row - lax.broadcasted_iota(I32, (1, LANE), 1)
    outs, lses = [], []
    for window, dil in DILATED_CONFIGS:
        span = window // dil
        shift = int(math.log2(dil))

        def valid(rel, dil=dil, shift=shift, span=span):
            return (rel >= 0) & ((rel & (dil - 1)) == 0) & ((rel >> shift) <= span)

        sc = jnp.where(valid(rel_c), s_c, NEG_INF)
        sn = jnp.where(valid(rel_n), s_n, NEG_INF)
        m = jnp.maximum(jnp.max(sc, axis=-1, keepdims=True), jnp.max(sn, axis=-1, keepdims=True))
        pc = jnp.exp(sc - m)
        pn = jnp.exp(sn - m)
        l = jnp.sum(pc, axis=-1, keepdims=True) + jnp.sum(pn, axis=-1, keepdims=True)
        outs.append((_dot_nt(pc.astype(BF16), cv) + _dot(pn.astype(BF16), vn)) / l)
        lses.append(m + jnp.log(l))
    mx = jnp.maximum(jnp.maximum(lses[0], lses[1]), lses[2])
    es = [jnp.exp(ls - mx) for ls in lses]
    y_all = (es[0] * outs[0] + es[1] * outs[1] + es[2] * outs[2]) / (es[0] + es[1] + es[2])
    y = jnp.zeros((rows, MIX), F32)
    for h in range(N_HEADS):
        y = jnp.where(head == h, y_all[h * rows:(h + 1) * rows, :], y)
    y_ref[...] = y


def _swa_sample(h, mix, cache_k, cache_v, layer, *, rows, n_seq, row0):
    past = cache_k.shape[3]
    rb0 = row0 // rows

    def hspec(col):
        return pl.BlockSpec((rows, MIX), lambda b: (rb0 + b, col))

    cspec = pl.BlockSpec((1, 1, MIX, past), lambda b: (layer, b, 0, 0))
    nbytes = 2 * (2 * past * MIX * 4 + 4 * rows * MIX * 4) + 12 * N_HEADS * rows * past * 4
    return pl.pallas_call(
        functools.partial(_swa_sample_body, past=past, rows=rows),
        grid=(n_seq,),
        in_specs=[hspec(COL_SQ), hspec(COL_SK), hspec(COL_SV), cspec, cspec,
                  pl.BlockSpec(memory_space=pl.ANY)],
        out_specs=pl.BlockSpec((rows, MIX), lambda b: (rb0 + b, 1)),
        out_shape=jax.ShapeDtypeStruct(mix.shape, F32),
        input_output_aliases={5: 0},
        compiler_params=_params(("arbitrary",), nbytes),
        name="swa_sample",
    )(h, h, h, cache_k, cache_v, mix)


def _conv_body(val_ref, gate_ref, buf_ref, w_ref, cb_ref, g_ref, b_ref, mix_ref,
               y_ref, glu_ref, ctx, *, rows, n_tiles):
    del mix_ref
    t = pl.program_id(1)

    @pl.when(t == 0)
    def _():
        ctx[0:CONV_CTX, :] = buf_ref[0]

    glu = val_ref[...] * _sigmoid(gate_ref[...])
    glu_ref[...] = glu
    ctx[CONV_CTX:CONV_CTX + rows, :] = glu
    lead = CONV_CTX - (CONV_WIDTH - 1)
    acc = jnp.zeros((rows, MIX), F32)
    for j in range(CONV_WIDTH):
        acc = acc + ctx[pl.ds(lead + j, rows), :] * w_ref[j:j + 1, :]
    z = _layer_norm(acc + cb_ref[...], g_ref[...], b_ref[...])
    y_ref[...] = z * _sigmoid(z)
    if n_tiles > 1:
        ctx[0:CONV_CTX, :] = ctx[rows:rows + CONV_CTX, :]


def _conv(h, mix, buf, w, cb, g, b, *, rows, n_seq, n_tiles, row0):
    rb0 = row0 // rows
    rmap = lambda col: (lambda s, t: (rb0 + s * n_tiles + t, col))
    const = lambda shape: pl.BlockSpec(shape, lambda s, t: (0, 0))
    nbytes = 2 * 4 * rows * MIX * 4 + (CONV_CTX + rows) * MIX * 4 + 8 * rows * MIX * 4
    return pl.pallas_call(
        functools.partial(_conv_body, rows=rows, n_tiles=n_tiles),
        grid=(n_seq, n_tiles),
        in_specs=[pl.BlockSpec((rows, MIX), rmap(COL_CV)), pl.BlockSpec((rows, MIX), rmap(COL_CG)),
                  pl.BlockSpec((1, CONV_CTX, MIX), lambda s, t: (s, 0, 0)),
                  const((CONV_CTX, MIX)), const((1, MIX)), const((1, MIX)), const((1, MIX)),
                  pl.BlockSpec(memory_space=pl.ANY)],
        out_specs=[pl.BlockSpec((rows, MIX), rmap(2)),
                   pl.BlockSpec((rows, MIX), lambda s, t: (s * n_tiles + t, 0))],
        out_shape=[jax.ShapeDtypeStruct(mix.shape, F32),
                   jax.ShapeDtypeStruct((n_seq * n_tiles * rows, MIX), F32)],
        scratch_shapes=[pltpu.VMEM((CONV_CTX + rows, MIX), F32)],
        input_output_aliases={7: 0},
        compiler_params=_params(("arbitrary", "arbitrary"), nbytes),
        name="conv",
    )(h, h, buf, w, cb, g, b, mix)


def _gelu_tanh(x):
    c = math.sqrt(2.0 / math.pi)
    return x * (0.5 * (1.0 + jnp.tanh(c * (x + 0.044715 * (x * x * x)))))


def _sgu_body(u_ref, v_ref, g_ref, b_ref, ws_ref, bias_ref, avg_ref, mix_ref, y_ref, *vn_ref, rows, sub):
    del mix_ref
    tri = (lax.broadcasted_iota(I32, (SGU_CHUNK, SGU_CHUNK), 0)
           >= lax.broadcasted_iota(I32, (SGU_CHUNK, SGU_CHUNK), 1))
    wms = [jnp.where(tri, ws_ref[gi], 0.0).astype(BF16) for gi in range(N_HEADS)]
    head = _lane_head(MIX)
    for c in range(rows // sub):
        rs = slice(c * sub, (c + 1) * sub)
        u = _gelu_tanh(u_ref[rs, :])
        v = _gelu_tanh(v_ref[rs, :])
        vn = _group_norm(v, avg_ref[...]) * g_ref[...] + b_ref[...]
        if vn_ref:
            vn_ref[0][rs, :] = vn
        if sub < SGU_CHUNK:
            vn = jnp.concatenate([vn, jnp.zeros((SGU_CHUNK - sub, MIX), F32)], axis=0)
        vnb = vn.astype(BF16)
        z = jnp.zeros((SGU_CHUNK, MIX), F32)
        for gi in range(N_HEADS):
            z = jnp.where(head == gi, _dot(wms[gi], vnb), z)
        z = z + bias_ref[...]
        y_ref[rs, :] = u * z[0:sub, :]


def _sgu(h, mix, g, b, ws, bias, avg, *, rows, sub, n_steps, row0, emit_vn):
    rb0 = row0 // rows
    rmap = lambda col: (lambda c: (rb0 + c, col))
    const = lambda shape: pl.BlockSpec(shape, lambda c: (0,) * len(shape))
    nbytes = 2 * (4 * rows * MIX * 4 + N_HEADS * SGU_CHUNK * SGU_CHUNK * 4 + SGU_CHUNK * MIX * 4) + 8 * SGU_CHUNK * MIX * 4
    out_specs = [pl.BlockSpec((rows, MIX), rmap(3))]
    out_shape = [jax.ShapeDtypeStruct(mix.shape, F32)]
    if emit_vn:
        out_specs.append(pl.BlockSpec((rows, MIX), lambda c: (c, 0)))
        out_shape.append(jax.ShapeDtypeStruct((n_steps * rows, MIX), F32))
    return pl.pallas_call(
        functools.partial(_sgu_body, rows=rows, sub=sub),
        grid=(n_steps,),
        in_specs=[pl.BlockSpec((rows, MIX), rmap(COL_GU)), pl.BlockSpec((rows, MIX), rmap(COL_GV)),
                  const((1, MIX)), const((1, MIX)), const((N_HEADS, SGU_CHUNK, SGU_CHUNK)),
                  const((SGU_CHUNK, MIX)), const((MIX, MIX)), pl.BlockSpec(memory_space=pl.ANY)],
        out_specs=out_specs,
        out_shape=out_shape,
        input_output_aliases={7: 0},
        compiler_params=_params(("arbitrary",), nbytes),
        name="sgu",
    )(h, h, g, b, ws, bias, avg, mix)


def _out_router_body(mix_ref, w_ref, x_ref, g_ref, b_ref, wr_ref, br_ref,
                     x1_ref, route_ref, counts_ref, wbf, wrbf, logit_scr, both_scr, *, alpha):
    @pl.when(pl.program_id(0) == 0)
    def _():
        wbf[...] = w_ref[0].astype(BF16)
        wrbf[...] = wr_ref[...].astype(BF16)

    y = _dot(mix_ref[...].astype(BF16), wbf[...])
    x1 = _layer_norm(alpha * x_ref[...] + y, g_ref[...], b_ref[...])
    x1_ref[...] = x1
    logit_scr[...] = _dot(x1.astype(BF16), wrbf[...]) + br_ref[...]
    lane = lax.broadcasted_iota(I32, (1, LANE), 1).astype(F32)

    def first_max(mask, vals):
        masked = jnp.where(mask, vals, NEG_INF)
        top = jnp.max(masked, axis=-1, keepdims=True)
        idx = jnp.min(jnp.where(mask & (masked == top), lane, float(LANE)), axis=-1, keepdims=True)
        return top, idx

    def lanes(values, base):
        for ln, val in values:
            base = jnp.where(lane == ln, val, base)
        return base

    gmask = lane < N_GROUPS
    for sb in range(TM // ROUTE_SUB):
        rs = slice(sb * ROUTE_SUB, (sb + 1) * ROUTE_SUB)
        logits = logit_scr[rs, :]
        gmax, gsel = first_max(gmask, logits)
        g_gate = 1.0 / jnp.sum(jnp.where(gmask, jnp.exp(logits - gmax), 0.0), axis=-1, keepdims=True)
        lo = N_GROUPS + EXPERTS_PER_GROUP * gsel
        emask = (lane >= lo) & (lane < lo + EXPERTS_PER_GROUP)
        v1, i1 = first_max(emask, logits)
        v2, i2 = first_max(emask & (lane != i1), logits)
        e12 = jnp.exp(v2 - v1)
        e_a = i1 - N_GROUPS
        e_b = i2 - N_GROUPS
        both_scr[rs, :] = (lane == e_a).astype(F32) + (lane == e_b).astype(F32)
        route_ref[rs, :] = lanes(((0, e_a), (1, e_b), (2, g_gate * (1.0 / (1.0 + e12))),
                                  (3, g_gate * (e12 / (1.0 + e12)))), jnp.zeros((ROUTE_SUB, LANE), F32))

    both = both_scr[...]
    counts_ref[0] = jnp.sum(both, axis=0, keepdims=True)
    strict = (lax.broadcasted_iota(I32, (TM, TM), 0) > lax.broadcasted_iota(I32, (TM, TM), 1))
    both_scr[...] = _dot(strict.astype(BF16), both.astype(BF16))
    for sb in range(TM // ROUTE_SUB):
        rs = slice(sb * ROUTE_SUB, (sb + 1) * ROUTE_SUB)
        part = route_ref[rs, :]
        before = both_scr[rs, :]
        rank_a = jnp.sum(jnp.where(lane == part[:, 0:1], before, 0.0), axis=-1, keepdims=True)
        rank_b = jnp.sum(jnp.where(lane == part[:, 1:2], before, 0.0), axis=-1, keepdims=True)
        route_ref[rs, :] = lanes(((4, rank_a), (5, rank_b)), part)


def _out_router(mix, w_out, layer, x, g, b, wr, br, *, alpha):
    nt, d = x.shape
    kdim = mix.shape[1]
    const = lambda shape: pl.BlockSpec(shape, lambda i: (0, 0))
    row = lambda width: pl.BlockSpec((TM, width), lambda i: (i, 0))
    nbytes = 2 * (kdim * d * 4 + d * LANE * 4 + TM * kdim * 4 + 2 * TM * d * 4 + TM * LANE * 4) + kdim * d * 2
    return pl.pallas_call(
        functools.partial(_out_router_body, alpha=alpha),
        grid=(nt // TM,),
        in_specs=[row(kdim), pl.BlockSpec((1, kdim, d), lambda i: (layer, 0, 0)), row(d),
                  const((1, d)), const((1, d)), const((d, LANE)), const((1, LANE))],
        out_specs=[row(d), row(LANE), pl.BlockSpec((1, 1, LANE), lambda i: (i, 0, 0))],
        out_shape=[jax.ShapeDtypeStruct((nt, d), F32), jax.ShapeDtypeStruct((nt, LANE), F32),
                   jax.ShapeDtypeStruct((nt // TM, 1, LANE), F32)],
        scratch_shapes=[pltpu.VMEM((kdim, d), BF16), pltpu.VMEM((d, LANE), BF16),
                        pltpu.VMEM((TM, LANE), F32), pltpu.VMEM((TM, LANE), F32)],
        compiler_params=_params(("arbitrary",), nbytes),
        name="out_router",
    )(mix, w_out, x, g, b, wr, br)


def _local_slots(route, off_row):
    lane = lax.broadcasted_iota(I32, (1, LANE), 1).astype(F32)
    slots = []
    for kk in range(TOP_K):
        onehot = (lane == route[:, kk:kk + 1]).astype(F32)
        slots.append(jnp.sum(onehot * off_row, axis=-1, keepdims=True) + route[:, 4 + kk:5 + kk])
    return slots


def _chunk_copy(hbm, hbm_row, loc, chunk, sem, *, to_hbm):
    vm = loc.at[pl.ds(pl.multiple_of(chunk * MOE_CHUNK, MOE_CHUNK), MOE_CHUNK), :]
    hb = hbm.at[pl.ds(pl.multiple_of(hbm_row, MOE_CHUNK), MOE_CHUNK), :]
    return pltpu.make_async_copy(vm, hb, sem) if to_hbm else pltpu.make_async_copy(hb, vm, sem)


def _wait_chunks(hbm, buf, sem, count, *, to_hbm):
    def wait(c, carry):
        _chunk_copy(hbm, 0, buf, 0, sem, to_hbm=to_hbm).wait()
        return carry

    lax.fori_loop(0, count, wait, 0)


def _dispatch_body(dst_tab, n_chunks, x_ref, route_ref, off_ref, xs_in, xs_hbm, loc, sem):
    del xs_in
    i = pl.program_id(0)
    last = pl.num_programs(0) - 1
    cur = i % 2
    slot_a, slot_b = _local_slots(route_ref[...], off_ref[0])
    col = lax.broadcasted_iota(I32, (1, LOC_ROWS), 1).astype(F32)
    onehot = ((col == slot_a) | (col == slot_b)).astype(BF16)
    loc[cur] = _dot_tn(onehot, x_ref[...].astype(BF16))

    def start(c, carry):
        _chunk_copy(xs_hbm, dst_tab[i * MAX_CHUNKS + c], loc.at[cur], c, sem.at[cur], to_hbm=True).start()
        return carry

    lax.fori_loop(0, n_chunks[i], start, 0)

    @pl.when(i > 0)
    def _():
        _wait_chunks(xs_hbm, loc.at[1 - cur], sem.at[1 - cur], n_chunks[jnp.maximum(i - 1, 0)], to_hbm=True)

    @pl.when(i == last)
    def _():
        _wait_chunks(xs_hbm, loc.at[cur], sem.at[cur], n_chunks[i], to_hbm=True)


def _dispatch(dst_tab, n_chunks, x1, route, off, xs_prev):
    nt, d = x1.shape
    row = lambda width: pl.BlockSpec((TM, width), lambda i, *_: (i, 0))
    nbytes = 2 * (TM * d * 4 + TM * LANE * 4) + 2 * LOC_ROWS * d * 4 + 4 * TM * LOC_ROWS * 4 + LOC_ROWS * d * 4
    return pl.pallas_call(
        _dispatch_body,
        grid_spec=pltpu.PrefetchScalarGridSpec(
            num_scalar_prefetch=2,
            grid=(nt // TM,),
            in_specs=[row(d), row(LANE), pl.BlockSpec((1, 1, LANE), lambda i, *_: (i, 0, 0)),
                      pl.BlockSpec(memory_space=pl.ANY)],
            out_specs=pl.BlockSpec(memory_space=pl.ANY),
            scratch_shapes=[pltpu.VMEM((2, LOC_ROWS, d), F32), pltpu.SemaphoreType.DMA((2,))]),
        out_shape=jax.ShapeDtypeStruct(xs_prev.shape, F32),
        input_output_aliases={5: 0},
        compiler_params=_params(("arbitrary",), nbytes),
        name="moe_dispatch",
    )(dst_tab, n_chunks, x1, route, off, xs_prev)


def _ffn_body(blk_e, blk_valid, blk_src, x_ref, w1_ref, w3_ref, w2_ref, ys_in, y_ref, w1b, w3b, w2b):
    del blk_src, ys_in
    j = pl.program_id(0)

    @pl.when(blk_valid[j] > 0)
    def _():
        @pl.when((j == 0) | (blk_e[j] != blk_e[jnp.maximum(j - 1, 0)]))
        def _():
            w1b[...] = w1_ref[0, 0].astype(BF16)
            w3b[...] = w3_ref[0, 0].astype(BF16)
            w2b[...] = w2_ref[0, 0].astype(BF16)

        xb = x_ref[...].astype(BF16)
        h1 = _dot(xb, w1b[...])
        h3 = _dot(xb, w3b[...])
        act = (h1 * _sigmoid(h1)) * h3
        y_ref[...] = _dot(act.astype(BF16), w2b[...])


def _ffn(xs, w1, w3, w2, layer, blk_e, blk_valid, blk_src, ys_prev):
    d = xs.shape[1]
    de = w1.shape[3]
    n_blk = xs.shape[0] // MOE_BLK
    wmap = lambda j, be, bv, bs: (layer, be[j], 0, 0)
    nbytes = 2 * (3 * d * de * 4 + 2 * MOE_BLK * d * 4) + 3 * d * de * 2 + 6 * MOE_BLK * de * 4
    return pl.pallas_call(
        _ffn_body,
        grid_spec=pltpu.PrefetchScalarGridSpec(
            num_scalar_prefetch=3,
            grid=(n_blk,),
            in_specs=[pl.BlockSpec((MOE_BLK, d), lambda j, be, bv, bs: (bs[j], 0)),
                      pl.BlockSpec((1, 1, d, de), wmap), pl.BlockSpec((1, 1, d, de), wmap),
                      pl.BlockSpec((1, 1, de, d), wmap), pl.BlockSpec(memory_space=pl.ANY)],
            out_specs=pl.BlockSpec((MOE_BLK, d), lambda j, be, bv, bs: (bs[j], 0)),
            scratch_shapes=[pltpu.VMEM((d, de), BF16), pltpu.VMEM((d, de), BF16), pltpu.VMEM((de, d), BF16)]),
        out_shape=jax.ShapeDtypeStruct(xs.shape, F32),
        input_output_aliases={7: 0},
        compiler_params=_params(("arbitrary",), nbytes),
        name="expert_ffn",
    )(blk_e, blk_valid, blk_src, xs, w1, w3, w2, ys_prev)


def _combine_body(dst_tab, n_chunks, x_ref, route_ref, off_ref, g_ref, b_ref, ys_hbm, x2_ref,
                  loc, sem, *, alpha):
    i = pl.program_id(0)
    last = pl.num_programs(0) - 1
    cur = i % 2

    def fetch(tile, buf):
        def start(c, carry):
            _chunk_copy(ys_hbm, dst_tab[tile * MAX_CHUNKS + c], loc.at[buf], c, sem.at[buf], to_hbm=False).start()
            return carry

        lax.fori_loop(0, n_chunks[tile], start, 0)

    @pl.when(i == 0)
    def _():
        loc[...] = jnp.zeros_like(loc)
        fetch(0, 0)

    @pl.when(i < last)
    def _():
        fetch(jnp.minimum(i + 1, last), 1 - cur)

    route = route_ref[...]
    slot_a, slot_b = _local_slots(route, off_ref[0])
    col = lax.broadcasted_iota(I32, (1, LOC_ROWS), 1).astype(F32)
    gates = jnp.where(col == slot_a, route[:, 2:3], 0.0) + jnp.where(col == slot_b, route[:, 3:4], 0.0)
    _wait_chunks(ys_hbm, loc.at[cur], sem.at[cur], n_chunks[i], to_hbm=False)
    f = _dot(gates.astype(BF16), loc[cur].astype(BF16))
    x2_ref[...] = _layer_norm(alpha * x_ref[...] + f, g_ref[...], b_ref[...])


def _combine(dst_tab, n_chunks, x1, route, off, g, b, ys, *, alpha):
    nt, d = x1.shape
    row = lambda width: pl.BlockSpec((TM, width), lambda i, dt, nc: (i, 0))
    const = lambda shape: pl.BlockSpec(shape, lambda i, dt, nc: (0, 0))
    nbytes = 2 * (2 * TM * d * 4 + TM * LANE * 4) + LOC_ROWS * d * 4 + 4 * TM * LOC_ROWS * 4 + LOC_ROWS * d * 2
    return pl.pallas_call(
        functools.partial(_combine_body, alpha=alpha),
        grid_spec=pltpu.PrefetchScalarGridSpec(
            num_scalar_prefetch=2,
            grid=(nt // TM,),
            in_specs=[row(d), row(LANE), pl.BlockSpec((1, 1, LANE), lambda i, dt, nc: (i, 0, 0)),
                      const((1, d)), const((1, d)), pl.BlockSpec(memory_space=pl.ANY)],
            out_specs=row(d),
            scratch_shapes=[pltpu.VMEM((2, LOC_ROWS, d), F32), pltpu.SemaphoreType.DMA((2,))]),
        out_shape=jax.ShapeDtypeStruct((nt, d), F32),
        compiler_params=_params(("arbitrary",), nbytes + LOC_ROWS * d * 4),
        name="combine_ln",
    )(dst_tab, n_chunks, x1, route, off, g, b, ys)


def _rope_tables(pos, rot_dim, theta):
    half = rot_dim // 2
    inv = jnp.float32(theta) ** (-jnp.arange(half, dtype=F32) * (2.0 / rot_dim))
    ang = pos.astype(F32)[:, None] * inv[None, :]
    c, s = jnp.cos(ang), jnp.sin(ang)
    rest = HEAD_DIM - rot_dim
    ones = jnp.ones((pos.shape[0], rest), F32)
    zeros = jnp.zeros((pos.shape[0], rest), F32)
    cos_h = jnp.concatenate([c, c, ones], axis=1)
    sin_h = jnp.concatenate([-s, s, zeros], axis=1)
    return jnp.tile(cos_h, (1, N_HEADS)), jnp.tile(sin_h, (1, N_HEADS))


def _block_diag(s):
    n = s.shape[0]
    eye = jnp.eye(N_HEADS, dtype=s.dtype)
    return (s[:, :, :, None, :] * eye[None, :, None, :, None]).reshape(n, MIX, MIX)


def _diag_blocks(s):
    return jnp.stack([s[:, h * HEAD_DIM:(h + 1) * HEAD_DIM, h * HEAD_DIM:(h + 1) * HEAD_DIM]
                      for h in range(N_HEADS)], axis=1)


def _routing_plan(tile_counts, n_blk):
    cnt_t = tile_counts[:, 0, :N_EXPERTS].astype(I32)
    run_chunks = (cnt_t + MOE_CHUNK - 1) // MOE_CHUNK
    run_rows = run_chunks * MOE_CHUNK
    cnt = jnp.sum(run_rows, axis=0)
    pcnt = (cnt + MOE_BLK - 1) // MOE_BLK * MOE_BLK
    pstart = jnp.cumsum(pcnt) - pcnt
    n_act = jnp.sum(pcnt) // MOE_BLK
    gstart = pstart[None, :] + jnp.cumsum(run_rows, axis=0) - run_rows
    chunk0 = jnp.cumsum(run_chunks, axis=1) - run_chunks
    n_chunks = jnp.sum(run_chunks, axis=1).astype(I32)

    def spread(first, count, value, n):
        p = jnp.arange(n, dtype=I32)[:, None]
        f, c, v = first[..., None, :], count[..., None, :], value[..., None, :]
        return jnp.sum(jnp.where((f <= p) & (p < f + c), v + (p - f) * MOE_CHUNK, 0), axis=-1)

    dst_tab = spread(chunk0, run_chunks, gstart, MAX_CHUNKS)
    off = jnp.pad((chunk0 * MOE_CHUNK).astype(F32), ((0, 0), (0, LANE - N_EXPERTS)))[:, None, :]
    pos = jnp.arange(n_blk, dtype=I32)[:, None] * MOE_BLK
    mine = (pstart[None, :] <= pos) & (pos < (pstart + pcnt)[None, :])
    experts = jnp.arange(N_EXPERTS, dtype=I32)[None, :]
    last_e = jnp.max(jnp.where(cnt > 0, experts[0], 0))
    active = pos[:, 0] < n_act * MOE_BLK
    blk_e = jnp.where(active, jnp.sum(jnp.where(mine, experts, 0), axis=-1), last_e)
    blk_valid = jnp.sum(jnp.where(mine, jnp.clip((pstart + cnt)[None, :] - pos, 0, MOE_BLK), 0), axis=-1)
    blk_src = jnp.minimum(jnp.arange(n_blk, dtype=I32), jnp.maximum(n_act - 1, 0))
    i32 = lambda t: t.astype(I32)
    return i32(dst_tab).reshape(-1), n_chunks, off, i32(blk_e), i32(blk_valid), i32(blk_src)


def kernel(x_prompt, x_sample, state_ret, cache_swa_k, cache_swa_v, state_conv, w_in, w_out, ret_gn_g,
           conv_w, conv_b, conv_ln_g, conv_ln_b, sgu_ln_g, sgu_ln_b, sgu_w, sgu_b, ln1_g, ln1_b,
           ln2_g, ln2_b, router_g_w, router_g_b, router_e_w, router_e_b, moe_w1, moe_w3, moe_w2):
    depth = w_in.shape[0]
    n_seq, seq, d = x_prompt.shape
    n_dec, t_new, _ = x_sample.shape
    past = cache_swa_k.shape[2]
    rows_s = -(-t_new // SUBLANE) * SUBLANE
    n_p = n_seq * seq
    n_s = n_dec * rows_s
    nt = n_p + n_s
    assert seq % SWA_SB == 0 and seq % CONV_TT == 0 and seq % TM == 0 and n_s % TM == 0
    assert t_new % RET_CHUNK != 0 and t_new <= SGU_CHUNK and w_in.shape[2] == N_COLBLK * MIX
    assert seq % SGU_ROWS == 0 and seq % RET_ROWS == 0 and n_dec % SAMPLE_SEQS == 0
    assert n_p % (SAMPLE_SEQS * rows_s) == 0
    alpha = (2 * depth) ** 0.25
    keep = min(DILATED_CONFIGS[-1][0], seq)
    n_blk = -(-(nt * TOP_K + (nt // TM) * N_EXPERTS * (MOE_CHUNK - 1) + N_EXPERTS * (MOE_BLK - 1)) // MOE_BLK)
    cache_k = jnp.transpose(cache_swa_k, (0, 1, 3, 4, 2)).reshape(depth, n_dec, MIX, past)
    cache_v = jnp.transpose(cache_swa_v, (0, 1, 3, 4, 2)).reshape(depth, n_dec, MIX, past)

    x = jnp.concatenate([x_prompt.reshape(n_p, d),
                         jnp.pad(x_sample, ((0, 0), (0, rows_s - t_new), (0, 0))).reshape(n_s, d)], axis=0)
    pos = jnp.concatenate([jnp.arange(seq, dtype=I32),
                           PAST_LEN + jnp.arange(n_s, dtype=I32) % rows_s])
    rope = _rope_tables(pos, HEAD_DIM, RET_THETA) + _rope_tables(pos, ROPE_DIM, ROPE_THETA)
    ret_tabs_p = _retention_tables(RET_CHUNK, RET_CHUNK)
    ret_tabs_s = _retention_tables(t_new, rows_s)
    lane_head = jnp.arange(MIX) // HEAD_DIM
    avg = ((lane_head[:, None] == lane_head[None, :]).astype(F32) / HEAD_DIM).astype(BF16)
    zero_state = jnp.zeros((n_seq, MIX, MIX), F32)
    zero_ctx = jnp.zeros((n_seq, CONV_CTX, MIX), F32)
    row2 = lambda v: v.reshape(1, -1)
    mix = jnp.zeros((nt, 4 * MIX), F32)
    xs = jnp.zeros((n_blk * MOE_BLK, d), F32)
    ys = jnp.zeros((n_blk * MOE_BLK, d), F32)

    outs = [[] for _ in range(9)]
    for l in range(depth):
        h = _in_proj(x, w_in, l, rope, n_p // TM, seq // TM)
        gn = row2(ret_gn_g[l])
        mix, s_p = _retention(h, mix, zero_state, ret_tabs_p, avg, gn, rows=RET_CHUNK, n_seq=n_seq,
                              n_chunks=seq // RET_CHUNK, seqs=1, chunks=RET_ROWS // RET_CHUNK, row0=0)
        mix, s_s = _retention(h, mix, _block_diag(state_ret[l]), ret_tabs_s, avg, gn, rows=rows_s, n_seq=n_dec,
                              n_chunks=1, seqs=SAMPLE_SEQS, chunks=1, row0=n_p)
        mix = _swa_prompt(h, mix, seq=seq, n_seq=n_seq)
        mix = _swa_sample(h, mix, cache_k, cache_v, l, rows=rows_s, n_seq=n_dec, row0=n_p)
        cw = jnp.pad(conv_w[l], ((0, CONV_CTX - CONV_WIDTH), (0, 0)))
        cargs = (cw, row2(conv_b[l]), row2(conv_ln_g[l]), row2(conv_ln_b[l]))
        mix, glu_p = _conv(h, mix, zero_ctx, *cargs, rows=CONV_TT, n_seq=n_seq, n_tiles=seq // CONV_TT, row0=0)
        ctx_s = jnp.pad(state_conv[l], ((0, 0), (CONV_CTX - (CONV_WIDTH - 1), 0), (0, 0)))
        mix, glu_s = _conv(h, mix, ctx_s, *cargs, rows=rows_s, n_seq=n_dec, n_tiles=1, row0=n_p)
        sbias = jnp.repeat(sgu_b[l].T, HEAD_DIM, axis=1)
        sargs = (row2(sgu_ln_g[l]), row2(sgu_ln_b[l]), sgu_w[l], sbias, avg)
        mix, = _sgu(h, mix, *sargs, rows=SGU_ROWS, sub=SGU_CHUNK, n_steps=n_p // SGU_ROWS, row0=0, emit_vn=False)
        mix, vn_s = _sgu(h, mix, *sargs, rows=SAMPLE_SEQS * rows_s, sub=rows_s, n_steps=n_dec // SAMPLE_SEQS,
                         row0=n_p, emit_vn=True)

        wr = jnp.concatenate([router_g_w[l], jnp.transpose(router_e_w[l], (1, 0, 2)).reshape(d, N_EXPERTS),
                              jnp.zeros((d, LANE - N_GROUPS - N_EXPERTS), F32)], axis=1)
        br = jnp.concatenate([router_g_b[l], router_e_b[l].reshape(-1),
                              jnp.zeros((LANE - N_GROUPS - N_EXPERTS,), F32)]).reshape(1, LANE)
        x1, route, counts = _out_router(mix, w_out, l, x, row2(ln1_g[l]), row2(ln1_b[l]), wr, br, alpha=alpha)
        dst_tab, n_chunks, off, blk_e, blk_valid, blk_src = _routing_plan(counts, n_blk)
        xs = _dispatch(dst_tab, n_chunks, x1, route, off, xs)
        ys = _ffn(xs, moe_w1, moe_w3, moe_w2, l, blk_e, blk_valid, blk_src, ys)
        x = _combine(dst_tab, n_chunks, x1, route, off, row2(ln2_g[l]), row2(ln2_b[l]), ys, alpha=alpha)

        def tail(col, n_keep):
            return jnp.stack([lax.slice(h, ((b + 1) * seq - n_keep, col * MIX), ((b + 1) * seq, (col + 1) * MIX))
                              for b in range(n_seq)])

        def new_rows(col):
            blk = lax.slice(h, (n_p, col * MIX), (nt, (col + 1) * MIX))
            return blk.reshape(n_dec, rows_s, MIX)[:, :t_new]

        heads = lambda t: t.reshape(t.shape[0], t.shape[1], N_HEADS, HEAD_DIM)
        outs[0].append(_diag_blocks(s_p))
        outs[1].append(heads(tail(COL_SK, keep)))
        outs[2].append(heads(tail(COL_SV, keep)))
        outs[3].append(glu_p.reshape(n_seq, seq, MIX)[:, seq - (CONV_WIDTH - 1):])
        outs[4].append(_diag_blocks(s_s))
        outs[5].append(heads(new_rows(COL_SK)))
        outs[6].append(heads(new_rows(COL_SV)))
        outs[7].append(glu_s.reshape(n_dec, rows_s, MIX)[:, :t_new])
        outs[8].append(vn_s.reshape(n_dec, rows_s, MIX)[:, :t_new])

    y_prompt = x[:n_p].reshape(n_seq, seq, d)
    y_sample = x[n_p:].reshape(n_dec, rows_s, d)[:, :t_new]
    return (y_prompt, y_sample) + tuple(jnp.stack(o) for o in outs)
```

```python
import functools
import math

import jax
import jax.numpy as jnp
from jax import lax
from jax.experimental import pallas as pl
from jax.experimental.pallas import tpu as pltpu

F32 = jnp.float32
BF16 = jnp.bfloat16
I32 = jnp.int32

HEAD_DIM = 64
N_HEADS = 4
MIX = N_HEADS * HEAD_DIM
(COL_RQ, COL_RK, COL_RV, COL_RG, COL_SQ, COL_SK, COL_SV,
 COL_CV, COL_CG, COL_GU, COL_GV) = range(11)
N_COLBLK = 11
RET_CHUNK = 128
RET_THETA = 10000.0
ROPE_THETA = 500000.0
ROPE_DIM = HEAD_DIM // 4
DILATED_CONFIGS = ((128, 1), (512, 4), (2048, 16))
DIL_BLOCK = 128
CONV_WIDTH = 31
SGU_CHUNK = 128
N_GROUPS = 4
EXPERTS_PER_GROUP = 8
N_EXPERTS = N_GROUPS * EXPERTS_PER_GROUP
TOP_K = 2
PAST_LEN = 8192
LN_EPS = 1e-5
NEG_INF = -1e30

LANE = 128
SUBLANE = 8
VMEM_CAP_BYTES = 60000 * 1024
COMPILER_TEMP_BYTES = 12 * 1024 * 1024

TM = 512
TD = 256
ROUTE_SUB = 64
MOE_BLK = 512
MOE_CHUNK = SUBLANE
LOC_ROWS = -(-(TD * TOP_K + N_EXPERTS * (MOE_CHUNK - 1)) // LANE) * LANE
MAX_CHUNKS = LOC_ROWS // MOE_CHUNK
CONV_TT = 512
CONV_CTX = 32
CONV_SUB = 128
SWA_SB = DIL_BLOCK * 16
SWA_TILE = 256
SWA_UNROLL = 4
SGU_ROWS = 1024
RET_ROWS = 1024
SAMPLE_SEQS = 8


def _vmem(nbytes):
    return int(min(VMEM_CAP_BYTES, nbytes + COMPILER_TEMP_BYTES))


def _params(sem, nbytes):
    return pltpu.CompilerParams(dimension_semantics=sem, vmem_limit_bytes=_vmem(nbytes))


def _dot(a, b):
    return jnp.dot(a, b, preferred_element_type=F32)


def _dot_nt(a, b):
    return lax.dot_general(a, b, (((1,), (1,)), ((), ())), preferred_element_type=F32)


def _dot_tn(a, b):
    return lax.dot_general(a, b, (((0,), (0,)), ((), ())), preferred_element_type=F32)


def _split_dot(x, m):
    hi = x.astype(BF16)
    lo = (x - hi.astype(F32)).astype(BF16)
    return _dot(hi, m) + _dot(lo, m)


def _sigmoid(x):
    return 1.0 / (1.0 + jnp.exp(-x))


def _layer_norm(z, g, b):
    mu = jnp.mean(z, axis=-1, keepdims=True)
    d = z - mu
    var = jnp.mean(d * d, axis=-1, keepdims=True)
    return d * lax.rsqrt(var + LN_EPS) * g + b


def _group_norm(z, avg):
    mu = _split_dot(z, avg)
    d = z - mu
    var = _split_dot(d * d, avg)
    return d * lax.rsqrt(var + LN_EPS)


def _lane_head(width):
    return lax.broadcasted_iota(I32, (1, width), 1) // HEAD_DIM


def _rotate(x, cos, sin_signed, half):
    first = (lax.broadcasted_iota(I32, (1, LANE), 1) % HEAD_DIM) < half
    parts = []
    for p in range(x.shape[1] // LANE):
        t = x[:, p * LANE:(p + 1) * LANE]
        up = pltpu.roll(t, half, 1)
        down = pltpu.roll(t, LANE - half, 1)
        parts.append(jnp.where(first, down, up))
    return x * cos + jnp.concatenate(parts, axis=1) * sin_signed


def _in_proj_body(x_ref, w_ref, cr_ref, sr_ref, cs_ref, ss_ref, h_ref, wbf_ref):
    @pl.when(pl.program_id(0) == 0)
    def _():
        for j in range(N_COLBLK):
            wbf_ref[:, j * MIX:(j + 1) * MIX] = w_ref[0, :, j * MIX:(j + 1) * MIX].astype(BF16)

    xb = x_ref[...].astype(BF16)
    for j in range(N_COLBLK):
        hj = _dot(xb, wbf_ref[:, j * MIX:(j + 1) * MIX])
        if j in (COL_RQ, COL_RK):
            hj = _rotate(hj, cr_ref[...], sr_ref[...], HEAD_DIM // 2)
        if j == COL_RK:
            hj = hj * (HEAD_DIM ** -0.5)
        if j in (COL_SQ, COL_SK):
            hj = _rotate(hj, cs_ref[...], ss_ref[...], ROPE_DIM // 2)
        h_ref[:, j * MIX:(j + 1) * MIX] = hj


def _in_proj(x, w, layer, tabs, n_prompt_tiles, tiles_per_seq):
    nt, d = x.shape
    width = w.shape[2]

    def tab_map(i):
        return (jnp.where(i < n_prompt_tiles, i % tiles_per_seq, tiles_per_seq + i - n_prompt_tiles), 0)

    tab_spec = pl.BlockSpec((TM, MIX), tab_map)
    nbytes = d * width * 4 + d * width * 2 + 2 * TM * d * 4 + 8 * TM * MIX * 4 + 2 * TM * width * 4
    return pl.pallas_call(
        _in_proj_body,
        grid=(nt // TM,),
        in_specs=[pl.BlockSpec((TM, d), lambda i: (i, 0)),
                  pl.BlockSpec((1, d, width), lambda i: (layer, 0, 0), pipeline_mode=pl.Buffered(1)),
                  tab_spec, tab_spec, tab_spec, tab_spec],
        out_specs=pl.BlockSpec((TM, width), lambda i: (i, 0)),
        out_shape=jax.ShapeDtypeStruct((nt, width), F32),
        scratch_shapes=[pltpu.VMEM((d, width), BF16)],
        compiler_params=_params(("arbitrary",), nbytes),
        name="in_proj",
    )(x, w, *tabs)


def _retention_body(q_ref, k_ref, v_ref, g_ref, s0_ref, dec_ref, qdec_ref, kdec_ref,
                    cmat_ref, bdm_ref, avg_ref, gn_ref, mix_ref, y_ref, sout_ref, s_scr,
                    *, n_steps, seqs, chunks, rows):
    del mix_ref
    c = pl.program_id(1)

    @pl.when(c == 0)
    def _():
        s_scr[...] = s0_ref[...]

    head = _lane_head(MIX)
    for s in range(seqs):
        state = s_scr[s]
        for cc in range(chunks):
            rs = slice((s * chunks + cc) * rows, (s * chunks + cc + 1) * rows)
            q = q_ref[rs, :]
            k = k_ref[rs, :]
            vb = v_ref[rs, :].astype(BF16)
            kb = k.astype(BF16)
            o = _dot(q.astype(BF16), state.astype(BF16)) * qdec_ref[...]
            for h in range(N_HEADS):
                mh = head == h
                a = _dot_nt(jnp.where(mh, q, 0.0).astype(BF16), kb) * dec_ref[h]
                o = o + jnp.where(mh, _dot(a.astype(BF16), vb), 0.0)
            upd = _dot_tn((k * kdec_ref[...]).astype(BF16), vb)
            state = state * cmat_ref[...] + upd * bdm_ref[...]
            g = g_ref[rs, :]
            y_ref[rs, :] = _group_norm(o, avg_ref[...]) * gn_ref[...] * (g * _sigmoid(g))
        s_scr[s] = state

    @pl.when(c == n_steps - 1)
    def _():
        sout_ref[...] = s_scr[...]


def _retention_tables(l_real, l_pad):
    hh = jnp.arange(N_HEADS, dtype=F32)
    log_g = jnp.log1p(-jnp.exp2(-5.0 - hh))
    i = jnp.arange(l_pad, dtype=F32)
    rel = i[:, None] - i[None, :]
    dec = jnp.where(rel >= 0, jnp.exp(log_g[:, None, None] * jnp.maximum(rel, 0.0)), 0.0)
    qd = jnp.exp(log_g[:, None] * (i + 1.0))
    kd = jnp.where(i < l_real, jnp.exp(log_g[:, None] * (l_real - 1.0 - i)), 0.0)
    cd = jnp.exp(log_g * l_real)
    expand = lambda t: jnp.repeat(t.T, HEAD_DIM, axis=1)
    row_head = jnp.arange(MIX) // HEAD_DIM
    bdm = (row_head[:, None] == row_head[None, :]).astype(F32)
    cmat = cd[row_head][:, None] * bdm
    return dec, expand(qd), expand(kd), cmat, bdm


def _retention(h, mix, s0, tabs, avg, gn_g, *, rows, n_seq, n_chunks, seqs, chunks, row0):
    assert (seqs == 1 or chunks == n_chunks) and n_seq % seqs == 0 and n_chunks % chunks == 0
    blk = seqs * chunks * rows
    rb0 = row0 // blk
    n_steps = n_chunks // chunks
    dec, qdec, kdec, cmat, bdm = tabs

    def hspec(col):
        return pl.BlockSpec((blk, MIX), lambda b, c: (rb0 + b * n_steps + c, col))

    const2 = lambda shape: pl.BlockSpec(shape, lambda b, c: (0, 0))
    state_spec = pl.BlockSpec((seqs, MIX, MIX), lambda b, c: (b, 0, 0))
    in_specs = [hspec(COL_RQ), hspec(COL_RK), hspec(COL_RV), hspec(COL_RG), state_spec,
                pl.BlockSpec((N_HEADS, rows, rows), lambda b, c: (0, 0, 0)),
                const2((rows, MIX)), const2((rows, MIX)),
                const2((MIX, MIX)), const2((MIX, MIX)), const2((MIX, MIX)), const2((1, MIX)),
                pl.BlockSpec(memory_space=pl.ANY)]
    args = [h, h, h, h, s0, dec, qdec, kdec, cmat, bdm, avg, gn_g, mix]
    nbytes = (2 * (5 * blk * MIX * 4 + 2 * rows * MIX * 4 + (3 + 2 * seqs) * MIX * MIX * 4 + N_HEADS * rows * rows * 4)
              + seqs * MIX * MIX * 4)
    return pl.pallas_call(
        functools.partial(_retention_body, n_steps=n_steps, seqs=seqs, chunks=chunks, rows=rows),
        grid=(n_seq // seqs, n_steps),
        in_specs=in_specs,
        out_specs=[pl.BlockSpec((blk, MIX), lambda b, c: (rb0 + b * n_steps + c, 0)), state_spec],
        out_shape=[jax.ShapeDtypeStruct(mix.shape, F32),
                   jax.ShapeDtypeStruct((n_seq, MIX, MIX), F32)],
        scratch_shapes=[pltpu.VMEM((seqs, MIX, MIX), F32)],
        input_output_aliases={12: 0},
        compiler_params=_params(("arbitrary", "arbitrary"), nbytes),
        name="retention",
    )(*args)


def _swa_prompt_body(q_ref, k_ref, v_ref, mix_ref, y_ref, o0, o1, o2, l0, l1, l2, *, seq):
    del mix_ref
    o_scr = (o0, o1, o2)
    l_scr = (l0, l1, l2)
    head = _lane_head(LANE)
    qi = lax.broadcasted_iota(I32, (DIL_BLOCK, 1), 0)
    ki = lax.broadcasted_iota(I32, (1, 2 * DIL_BLOCK), 1)
    rel = DIL_BLOCK + qi - ki
    scale = HEAD_DIM ** -0.5
    for sb in range(seq // SWA_SB):
        for ci, (window, dil) in enumerate(DILATED_CONFIGS):
            span = window // dil
            band = (rel >= 0) & (rel <= span)

            def block(i, carry, sb=sb, ci=ci, dil=dil, band=band):
                nloc = i // dil
                off = nloc * (DIL_BLOCK * dil) + i % dil
                start = sb * SWA_SB + off
                if sb == 0:
                    first = nloc == 0
                    pstart = jnp.where(first, start, start - DIL_BLOCK * dil)
                    valid = band & (ki >= jnp.where(first, DIL_BLOCK, 0))
                else:
                    pstart = start - DIL_BLOCK * dil
                    valid = band
                stride = dil if dil > 1 else None
                cur = pl.ds(start, DIL_BLOCK, stride=stride)
                prev = pl.ds(pstart, DIL_BLOCK, stride=stride)
                qb = q_ref[cur, :]
                kb = jnp.concatenate([k_ref[prev, :], k_ref[cur, :]], axis=0).astype(BF16)
                vb = jnp.concatenate([v_ref[prev, :], v_ref[cur, :]], axis=0).astype(BF16)
                o = jnp.zeros((DIL_BLOCK, LANE), F32)
                ls = jnp.zeros((DIL_BLOCK, LANE), F32)
                for hh in range(LANE // HEAD_DIM):
                    mh = head == hh
                    s = _dot_nt(jnp.where(mh, qb, 0.0).astype(BF16), kb) * scale
                    s = jnp.where(valid, s, NEG_INF)
                    m = jnp.max(s, axis=-1, keepdims=True)
                    p = jnp.exp(s - m)
                    l = jnp.sum(p, axis=-1, keepdims=True)
                    o = jnp.where(mh, _dot(p.astype(BF16), vb) / l, o)
                    ls = jnp.where(mh, m + jnp.log(l), ls)
                dst = pl.ds(off, DIL_BLOCK, stride=stride)
                o_scr[ci][dst, :] = o
                l_scr[ci][dst, :] = ls
                return carry

            lax.fori_loop(0, SWA_SB // DIL_BLOCK, block, 0, unroll=SWA_UNROLL)

        def combine(t, carry, sb=sb):
            rows = pl.ds(t * SWA_TILE, SWA_TILE)
            la, lb, lc = l_scr[0][rows, :], l_scr[1][rows, :], l_scr[2][rows, :]
            mx = jnp.maximum(jnp.maximum(la, lb), lc)
            ea, eb, ec = jnp.exp(la - mx), jnp.exp(lb - mx), jnp.exp(lc - mx)
            num = ea * o_scr[0][rows, :] + eb * o_scr[1][rows, :] + ec * o_scr[2][rows, :]
            y_ref[pl.ds(sb * SWA_SB + t * SWA_TILE, SWA_TILE), :] = num / (ea + eb + ec)
            return carry

        lax.fori_loop(0, SWA_SB // SWA_TILE, combine, 0)


def _swa_prompt(h, mix, *, seq, n_seq):
    halves = MIX // LANE

    def hspec(col):
        return pl.BlockSpec((seq, LANE), lambda b, p: (b, col * halves + p))

    nbytes = 2 * 4 * seq * LANE * 4 + 6 * SWA_SB * LANE * 4
    return pl.pallas_call(
        functools.partial(_swa_prompt_body, seq=seq),
        grid=(n_seq, halves),
        in_specs=[hspec(COL_SQ), hspec(COL_SK), hspec(COL_SV), pl.BlockSpec(memory_space=pl.ANY)],
        out_specs=pl.BlockSpec((seq, LANE), lambda b, p: (b, halves + p)),
        out_shape=jax.ShapeDtypeStruct(mix.shape, F32),
        scratch_shapes=[pltpu.VMEM((SWA_SB, LANE), F32)] * 6,
        input_output_aliases={3: 0},
        compiler_params=_params(("arbitrary", "arbitrary"), nbytes),
        name="swa_prompt",
    )(h, h, h, mix)


def _swa_sample_body(q_ref, k_ref, v_ref, ck_ref, cv_ref, mix_ref, y_ref, *, past, rows):
    del mix_ref
    q = q_ref[...]
    head = _lane_head(MIX)
    qm = jnp.concatenate([jnp.where(head == h, q, 0.0) for h in range(N_HEADS)], axis=0).astype(BF16)
    ck = ck_ref[0, 0].astype(BF16)
    cv = cv_ref[0, 0].astype(BF16)
    zpad = jnp.zeros((LANE - rows, MIX), F32)
    kn = jnp.concatenate([k_ref[...], zpad], axis=0).astype(BF16)
    vn = jnp.concatenate([v_ref[...], zpad], axis=0).astype(BF16)
    scale = HEAD_DIM ** -0.5
    s_c = _dot(qm, ck) * scale
    s_n = _dot_nt(qm, kn) * scale
    t_row = lax.broadcasted_iota(I32, (N_HEADS * rows, 1), 0) % rows
    rel_c = past + t_row - lax.broadcasted_iota(I32, (1, past), 1)
    rel_n = t_row - lax.broadcasted_iota(I32, (1, LANE), 1)
    outs, lses = [], []
    for window, dil in DILATED_CONFIGS:
        span = window // dil
        shift = int(math.log2(dil))

        def valid(rel, dil=dil, shift=shift, span=span):
            return (rel >= 0) & ((rel & (dil - 1)) == 0) & ((rel >> shift) <= span)

        sc = jnp.where(valid(rel_c), s_c, NEG_INF)
        sn = jnp.where(valid(rel_n), s_n, NEG_INF)
        m = jnp.maximum(jnp.max(sc, axis=-1, keepdims=True), jnp.max(sn, axis=-1, keepdims=True))
        pc = jnp.exp(sc - m)
        pn = jnp.exp(sn - m)
        l = jnp.sum(pc, axis=-1, keepdims=True) + jnp.sum(pn, axis=-1, keepdims=True)
        outs.append((_dot_nt(pc.astype(BF16), cv) + _dot(pn.astype(BF16), vn)) / l)
        lses.append(m + jnp.log(l))
    mx = jnp.maximum(jnp.maximum(lses[0], lses[1]), lses[2])
    es = [jnp.exp(ls - mx) for ls in lses]
    y_all = (es[0] * outs[0] + es[1] * outs[1] + es[2] * outs[2]) / (es[0] + es[1] + es[2])
    y = jnp.zeros((rows, MIX), F32)
    for h in range(N_HEADS):
        y = jnp.where(head == h, y_all[h * rows:(h + 1) * rows, :], y)
    y_ref[...] = y


def _swa_sample(h, mix, cache_k, cache_v, layer, *, rows, n_seq, row0):
    past = cache_k.shape[3]
    rb0 = row0 // rows

    def hspec(col):
        return pl.BlockSpec((rows, MIX), lambda b: (rb0 + b, col))

    cspec = pl.BlockSpec((1, 1, MIX, past), lambda b: (layer, b, 0, 0))
    nbytes = 2 * (2 * past * MIX * 4 + 4 * rows * MIX * 4) + 12 * N_HEADS * rows * past * 4
    return pl.pallas_call(
        functools.partial(_swa_sample_body, past=past, rows=rows),
        grid=(n_seq,),
        in_specs=[hspec(COL_SQ), hspec(COL_SK), hspec(COL_SV), cspec, cspec,
                  pl.BlockSpec(memory_space=pl.ANY)],
        out_specs=pl.BlockSpec((rows, MIX), lambda b: (rb0 + b, 1)),
        out_shape=jax.ShapeDtypeStruct(mix.shape, F32),
        input_output_aliases={5: 0},
        compiler_params=_params(("arbitrary",), nbytes),
        name="swa_sample",
    )(h, h, h, cache_k, cache_v, mix)


def _conv_body(val_ref, gate_ref, buf_ref, w_ref, cb_ref, g_ref, b_ref, mix_ref,
               y_ref, glu_ref, ctx, *, rows, n_tiles):
    del mix_ref
    t = pl.program_id(1)

    @pl.when(t == 0)
    def _():
        ctx[0:CONV_CTX, :] = buf_ref[0]

    glu = val_ref[...] * _sigmoid(gate_ref[...])
    glu_ref[...] = glu
    ctx[CONV_CTX:CONV_CTX + rows, :] = glu
    lead = CONV_CTX - (CONV_WIDTH - 1)
    sub = min(rows, CONV_SUB)
    for base in range(0, rows, sub):
        window = ctx[base:base + sub + CONV_CTX, :]
        acc = jnp.zeros((sub, MIX), F32)
        for r in range(SUBLANE):
            shifted = pltpu.roll(window, sub + CONV_CTX - (lead + r), 0)
            for j in range(r, CONV_WIDTH, SUBLANE):
                acc = acc + shifted[j - r:j - r + sub, :] * w_ref[j:j + 1, :]
        z = _layer_norm(acc + cb_ref[...], g_ref[...], b_ref[...])
        y_ref[base:base + sub, :] = z * _sigmoid(z)
    if n_tiles > 1:
        ctx[0:CONV_CTX, :] = ctx[rows:rows + CONV_CTX, :]


def _conv(h, mix, buf, w, cb, g, b, *, rows, n_seq, n_tiles, row0):
    rb0 = row0 // rows
    rmap = lambda col: (lambda s, t: (rb0 + s * n_tiles + t, col))
    const = lambda shape: pl.BlockSpec(shape, lambda s, t: (0, 0))
    nbytes = 2 * 4 * rows * MIX * 4 + (CONV_CTX + rows) * MIX * 4 + 8 * rows * MIX * 4
    return pl.pallas_call(
        functools.partial(_conv_body, rows=rows, n_tiles=n_tiles),
        grid=(n_seq, n_tiles),
        in_specs=[pl.BlockSpec((rows, MIX), rmap(COL_CV)), pl.BlockSpec((rows, MIX), rmap(COL_CG)),
                  pl.BlockSpec((1, CONV_CTX, MIX), lambda s, t: (s, 0, 0)),
                  const((CONV_CTX, MIX)), const((1, MIX)), const((1, MIX)), const((1, MIX)),
                  pl.BlockSpec(memory_space=pl.ANY)],
        out_specs=[pl.BlockSpec((rows, MIX), rmap(2)),
                   pl.BlockSpec((rows, MIX), lambda s, t: (s * n_tiles + t, 0))],
        out_shape=[jax.ShapeDtypeStruct(mix.shape, F32),
                   jax.ShapeDtypeStruct((n_seq * n_tiles * rows, MIX), F32)],
        scratch_shapes=[pltpu.VMEM((CONV_CTX + rows, MIX), F32)],
        input_output_aliases={7: 0},
        compiler_params=_params(("arbitrary", "arbitrary"), nbytes),
        name="conv",
    )(h, h, buf, w, cb, g, b, mix)


def _gelu_tanh(x):
    c = math.sqrt(2.0 / math.pi)
    return x * (0.5 * (1.0 + jnp.tanh(c * (x + 0.044715 * (x * x * x)))))


def _sgu_body(u_ref, v_ref, g_ref, b_ref, ws_ref, bias_ref, avg_ref, mix_ref, y_ref, *vn_ref, rows, sub):
    del mix_ref
    tri = (lax.broadcasted_iota(I32, (SGU_CHUNK, SGU_CHUNK), 0)
           >= lax.broadcasted_iota(I32, (SGU_CHUNK, SGU_CHUNK), 1))
    wms = [jnp.where(tri, ws_ref[gi], 0.0).astype(BF16) for gi in range(N_HEADS)]
    head = _lane_head(MIX)
    for c in range(rows // sub):
        rs = slice(c * sub, (c + 1) * sub)
        u = _gelu_tanh(u_ref[rs, :])
        v = _gelu_tanh(v_ref[rs, :])
        vn = _group_norm(v, avg_ref[...]) * g_ref[...] + b_ref[...]
        if vn_ref:
            vn_ref[0][rs, :] = vn
        if sub < SGU_CHUNK:
            vn = jnp.concatenate([vn, jnp.zeros((SGU_CHUNK - sub, MIX), F32)], axis=0)
        vnb = vn.astype(BF16)
        z = jnp.zeros((SGU_CHUNK, MIX), F32)
        for gi in range(N_HEADS):
            z = jnp.where(head == gi, _dot(wms[gi], vnb), z)
        z = z + bias_ref[...]
        y_ref[rs, :] = u * z[0:sub, :]


def _sgu(h, mix, g, b, ws, bias, avg, *, rows, sub, n_steps, row0, emit_vn):
    rb0 = row0 // rows
    rmap = lambda col: (lambda c: (rb0 + c, col))
    const = lambda shape: pl.BlockSpec(shape, lambda c: (0,) * len(shape))
    nbytes = 2 * (4 * rows * MIX * 4 + N_HEADS * SGU_CHUNK * SGU_CHUNK * 4 + SGU_CHUNK * MIX * 4) + 8 * SGU_CHUNK * MIX * 4
    out_specs = [pl.BlockSpec((rows, MIX), rmap(3))]
    out_shape = [jax.ShapeDtypeStruct(mix.shape, F32)]
    if emit_vn:
        out_specs.append(pl.BlockSpec((rows, MIX), lambda c: (c, 0)))
        out_shape.append(jax.ShapeDtypeStruct((n_steps * rows, MIX), F32))
    return pl.pallas_call(
        functools.partial(_sgu_body, rows=rows, sub=sub),
        grid=(n_steps,),
        in_specs=[pl.BlockSpec((rows, MIX), rmap(COL_GU)), pl.BlockSpec((rows, MIX), rmap(COL_GV)),
                  const((1, MIX)), const((1, MIX)), const((N_HEADS, SGU_CHUNK, SGU_CHUNK)),
                  const((SGU_CHUNK, MIX)), const((MIX, MIX)), pl.BlockSpec(memory_space=pl.ANY)],
        out_specs=out_specs,
        out_shape=out_shape,
        input_output_aliases={7: 0},
        compiler_params=_params(("arbitrary",), nbytes),
        name="sgu",
    )(h, h, g, b, ws, bias, avg, mix)


def _out_router_body(mix_ref, w_ref, x_ref, g_ref, b_ref, wr_ref, br_ref,
                     x1_ref, route_ref, counts_ref, wbf, wrbf, logit_scr, both_scr, *, alpha):
    @pl.when(pl.program_id(0) == 0)
    def _():
        wbf[...] = w_ref[0].astype(BF16)
        wrbf[...] = wr_ref[...].astype(BF16)

    y = _dot(mix_ref[...].astype(BF16), wbf[...])
    x1 = _layer_norm(alpha * x_ref[...] + y, g_ref[...], b_ref[...])
    x1_ref[...] = x1
    logit_scr[...] = _dot(x1.astype(BF16), wrbf[...]) + br_ref[...]
    lane = lax.broadcasted_iota(I32, (1, LANE), 1).astype(F32)

    def first_max(mask, vals):
        masked = jnp.where(mask, vals, NEG_INF)
        top = jnp.max(masked, axis=-1, keepdims=True)
        idx = jnp.min(jnp.where(mask & (masked == top), lane, float(LANE)), axis=-1, keepdims=True)
        return top, idx

    def lanes(values, base):
        for ln, val in values:
            base = jnp.where(lane == ln, val, base)
        return base

    gmask = lane < N_GROUPS
    for sb in range(TM // ROUTE_SUB):
        rs = slice(sb * ROUTE_SUB, (sb + 1) * ROUTE_SUB)
        logits = logit_scr[rs, :]
        gmax, gsel = first_max(gmask, logits)
        g_gate = 1.0 / jnp.sum(jnp.where(gmask, jnp.exp(logits - gmax), 0.0), axis=-1, keepdims=True)
        lo = N_GROUPS + EXPERTS_PER_GROUP * gsel
        emask = (lane >= lo) & (lane < lo + EXPERTS_PER_GROUP)
        v1, i1 = first_max(emask, logits)
        v2, i2 = first_max(emask & (lane != i1), logits)
        e12 = jnp.exp(v2 - v1)
        e_a = i1 - N_GROUPS
        e_b = i2 - N_GROUPS
        both_scr[rs, :] = (lane == e_a).astype(F32) + (lane == e_b).astype(F32)
        route_ref[rs, :] = lanes(((0, e_a), (1, e_b), (2, g_gate * (1.0 / (1.0 + e12))),
                                  (3, g_gate * (e12 / (1.0 + e12)))), jnp.zeros((ROUTE_SUB, LANE), F32))

    strict = (lax.broadcasted_iota(I32, (TD, TD), 0) > lax.broadcasted_iota(I32, (TD, TD), 1)).astype(BF16)
    for td in range(TM // TD):
        rs = slice(td * TD, (td + 1) * TD)
        both = both_scr[rs, :]
        counts_ref[td] = jnp.sum(both, axis=0, keepdims=True)
        both_scr[rs, :] = _dot(strict, both.astype(BF16))
    for sb in range(TM // ROUTE_SUB):
        rs = slice(sb * ROUTE_SUB, (sb + 1) * ROUTE_SUB)
        part = route_ref[rs, :]
        before = both_scr[rs, :]
        rank_a = jnp.sum(jnp.where(lane == part[:, 0:1], before, 0.0), axis=-1, keepdims=True)
        rank_b = jnp.sum(jnp.where(lane == part[:, 1:2], before, 0.0), axis=-1, keepdims=True)
        route_ref[rs, :] = lanes(((4, rank_a), (5, rank_b)), part)


def _out_router(mix, w_out, layer, x, g, b, wr, br, *, alpha):
    nt, d = x.shape
    kdim = mix.shape[1]
    const = lambda shape: pl.BlockSpec(shape, lambda i: (0, 0))
    row = lambda width: pl.BlockSpec((TM, width), lambda i: (i, 0))
    nbytes = 2 * (kdim * d * 4 + d * LANE * 4 + TM * kdim * 4 + 2 * TM * d * 4 + TM * LANE * 4) + kdim * d * 2
    return pl.pallas_call(
        functools.partial(_out_router_body, alpha=alpha),
        grid=(nt // TM,),
        in_specs=[row(kdim), pl.BlockSpec((1, kdim, d), lambda i: (layer, 0, 0)), row(d),
                  const((1, d)), const((1, d)), const((d, LANE)), const((1, LANE))],
        out_specs=[row(d), row(LANE), pl.BlockSpec((TM // TD, 1, LANE), lambda i: (i, 0, 0))],
        out_shape=[jax.ShapeDtypeStruct((nt, d), F32), jax.ShapeDtypeStruct((nt, LANE), F32),
                   jax.ShapeDtypeStruct((nt // TD, 1, LANE), F32)],
        scratch_shapes=[pltpu.VMEM((kdim, d), BF16), pltpu.VMEM((d, LANE), BF16),
                        pltpu.VMEM((TM, LANE), F32), pltpu.VMEM((TM, LANE), F32)],
        compiler_params=_params(("arbitrary",), nbytes),
        name="out_router",
    )(mix, w_out, x, g, b, wr, br)


def _local_slots(route, off_row):
    lane = lax.broadcasted_iota(I32, (1, LANE), 1).astype(F32)
    slots = []
    for kk in range(TOP_K):
        onehot = (lane == route[:, kk:kk + 1]).astype(F32)
        slots.append(jnp.sum(onehot * off_row, axis=-1, keepdims=True) + route[:, 4 + kk:5 + kk])
    return slots


def _chunk_copy(hbm, hbm_row, loc, chunk, sem, *, to_hbm):
    vm = loc.at[pl.ds(pl.multiple_of(chunk * MOE_CHUNK, MOE_CHUNK), MOE_CHUNK), :]
    hb = hbm.at[pl.ds(pl.multiple_of(hbm_row, MOE_CHUNK), MOE_CHUNK), :]
    return pltpu.make_async_copy(vm, hb, sem) if to_hbm else pltpu.make_async_copy(hb, vm, sem)


def _wait_chunks(hbm, buf, sem, count, *, to_hbm):
    def wait(c, carry):
        _chunk_copy(hbm, 0, buf, 0, sem, to_hbm=to_hbm).wait()
        return carry

    lax.fori_loop(0, count, wait, 0)


def _dispatch_body(dst_tab, n_chunks, x_ref, route_ref, off_ref, xs_in, xs_hbm, loc, sem):
    del xs_in
    i = pl.program_id(0)
    last = pl.num_programs(0) - 1
    cur = i % 2
    slot_a, slot_b = _local_slots(route_ref[...], off_ref[0])
    col = lax.broadcasted_iota(I32, (1, LOC_ROWS), 1).astype(F32)
    onehot = ((col == slot_a) | (col == slot_b)).astype(BF16)
    loc[cur] = _dot_tn(onehot, x_ref[...].astype(BF16))

    def start(c, carry):
        _chunk_copy(xs_hbm, dst_tab[i * MAX_CHUNKS + c], loc.at[cur], c, sem.at[cur], to_hbm=True).start()
        return carry

    lax.fori_loop(0, n_chunks[i], start, 0)

    @pl.when(i > 0)
    def _():
        _wait_chunks(xs_hbm, loc.at[1 - cur], sem.at[1 - cur], n_chunks[jnp.maximum(i - 1, 0)], to_hbm=True)

    @pl.when(i == last)
    def _():
        _wait_chunks(xs_hbm, loc.at[cur], sem.at[cur], n_chunks[i], to_hbm=True)


def _dispatch(dst_tab, n_chunks, x1, route, off, xs_prev):
    nt, d = x1.shape
    row = lambda width: pl.BlockSpec((TD, width), lambda i, *_: (i, 0))
    nbytes = 2 * (TD * d * 4 + TD * LANE * 4) + 2 * LOC_ROWS * d * 4 + 4 * TD * LOC_ROWS * 4 + LOC_ROWS * d * 4
    return pl.pallas_call(
        _dispatch_body,
        grid_spec=pltpu.PrefetchScalarGridSpec(
            num_scalar_prefetch=2,
            grid=(nt // TD,),
            in_specs=[row(d), row(LANE), pl.BlockSpec((1, 1, LANE), lambda i, *_: (i, 0, 0)),
                      pl.BlockSpec(memory_space=pl.ANY)],
            out_specs=pl.BlockSpec(memory_space=pl.ANY),
            scratch_shapes=[pltpu.VMEM((2, LOC_ROWS, d), F32), pltpu.SemaphoreType.DMA((2,))]),
        out_shape=jax.ShapeDtypeStruct(xs_prev.shape, F32),
        input_output_aliases={5: 0},
        compiler_params=_params(("arbitrary",), nbytes),
        name="moe_dispatch",
    )(dst_tab, n_chunks, x1, route, off, xs_prev)


def _ffn_body(blk_e, blk_valid, blk_src, x_ref, w1_ref, w3_ref, w2_ref, ys_in, y_ref, w1b, w3b, w2b):
    del blk_src, ys_in
    j = pl.program_id(0)

    @pl.when(blk_valid[j] > 0)
    def _():
        @pl.when((j == 0) | (blk_e[j] != blk_e[jnp.maximum(j - 1, 0)]))
        def _():
            w1b[...] = w1_ref[0, 0].astype(BF16)
            w3b[...] = w3_ref[0, 0].astype(BF16)
            w2b[...] = w2_ref[0, 0].astype(BF16)

        xb = x_ref[...].astype(BF16)
        h1 = _dot(xb, w1b[...])
        h3 = _dot(xb, w3b[...])
        act = (h1 * _sigmoid(h1)) * h3
        y_ref[...] = _dot(act.astype(BF16), w2b[...])


def _ffn(xs, w1, w3, w2, layer, blk_e, blk_valid, blk_src, ys_prev):
    d = xs.shape[1]
    de = w1.shape[3]
    n_blk = xs.shape[0] // MOE_BLK
    wmap = lambda j, be, bv, bs: (layer, be[j], 0, 0)
    nbytes = 2 * (3 * d * de * 4 + 2 * MOE_BLK * d * 4) + 3 * d * de * 2 + 6 * MOE_BLK * de * 4
    return pl.pallas_call(
        _ffn_body,
        grid_spec=pltpu.PrefetchScalarGridSpec(
            num_scalar_prefetch=3,
            grid=(n_blk,),
            in_specs=[pl.BlockSpec((MOE_BLK, d), lambda j, be, bv, bs: (bs[j], 0)),
                      pl.BlockSpec((1, 1, d, de), wmap), pl.BlockSpec((1, 1, d, de), wmap),
                      pl.BlockSpec((1, 1, de, d), wmap), pl.BlockSpec(memory_space=pl.ANY)],
            out_specs=pl.BlockSpec((MOE_BLK, d), lambda j, be, bv, bs: (bs[j], 0)),
            scratch_shapes=[pltpu.VMEM((d, de), BF16), pltpu.VMEM((d, de), BF16), pltpu.VMEM((de, d), BF16)]),
        out_shape=jax.ShapeDtypeStruct(xs.shape, F32),
        input_output_aliases={7: 0},
        compiler_params=_params(("arbitrary",), nbytes),
        name="expert_ffn",
    )(blk_e, blk_valid, blk_src, xs, w1, w3, w2, ys_prev)


def _combine_body(dst_tab, n_chunks, x_ref, route_ref, off_ref, g_ref, b_ref, ys_hbm, x2_ref,
                  loc, sem, *, alpha):
    i = pl.program_id(0)
    last = pl.num_programs(0) - 1
    cur = i % 2

    def fetch(tile, buf):
        def start(c, carry):
            _chunk_copy(ys_hbm, dst_tab[tile * MAX_CHUNKS + c], loc.at[buf], c, sem.at[buf], to_hbm=False).start()
            return carry

        lax.fori_loop(0, n_chunks[tile], start, 0)

    @pl.when(i == 0)
    def _():
        loc[...] = jnp.zeros_like(loc)
        fetch(0, 0)

    @pl.when(i < last)
    def _():
        fetch(jnp.minimum(i + 1, last), 1 - cur)

    route = route_ref[...]
    slot_a, slot_b = _local_slots(route, off_ref[0])
    col = lax.broadcasted_iota(I32, (1, LOC_ROWS), 1).astype(F32)
    gates = jnp.where(col == slot_a, route[:, 2:3], 0.0) + jnp.where(col == slot_b, route[:, 3:4], 0.0)
    _wait_chunks(ys_hbm, loc.at[cur], sem.at[cur], n_chunks[i], to_hbm=False)
    f = _dot(gates.astype(BF16), loc[cur].astype(BF16))
    x2_ref[...] = _layer_norm(alpha * x_ref[...] + f, g_ref[...], b_ref[...])


def _combine(dst_tab, n_chunks, x1, route, off, g, b, ys, *, alpha):
    nt, d = x1.shape
    row = lambda width: pl.BlockSpec((TD, width), lambda i, dt, nc: (i, 0))
    const = lambda shape: pl.BlockSpec(shape, lambda i, dt, nc: (0, 0))
    nbytes = 2 * (2 * TD * d * 4 + TD * LANE * 4) + LOC_ROWS * d * 4 + 4 * TD * LOC_ROWS * 4 + LOC_ROWS * d * 2
    return pl.pallas_call(
        functools.partial(_combine_body, alpha=alpha),
        grid_spec=pltpu.PrefetchScalarGridSpec(
            num_scalar_prefetch=2,
            grid=(nt // TD,),
            in_specs=[row(d), row(LANE), pl.BlockSpec((1, 1, LANE), lambda i, dt, nc: (i, 0, 0)),
                      const((1, d)), const((1, d)), pl.BlockSpec(memory_space=pl.ANY)],
            out_specs=row(d),
            scratch_shapes=[pltpu.VMEM((2, LOC_ROWS, d), F32), pltpu.SemaphoreType.DMA((2,))]),
        out_shape=jax.ShapeDtypeStruct((nt, d), F32),
        compiler_params=_params(("arbitrary",), nbytes + LOC_ROWS * d * 4),
        name="combine_ln",
    )(dst_tab, n_chunks, x1, route, off, g, b, ys)


def _rope_tables(pos, rot_dim, theta):
    half = rot_dim // 2
    inv = jnp.float32(theta) ** (-jnp.arange(half, dtype=F32) * (2.0 / rot_dim))
    ang = pos.astype(F32)[:, None] * inv[None, :]
    c, s = jnp.cos(ang), jnp.sin(ang)
    rest = HEAD_DIM - rot_dim
    ones = jnp.ones((pos.shape[0], rest), F32)
    zeros = jnp.zeros((pos.shape[0], rest), F32)
    cos_h = jnp.concatenate([c, c, ones], axis=1)
    sin_h = jnp.concatenate([-s, s, zeros], axis=1)
    return jnp.tile(cos_h, (1, N_HEADS)), jnp.tile(sin_h, (1, N_HEADS))


def _block_diag(s):
    n = s.shape[0]
    eye = jnp.eye(N_HEADS, dtype=s.dtype)
    return (s[:, :, :, None, :] * eye[None, :, None, :, None]).reshape(n, MIX, MIX)


def _diag_blocks(s):
    return jnp.stack([s[:, h * HEAD_DIM:(h + 1) * HEAD_DIM, h * HEAD_DIM:(h + 1) * HEAD_DIM]
                      for h in range(N_HEADS)], axis=1)


def _routing_plan(tile_counts, n_blk):
    cnt_t = tile_counts[:, 0, :N_EXPERTS].astype(I32)
    run_chunks = (cnt_t + MOE_CHUNK - 1) // MOE_CHUNK
    run_rows = run_chunks * MOE_CHUNK
    cnt = jnp.sum(run_rows, axis=0)
    pcnt = (cnt + MOE_BLK - 1) // MOE_BLK * MOE_BLK
    pstart = jnp.cumsum(pcnt) - pcnt
    n_act = jnp.sum(pcnt) // MOE_BLK
    gstart = pstart[None, :] + jnp.cumsum(run_rows, axis=0) - run_rows
    chunk0 = jnp.cumsum(run_chunks, axis=1) - run_chunks
    n_chunks = jnp.sum(run_chunks, axis=1).astype(I32)

    def spread(first, count, value, n):
        p = jnp.arange(n, dtype=I32)[:, None]
        f, c, v = first[..., None, :], count[..., None, :], value[..., None, :]
        return jnp.sum(jnp.where((f <= p) & (p < f + c), v + (p - f) * MOE_CHUNK, 0), axis=-1)

    dst_tab = spread(chunk0, run_chunks, gstart, MAX_CHUNKS)
    off = jnp.pad((chunk0 * MOE_CHUNK).astype(F32), ((0, 0), (0, LANE - N_EXPERTS)))[:, None, :]
    pos = jnp.arange(n_blk, dtype=I32)[:, None] * MOE_BLK
    mine = (pstart[None, :] <= pos) & (pos < (pstart + pcnt)[None, :])
    experts = jnp.arange(N_EXPERTS, dtype=I32)[None, :]
    last_e = jnp.max(jnp.where(cnt > 0, experts[0], 0))
    active = pos[:, 0] < n_act * MOE_BLK
    blk_e = jnp.where(active, jnp.sum(jnp.where(mine, experts, 0), axis=-1), last_e)
    blk_valid = jnp.sum(jnp.where(mine, jnp.clip((pstart + cnt)[None, :] - pos, 0, MOE_BLK), 0), axis=-1)
    blk_src = jnp.minimum(jnp.arange(n_blk, dtype=I32), jnp.maximum(n_act - 1, 0))
    i32 = lambda t: t.astype(I32)
    return i32(dst_tab).reshape(-1), n_chunks, off, i32(blk_e), i32(blk_valid), i32(blk_src)


def kernel(x_prompt, x_sample, state_ret, cache_swa_k, cache_swa_v, state_conv, w_in, w_out, ret_gn_g,
           conv_w, conv_b, conv_ln_g, conv_ln_b, sgu_ln_g, sgu_ln_b, sgu_w, sgu_b, ln1_g, ln1_b,
           ln2_g, ln2_b, router_g_w, router_g_b, router_e_w, router_e_b, moe_w1, moe_w3, moe_w2):
    depth = w_in.shape[0]
    n_seq, seq, d = x_prompt.shape
    n_dec, t_new, _ = x_sample.shape
    past = cache_swa_k.shape[2]
    rows_s = -(-t_new // SUBLANE) * SUBLANE
    n_p = n_seq * seq
    n_s = n_dec * rows_s
    n_tail = -(-n_s // TM) * TM
    nt = n_p + n_tail
    assert seq % SWA_SB == 0 and seq % CONV_TT == 0 and seq % TM == 0 and TM % TD == 0
    assert t_new % RET_CHUNK != 0 and t_new <= SGU_CHUNK and w_in.shape[2] == N_COLBLK * MIX
    assert seq % SGU_ROWS == 0 and seq % RET_ROWS == 0 and n_dec % SAMPLE_SEQS == 0
    assert n_p % (SAMPLE_SEQS * rows_s) == 0
    alpha = (2 * depth) ** 0.25
    keep = min(DILATED_CONFIGS[-1][0], seq)
    n_blk = -(-(nt * TOP_K + (nt // TD) * N_EXPERTS * (MOE_CHUNK - 1) + N_EXPERTS * (MOE_BLK - 1)) // MOE_BLK)
    cache_k = jnp.transpose(cache_swa_k, (0, 1, 3, 4, 2)).reshape(depth, n_dec, MIX, past)
    cache_v = jnp.transpose(cache_swa_v, (0, 1, 3, 4, 2)).reshape(depth, n_dec, MIX, past)

    x = jnp.concatenate([x_prompt.reshape(n_p, d),
                         jnp.pad(x_sample, ((0, 0), (0, rows_s - t_new), (0, 0))).reshape(n_s, d),
                         jnp.zeros((n_tail - n_s, d), F32)], axis=0)
    pos = jnp.concatenate([jnp.arange(seq, dtype=I32),
                           PAST_LEN + jnp.arange(n_tail, dtype=I32) % rows_s])
    rope = _rope_tables(pos, HEAD_DIM, RET_THETA) + _rope_tables(pos, ROPE_DIM, ROPE_THETA)
    ret_tabs_p = _retention_tables(RET_CHUNK, RET_CHUNK)
    ret_tabs_s = _retention_tables(t_new, rows_s)
    lane_head = jnp.arange(MIX) // HEAD_DIM
    avg = ((lane_head[:, None] == lane_head[None, :]).astype(F32) / HEAD_DIM).astype(BF16)
    zero_state = jnp.zeros((n_seq, MIX, MIX), F32)
    zero_ctx = jnp.zeros((n_seq, CONV_CTX, MIX), F32)
    row2 = lambda v: v.reshape(1, -1)
    mix = jnp.zeros((nt, 4 * MIX), F32)
    xs = jnp.zeros((n_blk * MOE_BLK, d), F32)
    ys = jnp.zeros((n_blk * MOE_BLK, d), F32)

    outs = [[] for _ in range(9)]
    for l in range(depth):
        h = _in_proj(x, w_in, l, rope, n_p // TM, seq // TM)
        gn = row2(ret_gn_g[l])
        mix, s_p = _retention(h, mix, zero_state, ret_tabs_p, avg, gn, rows=RET_CHUNK, n_seq=n_seq,
                              n_chunks=seq // RET_CHUNK, seqs=1, chunks=RET_ROWS // RET_CHUNK, row0=0)
        mix, s_s = _retention(h, mix, _block_diag(state_ret[l]), ret_tabs_s, avg, gn, rows=rows_s, n_seq=n_dec,
                              n_chunks=1, seqs=SAMPLE_SEQS, chunks=1, row0=n_p)
        mix = _swa_prompt(h, mix, seq=seq, n_seq=n_seq)
        mix = _swa_sample(h, mix, cache_k, cache_v, l, rows=rows_s, n_seq=n_dec, row0=n_p)
        cw = jnp.pad(conv_w[l], ((0, CONV_CTX - CONV_WIDTH), (0, 0)))
        cargs = (cw, row2(conv_b[l]), row2(conv_ln_g[l]), row2(conv_ln_b[l]))
        mix, glu_p = _conv(h, mix, zero_ctx, *cargs, rows=CONV_TT, n_seq=n_seq, n_tiles=seq // CONV_TT, row0=0)
        ctx_s = jnp.pad(state_conv[l], ((0, 0), (CONV_CTX - (CONV_WIDTH - 1), 0), (0, 0)))
        mix, glu_s = _conv(h, mix, ctx_s, *cargs, rows=rows_s, n_seq=n_dec, n_tiles=1, row0=n_p)
        sbias = jnp.repeat(sgu_b[l].T, HEAD_DIM, axis=1)
        sargs = (row2(sgu_ln_g[l]), row2(sgu_ln_b[l]), sgu_w[l], sbias, avg)
        mix, = _sgu(h, mix, *sargs, rows=SGU_ROWS, sub=SGU_CHUNK, n_steps=n_p // SGU_ROWS, row0=0, emit_vn=False)
        mix, vn_s = _sgu(h, mix, *sargs, rows=SAMPLE_SEQS * rows_s, sub=rows_s, n_steps=n_dec // SAMPLE_SEQS,
                         row0=n_p, emit_vn=True)

        wr = jnp.concatenate([router_g_w[l], jnp.transpose(router_e_w[l], (1, 0, 2)).reshape(d, N_EXPERTS),
                              jnp.zeros((d, LANE - N_GROUPS - N_EXPERTS), F32)], axis=1)
        br = jnp.concatenate([router_g_b[l], router_e_b[l].reshape(-1),
                              jnp.zeros((LANE - N_GROUPS - N_EXPERTS,), F32)]).reshape(1, LANE)
        x1, route, counts = _out_router(mix, w_out, l, x, row2(ln1_g[l]), row2(ln1_b[l]), wr, br, alpha=alpha)
        dst_tab, n_chunks, off, blk_e, blk_valid, blk_src = _routing_plan(counts, n_blk)
        xs = _dispatch(dst_tab, n_chunks, x1, route, off, xs)
        ys = _ffn(xs, moe_w1, moe_w3, moe_w2, l, blk_e, blk_valid, blk_src, ys)
        x = _combine(dst_tab, n_chunks, x1, route, off, row2(ln2_g[l]), row2(ln2_b[l]), ys, alpha=alpha)

        def tail(col, n_keep):
            return jnp.stack([lax.slice(h, ((b + 1) * seq - n_keep, col * MIX), ((b + 1) * seq, (col + 1) * MIX))
                              for b in range(n_seq)])

        def new_rows(col):
            blk = lax.slice(h, (n_p, col * MIX), (n_p + n_s, (col + 1) * MIX))
            return blk.reshape(n_dec, rows_s, MIX)[:, :t_new]

        heads = lambda t: t.reshape(t.shape[0], t.shape[1], N_HEADS, HEAD_DIM)
        outs[0].append(_diag_blocks(s_p))
        outs[1].append(heads(tail(COL_SK, keep)))
        outs[2].append(heads(tail(COL_SV, keep)))
        outs[3].append(glu_p.reshape(n_seq, seq, MIX)[:, seq - (CONV_WIDTH - 1):])
        outs[4].append(_diag_blocks(s_s))
        outs[5].append(heads(new_rows(COL_SK)))
        outs[6].append(heads(new_rows(COL_SV)))
        outs[7].append(glu_s.reshape(n_dec, rows_s, MIX)[:, :t_new])
        outs[8].append(vn_s.reshape(n_dec, rows_s, MIX)[:, :t_new])

    y_prompt = x[:n_p].reshape(n_seq, seq, d)
    y_sample = x[n_p:n_p + n_s].reshape(n_dec, rows_s, d)[:, :t_new]
    return (y_prompt, y_sample) + tuple(jnp.stack(o) for o in outs)
```

```python
import functools
import math

import jax
import jax.numpy as jnp
from jax import lax
from jax.experimental import pallas as pl
from jax.experimental.pallas import tpu as pltpu

F32 = jnp.float32
BF16 = jnp.bfloat16
I32 = jnp.int32

HEAD_DIM = 64
N_HEADS = 4
MIX = N_HEADS * HEAD_DIM
(COL_RQ, COL_RK, COL_RV, COL_RG, COL_SQ, COL_SK, COL_SV,
 COL_CV, COL_CG, COL_GU, COL_GV) = range(11)
N_COLBLK = 11
RET_CHUNK = 128
RET_THETA = 10000.0
ROPE_THETA = 500000.0
ROPE_DIM = HEAD_DIM // 4
DILATED_CONFIGS = ((128, 1), (512, 4), (2048, 16))
DIL_BLOCK = 128
CONV_WIDTH = 31
SGU_CHUNK = 128
N_GROUPS = 4
EXPERTS_PER_GROUP = 8
N_EXPERTS = N_GROUPS * EXPERTS_PER_GROUP
TOP_K = 2
PAST_LEN = 8192
LN_EPS = 1e-5
NEG_INF = -1e30

LANE = 128
SUBLANE = 8
VMEM_CAP_BYTES = 60000 * 1024
COMPILER_TEMP_BYTES = 12 * 1024 * 1024

TM = 512
TD = 256
OUT_ROWS = 128
ROUTE_ROWS = -(-(N_GROUPS + N_EXPERTS) // SUBLANE) * SUBLANE
MOE_BLK = 512
MOE_CHUNK = SUBLANE
LOC_ROWS = -(-(TD * TOP_K + N_EXPERTS * (MOE_CHUNK - 1)) // LANE) * LANE
MAX_CHUNKS = LOC_ROWS // MOE_CHUNK
CONV_TT = 512
CONV_CTX = 32
CONV_SUB = 128
SWA_SB = DIL_BLOCK * 16
SWA_TILE = 256
SWA_UNROLL = 4
SGU_ROWS = 1024
RET_ROWS = 1024
SAMPLE_SEQS = 8
SWA_SAMPLE_SEQS = 4


def _vmem(nbytes):
    return int(min(VMEM_CAP_BYTES, nbytes + COMPILER_TEMP_BYTES))


def _params(sem, nbytes):
    return pltpu.CompilerParams(dimension_semantics=sem, vmem_limit_bytes=_vmem(nbytes))


def _dot(a, b):
    return jnp.dot(a, b, preferred_element_type=F32)


def _dot_nt(a, b):
    return lax.dot_general(a, b, (((1,), (1,)), ((), ())), preferred_element_type=F32)


def _dot_tn(a, b):
    return lax.dot_general(a, b, (((0,), (0,)), ((), ())), preferred_element_type=F32)


def _split_dot(x, m):
    hi = x.astype(BF16)
    lo = (x - hi.astype(F32)).astype(BF16)
    return _dot(hi, m) + _dot(lo, m)


def _sigmoid(x):
    return 1.0 / (1.0 + jnp.exp(-x))


def _layer_norm(z, g, b):
    mu = jnp.mean(z, axis=-1, keepdims=True)
    d = z - mu
    var = jnp.mean(d * d, axis=-1, keepdims=True)
    return d * lax.rsqrt(var + LN_EPS) * g + b


def _group_norm(z, avg):
    mu = _split_dot(z, avg)
    d = z - mu
    var = _split_dot(d * d, avg)
    return d * lax.rsqrt(var + LN_EPS)


def _lane_head(width):
    return lax.broadcasted_iota(I32, (1, width), 1) // HEAD_DIM


def _rotate(x, cos, sin_signed, half):
    first = (lax.broadcasted_iota(I32, (1, LANE), 1) % HEAD_DIM) < half
    parts = []
    for p in range(x.shape[1] // LANE):
        t = x[:, p * LANE:(p + 1) * LANE]
        up = pltpu.roll(t, half, 1)
        down = pltpu.roll(t, LANE - half, 1)
        parts.append(jnp.where(first, down, up))
    return x * cos + jnp.concatenate(parts, axis=1) * sin_signed


def _in_proj_body(x_ref, w_ref, cr_ref, sr_ref, cs_ref, ss_ref, h_ref, wbf_ref):
    @pl.when(pl.program_id(0) == 0)
    def _():
        for j in range(N_COLBLK):
            wbf_ref[:, j * MIX:(j + 1) * MIX] = w_ref[0, :, j * MIX:(j + 1) * MIX].astype(BF16)

    xb = x_ref[...].astype(BF16)
    for j in range(N_COLBLK):
        hj = _dot(xb, wbf_ref[:, j * MIX:(j + 1) * MIX])
        if j in (COL_RQ, COL_RK):
            hj = _rotate(hj, cr_ref[...], sr_ref[...], HEAD_DIM // 2)
        if j == COL_RK:
            hj = hj * (HEAD_DIM ** -0.5)
        if j in (COL_SQ, COL_SK):
            hj = _rotate(hj, cs_ref[...], ss_ref[...], ROPE_DIM // 2)
        h_ref[:, j * MIX:(j + 1) * MIX] = hj


def _in_proj(x, w, layer, tabs, n_prompt_tiles, tiles_per_seq):
    nt, d = x.shape
    width = w.shape[2]

    def tab_map(i):
        return (jnp.where(i < n_prompt_tiles, i % tiles_per_seq, tiles_per_seq + i - n_prompt_tiles), 0)

    tab_spec = pl.BlockSpec((TM, MIX), tab_map)
    nbytes = d * width * 4 + d * width * 2 + 2 * TM * d * 4 + 8 * TM * MIX * 4 + 2 * TM * width * 4
    return pl.pallas_call(
        _in_proj_body,
        grid=(nt // TM,),
        in_specs=[pl.BlockSpec((TM, d), lambda i: (i, 0)),
                  pl.BlockSpec((1, d, width), lambda i: (layer, 0, 0), pipeline_mode=pl.Buffered(1)),
                  tab_spec, tab_spec, tab_spec, tab_spec],
        out_specs=pl.BlockSpec((TM, width), lambda i: (i, 0)),
        out_shape=jax.ShapeDtypeStruct((nt, width), F32),
        scratch_shapes=[pltpu.VMEM((d, width), BF16)],
        compiler_params=_params(("arbitrary",), nbytes),
        name="in_proj",
    )(x, w, *tabs)


def _retention_body(q_ref, k_ref, v_ref, g_ref, s0_ref, dec_ref, qdec_ref, kdec_ref,
                    cmat_ref, bdm_ref, avg_ref, gn_ref, mix_ref, y_ref, sout_ref, s_scr,
                    *, n_steps, seqs, chunks, rows):
    del mix_ref
    c = pl.program_id(1)

    @pl.when(c == 0)
    def _():
        s_scr[...] = s0_ref[...]

    head = _lane_head(MIX)
    for s in range(seqs):
        state = s_scr[s]
        for cc in range(chunks):
            rs = slice((s * chunks + cc) * rows, (s * chunks + cc + 1) * rows)
            q = q_ref[rs, :]
            k = k_ref[rs, :]
            vb = v_ref[rs, :].astype(BF16)
            kb = k.astype(BF16)
            o = _dot(q.astype(BF16), state.astype(BF16)) * qdec_ref[...]
            for h in range(N_HEADS):
                mh = head == h
                a = _dot_nt(jnp.where(mh, q, 0.0).astype(BF16), kb) * dec_ref[h]
                o = o + jnp.where(mh, _dot(a.astype(BF16), vb), 0.0)
            upd = _dot_tn((k * kdec_ref[...]).astype(BF16), vb)
            state = state * cmat_ref[...] + upd * bdm_ref[...]
            g = g_ref[rs, :]
            y_ref[rs, :] = _group_norm(o, avg_ref[...]) * gn_ref[...] * (g * _sigmoid(g))
        s_scr[s] = state

    @pl.when(c == n_steps - 1)
    def _():
        sout_ref[...] = s_scr[...]


def _retention_tables(l_real, l_pad):
    hh = jnp.arange(N_HEADS, dtype=F32)
    log_g = jnp.log1p(-jnp.exp2(-5.0 - hh))
    i = jnp.arange(l_pad, dtype=F32)
    rel = i[:, None] - i[None, :]
    dec = jnp.where(rel >= 0, jnp.exp(log_g[:, None, None] * jnp.maximum(rel, 0.0)), 0.0)
    qd = jnp.exp(log_g[:, None] * (i + 1.0))
    kd = jnp.where(i < l_real, jnp.exp(log_g[:, None] * (l_real - 1.0 - i)), 0.0)
    cd = jnp.exp(log_g * l_real)
    expand = lambda t: jnp.repeat(t.T, HEAD_DIM, axis=1)
    row_head = jnp.arange(MIX) // HEAD_DIM
    bdm = (row_head[:, None] == row_head[None, :]).astype(F32)
    cmat = cd[row_head][:, None] * bdm
    return dec, expand(qd), expand(kd), cmat, bdm


def _retention(h, mix, s0, tabs, avg, gn_g, *, rows, n_seq, n_chunks, seqs, chunks, row0):
    assert (seqs == 1 or chunks == n_chunks) and n_seq % seqs == 0 and n_chunks % chunks == 0
    blk = seqs * chunks * rows
    rb0 = row0 // blk
    n_steps = n_chunks // chunks
    dec, qdec, kdec, cmat, bdm = tabs

    def hspec(col):
        return pl.BlockSpec((blk, MIX), lambda b, c: (rb0 + b * n_steps + c, col))

    const2 = lambda shape: pl.BlockSpec(shape, lambda b, c: (0, 0))
    state_spec = pl.BlockSpec((seqs, MIX, MIX), lambda b, c: (b, 0, 0))
    in_specs = [hspec(COL_RQ), hspec(COL_RK), hspec(COL_RV), hspec(COL_RG), state_spec,
                pl.BlockSpec((N_HEADS, rows, rows), lambda b, c: (0, 0, 0)),
                const2((rows, MIX)), const2((rows, MIX)),
                const2((MIX, MIX)), const2((MIX, MIX)), const2((MIX, MIX)), const2((1, MIX)),
                pl.BlockSpec(memory_space=pl.ANY)]
    args = [h, h, h, h, s0, dec, qdec, kdec, cmat, bdm, avg, gn_g, mix]
    nbytes = (2 * (5 * blk * MIX * 4 + 2 * rows * MIX * 4 + (3 + 2 * seqs) * MIX * MIX * 4 + N_HEADS * rows * rows * 4)
              + seqs * MIX * MIX * 4)
    return pl.pallas_call(
        functools.partial(_retention_body, n_steps=n_steps, seqs=seqs, chunks=chunks, rows=rows),
        grid=(n_seq // seqs, n_steps),
        in_specs=in_specs,
        out_specs=[pl.BlockSpec((blk, MIX), lambda b, c: (rb0 + b * n_steps + c, 0)), state_spec],
        out_shape=[jax.ShapeDtypeStruct(mix.shape, F32),
                   jax.ShapeDtypeStruct((n_seq, MIX, MIX), F32)],
        scratch_shapes=[pltpu.VMEM((seqs, MIX, MIX), F32)],
        input_output_aliases={12: 0},
        compiler_params=_params(("arbitrary", "arbitrary"), nbytes),
        name="retention",
    )(*args)


def _swa_prompt_body(q_ref, k_ref, v_ref, mix_ref, y_ref, o0, o1, o2, l0, l1, l2, *, seq):
    del mix_ref
    o_scr = (o0, o1, o2)
    l_scr = (l0, l1, l2)
    head = _lane_head(LANE)
    qi = lax.broadcasted_iota(I32, (DIL_BLOCK, 1), 0)
    ki = lax.broadcasted_iota(I32, (1, 2 * DIL_BLOCK), 1)
    rel = DIL_BLOCK + qi - ki
    scale = HEAD_DIM ** -0.5
    for sb in range(seq // SWA_SB):
        for ci, (window, dil) in enumerate(DILATED_CONFIGS):
            span = window // dil
            band = (rel >= 0) & (rel <= span)

            def block(i, carry, sb=sb, ci=ci, dil=dil, band=band):
                nloc = i // dil
                off = nloc * (DIL_BLOCK * dil) + i % dil
                start = sb * SWA_SB + off
                if sb == 0:
                    first = nloc == 0
                    pstart = jnp.where(first, start, start - DIL_BLOCK * dil)
                    valid = band & (ki >= jnp.where(first, DIL_BLOCK, 0))
                else:
                    pstart = start - DIL_BLOCK * dil
                    valid = band
                stride = dil if dil > 1 else None
                cur = pl.ds(start, DIL_BLOCK, stride=stride)
                prev = pl.ds(pstart, DIL_BLOCK, stride=stride)
                qb = q_ref[cur, :]
                kb = jnp.concatenate([k_ref[prev, :], k_ref[cur, :]], axis=0).astype(BF16)
                vb = jnp.concatenate([v_ref[prev, :], v_ref[cur, :]], axis=0).astype(BF16)
                o = jnp.zeros((DIL_BLOCK, LANE), F32)
                ls = jnp.zeros((DIL_BLOCK, LANE), F32)
                for hh in range(LANE // HEAD_DIM):
                    mh = head == hh
                    s = _dot_nt(jnp.where(mh, qb, 0.0).astype(BF16), kb) * scale
                    s = jnp.where(valid, s, NEG_INF)
                    m = jnp.max(s, axis=-1, keepdims=True)
                    p = jnp.exp(s - m)
                    l = jnp.sum(p, axis=-1, keepdims=True)
                    o = jnp.where(mh, _dot(p.astype(BF16), vb) / l, o)
                    ls = jnp.where(mh, m + jnp.log(l), ls)
                dst = pl.ds(off, DIL_BLOCK, stride=stride)
                o_scr[ci][dst, :] = o
                l_scr[ci][dst, :] = ls
                return carry

            lax.fori_loop(0, SWA_SB // DIL_BLOCK, block, 0, unroll=SWA_UNROLL)

        def combine(t, carry, sb=sb):
            rows = pl.ds(t * SWA_TILE, SWA_TILE)
            la, lb, lc = l_scr[0][rows, :], l_scr[1][rows, :], l_scr[2][rows, :]
            mx = jnp.maximum(jnp.maximum(la, lb), lc)
            ea, eb, ec = jnp.exp(la - mx), jnp.exp(lb - mx), jnp.exp(lc - mx)
            num = ea * o_scr[0][rows, :] + eb * o_scr[1][rows, :] + ec * o_scr[2][rows, :]
            y_ref[pl.ds(sb * SWA_SB + t * SWA_TILE, SWA_TILE), :] = num / (ea + eb + ec)
            return carry

        lax.fori_loop(0, SWA_SB // SWA_TILE, combine, 0)


def _swa_prompt(h, mix, *, seq, n_seq):
    halves = MIX // LANE

    def hspec(col):
        return pl.BlockSpec((seq, LANE), lambda b, p: (b, col * halves + p))

    nbytes = 2 * 4 * seq * LANE * 4 + 6 * SWA_SB * LANE * 4
    return pl.pallas_call(
        functools.partial(_swa_prompt_body, seq=seq),
        grid=(n_seq, halves),
        in_specs=[hspec(COL_SQ), hspec(COL_SK), hspec(COL_SV), pl.BlockSpec(memory_space=pl.ANY)],
        out_specs=pl.BlockSpec((seq, LANE), lambda b, p: (b, halves + p)),
        out_shape=jax.ShapeDtypeStruct(mix.shape, F32),
        scratch_shapes=[pltpu.VMEM((SWA_SB, LANE), F32)] * 6,
        input_output_aliases={3: 0},
        compiler_params=_params(("arbitrary", "arbitrary"), nbytes),
        name="swa_prompt",
    )(h, h, h, mix)


def _swa_sample_body(q_ref, k_ref, v_ref, ck_ref, cv_ref, mix_ref, y_ref, *, past, rows, seqs):
    del mix_ref
    head = _lane_head(MIX)
    hr = N_HEADS * rows
    scale = HEAD_DIM ** -0.5
    t_row = lax.broadcasted_iota(I32, (hr, 1), 0) % rows
    rel_c = past + t_row - lax.broadcasted_iota(I32, (1, past), 1)
    rel_n = t_row - lax.broadcasted_iota(I32, (1, LANE), 1)
    valid_c, valid_n = [], []
    for window, dil in DILATED_CONFIGS:
        span = window // dil
        shift = int(math.log2(dil))
        for rel, dst in ((rel_c, valid_c), (rel_n, valid_n)):
            dst.append((rel >= 0) & ((rel & (dil - 1)) == 0) & ((rel >> shift) <= span))
    zpad = jnp.zeros((LANE - rows, MIX), F32)
    for s in range(seqs):
        rs = slice(s * rows, (s + 1) * rows)
        q = q_ref[rs, :]
        qm = jnp.concatenate([jnp.where(head == h, q, 0.0) for h in range(N_HEADS)], axis=0).astype(BF16)
        ck = ck_ref[0, s].astype(BF16)
        cv = cv_ref[0, s].astype(BF16)
        kn = jnp.concatenate([k_ref[rs, :], zpad], axis=0).astype(BF16)
        vn = jnp.concatenate([v_ref[rs, :], zpad], axis=0).astype(BF16)
        s_c = _dot(qm, ck) * scale
        s_n = _dot_nt(qm, kn) * scale
        pcs, pns, ls, lses = [], [], [], []
        for ci in range(len(DILATED_CONFIGS)):
            sc = jnp.where(valid_c[ci], s_c, NEG_INF)
            sn = jnp.where(valid_n[ci], s_n, NEG_INF)
            m = jnp.maximum(jnp.max(sc, axis=-1, keepdims=True), jnp.max(sn, axis=-1, keepdims=True))
            pc = jnp.exp(sc - m)
            pn = jnp.exp(sn - m)
            l = jnp.sum(pc, axis=-1, keepdims=True) + jnp.sum(pn, axis=-1, keepdims=True)
            pcs.append(pc.astype(BF16))
            pns.append(pn.astype(BF16))
            ls.append(l)
            lses.append(m + jnp.log(l))
        o_all = _dot_nt(jnp.concatenate(pcs, axis=0), cv) + _dot(jnp.concatenate(pns, axis=0), vn)
        outs = [o_all[ci * hr:(ci + 1) * hr, :] / ls[ci] for ci in range(len(DILATED_CONFIGS))]
        mx = jnp.maximum(jnp.maximum(lses[0], lses[1]), lses[2])
        es = [jnp.exp(le - mx) for le in lses]
        y_all = (es[0] * outs[0] + es[1] * outs[1] + es[2] * outs[2]) / (es[0] + es[1] + es[2])
        y = jnp.zeros((rows, MIX), F32)
        for h in range(N_HEADS):
            y = jnp.where(head == h, y_all[h * rows:(h + 1) * rows, :], y)
        y_ref[rs, :] = y


def _swa_sample(h, mix, cache_k, cache_v, layer, *, rows, n_seq, seqs, row0):
    past = cache_k.shape[3]
    blk = seqs * rows
    rb0 = row0 // blk

    def hspec(col):
        return pl.BlockSpec((blk, MIX), lambda b: (rb0 + b, col))

    cspec = pl.BlockSpec((1, seqs, MIX, past), lambda b: (layer, b, 0, 0))
    nbytes = 2 * (2 * seqs * past * MIX * 4 + 4 * blk * MIX * 4) + 16 * N_HEADS * rows * past * 4 + 2 * past * MIX * 2
    return pl.pallas_call(
        functools.partial(_swa_sample_body, past=past, rows=rows, seqs=seqs),
        grid=(n_seq // seqs,),
        in_specs=[hspec(COL_SQ), hspec(COL_SK), hspec(COL_SV), cspec, cspec,
                  pl.BlockSpec(memory_space=pl.ANY)],
        out_specs=pl.BlockSpec((blk, MIX), lambda b: (rb0 + b, 1)),
        out_shape=jax.ShapeDtypeStruct(mix.shape, F32),
        input_output_aliases={5: 0},
        compiler_params=_params(("arbitrary",), nbytes),
        name="swa_sample",
    )(h, h, h, cache_k, cache_v, mix)


def _conv_body(val_ref, gate_ref, buf_ref, w_ref, cb_ref, g_ref, b_ref, mix_ref,
               y_ref, glu_ref, ctx, *, rows, n_tiles):
    del mix_ref
    t = pl.program_id(1)

    @pl.when(t == 0)
    def _():
        ctx[0:CONV_CTX, :] = buf_ref[0]

    glu = val_ref[...] * _sigmoid(gate_ref[...])
    glu_ref[...] = glu
    ctx[CONV_CTX:CONV_CTX + rows, :] = glu
    lead = CONV_CTX - (CONV_WIDTH - 1)
    sub = min(rows, CONV_SUB)
    for base in range(0, rows, sub):
        window = ctx[base:base + sub + CONV_CTX, :]
        acc = jnp.zeros((sub, MIX), F32)
        for r in range(SUBLANE):
            shifted = pltpu.roll(window, sub + CONV_CTX - (lead + r), 0)
            for j in range(r, CONV_WIDTH, SUBLANE):
                acc = acc + shifted[j - r:j - r + sub, :] * w_ref[j:j + 1, :]
        z = _layer_norm(acc + cb_ref[...], g_ref[...], b_ref[...])
        y_ref[base:base + sub, :] = z * _sigmoid(z)
    if n_tiles > 1:
        ctx[0:CONV_CTX, :] = ctx[rows:rows + CONV_CTX, :]


def _conv(h, mix, buf, w, cb, g, b, *, rows, n_seq, n_tiles, row0):
    rb0 = row0 // rows
    rmap = lambda col: (lambda s, t: (rb0 + s * n_tiles + t, col))
    const = lambda shape: pl.BlockSpec(shape, lambda s, t: (0, 0))
    nbytes = 2 * 4 * rows * MIX * 4 + (CONV_CTX + rows) * MIX * 4 + 8 * rows * MIX * 4
    return pl.pallas_call(
        functools.partial(_conv_body, rows=rows, n_tiles=n_tiles),
        grid=(n_seq, n_tiles),
        in_specs=[pl.BlockSpec((rows, MIX), rmap(COL_CV)), pl.BlockSpec((rows, MIX), rmap(COL_CG)),
                  pl.BlockSpec((1, CONV_CTX, MIX), lambda s, t: (s, 0, 0)),
                  const((CONV_CTX, MIX)), const((1, MIX)), const((1, MIX)), const((1, MIX)),
                  pl.BlockSpec(memory_space=pl.ANY)],
        out_specs=[pl.BlockSpec((rows, MIX), rmap(2)),
                   pl.BlockSpec((rows, MIX), lambda s, t: (s * n_tiles + t, 0))],
        out_shape=[jax.ShapeDtypeStruct(mix.shape, F32),
                   jax.ShapeDtypeStruct((n_seq * n_tiles * rows, MIX), F32)],
        scratch_shapes=[pltpu.VMEM((CONV_CTX + rows, MIX), F32)],
        input_output_aliases={7: 0},
        compiler_params=_params(("arbitrary", "arbitrary"), nbytes),
        name="conv",
    )(h, h, buf, w, cb, g, b, mix)


def _gelu_tanh(x):
    c = math.sqrt(2.0 / math.pi)
    return x * (0.5 * (1.0 + jnp.tanh(c * (x + 0.044715 * (x * x * x)))))


def _sgu_body(u_ref, v_ref, g_ref, b_ref, ws_ref, bias_ref, avg_ref, mix_ref, y_ref, *vn_ref, rows, sub):
    del mix_ref
    tri = (lax.broadcasted_iota(I32, (SGU_CHUNK, SGU_CHUNK), 0)
           >= lax.broadcasted_iota(I32, (SGU_CHUNK, SGU_CHUNK), 1))
    wms = [jnp.where(tri, ws_ref[gi], 0.0).astype(BF16) for gi in range(N_HEADS)]
    head = _lane_head(MIX)
    for c in range(rows // sub):
        rs = slice(c * sub, (c + 1) * sub)
        u = _gelu_tanh(u_ref[rs, :])
        v = _gelu_tanh(v_ref[rs, :])
        vn = _group_norm(v, avg_ref[...]) * g_ref[...] + b_ref[...]
        if vn_ref:
            vn_ref[0][rs, :] = vn
        if sub < SGU_CHUNK:
            vn = jnp.concatenate([vn, jnp.zeros((SGU_CHUNK - sub, MIX), F32)], axis=0)
        vnb = vn.astype(BF16)
        z = jnp.zeros((SGU_CHUNK, MIX), F32)
        for gi in range(N_HEADS):
            z = jnp.where(head == gi, _dot(wms[gi], vnb), z)
        z = z + bias_ref[...]
        y_ref[rs, :] = u * z[0:sub, :]


def _sgu(h, mix, g, b, ws, bias, avg, *, rows, sub, n_steps, row0, emit_vn):
    rb0 = row0 // rows
    rmap = lambda col: (lambda c: (rb0 + c, col))
    const = lambda shape: pl.BlockSpec(shape, lambda c: (0,) * len(shape))
    nbytes = 2 * (4 * rows * MIX * 4 + N_HEADS * SGU_CHUNK * SGU_CHUNK * 4 + SGU_CHUNK * MIX * 4) + 8 * SGU_CHUNK * MIX * 4
    out_specs = [pl.BlockSpec((rows, MIX), rmap(3))]
    out_shape = [jax.ShapeDtypeStruct(mix.shape, F32)]
    if emit_vn:
        out_specs.append(pl.BlockSpec((rows, MIX), lambda c: (c, 0)))
        out_shape.append(jax.ShapeDtypeStruct((n_steps * rows, MIX), F32))
    return pl.pallas_call(
        functools.partial(_sgu_body, rows=rows, sub=sub),
        grid=(n_steps,),
        in_specs=[pl.BlockSpec((rows, MIX), rmap(COL_GU)), pl.BlockSpec((rows, MIX), rmap(COL_GV)),
                  const((1, MIX)), const((1, MIX)), const((N_HEADS, SGU_CHUNK, SGU_CHUNK)),
                  const((SGU_CHUNK, MIX)), const((MIX, MIX)), pl.BlockSpec(memory_space=pl.ANY)],
        out_specs=out_specs,
        out_shape=out_shape,
        input_output_aliases={7: 0},
        compiler_params=_params(("arbitrary",), nbytes),
        name="sgu",
    )(h, h, g, b, ws, bias, avg, mix)


def _out_router_body(mix_ref, w_ref, x_ref, g_ref, b_ref, wr_ref, br_ref,
                     x1_ref, route_ref, counts_ref, wbf, wrbf, logit_scr, *, alpha):
    @pl.when(pl.program_id(0) == 0)
    def _():
        wbf[...] = w_ref[0].astype(BF16)
        wrbf[...] = wr_ref[...].astype(BF16)

    parts = [slice(r0, r0 + OUT_ROWS) for r0 in range(0, TM, OUT_ROWS)]
    ys = [_dot(mix_ref[rs, :].astype(BF16), wbf[...]) for rs in parts]
    for rs, y in zip(parts, ys):
        x1 = _layer_norm(alpha * x_ref[rs, :] + y, g_ref[...], b_ref[...])
        x1_ref[rs, :] = x1
        logit_scr[:, rs] = _dot_nt(wrbf[...], x1.astype(BF16)) + br_ref[...]
    logits = logit_scr[0:ROUTE_ROWS, :]
    row = lax.broadcasted_iota(I32, (ROUTE_ROWS, 1), 0).astype(F32)

    def first_max(mask, vals):
        masked = jnp.where(mask, vals, NEG_INF)
        top = jnp.max(masked, axis=0, keepdims=True)
        idx = jnp.min(jnp.where(mask & (masked == top), row, float(LANE)), axis=0, keepdims=True)
        return top, idx

    gmask = row < N_GROUPS
    gmax, gsel = first_max(gmask, logits)
    g_gate = 1.0 / jnp.sum(jnp.where(gmask, jnp.exp(logits - gmax), 0.0), axis=0, keepdims=True)
    lo = N_GROUPS + EXPERTS_PER_GROUP * gsel
    emask = (row >= lo) & (row < lo + EXPERTS_PER_GROUP)
    v1, i1 = first_max(emask, logits)
    v2, i2 = first_max(emask & (row != i1), logits)
    e12 = jnp.exp(v2 - v1)
    both = (row == i1).astype(F32) + (row == i2).astype(F32)

    earlier = (lax.broadcasted_iota(I32, (TD, TD), 0) < lax.broadcasted_iota(I32, (TD, TD), 1)).astype(BF16)
    rank_a, rank_b = [], []
    for td in range(TM // TD):
        cs = slice(td * TD, (td + 1) * TD)
        tile = both[:, cs]
        counts_ref[td] = jnp.sum(tile, axis=1, keepdims=True)
        before = _dot(tile.astype(BF16), earlier)
        rank_a.append(jnp.sum(jnp.where(row == i1[:, cs], before, 0.0), axis=0, keepdims=True))
        rank_b.append(jnp.sum(jnp.where(row == i2[:, cs], before, 0.0), axis=0, keepdims=True))
    values = (i1 - N_GROUPS, i2 - N_GROUPS, g_gate * (1.0 / (1.0 + e12)), g_gate * (e12 / (1.0 + e12)),
              jnp.concatenate(rank_a, axis=1), jnp.concatenate(rank_b, axis=1))
    out_row = lax.broadcasted_iota(I32, (SUBLANE, 1), 0)
    route = jnp.zeros((SUBLANE, TM), F32)
    for k, val in enumerate(values):
        route = jnp.where(out_row == k, val, route)
    route_ref[...] = route


def _out_router(mix, w_out, layer, x, g, b, wr, br, *, alpha):
    nt, d = x.shape
    kdim = mix.shape[1]
    const = lambda shape: pl.BlockSpec(shape, lambda i: (0, 0))
    row = lambda width: pl.BlockSpec((TM, width), lambda i: (i, 0))
    nbytes = 2 * (kdim * d * 4 + 2 * d * LANE * 4 + TM * kdim * 4 + 2 * TM * d * 4 + TM * LANE * 4) + kdim * d * 2
    return pl.pallas_call(
        functools.partial(_out_router_body, alpha=alpha),
        grid=(nt // TM,),
        in_specs=[row(kdim), pl.BlockSpec((1, kdim, d), lambda i: (layer, 0, 0)), row(d),
                  const((1, d)), const((1, d)), const((LANE, d)), const((LANE, 1))],
        out_specs=[row(d), pl.BlockSpec((SUBLANE, TM), lambda i: (0, i)),
                   pl.BlockSpec((TM // TD, ROUTE_ROWS, 1), lambda i: (i, 0, 0))],
        out_shape=[jax.ShapeDtypeStruct((nt, d), F32), jax.ShapeDtypeStruct((SUBLANE, nt), F32),
                   jax.ShapeDtypeStruct((nt // TD, ROUTE_ROWS, 1), F32)],
        scratch_shapes=[pltpu.VMEM((kdim, d), BF16), pltpu.VMEM((LANE, d), BF16), pltpu.VMEM((LANE, TM), F32)],
        compiler_params=_params(("arbitrary",), nbytes),
        name="out_router",
    )(mix, w_out, x, g, b, wr, br)


def _local_slots(route, off_col):
    expert = lax.broadcasted_iota(I32, (N_EXPERTS, 1), 0).astype(F32)
    slots = []
    for kk in range(TOP_K):
        start = jnp.sum(jnp.where(expert == route[kk:kk + 1, :], off_col, 0.0), axis=0, keepdims=True)
        slots.append(start + route[4 + kk:5 + kk, :])
    return slots


def _chunk_copy(hbm, hbm_row, loc, chunk, sem, *, to_hbm):
    vm = loc.at[pl.ds(pl.multiple_of(chunk * MOE_CHUNK, MOE_CHUNK), MOE_CHUNK), :]
    hb = hbm.at[pl.ds(pl.multiple_of(hbm_row, MOE_CHUNK), MOE_CHUNK), :]
    return pltpu.make_async_copy(vm, hb, sem) if to_hbm else pltpu.make_async_copy(hb, vm, sem)


def _wait_chunks(hbm, buf, sem, count, *, to_hbm):
    def wait(c, carry):
        _chunk_copy(hbm, 0, buf, 0, sem, to_hbm=to_hbm).wait()
        return carry

    lax.fori_loop(0, count, wait, 0)


def _dispatch_body(dst_tab, n_chunks, x_ref, route_ref, off_ref, xs_in, xs_hbm, loc, sem):
    del xs_in
    i = pl.program_id(0)
    last = pl.num_programs(0) - 1
    cur = i % 2
    slot_a, slot_b = _local_slots(route_ref[...], off_ref[0])
    buf_row = lax.broadcasted_iota(I32, (LOC_ROWS, 1), 0).astype(F32)
    onehot = ((buf_row == slot_a) | (buf_row == slot_b)).astype(BF16)
    loc[cur] = _dot(onehot, x_ref[...].astype(BF16))

    def start(c, carry):
        _chunk_copy(xs_hbm, dst_tab[i * MAX_CHUNKS + c], loc.at[cur], c, sem.at[cur], to_hbm=True).start()
        return carry

    lax.fori_loop(0, n_chunks[i], start, 0)

    @pl.when(i > 0)
    def _():
        _wait_chunks(xs_hbm, loc.at[1 - cur], sem.at[1 - cur], n_chunks[jnp.maximum(i - 1, 0)], to_hbm=True)

    @pl.when(i == last)
    def _():
        _wait_chunks(xs_hbm, loc.at[cur], sem.at[cur], n_chunks[i], to_hbm=True)


def _dispatch(dst_tab, n_chunks, x1, route, off, xs_prev):
    nt, d = x1.shape
    row = lambda width: pl.BlockSpec((TD, width), lambda i, *_: (i, 0))
    nbytes = 2 * (TD * d * 4 + TD * LANE * 4) + 2 * LOC_ROWS * d * 4 + 4 * TD * LOC_ROWS * 4 + LOC_ROWS * d * 4
    return pl.pallas_call(
        _dispatch_body,
        grid_spec=pltpu.PrefetchScalarGridSpec(
            num_scalar_prefetch=2,
            grid=(nt // TD,),
            in_specs=[row(d), pl.BlockSpec((SUBLANE, TD), lambda i, *_: (0, i)),
                      pl.BlockSpec((1, N_EXPERTS, 1), lambda i, *_: (i, 0, 0)),
                      pl.BlockSpec(memory_space=pl.ANY)],
            out_specs=pl.BlockSpec(memory_space=pl.ANY),
            scratch_shapes=[pltpu.VMEM((2, LOC_ROWS, d), F32), pltpu.SemaphoreType.DMA((2,))]),
        out_shape=jax.ShapeDtypeStruct(xs_prev.shape, F32),
        input_output_aliases={5: 0},
        compiler_params=_params(("arbitrary",), nbytes),
        name="moe_dispatch",
    )(dst_tab, n_chunks, x1, route, off, xs_prev)


def _ffn_body(blk_e, blk_valid, blk_src, x_ref, w1_ref, w3_ref, w2_ref, y_ref, w1b, w3b, w2b):
    del blk_src
    j = pl.program_id(0)

    @pl.when(blk_valid[j] > 0)
    def _():
        @pl.when((j == 0) | (blk_e[j] != blk_e[jnp.maximum(j - 1, 0)]))
        def _():
            w1b[...] = w1_ref[0, 0].astype(BF16)
            w3b[...] = w3_ref[0, 0].astype(BF16)
            w2b[...] = w2_ref[0, 0].astype(BF16)

        xb = x_ref[...].astype(BF16)
        h1 = _dot(xb, w1b[...])
        h3 = _dot(xb, w3b[...])
        act = (h1 * _sigmoid(h1)) * h3
        y_ref[...] = _dot(act.astype(BF16), w2b[...])


def _ffn(xs, w1, w3, w2, layer, blk_e, blk_valid, blk_src):
    d = xs.shape[1]
    de = w1.shape[3]
    n_blk = xs.shape[0] // MOE_BLK
    wmap = lambda j, be, bv, bs: (layer, be[j], 0, 0)
    nbytes = 2 * (3 * d * de * 4 + 2 * MOE_BLK * d * 4) + 3 * d * de * 2 + 6 * MOE_BLK * de * 4
    return pl.pallas_call(
        _ffn_body,
        grid_spec=pltpu.PrefetchScalarGridSpec(
            num_scalar_prefetch=3,
            grid=(n_blk,),
            in_specs=[pl.BlockSpec((MOE_BLK, d), lambda j, be, bv, bs: (bs[j], 0)),
                      pl.BlockSpec((1, 1, d, de), wmap), pl.BlockSpec((1, 1, d, de), wmap),
                      pl.BlockSpec((1, 1, de, d), wmap)],
            out_specs=pl.BlockSpec((MOE_BLK, d), lambda j, be, bv, bs: (bs[j], 0)),
            scratch_shapes=[pltpu.VMEM((d, de), BF16), pltpu.VMEM((d, de), BF16), pltpu.VMEM((de, d), BF16)]),
        out_shape=jax.ShapeDtypeStruct(xs.shape, F32),
        input_output_aliases={3: 0},
        compiler_params=_params(("arbitrary",), nbytes),
        name="expert_ffn",
    )(blk_e, blk_valid, blk_src, xs, w1, w3, w2)


def _combine_body(dst_tab, n_chunks, x_ref, route_ref, off_ref, g_ref, b_ref, ys_hbm, x2_ref,
                  loc, sem, *, alpha):
    i = pl.program_id(0)
    last = pl.num_programs(0) - 1
    cur = i % 2

    def fetch(tile, buf):
        def start(c, carry):
            _chunk_copy(ys_hbm, dst_tab[tile * MAX_CHUNKS + c], loc.at[buf], c, sem.at[buf], to_hbm=False).start()
            return carry

        lax.fori_loop(0, n_chunks[tile], start, 0)

    @pl.when(i == 0)
    def _():
        loc[...] = jnp.zeros_like(loc)
        fetch(0, 0)

    @pl.when(i < last)
    def _():
        fetch(jnp.minimum(i + 1, last), 1 - cur)

    route = route_ref[...]
    slot_a, slot_b = _local_slots(route, off_ref[0])
    buf_row = lax.broadcasted_iota(I32, (LOC_ROWS, 1), 0).astype(F32)
    gates = (jnp.where(buf_row == slot_a, route[2:3, :], 0.0)
             + jnp.where(buf_row == slot_b, route[3:4, :], 0.0))
    _wait_chunks(ys_hbm, loc.at[cur], sem.at[cur], n_chunks[i], to_hbm=False)
    f = _dot_tn(gates.astype(BF16), loc[cur].astype(BF16))
    x2_ref[...] = _layer_norm(alpha * x_ref[...] + f, g_ref[...], b_ref[...])


def _combine(dst_tab, n_chunks, x1, route, off, g, b, ys, *, alpha):
    nt, d = x1.shape
    row = lambda width: pl.BlockSpec((TD, width), lambda i, dt, nc: (i, 0))
    const = lambda shape: pl.BlockSpec(shape, lambda i, dt, nc: (0, 0))
    nbytes = 2 * (2 * TD * d * 4 + TD * LANE * 4) + LOC_ROWS * d * 4 + 4 * TD * LOC_ROWS * 4 + LOC_ROWS * d * 2
    return pl.pallas_call(
        functools.partial(_combine_body, alpha=alpha),
        grid_spec=pltpu.PrefetchScalarGridSpec(
            num_scalar_prefetch=2,
            grid=(nt // TD,),
            in_specs=[row(d), pl.BlockSpec((SUBLANE, TD), lambda i, dt, nc: (0, i)),
                      pl.BlockSpec((1, N_EXPERTS, 1), lambda i, dt, nc: (i, 0, 0)),
                      const((1, d)), const((1, d)), pl.BlockSpec(memory_space=pl.ANY)],
            out_specs=row(d),
            scratch_shapes=[pltpu.VMEM((2, LOC_ROWS, d), F32), pltpu.SemaphoreType.DMA((2,))]),
        out_shape=jax.ShapeDtypeStruct((nt, d), F32),
        compiler_params=_params(("arbitrary",), nbytes + LOC_ROWS * d * 4),
        name="combine_ln",
    )(dst_tab, n_chunks, x1, route, off, g, b, ys)


def _rope_tables(pos, rot_dim, theta):
    half = rot_dim // 2
    inv = jnp.float32(theta) ** (-jnp.arange(half, dtype=F32) * (2.0 / rot_dim))
    ang = pos.astype(F32)[:, None] * inv[None, :]
    c, s = jnp.cos(ang), jnp.sin(ang)
    rest = HEAD_DIM - rot_dim
    ones = jnp.ones((pos.shape[0], rest), F32)
    zeros = jnp.zeros((pos.shape[0], rest), F32)
    cos_h = jnp.concatenate([c, c, ones], axis=1)
    sin_h = jnp.concatenate([-s, s, zeros], axis=1)
    return jnp.tile(cos_h, (1, N_HEADS)), jnp.tile(sin_h, (1, N_HEADS))


def _block_diag(s):
    rows = [jnp.pad(s[:, h], ((0, 0), (0, 0), (h * HEAD_DIM, MIX - (h + 1) * HEAD_DIM))) for h in range(N_HEADS)]
    return jnp.concatenate(rows, axis=1)


def _diag_blocks(s):
    return jnp.stack([s[:, h * HEAD_DIM:(h + 1) * HEAD_DIM, h * HEAD_DIM:(h + 1) * HEAD_DIM]
                      for h in range(N_HEADS)], axis=1)


def _routing_plan(tile_counts, n_blk):
    cnt_t = tile_counts[:, N_GROUPS:N_GROUPS + N_EXPERTS, 0].astype(I32)
    run_chunks = (cnt_t + MOE_CHUNK - 1) // MOE_CHUNK
    run_rows = run_chunks * MOE_CHUNK
    cnt = jnp.sum(run_rows, axis=0)
    pcnt = (cnt + MOE_BLK - 1) // MOE_BLK * MOE_BLK
    pstart = jnp.cumsum(pcnt) - pcnt
    n_act = jnp.sum(pcnt) // MOE_BLK
    gstart = pstart[None, :] + jnp.cumsum(run_rows, axis=0) - run_rows
    chunk0 = jnp.cumsum(run_chunks, axis=1) - run_chunks
    n_chunks = jnp.sum(run_chunks, axis=1).astype(I32)

    def spread(first, count, value, n):
        p = jnp.arange(n, dtype=I32)[:, None]
        f, c, v = first[..., None, :], count[..., None, :], value[..., None, :]
        return jnp.sum(jnp.where((f <= p) & (p < f + c), v + (p - f) * MOE_CHUNK, 0), axis=-1)

    dst_tab = spread(chunk0, run_chunks, gstart, MAX_CHUNKS)
    off = (chunk0 * MOE_CHUNK).astype(F32)[:, :, None]
    pos = jnp.arange(n_blk, dtype=I32)[:, None] * MOE_BLK
    mine = (pstart[None, :] <= pos) & (pos < (pstart + pcnt)[None, :])
    experts = jnp.arange(N_EXPERTS, dtype=I32)[None, :]
    last_e = jnp.max(jnp.where(cnt > 0, experts[0], 0))
    active = pos[:, 0] < n_act * MOE_BLK
    blk_e = jnp.where(active, jnp.sum(jnp.where(mine, experts, 0), axis=-1), last_e)
    blk_valid = jnp.sum(jnp.where(mine, jnp.clip((pstart + cnt)[None, :] - pos, 0, MOE_BLK), 0), axis=-1)
    blk_src = jnp.minimum(jnp.arange(n_blk, dtype=I32), jnp.maximum(n_act - 1, 0))
    i32 = lambda t: t.astype(I32)
    return i32(dst_tab).reshape(-1), n_chunks, off, i32(blk_e), i32(blk_valid), i32(blk_src)


def kernel(x_prompt, x_sample, state_ret, cache_swa_k, cache_swa_v, state_conv, w_in, w_out, ret_gn_g,
           conv_w, conv_b, conv_ln_g, conv_ln_b, sgu_ln_g, sgu_ln_b, sgu_w, sgu_b, ln1_g, ln1_b,
           ln2_g, ln2_b, router_g_w, router_g_b, router_e_w, router_e_b, moe_w1, moe_w3, moe_w2):
    depth = w_in.shape[0]
    n_seq, seq, d = x_prompt.shape
    n_dec, t_new, _ = x_sample.shape
    past = cache_swa_k.shape[2]
    rows_s = -(-t_new // SUBLANE) * SUBLANE
    n_p = n_seq * seq
    n_s = n_dec * rows_s
    n_tail = -(-n_s // TM) * TM
    nt = n_p + n_tail
    assert seq % SWA_SB == 0 and seq % CONV_TT == 0 and seq % TM == 0 and TM % TD == 0
    assert t_new % RET_CHUNK != 0 and t_new <= SGU_CHUNK and w_in.shape[2] == N_COLBLK * MIX
    assert seq % SGU_ROWS == 0 and seq % RET_ROWS == 0 and n_dec % SAMPLE_SEQS == 0
    assert n_p % (SAMPLE_SEQS * rows_s) == 0
    alpha = (2 * depth) ** 0.25
    keep = min(DILATED_CONFIGS[-1][0], seq)
    n_blk = -(-(nt * TOP_K + (nt // TD) * N_EXPERTS * (MOE_CHUNK - 1) + N_EXPERTS * (MOE_BLK - 1)) // MOE_BLK)
    cache_k = jnp.transpose(cache_swa_k, (0, 1, 3, 4, 2)).reshape(depth, n_dec, MIX, past)
    cache_v = jnp.transpose(cache_swa_v, (0, 1, 3, 4, 2)).reshape(depth, n_dec, MIX, past)

    x = jnp.concatenate([x_prompt.reshape(n_p, d),
                         jnp.pad(x_sample, ((0, 0), (0, rows_s - t_new), (0, 0))).reshape(n_s, d),
                         jnp.zeros((n_tail - n_s, d), F32)], axis=0)
    pos = jnp.concatenate([jnp.arange(seq, dtype=I32),
                           PAST_LEN + jnp.arange(n_tail, dtype=I32) % rows_s])
    rope = _rope_tables(pos, HEAD_DIM, RET_THETA) + _rope_tables(pos, ROPE_DIM, ROPE_THETA)
    ret_tabs_p = _retention_tables(RET_CHUNK, RET_CHUNK)
    ret_tabs_s = _retention_tables(t_new, rows_s)
    lane_head = jnp.arange(MIX) // HEAD_DIM
    avg = ((lane_head[:, None] == lane_head[None, :]).astype(F32) / HEAD_DIM).astype(BF16)
    zero_state = jnp.zeros((n_seq, MIX, MIX), F32)
    zero_ctx = jnp.zeros((n_seq, CONV_CTX, MIX), F32)
    row2 = lambda v: v.reshape(1, -1)
    mix = jnp.zeros((nt, 4 * MIX), F32)
    slots = jnp.zeros((n_blk * MOE_BLK, d), F32)

    outs = [[] for _ in range(9)]
    for l in range(depth):
        h = _in_proj(x, w_in, l, rope, n_p // TM, seq // TM)
        gn = row2(ret_gn_g[l])
        mix, s_p = _retention(h, mix, zero_state, ret_tabs_p, avg, gn, rows=RET_CHUNK, n_seq=n_seq,
                              n_chunks=seq // RET_CHUNK, seqs=1, chunks=RET_ROWS // RET_CHUNK, row0=0)
        mix, s_s = _retention(h, mix, _block_diag(state_ret[l]), ret_tabs_s, avg, gn, rows=rows_s, n_seq=n_dec,
                              n_chunks=1, seqs=SAMPLE_SEQS, chunks=1, row0=n_p)
        mix = _swa_prompt(h, mix, seq=seq, n_seq=n_seq)
        mix = _swa_sample(h, mix, cache_k, cache_v, l, rows=rows_s, n_seq=n_dec, seqs=SWA_SAMPLE_SEQS, row0=n_p)
        cw = jnp.pad(conv_w[l], ((0, CONV_CTX - CONV_WIDTH), (0, 0)))
        cargs = (cw, row2(conv_b[l]), row2(conv_ln_g[l]), row2(conv_ln_b[l]))
        mix, glu_p = _conv(h, mix, zero_ctx, *cargs, rows=CONV_TT, n_seq=n_seq, n_tiles=seq // CONV_TT, row0=0)
        ctx_s = jnp.pad(state_conv[l], ((0, 0), (CONV_CTX - (CONV_WIDTH - 1), 0), (0, 0)))
        mix, glu_s = _conv(h, mix, ctx_s, *cargs, rows=rows_s, n_seq=n_dec, n_tiles=1, row0=n_p)
        sbias = jnp.repeat(sgu_b[l].T, HEAD_DIM, axis=1)
        sargs = (row2(sgu_ln_g[l]), row2(sgu_ln_b[l]), sgu_w[l], sbias, avg)
        mix, = _sgu(h, mix, *sargs, rows=SGU_ROWS, sub=SGU_CHUNK, n_steps=n_p // SGU_ROWS, row0=0, emit_vn=False)
        mix, vn_s = _sgu(h, mix, *sargs, rows=SAMPLE_SEQS * rows_s, sub=rows_s, n_steps=n_dec // SAMPLE_SEQS,
                         row0=n_p, emit_vn=True)

        wr = jnp.concatenate([router_g_w[l].T, jnp.transpose(router_e_w[l], (0, 2, 1)).reshape(N_EXPERTS, d),
                              jnp.zeros((LANE - N_GROUPS - N_EXPERTS, d), F32)], axis=0)
        br = jnp.concatenate([router_g_b[l], router_e_b[l].reshape(-1),
                              jnp.zeros((LANE - N_GROUPS - N_EXPERTS,), F32)]).reshape(LANE, 1)
        x1, route, counts = _out_router(mix, w_out, l, x, row2(ln1_g[l]), row2(ln1_b[l]), wr, br, alpha=alpha)
        dst_tab, n_chunks, off, blk_e, blk_valid, blk_src = _routing_plan(counts, n_blk)
        slots = _dispatch(dst_tab, n_chunks, x1, route, off, slots)
        slots = _ffn(slots, moe_w1, moe_w3, moe_w2, l, blk_e, blk_valid, blk_src)
        x = _combine(dst_tab, n_chunks, x1, route, off, row2(ln2_g[l]), row2(ln2_b[l]), slots, alpha=alpha)

        def tail(col, n_keep):
            return jnp.stack([lax.slice(h, ((b + 1) * seq - n_keep, col * MIX), ((b + 1) * seq, (col + 1) * MIX))
                              for b in range(n_seq)])

        def new_rows(col):
            blk = lax.slice(h, (n_p, col * MIX), (n_p + n_s, (col + 1) * MIX))
            return blk.reshape(n_dec, rows_s, MIX)[:, :t_new]

        heads = lambda t: t.reshape(t.shape[0], t.shape[1], N_HEADS, HEAD_DIM)
        outs[0].append(_diag_blocks(s_p))
        outs[1].append(heads(tail(COL_SK, keep)))
        outs[2].append(heads(tail(COL_SV, keep)))
        outs[3].append(glu_p.reshape(n_seq, seq, MIX)[:, seq - (CONV_WIDTH - 1):])
        outs[4].append(_diag_blocks(s_s))
        outs[5].append(heads(new_rows(COL_SK)))
        outs[6].append(heads(new_rows(COL_SV)))
        outs[7].append(glu_s.reshape(n_dec, rows_s, MIX)[:, :t_new])
        outs[8].append(vn_s.reshape(n_dec, rows_s, MIX)[:, :t_new])

    y_prompt = x[:n_p].reshape(n_seq, seq, d)
    y_sample = x[n_p:n_p + n_s].reshape(n_dec, rows_s, d)[:, :t_new]
    return (y_prompt, y_sample) + tuple(jnp.stack(o) for o in outs)
```

```python
import functools
import math

import jax
import jax.numpy as jnp
from jax import lax
from jax.experimental import pallas as pl
from jax.experimental.pallas import tpu as pltpu

F32 = jnp.float32
BF16 = jnp.bfloat16
I32 = jnp.int32
U32 = jnp.uint32

HEAD_DIM = 64
N_HEADS = 4
MIX = N_HEADS * HEAD_DIM
(COL_RQ, COL_RK, COL_RV, COL_RG, COL_SQ, COL_SK, COL_SV,
 COL_CV, COL_CG, COL_GU, COL_GV) = range(11)
N_COLBLK = 11
RET_CHUNK = 128
RET_THETA = 10000.0
ROPE_THETA = 500000.0
ROPE_DIM = HEAD_DIM // 4
DILATED_CONFIGS = ((128, 1), (512, 4), (2048, 16))
DIL_BLOCK = 128
CONV_WIDTH = 31
SGU_CHUNK = 128
N_GROUPS = 4
EXPERTS_PER_GROUP = 8
N_EXPERTS = N_GROUPS * EXPERTS_PER_GROUP
TOP_K = 2
PAST_LEN = 8192
LN_EPS = 1e-5
NEG_INF = -1e30

LANE = 128
SUBLANE = 8
VMEM_CAP_BYTES = 60000 * 1024
COMPILER_TEMP_BYTES = 12 * 1024 * 1024

TM = 512
TD = 256
OUT_ROWS = 128
ROUTE_ROWS = -(-(N_GROUPS + N_EXPERTS) // SUBLANE) * SUBLANE
MOE_BLK = 512
MOE_CHUNK = SUBLANE
LOC_ROWS = -(-(TD * TOP_K + N_EXPERTS * (MOE_CHUNK - 1)) // LANE) * LANE
MAX_CHUNKS = LOC_ROWS // MOE_CHUNK
CONV_TT = 512
CONV_CTX = 32
CONV_SUB = 128
SWA_SB = DIL_BLOCK * 16
SWA_TILE = 256
SWA_UNROLL = 4
SGU_ROWS = 1024
RET_ROWS = 1024
SAMPLE_SEQS = 8
SWA_SAMPLE_SEQS = 4


def _vmem(nbytes):
    return int(min(VMEM_CAP_BYTES, nbytes + COMPILER_TEMP_BYTES))


def _params(sem, nbytes):
    return pltpu.CompilerParams(dimension_semantics=sem, vmem_limit_bytes=_vmem(nbytes))


def _dot(a, b):
    return jnp.dot(a, b, preferred_element_type=F32)


def _dot_nt(a, b):
    return lax.dot_general(a, b, (((1,), (1,)), ((), ())), preferred_element_type=F32)


def _dot_tn(a, b):
    return lax.dot_general(a, b, (((0,), (0,)), ((), ())), preferred_element_type=F32)


def _split_dot(x, m):
    hi = x.astype(BF16)
    lo = (x - hi.astype(F32)).astype(BF16)
    return _dot(hi, m) + _dot(lo, m)


def _sigmoid(x):
    return 1.0 / (1.0 + jnp.exp(-x))


def _layer_norm(z, g, b):
    mu = jnp.mean(z, axis=-1, keepdims=True)
    d = z - mu
    var = jnp.mean(d * d, axis=-1, keepdims=True)
    return d * lax.rsqrt(var + LN_EPS) * g + b


def _group_norm(z, avg):
    mu = _split_dot(z, avg)
    d = z - mu
    var = _split_dot(d * d, avg)
    return d * lax.rsqrt(var + LN_EPS)


def _lane_head(width):
    return lax.broadcasted_iota(I32, (1, width), 1) // HEAD_DIM


def _rotate(x, cos, sin_signed, half):
    first = (lax.broadcasted_iota(I32, (1, LANE), 1) % HEAD_DIM) < half
    parts = []
    for p in range(x.shape[1] // LANE):
        t = x[:, p * LANE:(p + 1) * LANE]
        up = pltpu.roll(t, half, 1)
        down = pltpu.roll(t, LANE - half, 1)
        parts.append(jnp.where(first, down, up))
    return x * cos + jnp.concatenate(parts, axis=1) * sin_signed


def _in_proj_body(x_ref, w_ref, cr_ref, sr_ref, cs_ref, ss_ref, h_ref, wbf_ref):
    @pl.when(pl.program_id(0) == 0)
    def _():
        for j in range(N_COLBLK):
            wbf_ref[:, j * MIX:(j + 1) * MIX] = w_ref[0, :, j * MIX:(j + 1) * MIX].astype(BF16)

    xb = x_ref[...].astype(BF16)
    for j in range(N_COLBLK):
        hj = _dot(xb, wbf_ref[:, j * MIX:(j + 1) * MIX])
        if j in (COL_RQ, COL_RK):
            hj = _rotate(hj, cr_ref[...], sr_ref[...], HEAD_DIM // 2)
        if j == COL_RK:
            hj = hj * (HEAD_DIM ** -0.5)
        if j in (COL_SQ, COL_SK):
            hj = _rotate(hj, cs_ref[...], ss_ref[...], ROPE_DIM // 2)
        h_ref[:, j * MIX:(j + 1) * MIX] = hj


def _in_proj(x, w, layer, tabs, n_prompt_tiles, tiles_per_seq):
    nt, d = x.shape
    width = w.shape[2]

    def tab_map(i):
        return (jnp.where(i < n_prompt_tiles, i % tiles_per_seq, tiles_per_seq + i - n_prompt_tiles), 0)

    tab_spec = pl.BlockSpec((TM, MIX), tab_map)
    nbytes = d * width * 4 + d * width * 2 + 2 * TM * d * 4 + 8 * TM * MIX * 4 + 2 * TM * width * 4
    return pl.pallas_call(
        _in_proj_body,
        grid=(nt // TM,),
        in_specs=[pl.BlockSpec((TM, d), lambda i: (i, 0)),
                  pl.BlockSpec((1, d, width), lambda i: (layer, 0, 0), pipeline_mode=pl.Buffered(1)),
                  tab_spec, tab_spec, tab_spec, tab_spec],
        out_specs=pl.BlockSpec((TM, width), lambda i: (i, 0)),
        out_shape=jax.ShapeDtypeStruct((nt, width), F32),
        scratch_shapes=[pltpu.VMEM((d, width), BF16)],
        compiler_params=_params(("arbitrary",), nbytes),
        name="in_proj",
    )(x, w, *tabs)


def _retention_body(q_ref, k_ref, v_ref, g_ref, s0_ref, dec_ref, qdec_ref, kdec_ref,
                    cmat_ref, bdm_ref, avg_ref, gn_ref, mix_ref, y_ref, sout_ref, s_scr,
                    *, n_steps, seqs, chunks, rows):
    del mix_ref
    c = pl.program_id(1)

    @pl.when(c == 0)
    def _():
        s_scr[...] = s0_ref[...]

    head = _lane_head(MIX)
    for s in range(seqs):
        state = s_scr[s]
        for cc in range(chunks):
            rs = slice((s * chunks + cc) * rows, (s * chunks + cc + 1) * rows)
            q = q_ref[rs, :]
            k = k_ref[rs, :]
            vb = v_ref[rs, :].astype(BF16)
            kb = k.astype(BF16)
            o = _dot(q.astype(BF16), state.astype(BF16)) * qdec_ref[...]
            for h in range(N_HEADS):
                mh = head == h
                a = _dot_nt(jnp.where(mh, q, 0.0).astype(BF16), kb) * dec_ref[h]
                o = o + jnp.where(mh, _dot(a.astype(BF16), vb), 0.0)
            upd = _dot_tn((k * kdec_ref[...]).astype(BF16), vb)
            state = state * cmat_ref[...] + upd * bdm_ref[...]
            g = g_ref[rs, :]
            y_ref[rs, :] = _group_norm(o, avg_ref[...]) * gn_ref[...] * (g * _sigmoid(g))
        s_scr[s] = state

    @pl.when(c == n_steps - 1)
    def _():
        sout_ref[...] = s_scr[...]


def _retention_tables(l_real, l_pad):
    hh = jnp.arange(N_HEADS, dtype=F32)
    log_g = jnp.log1p(-jnp.exp2(-5.0 - hh))
    i = jnp.arange(l_pad, dtype=F32)
    rel = i[:, None] - i[None, :]
    dec = jnp.where(rel >= 0, jnp.exp(log_g[:, None, None] * jnp.maximum(rel, 0.0)), 0.0)
    qd = jnp.exp(log_g[:, None] * (i + 1.0))
    kd = jnp.where(i < l_real, jnp.exp(log_g[:, None] * (l_real - 1.0 - i)), 0.0)
    cd = jnp.exp(log_g * l_real)
    expand = lambda t: jnp.repeat(t.T, HEAD_DIM, axis=1)
    row_head = jnp.arange(MIX) // HEAD_DIM
    bdm = (row_head[:, None] == row_head[None, :]).astype(F32)
    cmat = cd[row_head][:, None] * bdm
    return dec, expand(qd), expand(kd), cmat, bdm


def _retention(h, mix, s0, tabs, avg, gn_g, *, rows, n_seq, n_chunks, seqs, chunks, row0):
    assert (seqs == 1 or chunks == n_chunks) and n_seq % seqs == 0 and n_chunks % chunks == 0
    blk = seqs * chunks * rows
    rb0 = row0 // blk
    n_steps = n_chunks // chunks
    dec, qdec, kdec, cmat, bdm = tabs

    def hspec(col):
        return pl.BlockSpec((blk, MIX), lambda b, c: (rb0 + b * n_steps + c, col))

    const2 = lambda shape: pl.BlockSpec(shape, lambda b, c: (0, 0))
    state_spec = pl.BlockSpec((seqs, MIX, MIX), lambda b, c: (b, 0, 0))
    in_specs = [hspec(COL_RQ), hspec(COL_RK), hspec(COL_RV), hspec(COL_RG), state_spec,
                pl.BlockSpec((N_HEADS, rows, rows), lambda b, c: (0, 0, 0)),
                const2((rows, MIX)), const2((rows, MIX)),
                const2((MIX, MIX)), const2((MIX, MIX)), const2((MIX, MIX)), const2((1, MIX)),
                pl.BlockSpec(memory_space=pl.ANY)]
    args = [h, h, h, h, s0, dec, qdec, kdec, cmat, bdm, avg, gn_g, mix]
    nbytes = (2 * (5 * blk * MIX * 4 + 2 * rows * MIX * 4 + (3 + 2 * seqs) * MIX * MIX * 4 + N_HEADS * rows * rows * 4)
              + seqs * MIX * MIX * 4)
    return pl.pallas_call(
        functools.partial(_retention_body, n_steps=n_steps, seqs=seqs, chunks=chunks, rows=rows),
        grid=(n_seq // seqs, n_steps),
        in_specs=in_specs,
        out_specs=[pl.BlockSpec((blk, MIX), lambda b, c: (rb0 + b * n_steps + c, 0)), state_spec],
        out_shape=[jax.ShapeDtypeStruct(mix.shape, F32),
                   jax.ShapeDtypeStruct((n_seq, MIX, MIX), F32)],
        scratch_shapes=[pltpu.VMEM((seqs, MIX, MIX), F32)],
        input_output_aliases={12: 0},
        compiler_params=_params(("arbitrary", "arbitrary"), nbytes),
        name="retention",
    )(*args)


def _swa_prompt_body(q_ref, k_ref, v_ref, mix_ref, y_ref, o0, o1, o2, l0, l1, l2, *, seq):
    del mix_ref
    o_scr = (o0, o1, o2)
    l_scr = (l0, l1, l2)
    head = _lane_head(LANE)
    qi = lax.broadcasted_iota(I32, (DIL_BLOCK, 1), 0)
    ki = lax.broadcasted_iota(I32, (1, 2 * DIL_BLOCK), 1)
    rel = DIL_BLOCK + qi - ki
    scale = HEAD_DIM ** -0.5
    for sb in range(seq // SWA_SB):
        for ci, (window, dil) in enumerate(DILATED_CONFIGS):
            span = window // dil
            band = (rel >= 0) & (rel <= span)

            def block(i, carry, sb=sb, ci=ci, dil=dil, band=band):
                nloc = i // dil
                off = nloc * (DIL_BLOCK * dil) + i % dil
                start = sb * SWA_SB + off
                if sb == 0:
                    first = nloc == 0
                    pstart = jnp.where(first, start, start - DIL_BLOCK * dil)
                    valid = band & (ki >= jnp.where(first, DIL_BLOCK, 0))
                else:
                    pstart = start - DIL_BLOCK * dil
                    valid = band
                stride = dil if dil > 1 else None
                cur = pl.ds(start, DIL_BLOCK, stride=stride)
                prev = pl.ds(pstart, DIL_BLOCK, stride=stride)
                qb = q_ref[cur, :]
                kb = jnp.concatenate([k_ref[prev, :], k_ref[cur, :]], axis=0).astype(BF16)
                vb = jnp.concatenate([v_ref[prev, :], v_ref[cur, :]], axis=0).astype(BF16)
                o = jnp.zeros((DIL_BLOCK, LANE), F32)
                ls = jnp.zeros((DIL_BLOCK, LANE), F32)
                for hh in range(LANE // HEAD_DIM):
                    mh = head == hh
                    s = _dot_nt(jnp.where(mh, qb, 0.0).astype(BF16), kb) * scale
                    s = jnp.where(valid, s, NEG_INF)
                    m = jnp.max(s, axis=-1, keepdims=True)
                    p = jnp.exp(s - m)
                    l = jnp.sum(p, axis=-1, keepdims=True)
                    o = jnp.where(mh, _dot(p.astype(BF16), vb) / l, o)
                    ls = jnp.where(mh, m + jnp.log(l), ls)
                dst = pl.ds(off, DIL_BLOCK, stride=stride)
                o_scr[ci][dst, :] = o
                l_scr[ci][dst, :] = ls
                return carry

            lax.fori_loop(0, SWA_SB // DIL_BLOCK, block, 0, unroll=SWA_UNROLL)

        def combine(t, carry, sb=sb):
            rows = pl.ds(t * SWA_TILE, SWA_TILE)
            la, lb, lc = l_scr[0][rows, :], l_scr[1][rows, :], l_scr[2][rows, :]
            mx = jnp.maximum(jnp.maximum(la, lb), lc)
            ea, eb, ec = jnp.exp(la - mx), jnp.exp(lb - mx), jnp.exp(lc - mx)
            num = ea * o_scr[0][rows, :] + eb * o_scr[1][rows, :] + ec * o_scr[2][rows, :]
            y_ref[pl.ds(sb * SWA_SB + t * SWA_TILE, SWA_TILE), :] = num / (ea + eb + ec)
            return carry

        lax.fori_loop(0, SWA_SB // SWA_TILE, combine, 0)


def _swa_prompt(h, mix, *, seq, n_seq):
    halves = MIX // LANE

    def hspec(col):
        return pl.BlockSpec((seq, LANE), lambda b, p: (b, col * halves + p))

    nbytes = 2 * 4 * seq * LANE * 4 + 6 * SWA_SB * LANE * 4
    return pl.pallas_call(
        functools.partial(_swa_prompt_body, seq=seq),
        grid=(n_seq, halves),
        in_specs=[hspec(COL_SQ), hspec(COL_SK), hspec(COL_SV), pl.BlockSpec(memory_space=pl.ANY)],
        out_specs=pl.BlockSpec((seq, LANE), lambda b, p: (b, halves + p)),
        out_shape=jax.ShapeDtypeStruct(mix.shape, F32),
        scratch_shapes=[pltpu.VMEM((SWA_SB, LANE), F32)] * 6,
        input_output_aliases={3: 0},
        compiler_params=_params(("arbitrary", "arbitrary"), nbytes),
        name="swa_prompt",
    )(h, h, h, mix)


def _swa_sample_body(q_ref, k_ref, v_ref, ck_ref, cv_ref, mix_ref, y_ref, *, past, rows, seqs):
    del mix_ref
    head = _lane_head(MIX)
    hr = N_HEADS * rows
    scale = HEAD_DIM ** -0.5
    t_row = lax.broadcasted_iota(I32, (hr, 1), 0) % rows
    rel_c = past + t_row - lax.broadcasted_iota(I32, (1, past), 1)
    rel_n = t_row - lax.broadcasted_iota(I32, (1, LANE), 1)
    valid_c, valid_n = [], []
    for window, dil in DILATED_CONFIGS:
        span = window // dil
        shift = int(math.log2(dil))
        for rel, dst in ((rel_c, valid_c), (rel_n, valid_n)):
            dst.append((rel >= 0) & ((rel & (dil - 1)) == 0) & ((rel >> shift) <= span))
    zpad = jnp.zeros((LANE - rows, MIX), F32)
    for s in range(seqs):
        rs = slice(s * rows, (s + 1) * rows)
        q = q_ref[rs, :]
        qm = jnp.concatenate([jnp.where(head == h, q, 0.0) for h in range(N_HEADS)], axis=0).astype(BF16)
        ck = ck_ref[0, s].astype(BF16)
        cv = cv_ref[0, s].astype(BF16)
        kn = jnp.concatenate([k_ref[rs, :], zpad], axis=0).astype(BF16)
        vn = jnp.concatenate([v_ref[rs, :], zpad], axis=0).astype(BF16)
        s_c = _dot(qm, ck) * scale
        s_n = _dot_nt(qm, kn) * scale
        pcs, pns, ls, lses = [], [], [], []
        for ci in range(len(DILATED_CONFIGS)):
            sc = jnp.where(valid_c[ci], s_c, NEG_INF)
            sn = jnp.where(valid_n[ci], s_n, NEG_INF)
            m = jnp.maximum(jnp.max(sc, axis=-1, keepdims=True), jnp.max(sn, axis=-1, keepdims=True))
            pc = jnp.exp(sc - m)
            pn = jnp.exp(sn - m)
            l = jnp.sum(pc, axis=-1, keepdims=True) + jnp.sum(pn, axis=-1, keepdims=True)
            pcs.append(pc.astype(BF16))
            pns.append(pn.astype(BF16))
            ls.append(l)
            lses.append(m + jnp.log(l))
        o_all = _dot_nt(jnp.concatenate(pcs, axis=0), cv) + _dot(jnp.concatenate(pns, axis=0), vn)
        outs = [o_all[ci * hr:(ci + 1) * hr, :] / ls[ci] for ci in range(len(DILATED_CONFIGS))]
        mx = jnp.maximum(jnp.maximum(lses[0], lses[1]), lses[2])
        es = [jnp.exp(le - mx) for le in lses]
        y_all = (es[0] * outs[0] + es[1] * outs[1] + es[2] * outs[2]) / (es[0] + es[1] + es[2])
        y = jnp.zeros((rows, MIX), F32)
        for h in range(N_HEADS):
            y = jnp.where(head == h, y_all[h * rows:(h + 1) * rows, :], y)
        y_ref[rs, :] = y


def _swa_sample(h, mix, cache_k, cache_v, layer, *, rows, n_seq, seqs, row0):
    past = cache_k.shape[3]
    blk = seqs * rows
    rb0 = row0 // blk

    def hspec(col):
        return pl.BlockSpec((blk, MIX), lambda b: (rb0 + b, col))

    cspec = pl.BlockSpec((1, seqs, MIX, past), lambda b: (layer, b, 0, 0))
    nbytes = 2 * (2 * seqs * past * MIX * 4 + 4 * blk * MIX * 4) + 16 * N_HEADS * rows * past * 4 + 2 * past * MIX * 2
    return pl.pallas_call(
        functools.partial(_swa_sample_body, past=past, rows=rows, seqs=seqs),
        grid=(n_seq // seqs,),
        in_specs=[hspec(COL_SQ), hspec(COL_SK), hspec(COL_SV), cspec, cspec,
                  pl.BlockSpec(memory_space=pl.ANY)],
        out_specs=pl.BlockSpec((blk, MIX), lambda b: (rb0 + b, 1)),
        out_shape=jax.ShapeDtypeStruct(mix.shape, F32),
        input_output_aliases={5: 0},
        compiler_params=_params(("arbitrary",), nbytes),
        name="swa_sample",
    )(h, h, h, cache_k, cache_v, mix)


def _conv_body(val_ref, gate_ref, buf_ref, w_ref, cb_ref, g_ref, b_ref, mix_ref,
               y_ref, glu_ref, ctx, *, rows, n_tiles):
    del mix_ref
    t = pl.program_id(1)

    @pl.when(t == 0)
    def _():
        ctx[0:CONV_CTX, :] = buf_ref[0]

    glu = val_ref[...] * _sigmoid(gate_ref[...])
    glu_ref[...] = glu
    ctx[CONV_CTX:CONV_CTX + rows, :] = glu
    lead = CONV_CTX - (CONV_WIDTH - 1)
    sub = min(rows, CONV_SUB)
    for base in range(0, rows, sub):
        window = ctx[base:base + sub + CONV_CTX, :]
        acc = jnp.zeros((sub, MIX), F32)
        for r in range(SUBLANE):
            shifted = pltpu.roll(window, sub + CONV_CTX - (lead + r), 0)
            for j in range(r, CONV_WIDTH, SUBLANE):
                acc = acc + shifted[j - r:j - r + sub, :] * w_ref[j:j + 1, :]
        z = _layer_norm(acc + cb_ref[...], g_ref[...], b_ref[...])
        y_ref[base:base + sub, :] = z * _sigmoid(z)
    if n_tiles > 1:
        ctx[0:CONV_CTX, :] = ctx[rows:rows + CONV_CTX, :]


def _conv(h, mix, buf, w, cb, g, b, *, rows, n_seq, n_tiles, row0):
    rb0 = row0 // rows
    rmap = lambda col: (lambda s, t: (rb0 + s * n_tiles + t, col))
    const = lambda shape: pl.BlockSpec(shape, lambda s, t: (0, 0))
    nbytes = 2 * 4 * rows * MIX * 4 + (CONV_CTX + rows) * MIX * 4 + 8 * rows * MIX * 4
    return pl.pallas_call(
        functools.partial(_conv_body, rows=rows, n_tiles=n_tiles),
        grid=(n_seq, n_tiles),
        in_specs=[pl.BlockSpec((rows, MIX), rmap(COL_CV)), pl.BlockSpec((rows, MIX), rmap(COL_CG)),
                  pl.BlockSpec((1, CONV_CTX, MIX), lambda s, t: (s, 0, 0)),
                  const((CONV_CTX, MIX)), const((1, MIX)), const((1, MIX)), const((1, MIX)),
                  pl.BlockSpec(memory_space=pl.ANY)],
        out_specs=[pl.BlockSpec((rows, MIX), rmap(2)),
                   pl.BlockSpec((rows, MIX), lambda s, t: (s * n_tiles + t, 0))],
        out_shape=[jax.ShapeDtypeStruct(mix.shape, F32),
                   jax.ShapeDtypeStruct((n_seq * n_tiles * rows, MIX), F32)],
        scratch_shapes=[pltpu.VMEM((CONV_CTX + rows, MIX), F32)],
        input_output_aliases={7: 0},
        compiler_params=_params(("arbitrary", "arbitrary"), nbytes),
        name="conv",
    )(h, h, buf, w, cb, g, b, mix)


def _gelu_tanh(x):
    c = math.sqrt(2.0 / math.pi)
    return x * (0.5 * (1.0 + jnp.tanh(c * (x + 0.044715 * (x * x * x)))))


def _sgu_body(u_ref, v_ref, g_ref, b_ref, ws_ref, bias_ref, avg_ref, mix_ref, y_ref, *vn_ref, rows, sub):
    del mix_ref
    tri = (lax.broadcasted_iota(I32, (SGU_CHUNK, SGU_CHUNK), 0)
           >= lax.broadcasted_iota(I32, (SGU_CHUNK, SGU_CHUNK), 1))
    wms = [jnp.where(tri, ws_ref[gi], 0.0).astype(BF16) for gi in range(N_HEADS)]
    head = _lane_head(MIX)
    for c in range(rows // sub):
        rs = slice(c * sub, (c + 1) * sub)
        u = _gelu_tanh(u_ref[rs, :])
        v = _gelu_tanh(v_ref[rs, :])
        vn = _group_norm(v, avg_ref[...]) * g_ref[...] + b_ref[...]
        if vn_ref:
            vn_ref[0][rs, :] = vn
        if sub < SGU_CHUNK:
            vn = jnp.concatenate([vn, jnp.zeros((SGU_CHUNK - sub, MIX), F32)], axis=0)
        vnb = vn.astype(BF16)
        z = jnp.zeros((SGU_CHUNK, MIX), F32)
        for gi in range(N_HEADS):
            z = jnp.where(head == gi, _dot(wms[gi], vnb), z)
        z = z + bias_ref[...]
        y_ref[rs, :] = u * z[0:sub, :]


def _sgu(h, mix, g, b, ws, bias, avg, *, rows, sub, n_steps, row0, emit_vn):
    rb0 = row0 // rows
    rmap = lambda col: (lambda c: (rb0 + c, col))
    const = lambda shape: pl.BlockSpec(shape, lambda c: (0,) * len(shape))
    nbytes = 2 * (4 * rows * MIX * 4 + N_HEADS * SGU_CHUNK * SGU_CHUNK * 4 + SGU_CHUNK * MIX * 4) + 8 * SGU_CHUNK * MIX * 4
    out_specs = [pl.BlockSpec((rows, MIX), rmap(3))]
    out_shape = [jax.ShapeDtypeStruct(mix.shape, F32)]
    if emit_vn:
        out_specs.append(pl.BlockSpec((rows, MIX), lambda c: (c, 0)))
        out_shape.append(jax.ShapeDtypeStruct((n_steps * rows, MIX), F32))
    return pl.pallas_call(
        functools.partial(_sgu_body, rows=rows, sub=sub),
        grid=(n_steps,),
        in_specs=[pl.BlockSpec((rows, MIX), rmap(COL_GU)), pl.BlockSpec((rows, MIX), rmap(COL_GV)),
                  const((1, MIX)), const((1, MIX)), const((N_HEADS, SGU_CHUNK, SGU_CHUNK)),
                  const((SGU_CHUNK, MIX)), const((MIX, MIX)), pl.BlockSpec(memory_space=pl.ANY)],
        out_specs=out_specs,
        out_shape=out_shape,
        input_output_aliases={7: 0},
        compiler_params=_params(("arbitrary",), nbytes),
        name="sgu",
    )(h, h, g, b, ws, bias, avg, mix)


def _out_router_body(mix_ref, w_ref, x_ref, g_ref, b_ref, wr_ref, br_ref,
                     x1_ref, route_ref, counts_ref, wbf, wrbf, logit_scr, *, alpha):
    @pl.when(pl.program_id(0) == 0)
    def _():
        wbf[...] = w_ref[0].astype(BF16)
        wrbf[...] = wr_ref[...].astype(BF16)

    parts = [slice(r0, r0 + OUT_ROWS) for r0 in range(0, TM, OUT_ROWS)]
    ys = [_dot(mix_ref[rs, :].astype(BF16), wbf[...]) for rs in parts]
    for rs, y in zip(parts, ys):
        x1 = _layer_norm(alpha * x_ref[rs, :] + y, g_ref[...], b_ref[...])
        x1_ref[rs, :] = x1
        logit_scr[:, rs] = _dot_nt(wrbf[...], x1.astype(BF16)) + br_ref[...]
    logits = logit_scr[0:ROUTE_ROWS, :]
    row = lax.broadcasted_iota(I32, (ROUTE_ROWS, 1), 0).astype(F32)

    def first_max(mask, vals):
        masked = jnp.where(mask, vals, NEG_INF)
        top = jnp.max(masked, axis=0, keepdims=True)
        idx = jnp.min(jnp.where(mask & (masked == top), row, float(LANE)), axis=0, keepdims=True)
        return top, idx

    gmask = row < N_GROUPS
    gmax, gsel = first_max(gmask, logits)
    g_gate = 1.0 / jnp.sum(jnp.where(gmask, jnp.exp(logits - gmax), 0.0), axis=0, keepdims=True)
    lo = N_GROUPS + EXPERTS_PER_GROUP * gsel
    emask = (row >= lo) & (row < lo + EXPERTS_PER_GROUP)
    v1, i1 = first_max(emask, logits)
    v2, i2 = first_max(emask & (row != i1), logits)
    e12 = jnp.exp(v2 - v1)
    both = (row == i1).astype(F32) + (row == i2).astype(F32)

    earlier = (lax.broadcasted_iota(I32, (TD, TD), 0) < lax.broadcasted_iota(I32, (TD, TD), 1)).astype(BF16)
    rank_a, rank_b = [], []
    for td in range(TM // TD):
        cs = slice(td * TD, (td + 1) * TD)
        tile = both[:, cs]
        counts_ref[td] = jnp.sum(tile, axis=1, keepdims=True)
        before = _dot(tile.astype(BF16), earlier)
        rank_a.append(jnp.sum(jnp.where(row == i1[:, cs], before, 0.0), axis=0, keepdims=True))
        rank_b.append(jnp.sum(jnp.where(row == i2[:, cs], before, 0.0), axis=0, keepdims=True))
    values = (i1 - N_GROUPS, i2 - N_GROUPS, g_gate * (1.0 / (1.0 + e12)), g_gate * (e12 / (1.0 + e12)),
              jnp.concatenate(rank_a, axis=1), jnp.concatenate(rank_b, axis=1))
    out_row = lax.broadcasted_iota(I32, (SUBLANE, 1), 0)
    route = jnp.zeros((SUBLANE, TM), F32)
    for k, val in enumerate(values):
        route = jnp.where(out_row == k, val, route)
    route_ref[...] = route


def _out_router(mix, w_out, layer, x, g, b, wr, br, *, alpha):
    nt, d = x.shape
    kdim = mix.shape[1]
    const = lambda shape: pl.BlockSpec(shape, lambda i: (0, 0))
    row = lambda width: pl.BlockSpec((TM, width), lambda i: (i, 0))
    nbytes = 2 * (kdim * d * 4 + 2 * d * LANE * 4 + TM * kdim * 4 + 2 * TM * d * 4 + TM * LANE * 4) + kdim * d * 2
    return pl.pallas_call(
        functools.partial(_out_router_body, alpha=alpha),
        grid=(nt // TM,),
        in_specs=[row(kdim), pl.BlockSpec((1, kdim, d), lambda i: (layer, 0, 0)), row(d),
                  const((1, d)), const((1, d)), const((LANE, d)), const((LANE, 1))],
        out_specs=[row(d), pl.BlockSpec((SUBLANE, TM), lambda i: (0, i)),
                   pl.BlockSpec((TM // TD, ROUTE_ROWS, 1), lambda i: (i, 0, 0))],
        out_shape=[jax.ShapeDtypeStruct((nt, d), F32), jax.ShapeDtypeStruct((SUBLANE, nt), F32),
                   jax.ShapeDtypeStruct((nt // TD, ROUTE_ROWS, 1), F32)],
        scratch_shapes=[pltpu.VMEM((kdim, d), BF16), pltpu.VMEM((LANE, d), BF16), pltpu.VMEM((LANE, TM), F32)],
        compiler_params=_params(("arbitrary",), nbytes),
        name="out_router",
    )(mix, w_out, x, g, b, wr, br)


def _local_slots(route, off_col):
    expert = lax.broadcasted_iota(I32, (N_EXPERTS, 1), 0).astype(F32)
    slots = []
    for kk in range(TOP_K):
        start = jnp.sum(jnp.where(expert == route[kk:kk + 1, :], off_col, 0.0), axis=0, keepdims=True)
        slots.append(start + route[4 + kk:5 + kk, :])
    return slots


def _pack_rows(y):
    half = y.shape[1] // 2
    bits = lax.bitcast_convert_type(y.astype(BF16).astype(F32), U32)
    return (bits[:, :half] & jnp.uint32(0xFFFF0000)) | (bits[:, half:] >> 16)


def _unpack_rows(w):
    hi = lax.bitcast_convert_type(w & jnp.uint32(0xFFFF0000), F32)
    lo = lax.bitcast_convert_type(w << 16, F32)
    return jnp.concatenate([hi, lo], axis=1)


def _chunk_copy(hbm, hbm_row, loc, chunk, sem, *, to_hbm):
    vm = loc.at[pl.ds(pl.multiple_of(chunk * MOE_CHUNK, MOE_CHUNK), MOE_CHUNK), :]
    hb = hbm.at[pl.ds(pl.multiple_of(hbm_row, MOE_CHUNK), MOE_CHUNK), :]
    return pltpu.make_async_copy(vm, hb, sem) if to_hbm else pltpu.make_async_copy(hb, vm, sem)


def _wait_chunks(hbm, buf, sem, count, *, to_hbm):
    def wait(c, carry):
        _chunk_copy(hbm, 0, buf, 0, sem, to_hbm=to_hbm).wait()
        return carry

    lax.fori_loop(0, count, wait, 0)


def _dispatch_body(dst_tab, n_chunks, x_ref, route_ref, off_ref, xs_in, xs_hbm, loc, sem):
    del xs_in
    i = pl.program_id(0)
    last = pl.num_programs(0) - 1
    cur = i % 2
    slot_a, slot_b = _local_slots(route_ref[...], off_ref[0])
    buf_row = lax.broadcasted_iota(I32, (LOC_ROWS, 1), 0).astype(F32)
    onehot = ((buf_row == slot_a) | (buf_row == slot_b)).astype(BF16)
    loc[cur] = _pack_rows(_dot(onehot, x_ref[...].astype(BF16)))

    def start(c, carry):
        _chunk_copy(xs_hbm, dst_tab[i * MAX_CHUNKS + c], loc.at[cur], c, sem.at[cur], to_hbm=True).start()
        return carry

    lax.fori_loop(0, n_chunks[i], start, 0)

    @pl.when(i > 0)
    def _():
        _wait_chunks(xs_hbm, loc.at[1 - cur], sem.at[1 - cur], n_chunks[jnp.maximum(i - 1, 0)], to_hbm=True)

    @pl.when(i == last)
    def _():
        _wait_chunks(xs_hbm, loc.at[cur], sem.at[cur], n_chunks[i], to_hbm=True)


def _dispatch(dst_tab, n_chunks, x1, route, off, xs_prev):
    nt, d = x1.shape
    row = lambda width: pl.BlockSpec((TD, width), lambda i, *_: (i, 0))
    nbytes = 2 * (TD * d * 4 + TD * LANE * 4) + 2 * LOC_ROWS * d * 4 + 4 * TD * LOC_ROWS * 4 + LOC_ROWS * d * 4
    return pl.pallas_call(
        _dispatch_body,
        grid_spec=pltpu.PrefetchScalarGridSpec(
            num_scalar_prefetch=2,
            grid=(nt // TD,),
            in_specs=[row(d), pl.BlockSpec((SUBLANE, TD), lambda i, *_: (0, i)),
                      pl.BlockSpec((1, N_EXPERTS, 1), lambda i, *_: (i, 0, 0)),
                      pl.BlockSpec(memory_space=pl.ANY)],
            out_specs=pl.BlockSpec(memory_space=pl.ANY),
            scratch_shapes=[pltpu.VMEM((2, LOC_ROWS, d // 2), U32), pltpu.SemaphoreType.DMA((2,))]),
        out_shape=jax.ShapeDtypeStruct(xs_prev.shape, U32),
        input_output_aliases={5: 0},
        compiler_params=_params(("arbitrary",), nbytes),
        name="moe_dispatch",
    )(dst_tab, n_chunks, x1, route, off, xs_prev)


def _ffn_body(blk_e, blk_valid, blk_src, x_ref, w1_ref, w3_ref, w2_ref, y_ref, w1b, w3b, w2b):
    del blk_src
    j = pl.program_id(0)

    @pl.when(blk_valid[j] > 0)
    def _():
        @pl.when((j == 0) | (blk_e[j] != blk_e[jnp.maximum(j - 1, 0)]))
        def _():
            w1b[...] = w1_ref[0, 0].astype(BF16)
            w3b[...] = w3_ref[0, 0].astype(BF16)
            w2b[...] = w2_ref[0, 0].astype(BF16)

        xb = _unpack_rows(x_ref[...]).astype(BF16)
        h1 = _dot(xb, w1b[...])
        h3 = _dot(xb, w3b[...])
        act = (h1 * _sigmoid(h1)) * h3
        y_ref[...] = _pack_rows(_dot(act.astype(BF16), w2b[...]))


def _ffn(xs, w1, w3, w2, layer, blk_e, blk_valid, blk_src):
    d = w1.shape[2]
    de = w1.shape[3]
    n_blk = xs.shape[0] // MOE_BLK
    wmap = lambda j, be, bv, bs: (layer, be[j], 0, 0)
    rows = pl.BlockSpec((MOE_BLK, d // 2), lambda j, be, bv, bs: (bs[j], 0))
    nbytes = 2 * (3 * d * de * 4 + MOE_BLK * d * 4) + 3 * d * de * 2 + 6 * MOE_BLK * de * 4 + 4 * MOE_BLK * d * 4
    return pl.pallas_call(
        _ffn_body,
        grid_spec=pltpu.PrefetchScalarGridSpec(
            num_scalar_prefetch=3,
            grid=(n_blk,),
            in_specs=[rows, pl.BlockSpec((1, 1, d, de), wmap), pl.BlockSpec((1, 1, d, de), wmap),
                      pl.BlockSpec((1, 1, de, d), wmap)],
            out_specs=rows,
            scratch_shapes=[pltpu.VMEM((d, de), BF16), pltpu.VMEM((d, de), BF16), pltpu.VMEM((de, d), BF16)]),
        out_shape=jax.ShapeDtypeStruct(xs.shape, U32),
        input_output_aliases={3: 0},
        compiler_params=_params(("arbitrary",), nbytes),
        name="expert_ffn",
    )(blk_e, blk_valid, blk_src, xs, w1, w3, w2)


def _combine_body(dst_tab, n_chunks, x_ref, route_ref, off_ref, g_ref, b_ref, ys_hbm, x2_ref,
                  loc, sem, *, alpha):
    i = pl.program_id(0)
    last = pl.num_programs(0) - 1
    cur = i % 2

    def fetch(tile, buf):
        def start(c, carry):
            _chunk_copy(ys_hbm, dst_tab[tile * MAX_CHUNKS + c], loc.at[buf], c, sem.at[buf], to_hbm=False).start()
            return carry

        lax.fori_loop(0, n_chunks[tile], start, 0)

    @pl.when(i == 0)
    def _():
        loc[...] = jnp.zeros_like(loc)
        fetch(0, 0)

    @pl.when(i < last)
    def _():
        fetch(jnp.minimum(i + 1, last), 1 - cur)

    route = route_ref[...]
    slot_a, slot_b = _local_slots(route, off_ref[0])
    buf_row = lax.broadcasted_iota(I32, (LOC_ROWS, 1), 0).astype(F32)
    gates = (jnp.where(buf_row == slot_a, route[2:3, :], 0.0)
             + jnp.where(buf_row == slot_b, route[3:4, :], 0.0))
    _wait_chunks(ys_hbm, loc.at[cur], sem.at[cur], n_chunks[i], to_hbm=False)
    f = _dot_tn(gates.astype(BF16), _unpack_rows(loc[cur]).astype(BF16))
    x2_ref[...] = _layer_norm(alpha * x_ref[...] + f, g_ref[...], b_ref[...])


def _combine(dst_tab, n_chunks, x1, route, off, g, b, ys, *, alpha):
    nt, d = x1.shape
    row = lambda width: pl.BlockSpec((TD, width), lambda i, dt, nc: (i, 0))
    const = lambda shape: pl.BlockSpec(shape, lambda i, dt, nc: (0, 0))
    nbytes = 2 * (2 * TD * d * 4 + TD * LANE * 4) + LOC_ROWS * d * 4 + 4 * TD * LOC_ROWS * 4 + LOC_ROWS * d * 2
    return pl.pallas_call(
        functools.partial(_combine_body, alpha=alpha),
        grid_spec=pltpu.PrefetchScalarGridSpec(
            num_scalar_prefetch=2,
            grid=(nt // TD,),
            in_specs=[row(d), pl.BlockSpec((SUBLANE, TD), lambda i, dt, nc: (0, i)),
                      pl.BlockSpec((1, N_EXPERTS, 1), lambda i, dt, nc: (i, 0, 0)),
                      const((1, d)), const((1, d)), pl.BlockSpec(memory_space=pl.ANY)],
            out_specs=row(d),
            scratch_shapes=[pltpu.VMEM((2, LOC_ROWS, d // 2), U32), pltpu.SemaphoreType.DMA((2,))]),
        out_shape=jax.ShapeDtypeStruct((nt, d), F32),
        compiler_params=_params(("arbitrary",), nbytes + LOC_ROWS * d * 4),
        name="combine_ln",
    )(dst_tab, n_chunks, x1, route, off, g, b, ys)


def _rope_tables(pos, rot_dim, theta):
    half = rot_dim // 2
    inv = jnp.float32(theta) ** (-jnp.arange(half, dtype=F32) * (2.0 / rot_dim))
    ang = pos.astype(F32)[:, None] * inv[None, :]
    c, s = jnp.cos(ang), jnp.sin(ang)
    rest = HEAD_DIM - rot_dim
    ones = jnp.ones((pos.shape[0], rest), F32)
    zeros = jnp.zeros((pos.shape[0], rest), F32)
    cos_h = jnp.concatenate([c, c, ones], axis=1)
    sin_h = jnp.concatenate([-s, s, zeros], axis=1)
    return jnp.tile(cos_h, (1, N_HEADS)), jnp.tile(sin_h, (1, N_HEADS))


def _block_diag(s):
    rows = [jnp.pad(s[:, h], ((0, 0), (0, 0), (h * HEAD_DIM, MIX - (h + 1) * HEAD_DIM))) for h in range(N_HEADS)]
    return jnp.concatenate(rows, axis=1)


def _diag_blocks(s):
    return jnp.stack([s[:, h * HEAD_DIM:(h + 1) * HEAD_DIM, h * HEAD_DIM:(h + 1) * HEAD_DIM]
                      for h in range(N_HEADS)], axis=1)


def _routing_plan(tile_counts, n_blk):
    cnt_t = tile_counts[:, N_GROUPS:N_GROUPS + N_EXPERTS, 0].astype(I32)
    run_chunks = (cnt_t + MOE_CHUNK - 1) // MOE_CHUNK
    run_rows = run_chunks * MOE_CHUNK
    cnt = jnp.sum(run_rows, axis=0)
    pcnt = (cnt + MOE_BLK - 1) // MOE_BLK * MOE_BLK
    pstart = jnp.cumsum(pcnt) - pcnt
    n_act = jnp.sum(pcnt) // MOE_BLK
    gstart = pstart[None, :] + jnp.cumsum(run_rows, axis=0) - run_rows
    chunk0 = jnp.cumsum(run_chunks, axis=1) - run_chunks
    n_chunks = jnp.sum(run_chunks, axis=1).astype(I32)

    def spread(first, count, value, n):
        p = jnp.arange(n, dtype=I32)[:, None]
        f, c, v = first[..., None, :], count[..., None, :], value[..., None, :]
        return jnp.sum(jnp.where((f <= p) & (p < f + c), v + (p - f) * MOE_CHUNK, 0), axis=-1)

    dst_tab = spread(chunk0, run_chunks, gstart, MAX_CHUNKS)
    off = (chunk0 * MOE_CHUNK).astype(F32)[:, :, None]
    pos = jnp.arange(n_blk, dtype=I32)[:, None] * MOE_BLK
    mine = (pstart[None, :] <= pos) & (pos < (pstart + pcnt)[None, :])
    experts = jnp.arange(N_EXPERTS, dtype=I32)[None, :]
    last_e = jnp.max(jnp.where(cnt > 0, experts[0], 0))
    active = pos[:, 0] < n_act * MOE_BLK
    blk_e = jnp.where(active, jnp.sum(jnp.where(mine, experts, 0), axis=-1), last_e)
    blk_valid = jnp.sum(jnp.where(mine, jnp.clip((pstart + cnt)[None, :] - pos, 0, MOE_BLK), 0), axis=-1)
    blk_src = jnp.minimum(jnp.arange(n_blk, dtype=I32), jnp.maximum(n_act - 1, 0))
    i32 = lambda t: t.astype(I32)
    return i32(dst_tab).reshape(-1), n_chunks, off, i32(blk_e), i32(blk_valid), i32(blk_src)


def kernel(x_prompt, x_sample, state_ret, cache_swa_k, cache_swa_v, state_conv, w_in, w_out, ret_gn_g,
           conv_w, conv_b, conv_ln_g, conv_ln_b, sgu_ln_g, sgu_ln_b, sgu_w, sgu_b, ln1_g, ln1_b,
           ln2_g, ln2_b, router_g_w, router_g_b, router_e_w, router_e_b, moe_w1, moe_w3, moe_w2):
    depth = w_in.shape[0]
    n_seq, seq, d = x_prompt.shape
    n_dec, t_new, _ = x_sample.shape
    past = cache_swa_k.shape[2]
    rows_s = -(-t_new // SUBLANE) * SUBLANE
    n_p = n_seq * seq
    n_s = n_dec * rows_s
    n_tail = -(-n_s // TM) * TM
    nt = n_p + n_tail
    assert seq % SWA_SB == 0 and seq % CONV_TT == 0 and seq % TM == 0 and TM % TD == 0
    assert t_new % RET_CHUNK != 0 and t_new <= SGU_CHUNK and w_in.shape[2] == N_COLBLK * MIX
    assert seq % SGU_ROWS == 0 and seq % RET_ROWS == 0 and n_dec % SAMPLE_SEQS == 0
    assert n_p % (SAMPLE_SEQS * rows_s) == 0
    alpha = (2 * depth) ** 0.25
    keep = min(DILATED_CONFIGS[-1][0], seq)
    n_blk = -(-(nt * TOP_K + (nt // TD) * N_EXPERTS * (MOE_CHUNK - 1) + N_EXPERTS * (MOE_BLK - 1)) // MOE_BLK)
    cache_k = jnp.transpose(cache_swa_k, (0, 1, 3, 4, 2)).reshape(depth, n_dec, MIX, past)
    cache_v = jnp.transpose(cache_swa_v, (0, 1, 3, 4, 2)).reshape(depth, n_dec, MIX, past)

    x = jnp.concatenate([x_prompt.reshape(n_p, d),
                         jnp.pad(x_sample, ((0, 0), (0, rows_s - t_new), (0, 0))).reshape(n_s, d),
                         jnp.zeros((n_tail - n_s, d), F32)], axis=0)
    pos = jnp.concatenate([jnp.arange(seq, dtype=I32),
                           PAST_LEN + jnp.arange(n_tail, dtype=I32) % rows_s])
    rope = _rope_tables(pos, HEAD_DIM, RET_THETA) + _rope_tables(pos, ROPE_DIM, ROPE_THETA)
    ret_tabs_p = _retention_tables(RET_CHUNK, RET_CHUNK)
    ret_tabs_s = _retention_tables(t_new, rows_s)
    lane_head = jnp.arange(MIX) // HEAD_DIM
    avg = ((lane_head[:, None] == lane_head[None, :]).astype(F32) / HEAD_DIM).astype(BF16)
    zero_state = jnp.zeros((n_seq, MIX, MIX), F32)
    zero_ctx = jnp.zeros((n_seq, CONV_CTX, MIX), F32)
    row2 = lambda v: v.reshape(1, -1)
    mix = jnp.zeros((nt, 4 * MIX), F32)
    slots = jnp.zeros((n_blk * MOE_BLK, d // 2), U32)

    outs = [[] for _ in range(9)]
    for l in range(depth):
        h = _in_proj(x, w_in, l, rope, n_p // TM, seq // TM)
        gn = row2(ret_gn_g[l])
        mix, s_p = _retention(h, mix, zero_state, ret_tabs_p, avg, gn, rows=RET_CHUNK, n_seq=n_seq,
                              n_chunks=seq // RET_CHUNK, seqs=1, chunks=RET_ROWS // RET_CHUNK, row0=0)
        mix, s_s = _retention(h, mix, _block_diag(state_ret[l]), ret_tabs_s, avg, gn, rows=rows_s, n_seq=n_dec,
                              n_chunks=1, seqs=SAMPLE_SEQS, chunks=1, row0=n_p)
        mix = _swa_prompt(h, mix, seq=seq, n_seq=n_seq)
        mix = _swa_sample(h, mix, cache_k, cache_v, l, rows=rows_s, n_seq=n_dec, seqs=SWA_SAMPLE_SEQS, row0=n_p)
        cw = jnp.pad(conv_w[l], ((0, CONV_CTX - CONV_WIDTH), (0, 0)))
        cargs = (cw, row2(conv_b[l]), row2(conv_ln_g[l]), row2(conv_ln_b[l]))
        mix, glu_p = _conv(h, mix, zero_ctx, *cargs, rows=CONV_TT, n_seq=n_seq, n_tiles=seq // CONV_TT, row0=0)
        ctx_s = jnp.pad(state_conv[l], ((0, 0), (CONV_CTX - (CONV_WIDTH - 1), 0), (0, 0)))
        mix, glu_s = _conv(h, mix, ctx_s, *cargs, rows=rows_s, n_seq=n_dec, n_tiles=1, row0=n_p)
        sbias = jnp.repeat(sgu_b[l].T, HEAD_DIM, axis=1)
        sargs = (row2(sgu_ln_g[l]), row2(sgu_ln_b[l]), sgu_w[l], sbias, avg)
        mix, = _sgu(h, mix, *sargs, rows=SGU_ROWS, sub=SGU_CHUNK, n_steps=n_p // SGU_ROWS, row0=0, emit_vn=False)
        mix, vn_s = _sgu(h, mix, *sargs, rows=SAMPLE_SEQS * rows_s, sub=rows_s, n_steps=n_dec // SAMPLE_SEQS,
                         row0=n_p, emit_vn=True)

        wr = jnp.concatenate([router_g_w[l].T, jnp.transpose(router_e_w[l], (0, 2, 1)).reshape(N_EXPERTS, d),
                              jnp.zeros((LANE - N_GROUPS - N_EXPERTS, d), F32)], axis=0)
        br = jnp.concatenate([router_g_b[l], router_e_b[l].reshape(-1),
                              jnp.zeros((LANE - N_GROUPS - N_EXPERTS,), F32)]).reshape(LANE, 1)
        x1, route, counts = _out_router(mix, w_out, l, x, row2(ln1_g[l]), row2(ln1_b[l]), wr, br, alpha=alpha)
        dst_tab, n_chunks, off, blk_e, blk_valid, blk_src = _routing_plan(counts, n_blk)
        slots = _dispatch(dst_tab, n_chunks, x1, route, off, slots)
        slots = _ffn(slots, moe_w1, moe_w3, moe_w2, l, blk_e, blk_valid, blk_src)
        x = _combine(dst_tab, n_chunks, x1, route, off, row2(ln2_g[l]), row2(ln2_b[l]), slots, alpha=alpha)

        def tail(col, n_keep):
            return jnp.stack([lax.slice(h, ((b + 1) * seq - n_keep, col * MIX), ((b + 1) * seq, (col + 1) * MIX))
                              for b in range(n_seq)])

        def new_rows(col):
            blk = lax.slice(h, (n_p, col * MIX), (n_p + n_s, (col + 1) * MIX))
            return blk.reshape(n_dec, rows_s, MIX)[:, :t_new]

        heads = lambda t: t.reshape(t.shape[0], t.shape[1], N_HEADS, HEAD_DIM)
        outs[0].append(_diag_blocks(s_p))
        outs[1].append(heads(tail(COL_SK, keep)))
        outs[2].append(heads(tail(COL_SV, keep)))
        outs[3].append(glu_p.reshape(n_seq, seq, MIX)[:, seq - (CONV_WIDTH - 1):])
        outs[4].append(_diag_blocks(s_s))
        outs[5].append(heads(new_rows(COL_SK)))
        outs[6].append(heads(new_rows(COL_SV)))
        outs[7].append(glu_s.reshape(n_dec, rows_s, MIX)[:, :t_new])
        outs[8].append(vn_s.reshape(n_dec, rows_s, MIX)[:, :t_new])

    y_prompt = x[:n_p].reshape(n_seq, seq, d)
    y_sample = x[n_p:n_p + n_s].reshape(n_dec, rows_s, d)[:, :t_new]
    return (y_prompt, y_sample) + tuple(jnp.stack(o) for o in outs)
```

```python
import functools
import math

import jax
import jax.numpy as jnp
from jax import lax
from jax.experimental import pallas as pl
from jax.experimental.pallas import tpu as pltpu

F32 = jnp.float32
BF16 = jnp.bfloat16
I32 = jnp.int32
U32 = jnp.uint32

HEAD_DIM = 64
N_HEADS = 4
MIX = N_HEADS * HEAD_DIM
(COL_RQ, COL_RK, COL_RV, COL_RG, COL_SQ, COL_SK, COL_SV,
 COL_CV, COL_CG, COL_GU, COL_GV) = range(11)
N_COLBLK = 11
RET_CHUNK = 128
RET_THETA = 10000.0
ROPE_THETA = 500000.0
ROPE_DIM = HEAD_DIM // 4
DILATED_CONFIGS = ((128, 1), (512, 4), (2048, 16))
DIL_BLOCK = 128
CONV_WIDTH = 31
SGU_CHUNK = 128
N_GROUPS = 4
EXPERTS_PER_GROUP = 8
N_EXPERTS = N_GROUPS * EXPERTS_PER_GROUP
TOP_K = 2
PAST_LEN = 8192
LN_EPS = 1e-5
NEG_INF = -1e30

LANE = 128
SUBLANE = 8
VMEM_CAP_BYTES = 60000 * 1024
COMPILER_TEMP_BYTES = 12 * 1024 * 1024

TM = 512
TD = 256
OUT_ROWS = 128
ROUTE_ROWS = -(-(N_GROUPS + N_EXPERTS) // SUBLANE) * SUBLANE
MOE_BLK = 512
MOE_CHUNK = SUBLANE
LOC_ROWS = -(-(TD * TOP_K + N_EXPERTS * (MOE_CHUNK - 1)) // LANE) * LANE
MAX_CHUNKS = LOC_ROWS // MOE_CHUNK
CHUNK_UNROLL = 4
MIX_ROWS = 1024
CONV_CTX = 32
CONV_SUB = 128
SWA_SB = DIL_BLOCK * 16
SWA_TILE = 256
SWA_UNROLL = 4
SAMPLE_SEQS = 8
SWA_SAMPLE_SEQS = 4


def _vmem(nbytes):
    return int(min(VMEM_CAP_BYTES, nbytes + COMPILER_TEMP_BYTES))


def _params(sem, nbytes):
    return pltpu.CompilerParams(dimension_semantics=sem, vmem_limit_bytes=_vmem(nbytes))


def _dot(a, b):
    return jnp.dot(a, b, preferred_element_type=F32)


def _dot_nt(a, b):
    return lax.dot_general(a, b, (((1,), (1,)), ((), ())), preferred_element_type=F32)


def _dot_tn(a, b):
    return lax.dot_general(a, b, (((0,), (0,)), ((), ())), preferred_element_type=F32)


def _split_dot(x, m):
    hi = x.astype(BF16)
    lo = (x - hi.astype(F32)).astype(BF16)
    return _dot(hi, m) + _dot(lo, m)


def _sigmoid(x):
    return 1.0 / (1.0 + jnp.exp(-x))


def _layer_norm(z, g, b):
    mu = jnp.mean(z, axis=-1, keepdims=True)
    d = z - mu
    var = jnp.mean(d * d, axis=-1, keepdims=True)
    return d * lax.rsqrt(var + LN_EPS) * g + b


def _group_norm(z, avg):
    mu = _split_dot(z, avg)
    d = z - mu
    var = _split_dot(d * d, avg)
    return d * lax.rsqrt(var + LN_EPS)


def _lane_head(width):
    return lax.broadcasted_iota(I32, (1, width), 1) // HEAD_DIM


def _rotate(x, cos, sin_signed, half):
    first = (lax.broadcasted_iota(I32, (1, LANE), 1) % HEAD_DIM) < half
    parts = []
    for p in range(x.shape[1] // LANE):
        t = x[:, p * LANE:(p + 1) * LANE]
        up = pltpu.roll(t, half, 1)
        down = pltpu.roll(t, LANE - half, 1)
        parts.append(jnp.where(first, down, up))
    return x * cos + jnp.concatenate(parts, axis=1) * sin_signed


def _in_proj_body(x_ref, w_ref, cr_ref, sr_ref, cs_ref, ss_ref, h_ref, wbf_ref):
    @pl.when(pl.program_id(0) == 0)
    def _():
        for j in range(N_COLBLK):
            wbf_ref[:, j * MIX:(j + 1) * MIX] = w_ref[0, :, j * MIX:(j + 1) * MIX].astype(BF16)

    xb = x_ref[...].astype(BF16)
    for j in range(N_COLBLK):
        hj = _dot(xb, wbf_ref[:, j * MIX:(j + 1) * MIX])
        if j in (COL_RQ, COL_RK):
            hj = _rotate(hj, cr_ref[...], sr_ref[...], HEAD_DIM // 2)
        if j == COL_RK:
            hj = hj * (HEAD_DIM ** -0.5)
        if j in (COL_SQ, COL_SK):
            hj = _rotate(hj, cs_ref[...], ss_ref[...], ROPE_DIM // 2)
        h_ref[:, j * MIX:(j + 1) * MIX] = hj


def _in_proj(x, w, layer, tabs, n_prompt_tiles, tiles_per_seq):
    nt, d = x.shape
    width = w.shape[2]

    def tab_map(i):
        return (jnp.where(i < n_prompt_tiles, i % tiles_per_seq, tiles_per_seq + i - n_prompt_tiles), 0)

    tab_spec = pl.BlockSpec((TM, MIX), tab_map)
    nbytes = d * width * 4 + d * width * 2 + 2 * TM * d * 4 + 8 * TM * MIX * 4 + 2 * TM * width * 4
    return pl.pallas_call(
        _in_proj_body,
        grid=(nt // TM,),
        in_specs=[pl.BlockSpec((TM, d), lambda i: (i, 0)),
                  pl.BlockSpec((1, d, width), lambda i: (layer, 0, 0), pipeline_mode=pl.Buffered(1)),
                  tab_spec, tab_spec, tab_spec, tab_spec],
        out_specs=pl.BlockSpec((TM, width), lambda i: (i, 0)),
        out_shape=jax.ShapeDtypeStruct((nt, width), F32),
        scratch_shapes=[pltpu.VMEM((d, width), BF16)],
        compiler_params=_params(("arbitrary",), nbytes),
        name="in_proj",
    )(x, w, *tabs)


def _retention_chunk(q, k, v, g, state, tabs):
    dec_ref, qdec_ref, kdec_ref, cmat_ref, bdm_ref, avg_ref, gn_ref = tabs
    head = _lane_head(MIX)
    vb = v.astype(BF16)
    kb = k.astype(BF16)
    o = _dot(q.astype(BF16), state.astype(BF16)) * qdec_ref[...]
    for h in range(N_HEADS):
        mh = head == h
        a = _dot_nt(jnp.where(mh, q, 0.0).astype(BF16), kb) * dec_ref[h]
        o = o + jnp.where(mh, _dot(a.astype(BF16), vb), 0.0)
    upd = _dot_tn((k * kdec_ref[...]).astype(BF16), vb)
    state = state * cmat_ref[...] + upd * bdm_ref[...]
    return _group_norm(o, avg_ref[...]) * gn_ref[...] * (g * _sigmoid(g)), state


def _retention_body(q_ref, k_ref, v_ref, g_ref, s0_ref, dec_ref, qdec_ref, kdec_ref,
                    cmat_ref, bdm_ref, avg_ref, gn_ref, mix_ref, y_ref, sout_ref, s_scr,
                    *, n_steps, seqs, chunks, rows):
    del mix_ref
    c = pl.program_id(1)
    tabs = (dec_ref, qdec_ref, kdec_ref, cmat_ref, bdm_ref, avg_ref, gn_ref)

    @pl.when(c == 0)
    def _():
        s_scr[...] = s0_ref[...]

    for s in range(seqs):
        state = s_scr[s]
        for cc in range(chunks):
            rs = slice((s * chunks + cc) * rows, (s * chunks + cc + 1) * rows)
            y_ref[rs, :], state = _retention_chunk(q_ref[rs, :], k_ref[rs, :], v_ref[rs, :], g_ref[rs, :], state, tabs)
        s_scr[s] = state

    @pl.when(c == n_steps - 1)
    def _():
        sout_ref[...] = s_scr[...]


def _retention_tables(l_real, l_pad):
    hh = jnp.arange(N_HEADS, dtype=F32)
    log_g = jnp.log1p(-jnp.exp2(-5.0 - hh))
    i = jnp.arange(l_pad, dtype=F32)
    rel = i[:, None] - i[None, :]
    dec = jnp.where(rel >= 0, jnp.exp(log_g[:, None, None] * jnp.maximum(rel, 0.0)), 0.0)
    qd = jnp.exp(log_g[:, None] * (i + 1.0))
    kd = jnp.where(i < l_real, jnp.exp(log_g[:, None] * (l_real - 1.0 - i)), 0.0)
    cd = jnp.exp(log_g * l_real)
    expand = lambda t: jnp.repeat(t.T, HEAD_DIM, axis=1)
    row_head = jnp.arange(MIX) // HEAD_DIM
    bdm = (row_head[:, None] == row_head[None, :]).astype(F32)
    cmat = cd[row_head][:, None] * bdm
    return dec, expand(qd), expand(kd), cmat, bdm


def _retention(h, mix, s0, tabs, avg, gn_g, *, rows, n_seq, n_chunks, seqs, chunks, row0):
    assert (seqs == 1 or chunks == n_chunks) and n_seq % seqs == 0 and n_chunks % chunks == 0
    blk = seqs * chunks * rows
    rb0 = row0 // blk
    n_steps = n_chunks // chunks
    dec, qdec, kdec, cmat, bdm = tabs

    def hspec(col):
        return pl.BlockSpec((blk, MIX), lambda b, c: (rb0 + b * n_steps + c, col))

    const2 = lambda shape: pl.BlockSpec(shape, lambda b, c: (0, 0))
    state_spec = pl.BlockSpec((seqs, MIX, MIX), lambda b, c: (b, 0, 0))
    in_specs = [hspec(COL_RQ), hspec(COL_RK), hspec(COL_RV), hspec(COL_RG), state_spec,
                pl.BlockSpec((N_HEADS, rows, rows), lambda b, c: (0, 0, 0)),
                const2((rows, MIX)), const2((rows, MIX)),
                const2((MIX, MIX)), const2((MIX, MIX)), const2((MIX, MIX)), const2((1, MIX)),
                pl.BlockSpec(memory_space=pl.ANY)]
    args = [h, h, h, h, s0, dec, qdec, kdec, cmat, bdm, avg, gn_g, mix]
    nbytes = (2 * (5 * blk * MIX * 4 + 2 * rows * MIX * 4 + (3 + 2 * seqs) * MIX * MIX * 4 + N_HEADS * rows * rows * 4)
              + seqs * MIX * MIX * 4)
    return pl.pallas_call(
        functools.partial(_retention_body, n_steps=n_steps, seqs=seqs, chunks=chunks, rows=rows),
        grid=(n_seq // seqs, n_steps),
        in_specs=in_specs,
        out_specs=[pl.BlockSpec((blk, MIX), lambda b, c: (rb0 + b * n_steps + c, 0)), state_spec],
        out_shape=[jax.ShapeDtypeStruct(mix.shape, F32),
                   jax.ShapeDtypeStruct((n_seq, MIX, MIX), F32)],
        scratch_shapes=[pltpu.VMEM((seqs, MIX, MIX), F32)],
        input_output_aliases={12: 0},
        compiler_params=_params(("arbitrary", "arbitrary"), nbytes),
        name="retention",
    )(*args)


def _swa_prompt_body(q_ref, k_ref, v_ref, mix_ref, y_ref, o0, o1, o2, l0, l1, l2, *, seq):
    del mix_ref
    o_scr = (o0, o1, o2)
    l_scr = (l0, l1, l2)
    head = _lane_head(LANE)
    qi = lax.broadcasted_iota(I32, (DIL_BLOCK, 1), 0)
    ki = lax.broadcasted_iota(I32, (1, 2 * DIL_BLOCK), 1)
    rel = DIL_BLOCK + qi - ki
    scale = HEAD_DIM ** -0.5
    for sb in range(seq // SWA_SB):
        for ci, (window, dil) in enumerate(DILATED_CONFIGS):
            span = window // dil
            band = (rel >= 0) & (rel <= span)

            def block(i, carry, sb=sb, ci=ci, dil=dil, band=band):
                nloc = i // dil
                off = nloc * (DIL_BLOCK * dil) + i % dil
                start = sb * SWA_SB + off
                if sb == 0:
                    first = nloc == 0
                    pstart = jnp.where(first, start, start - DIL_BLOCK * dil)
                    valid = band & (ki >= jnp.where(first, DIL_BLOCK, 0))
                else:
                    pstart = start - DIL_BLOCK * dil
                    valid = band
                stride = dil if dil > 1 else None
                cur = pl.ds(start, DIL_BLOCK, stride=stride)
                prev = pl.ds(pstart, DIL_BLOCK, stride=stride)
                qb = q_ref[cur, :]
                kb = jnp.concatenate([k_ref[prev, :], k_ref[cur, :]], axis=0).astype(BF16)
                vb = jnp.concatenate([v_ref[prev, :], v_ref[cur, :]], axis=0).astype(BF16)
                o = jnp.zeros((DIL_BLOCK, LANE), F32)
                ls = jnp.zeros((DIL_BLOCK, LANE), F32)
                for hh in range(LANE // HEAD_DIM):
                    mh = head == hh
                    s = _dot_nt(jnp.where(mh, qb, 0.0).astype(BF16), kb) * scale
                    s = jnp.where(valid, s, NEG_INF)
                    m = jnp.max(s, axis=-1, keepdims=True)
                    p = jnp.exp(s - m)
                    l = jnp.sum(p, axis=-1, keepdims=True)
                    o = jnp.where(mh, _dot(p.astype(BF16), vb) / l, o)
                    ls = jnp.where(mh, m + jnp.log(l), ls)
                dst = pl.ds(off, DIL_BLOCK, stride=stride)
                o_scr[ci][dst, :] = o
                l_scr[ci][dst, :] = ls
                return carry

            lax.fori_loop(0, SWA_SB // DIL_BLOCK, block, 0, unroll=SWA_UNROLL)

        def combine(t, carry, sb=sb):
            rows = pl.ds(t * SWA_TILE, SWA_TILE)
            la, lb, lc = l_scr[0][rows, :], l_scr[1][rows, :], l_scr[2][rows, :]
            mx = jnp.maximum(jnp.maximum(la, lb), lc)
            ea, eb, ec = jnp.exp(la - mx), jnp.exp(lb - mx), jnp.exp(lc - mx)
            num = ea * o_scr[0][rows, :] + eb * o_scr[1][rows, :] + ec * o_scr[2][rows, :]
            y_ref[pl.ds(sb * SWA_SB + t * SWA_TILE, SWA_TILE), :] = num / (ea + eb + ec)
            return carry

        lax.fori_loop(0, SWA_SB // SWA_TILE, combine, 0)


def _swa_prompt(h, mix, *, seq, n_seq):
    halves = MIX // LANE

    def hspec(col):
        return pl.BlockSpec((seq, LANE), lambda b, p: (b, col * halves + p))

    nbytes = 2 * 4 * seq * LANE * 4 + 6 * SWA_SB * LANE * 4
    return pl.pallas_call(
        functools.partial(_swa_prompt_body, seq=seq),
        grid=(n_seq, halves),
        in_specs=[hspec(COL_SQ), hspec(COL_SK), hspec(COL_SV), pl.BlockSpec(memory_space=pl.ANY)],
        out_specs=pl.BlockSpec((seq, LANE), lambda b, p: (b, halves + p)),
        out_shape=jax.ShapeDtypeStruct(mix.shape, F32),
        scratch_shapes=[pltpu.VMEM((SWA_SB, LANE), F32)] * 6,
        input_output_aliases={3: 0},
        compiler_params=_params(("arbitrary", "arbitrary"), nbytes),
        name="swa_prompt",
    )(h, h, h, mix)


def _swa_sample_body(q_ref, k_ref, v_ref, ck_ref, cv_ref, mix_ref, y_ref, *, past, rows, seqs):
    del mix_ref
    head = _lane_head(MIX)
    hr = N_HEADS * rows
    scale = HEAD_DIM ** -0.5
    t_row = lax.broadcasted_iota(I32, (hr, 1), 0) % rows
    rel_c = past + t_row - lax.broadcasted_iota(I32, (1, past), 1)
    rel_n = t_row - lax.broadcasted_iota(I32, (1, LANE), 1)
    valid_c, valid_n = [], []
    for window, dil in DILATED_CONFIGS:
        span = window // dil
        shift = int(math.log2(dil))
        for rel, dst in ((rel_c, valid_c), (rel_n, valid_n)):
            dst.append((rel >= 0) & ((rel & (dil - 1)) == 0) & ((rel >> shift) <= span))
    zpad = jnp.zeros((LANE - rows, MIX), F32)
    for s in range(seqs):
        rs = slice(s * rows, (s + 1) * rows)
        q = q_ref[rs, :]
        qm = jnp.concatenate([jnp.where(head == h, q, 0.0) for h in range(N_HEADS)], axis=0).astype(BF16)
        ck = ck_ref[0, s].astype(BF16)
        cv = cv_ref[0, s].astype(BF16)
        kn = jnp.concatenate([k_ref[rs, :], zpad], axis=0).astype(BF16)
        vn = jnp.concatenate([v_ref[rs, :], zpad], axis=0).astype(BF16)
        s_c = _dot(qm, ck) * scale
        s_n = _dot_nt(qm, kn) * scale
        pcs, pns, ls, lses = [], [], [], []
        for ci in range(len(DILATED_CONFIGS)):
            sc = jnp.where(valid_c[ci], s_c, NEG_INF)
            sn = jnp.where(valid_n[ci], s_n, NEG_INF)
            m = jnp.maximum(jnp.max(sc, axis=-1, keepdims=True), jnp.max(sn, axis=-1, keepdims=True))
            pc = jnp.exp(sc - m)
            pn = jnp.exp(sn - m)
            l = jnp.sum(pc, axis=-1, keepdims=True) + jnp.sum(pn, axis=-1, keepdims=True)
            pcs.append(pc.astype(BF16))
            pns.append(pn.astype(BF16))
            ls.append(l)
            lses.append(m + jnp.log(l))
        o_all = _dot_nt(jnp.concatenate(pcs, axis=0), cv) + _dot(jnp.concatenate(pns, axis=0), vn)
        outs = [o_all[ci * hr:(ci + 1) * hr, :] / ls[ci] for ci in range(len(DILATED_CONFIGS))]
        mx = jnp.maximum(jnp.maximum(lses[0], lses[1]), lses[2])
        es = [jnp.exp(le - mx) for le in lses]
        y_all = (es[0] * outs[0] + es[1] * outs[1] + es[2] * outs[2]) / (es[0] + es[1] + es[2])
        y = jnp.zeros((rows, MIX), F32)
        for h in range(N_HEADS):
            y = jnp.where(head == h, y_all[h * rows:(h + 1) * rows, :], y)
        y_ref[rs, :] = y


def _swa_sample(h, mix, cache_k, cache_v, layer, *, rows, n_seq, seqs, row0):
    past = cache_k.shape[3]
    blk = seqs * rows
    rb0 = row0 // blk

    def hspec(col):
        return pl.BlockSpec((blk, MIX), lambda b: (rb0 + b, col))

    cspec = pl.BlockSpec((1, seqs, MIX, past), lambda b: (layer, b, 0, 0))
    nbytes = 2 * (2 * seqs * past * MIX * 4 + 4 * blk * MIX * 4) + 16 * N_HEADS * rows * past * 4 + 2 * past * MIX * 2
    return pl.pallas_call(
        functools.partial(_swa_sample_body, past=past, rows=rows, seqs=seqs),
        grid=(n_seq // seqs,),
        in_specs=[hspec(COL_SQ), hspec(COL_SK), hspec(COL_SV), cspec, cspec,
                  pl.BlockSpec(memory_space=pl.ANY)],
        out_specs=pl.BlockSpec((blk, MIX), lambda b: (rb0 + b, 1)),
        out_shape=jax.ShapeDtypeStruct(mix.shape, F32),
        input_output_aliases={5: 0},
        compiler_params=_params(("arbitrary",), nbytes),
        name="swa_sample",
    )(h, h, h, cache_k, cache_v, mix)


def _conv_rows(ctx, base, sub, w_ref, cb_ref, g_ref, b_ref):
    lead = CONV_CTX - (CONV_WIDTH - 1)
    window = ctx[base:base + sub + CONV_CTX, :]
    acc = jnp.zeros((sub, MIX), F32)
    for r in range(SUBLANE):
        shifted = pltpu.roll(window, sub + CONV_CTX - (lead + r), 0)
        for j in range(r, CONV_WIDTH, SUBLANE):
            acc = acc + shifted[j - r:j - r + sub, :] * w_ref[j:j + 1, :]
    z = _layer_norm(acc + cb_ref[...], g_ref[...], b_ref[...])
    return z * _sigmoid(z)


def _conv_body(val_ref, gate_ref, buf_ref, w_ref, cb_ref, g_ref, b_ref, mix_ref,
               y_ref, glu_ref, ctx, *, rows, n_tiles):
    del mix_ref
    t = pl.program_id(1)

    @pl.when(t == 0)
    def _():
        ctx[0:CONV_CTX, :] = buf_ref[0]

    glu = val_ref[...] * _sigmoid(gate_ref[...])
    glu_ref[...] = glu
    ctx[CONV_CTX:CONV_CTX + rows, :] = glu
    sub = min(rows, CONV_SUB)
    for base in range(0, rows, sub):
        y_ref[base:base + sub, :] = _conv_rows(ctx, base, sub, w_ref, cb_ref, g_ref, b_ref)
    if n_tiles > 1:
        ctx[0:CONV_CTX, :] = ctx[rows:rows + CONV_CTX, :]


def _conv(h, mix, buf, w, cb, g, b, *, rows, n_seq, n_tiles, row0):
    rb0 = row0 // rows
    rmap = lambda col: (lambda s, t: (rb0 + s * n_tiles + t, col))
    const = lambda shape: pl.BlockSpec(shape, lambda s, t: (0, 0))
    nbytes = 2 * 4 * rows * MIX * 4 + (CONV_CTX + rows) * MIX * 4 + 8 * rows * MIX * 4
    return pl.pallas_call(
        functools.partial(_conv_body, rows=rows, n_tiles=n_tiles),
        grid=(n_seq, n_tiles),
        in_specs=[pl.BlockSpec((rows, MIX), rmap(COL_CV)), pl.BlockSpec((rows, MIX), rmap(COL_CG)),
                  pl.BlockSpec((1, CONV_CTX, MIX), lambda s, t: (s, 0, 0)),
                  const((CONV_CTX, MIX)), const((1, MIX)), const((1, MIX)), const((1, MIX)),
                  pl.BlockSpec(memory_space=pl.ANY)],
        out_specs=[pl.BlockSpec((rows, MIX), rmap(2)),
                   pl.BlockSpec((rows, MIX), lambda s, t: (s * n_tiles + t, 0))],
        out_shape=[jax.ShapeDtypeStruct(mix.shape, F32),
                   jax.ShapeDtypeStruct((n_seq * n_tiles * rows, MIX), F32)],
        scratch_shapes=[pltpu.VMEM((CONV_CTX + rows, MIX), F32)],
        input_output_aliases={7: 0},
        compiler_params=_params(("arbitrary", "arbitrary"), nbytes),
        name="conv",
    )(h, h, buf, w, cb, g, b, mix)


def _gelu_tanh(x):
    c = math.sqrt(2.0 / math.pi)
    return x * (0.5 * (1.0 + jnp.tanh(c * (x + 0.044715 * (x * x * x)))))


def _sgu_weights(ws_ref):
    tri = (lax.broadcasted_iota(I32, (SGU_CHUNK, SGU_CHUNK), 0)
           >= lax.broadcasted_iota(I32, (SGU_CHUNK, SGU_CHUNK), 1))
    return [jnp.where(tri, ws_ref[gi], 0.0).astype(BF16) for gi in range(N_HEADS)]


def _sgu_chunk(u_raw, v_raw, wms, tabs):
    g_ref, b_ref, bias_ref, avg_ref = tabs
    sub = u_raw.shape[0]
    head = _lane_head(MIX)
    u = _gelu_tanh(u_raw)
    vn = _group_norm(_gelu_tanh(v_raw), avg_ref[...]) * g_ref[...] + b_ref[...]
    vp = vn if sub == SGU_CHUNK else jnp.concatenate([vn, jnp.zeros((SGU_CHUNK - sub, MIX), F32)], axis=0)
    vnb = vp.astype(BF16)
    z = jnp.zeros((SGU_CHUNK, MIX), F32)
    for gi in range(N_HEADS):
        z = jnp.where(head == gi, _dot(wms[gi], vnb), z)
    z = z + bias_ref[...]
    return u * z[0:sub, :], vn


def _sgu_body(u_ref, v_ref, g_ref, b_ref, ws_ref, bias_ref, avg_ref, mix_ref, y_ref, *vn_ref, rows, sub):
    del mix_ref
    wms = _sgu_weights(ws_ref)
    tabs = (g_ref, b_ref, bias_ref, avg_ref)
    for c in range(rows // sub):
        rs = slice(c * sub, (c + 1) * sub)
        y_ref[rs, :], vn = _sgu_chunk(u_ref[rs, :], v_ref[rs, :], wms, tabs)
        if vn_ref:
            vn_ref[0][rs, :] = vn


def _sgu(h, mix, g, b, ws, bias, avg, *, rows, sub, n_steps, row0, emit_vn):
    rb0 = row0 // rows
    rmap = lambda col: (lambda c: (rb0 + c, col))
    const = lambda shape: pl.BlockSpec(shape, lambda c: (0,) * len(shape))
    nbytes = 2 * (4 * rows * MIX * 4 + N_HEADS * SGU_CHUNK * SGU_CHUNK * 4 + SGU_CHUNK * MIX * 4) + 8 * SGU_CHUNK * MIX * 4
    out_specs = [pl.BlockSpec((rows, MIX), rmap(3))]
    out_shape = [jax.ShapeDtypeStruct(mix.shape, F32)]
    if emit_vn:
        out_specs.append(pl.BlockSpec((rows, MIX), lambda c: (c, 0)))
        out_shape.append(jax.ShapeDtypeStruct((n_steps * rows, MIX), F32))
    return pl.pallas_call(
        functools.partial(_sgu_body, rows=rows, sub=sub),
        grid=(n_steps,),
        in_specs=[pl.BlockSpec((rows, MIX), rmap(COL_GU)), pl.BlockSpec((rows, MIX), rmap(COL_GV)),
                  const((1, MIX)), const((1, MIX)), const((N_HEADS, SGU_CHUNK, SGU_CHUNK)),
                  const((SGU_CHUNK, MIX)), const((MIX, MIX)), pl.BlockSpec(memory_space=pl.ANY)],
        out_specs=out_specs,
        out_shape=out_shape,
        input_output_aliases={7: 0},
        compiler_params=_params(("arbitrary",), nbytes),
        name="sgu",
    )(h, h, g, b, ws, bias, avg, mix)


def _mixers_body(q_ref, k_ref, v_ref, g_ref, cv_ref, cg_ref, gu_ref, gv_ref,
                 dec_ref, qdec_ref, kdec_ref, cmat_ref, bdm_ref, avg_ref, gn_ref,
                 cw_ref, cb_ref, clg_ref, clb_ref, sg_ref, sb_ref, ws_ref, bias_ref, mix_in,
                 mix_ref, glu_ref, sout_ref, s_scr, ctx, *, n_steps, rows):
    del mix_in
    c = pl.program_id(1)

    @pl.when(c == 0)
    def _():
        s_scr[...] = jnp.zeros_like(s_scr)
        ctx[0:CONV_CTX, :] = jnp.zeros((CONV_CTX, MIX), F32)

    ret_tabs = (dec_ref, qdec_ref, kdec_ref, cmat_ref, bdm_ref, avg_ref, gn_ref)
    sgu_tabs = (sg_ref, sb_ref, bias_ref, avg_ref)
    wms = _sgu_weights(ws_ref)
    glu = cv_ref[...] * _sigmoid(cg_ref[...])
    glu_ref[...] = glu
    ctx[CONV_CTX:CONV_CTX + rows, :] = glu
    state = s_scr[...]
    for cc in range(rows // RET_CHUNK):
        rs = slice(cc * RET_CHUNK, (cc + 1) * RET_CHUNK)
        y, state = _retention_chunk(q_ref[rs, :], k_ref[rs, :], v_ref[rs, :], g_ref[rs, :], state, ret_tabs)
        mix_ref[rs, 0:MIX] = y
        mix_ref[rs, 2 * MIX:3 * MIX] = _conv_rows(ctx, cc * RET_CHUNK, RET_CHUNK, cw_ref, cb_ref, clg_ref, clb_ref)
        mix_ref[rs, 3 * MIX:4 * MIX] = _sgu_chunk(gu_ref[rs, :], gv_ref[rs, :], wms, sgu_tabs)[0]
    mix_ref[:, MIX:2 * MIX] = jnp.zeros((rows, MIX), F32)
    s_scr[...] = state
    ctx[0:CONV_CTX, :] = ctx[rows:rows + CONV_CTX, :]

    @pl.when(c == n_steps - 1)
    def _():
        sout_ref[0] = s_scr[...]


def _mixers_prompt(h, mix, ret_tabs, avg, gn_g, conv_args, sgu_args, *, seq, n_seq):
    assert RET_CHUNK == SGU_CHUNK and seq % MIX_ROWS == 0
    rows = MIX_ROWS
    n_steps = seq // rows
    dec, qdec, kdec, cmat, bdm = ret_tabs
    cw, cb, clg, clb = conv_args
    sg, sb, ws, bias = sgu_args

    def hspec(col):
        return pl.BlockSpec((rows, MIX), lambda b, c: (b * n_steps + c, col))

    def const(arr):
        return pl.BlockSpec(arr.shape, lambda b, c: (0,) * arr.ndim)

    tables = [dec, qdec, kdec, cmat, bdm, avg, gn_g, cw, cb, clg, clb, sg, sb, ws, bias]
    cols = [COL_RQ, COL_RK, COL_RV, COL_RG, COL_CV, COL_CG, COL_GU, COL_GV]
    nbytes = (2 * (len(cols) + 5) * rows * MIX * 4 + 2 * sum(t.size * 4 for t in tables)
              + (CONV_CTX + rows) * MIX * 4 + MIX * MIX * 4)
    return pl.pallas_call(
        functools.partial(_mixers_body, n_steps=n_steps, rows=rows),
        grid=(n_seq, n_steps),
        in_specs=[hspec(col) for col in cols] + [const(t) for t in tables] + [pl.BlockSpec(memory_space=pl.ANY)],
        out_specs=[pl.BlockSpec((rows, 4 * MIX), lambda b, c: (b * n_steps + c, 0)),
                   pl.BlockSpec((rows, MIX), lambda b, c: (b * n_steps + c, 0)),
                   pl.BlockSpec((1, MIX, MIX), lambda b, c: (b, 0, 0))],
        out_shape=[jax.ShapeDtypeStruct(mix.shape, F32),
                   jax.ShapeDtypeStruct((n_seq * seq, MIX), F32),
                   jax.ShapeDtypeStruct((n_seq, MIX, MIX), F32)],
        scratch_shapes=[pltpu.VMEM((MIX, MIX), F32), pltpu.VMEM((CONV_CTX + rows, MIX), F32)],
        input_output_aliases={len(cols) + len(tables): 0},
        compiler_params=_params(("arbitrary", "arbitrary"), nbytes),
        name="mixers_prompt",
    )(*([h] * len(cols)), *tables, mix)


def _out_router_body(mix_ref, w_ref, x_ref, g_ref, b_ref, wr_ref, br_ref,
                     x1_ref, route_ref, counts_ref, wbf, wrbf, logit_scr, *, alpha):
    @pl.when(pl.program_id(0) == 0)
    def _():
        wbf[...] = w_ref[0].astype(BF16)
        wrbf[...] = wr_ref[...].astype(BF16)

    parts = [slice(r0, r0 + OUT_ROWS) for r0 in range(0, TM, OUT_ROWS)]
    ys = [_dot(mix_ref[rs, :].astype(BF16), wbf[...]) for rs in parts]
    for rs, y in zip(parts, ys):
        x1 = _layer_norm(alpha * x_ref[rs, :] + y, g_ref[...], b_ref[...])
        x1_ref[rs, :] = x1
        logit_scr[:, rs] = _dot_nt(wrbf[...], x1.astype(BF16)) + br_ref[...]
    logits = logit_scr[0:ROUTE_ROWS, :]
    row = lax.broadcasted_iota(I32, (ROUTE_ROWS, 1), 0).astype(F32)

    def first_max(mask, vals):
        masked = jnp.where(mask, vals, NEG_INF)
        top = jnp.max(masked, axis=0, keepdims=True)
        idx = jnp.min(jnp.where(mask & (masked == top), row, float(LANE)), axis=0, keepdims=True)
        return top, idx

    gmask = row < N_GROUPS
    gmax, gsel = first_max(gmask, logits)
    g_gate = 1.0 / jnp.sum(jnp.where(gmask, jnp.exp(logits - gmax), 0.0), axis=0, keepdims=True)
    lo = N_GROUPS + EXPERTS_PER_GROUP * gsel
    emask = (row >= lo) & (row < lo + EXPERTS_PER_GROUP)
    v1, i1 = first_max(emask, logits)
    v2, i2 = first_max(emask & (row != i1), logits)
    e12 = jnp.exp(v2 - v1)
    both = (row == i1).astype(F32) + (row == i2).astype(F32)

    earlier = (lax.broadcasted_iota(I32, (TD, TD), 0) < lax.broadcasted_iota(I32, (TD, TD), 1)).astype(BF16)
    rank_a, rank_b = [], []
    for td in range(TM // TD):
        cs = slice(td * TD, (td + 1) * TD)
        tile = both[:, cs]
        counts_ref[td] = jnp.sum(tile, axis=1, keepdims=True)
        before = _dot(tile.astype(BF16), earlier)
        rank_a.append(jnp.sum(jnp.where(row == i1[:, cs], before, 0.0), axis=0, keepdims=True))
        rank_b.append(jnp.sum(jnp.where(row == i2[:, cs], before, 0.0), axis=0, keepdims=True))
    values = (i1 - N_GROUPS, i2 - N_GROUPS, g_gate * (1.0 / (1.0 + e12)), g_gate * (e12 / (1.0 + e12)),
              jnp.concatenate(rank_a, axis=1), jnp.concatenate(rank_b, axis=1))
    out_row = lax.broadcasted_iota(I32, (SUBLANE, 1), 0)
    route = jnp.zeros((SUBLANE, TM), F32)
    for k, val in enumerate(values):
        route = jnp.where(out_row == k, val, route)
    route_ref[...] = route


def _out_router(mix, w_out, layer, x, g, b, wr, br, *, alpha):
    nt, d = x.shape
    kdim = mix.shape[1]
    const = lambda shape: pl.BlockSpec(shape, lambda i: (0, 0))
    row = lambda width: pl.BlockSpec((TM, width), lambda i: (i, 0))
    nbytes = 2 * (kdim * d * 4 + 2 * d * LANE * 4 + TM * kdim * 4 + 2 * TM * d * 4 + TM * LANE * 4) + kdim * d * 2
    return pl.pallas_call(
        functools.partial(_out_router_body, alpha=alpha),
        grid=(nt // TM,),
        in_specs=[row(kdim), pl.BlockSpec((1, kdim, d), lambda i: (layer, 0, 0)), row(d),
                  const((1, d)), const((1, d)), const((LANE, d)), const((LANE, 1))],
        out_specs=[row(d), pl.BlockSpec((SUBLANE, TM), lambda i: (0, i)),
                   pl.BlockSpec((TM // TD, ROUTE_ROWS, 1), lambda i: (i, 0, 0))],
        out_shape=[jax.ShapeDtypeStruct((nt, d), F32), jax.ShapeDtypeStruct((SUBLANE, nt), F32),
                   jax.ShapeDtypeStruct((nt // TD, ROUTE_ROWS, 1), F32)],
        scratch_shapes=[pltpu.VMEM((kdim, d), BF16), pltpu.VMEM((LANE, d), BF16), pltpu.VMEM((LANE, TM), F32)],
        compiler_params=_params(("arbitrary",), nbytes),
        name="out_router",
    )(mix, w_out, x, g, b, wr, br)


def _local_slots(route, off_col):
    expert = lax.broadcasted_iota(I32, (N_EXPERTS, 1), 0).astype(F32)
    slots = []
    for kk in range(TOP_K):
        start = jnp.sum(jnp.where(expert == route[kk:kk + 1, :], off_col, 0.0), axis=0, keepdims=True)
        slots.append(start + route[4 + kk:5 + kk, :])
    return slots


def _pack_rows(y):
    half = y.shape[1] // 2
    bits = lax.bitcast_convert_type(y.astype(BF16).astype(F32), U32)
    return (bits[:, :half] & jnp.uint32(0xFFFF0000)) | (bits[:, half:] >> 16)


def _unpack_rows(w):
    hi = lax.bitcast_convert_type(w & jnp.uint32(0xFFFF0000), F32)
    lo = lax.bitcast_convert_type(w << 16, F32)
    return jnp.concatenate([hi, lo], axis=1)


def _chunk_copy(hbm, hbm_row, loc, chunk, sem, *, to_hbm):
    vm = loc.at[pl.ds(pl.multiple_of(chunk * MOE_CHUNK, MOE_CHUNK), MOE_CHUNK), :]
    hb = hbm.at[pl.ds(pl.multiple_of(hbm_row, MOE_CHUNK), MOE_CHUNK), :]
    return pltpu.make_async_copy(vm, hb, sem) if to_hbm else pltpu.make_async_copy(hb, vm, sem)


def _for_each_chunk(count, fn):
    def group(t, carry):
        for u in range(CHUNK_UNROLL):
            fn(t * CHUNK_UNROLL + u)
        return carry

    def single(c, carry):
        fn(c)
        return carry

    whole = count // CHUNK_UNROLL
    lax.fori_loop(0, whole, group, 0)
    lax.fori_loop(whole * CHUNK_UNROLL, count, single, 0)


def _wait_chunks(hbm, buf, sem, count, *, to_hbm):
    _for_each_chunk(count, lambda c: _chunk_copy(hbm, 0, buf, 0, sem, to_hbm=to_hbm).wait())


def _dispatch_body(dst_tab, n_chunks, x_ref, route_ref, off_ref, xs_in, xs_hbm, loc, sem):
    del xs_in
    i = pl.program_id(0)
    last = pl.num_programs(0) - 1
    cur = i % 2
    slot_a, slot_b = _local_slots(route_ref[...], off_ref[0])
    buf_row = lax.broadcasted_iota(I32, (LOC_ROWS, 1), 0).astype(F32)
    onehot = ((buf_row == slot_a) | (buf_row == slot_b)).astype(BF16)
    loc[cur] = _pack_rows(_dot(onehot, x_ref[...].astype(BF16)))

    _for_each_chunk(n_chunks[i], lambda c: _chunk_copy(
        xs_hbm, dst_tab[i * MAX_CHUNKS + c], loc.at[cur], c, sem.at[cur], to_hbm=True).start())

    @pl.when(i > 0)
    def _():
        _wait_chunks(xs_hbm, loc.at[1 - cur], sem.at[1 - cur], n_chunks[jnp.maximum(i - 1, 0)], to_hbm=True)

    @pl.when(i == last)
    def _():
        _wait_chunks(xs_hbm, loc.at[cur], sem.at[cur], n_chunks[i], to_hbm=True)


def _dispatch(dst_tab, n_chunks, x1, route, off, xs_prev):
    nt, d = x1.shape
    row = lambda width: pl.BlockSpec((TD, width), lambda i, *_: (i, 0))
    nbytes = 2 * (TD * d * 4 + TD * LANE * 4) + 2 * LOC_ROWS * d * 4 + 4 * TD * LOC_ROWS * 4 + LOC_ROWS * d * 4
    return pl.pallas_call(
        _dispatch_body,
        grid_spec=pltpu.PrefetchScalarGridSpec(
            num_scalar_prefetch=2,
            grid=(nt // TD,),
            in_specs=[row(d), pl.BlockSpec((SUBLANE, TD), lambda i, *_: (0, i)),
                      pl.BlockSpec((1, N_EXPERTS, 1), lambda i, *_: (i, 0, 0)),
                      pl.BlockSpec(memory_space=pl.ANY)],
            out_specs=pl.BlockSpec(memory_space=pl.ANY),
            scratch_shapes=[pltpu.VMEM((2, LOC_ROWS, d // 2), U32), pltpu.SemaphoreType.DMA((2,))]),
        out_shape=jax.ShapeDtypeStruct(xs_prev.shape, U32),
        input_output_aliases={5: 0},
        compiler_params=_params(("arbitrary",), nbytes),
        name="moe_dispatch",
    )(dst_tab, n_chunks, x1, route, off, xs_prev)


def _ffn_body(blk_e, blk_valid, blk_src, x_ref, w1_ref, w3_ref, w2_ref, y_ref, w1b, w3b, w2b):
    del blk_src
    j = pl.program_id(0)

    @pl.when(blk_valid[j] > 0)
    def _():
        @pl.when((j == 0) | (blk_e[j] != blk_e[jnp.maximum(j - 1, 0)]))
        def _():
            w1b[...] = w1_ref[0, 0].astype(BF16)
            w3b[...] = w3_ref[0, 0].astype(BF16)
            w2b[...] = w2_ref[0, 0].astype(BF16)

        xb = _unpack_rows(x_ref[...]).astype(BF16)
        h1 = _dot(xb, w1b[...])
        h3 = _dot(xb, w3b[...])
        act = (h1 * _sigmoid(h1)) * h3
        y_ref[...] = _pack_rows(_dot(act.astype(BF16), w2b[...]))


def _ffn(xs, w1, w3, w2, layer, blk_e, blk_valid, blk_src):
    d = w1.shape[2]
    de = w1.shape[3]
    n_blk = xs.shape[0] // MOE_BLK
    wmap = lambda j, be, bv, bs: (layer, be[j], 0, 0)
    rows = pl.BlockSpec((MOE_BLK, d // 2), lambda j, be, bv, bs: (bs[j], 0))
    nbytes = 2 * (3 * d * de * 4 + MOE_BLK * d * 4) + 3 * d * de * 2 + 6 * MOE_BLK * de * 4 + 4 * MOE_BLK * d * 4
    return pl.pallas_call(
        _ffn_body,
        grid_spec=pltpu.PrefetchScalarGridSpec(
            num_scalar_prefetch=3,
            grid=(n_blk,),
            in_specs=[rows, pl.BlockSpec((1, 1, d, de), wmap), pl.BlockSpec((1, 1, d, de), wmap),
                      pl.BlockSpec((1, 1, de, d), wmap)],
            out_specs=rows,
            scratch_shapes=[pltpu.VMEM((d, de), BF16), pltpu.VMEM((d, de), BF16), pltpu.VMEM((de, d), BF16)]),
        out_shape=jax.ShapeDtypeStruct(xs.shape, U32),
        input_output_aliases={3: 0},
        compiler_params=_params(("arbitrary",), nbytes),
        name="expert_ffn",
    )(blk_e, blk_valid, blk_src, xs, w1, w3, w2)


def _combine_body(dst_tab, n_chunks, x_ref, route_ref, off_ref, g_ref, b_ref, ys_hbm, x2_ref,
                  loc, sem, *, alpha):
    i = pl.program_id(0)
    last = pl.num_programs(0) - 1
    cur = i % 2

    def fetch(tile, buf):
        _for_each_chunk(n_chunks[tile], lambda c: _chunk_copy(
            ys_hbm, dst_tab[tile * MAX_CHUNKS + c], loc.at[buf], c, sem.at[buf], to_hbm=False).start())

    @pl.when(i == 0)
    def _():
        loc[...] = jnp.zeros_like(loc)
        fetch(0, 0)

    @pl.when(i < last)
    def _():
        fetch(jnp.minimum(i + 1, last), 1 - cur)

    route = route_ref[...]
    slot_a, slot_b = _local_slots(route, off_ref[0])
    buf_row = lax.broadcasted_iota(I32, (LOC_ROWS, 1), 0).astype(F32)
    gates = (jnp.where(buf_row == slot_a, route[2:3, :], 0.0)
             + jnp.where(buf_row == slot_b, route[3:4, :], 0.0))
    _wait_chunks(ys_hbm, loc.at[cur], sem.at[cur], n_chunks[i], to_hbm=False)
    f = _dot_tn(gates.astype(BF16), _unpack_rows(loc[cur]).astype(BF16))
    x2_ref[...] = _layer_norm(alpha * x_ref[...] + f, g_ref[...], b_ref[...])


def _combine(dst_tab, n_chunks, x1, route, off, g, b, ys, *, alpha):
    nt, d = x1.shape
    row = lambda width: pl.BlockSpec((TD, width), lambda i, dt, nc: (i, 0))
    const = lambda shape: pl.BlockSpec(shape, lambda i, dt, nc: (0, 0))
    nbytes = 2 * (2 * TD * d * 4 + TD * LANE * 4) + LOC_ROWS * d * 4 + 4 * TD * LOC_ROWS * 4 + LOC_ROWS * d * 2
    return pl.pallas_call(
        functools.partial(_combine_body, alpha=alpha),
        grid_spec=pltpu.PrefetchScalarGridSpec(
            num_scalar_prefetch=2,
            grid=(nt // TD,),
            in_specs=[row(d), pl.BlockSpec((SUBLANE, TD), lambda i, dt, nc: (0, i)),
                      pl.BlockSpec((1, N_EXPERTS, 1), lambda i, dt, nc: (i, 0, 0)),
                      const((1, d)), const((1, d)), pl.BlockSpec(memory_space=pl.ANY)],
            out_specs=row(d),
            scratch_shapes=[pltpu.VMEM((2, LOC_ROWS, d // 2), U32), pltpu.SemaphoreType.DMA((2,))]),
        out_shape=jax.ShapeDtypeStruct((nt, d), F32),
        compiler_params=_params(("arbitrary",), nbytes + LOC_ROWS * d * 4),
        name="combine_ln",
    )(dst_tab, n_chunks, x1, route, off, g, b, ys)


def _rope_tables(pos, rot_dim, theta):
    half = rot_dim // 2
    inv = jnp.float32(theta) ** (-jnp.arange(half, dtype=F32) * (2.0 / rot_dim))
    ang = pos.astype(F32)[:, None] * inv[None, :]
    c, s = jnp.cos(ang), jnp.sin(ang)
    rest = HEAD_DIM - rot_dim
    ones = jnp.ones((pos.shape[0], rest), F32)
    zeros = jnp.zeros((pos.shape[0], rest), F32)
    cos_h = jnp.concatenate([c, c, ones], axis=1)
    sin_h = jnp.concatenate([-s, s, zeros], axis=1)
    return jnp.tile(cos_h, (1, N_HEADS)), jnp.tile(sin_h, (1, N_HEADS))


def _block_diag(s):
    rows = [jnp.pad(s[:, h], ((0, 0), (0, 0), (h * HEAD_DIM, MIX - (h + 1) * HEAD_DIM))) for h in range(N_HEADS)]
    return jnp.concatenate(rows, axis=1)


def _diag_blocks(s):
    return jnp.stack([s[:, h * HEAD_DIM:(h + 1) * HEAD_DIM, h * HEAD_DIM:(h + 1) * HEAD_DIM]
                      for h in range(N_HEADS)], axis=1)


def _routing_plan(tile_counts, n_blk):
    cnt_t = tile_counts[:, N_GROUPS:N_GROUPS + N_EXPERTS, 0].astype(I32)
    run_chunks = (cnt_t + MOE_CHUNK - 1) // MOE_CHUNK
    run_rows = run_chunks * MOE_CHUNK
    cnt = jnp.sum(run_rows, axis=0)
    pcnt = (cnt + MOE_BLK - 1) // MOE_BLK * MOE_BLK
    pstart = jnp.cumsum(pcnt) - pcnt
    n_act = jnp.sum(pcnt) // MOE_BLK
    gstart = pstart[None, :] + jnp.cumsum(run_rows, axis=0) - run_rows
    chunk0 = jnp.cumsum(run_chunks, axis=1) - run_chunks
    n_chunks = jnp.sum(run_chunks, axis=1).astype(I32)

    def spread(first, count, value, n):
        p = jnp.arange(n, dtype=I32)[:, None]
        f, c, v = first[..., None, :], count[..., None, :], value[..., None, :]
        return jnp.sum(jnp.where((f <= p) & (p < f + c), v + (p - f) * MOE_CHUNK, 0), axis=-1)

    dst_tab = spread(chunk0, run_chunks, gstart, MAX_CHUNKS)
    off = (chunk0 * MOE_CHUNK).astype(F32)[:, :, None]
    pos = jnp.arange(n_blk, dtype=I32)[:, None] * MOE_BLK
    mine = (pstart[None, :] <= pos) & (pos < (pstart + pcnt)[None, :])
    experts = jnp.arange(N_EXPERTS, dtype=I32)[None, :]
    last_e = jnp.max(jnp.where(cnt > 0, experts[0], 0))
    active = pos[:, 0] < n_act * MOE_BLK
    blk_e = jnp.where(active, jnp.sum(jnp.where(mine, experts, 0), axis=-1), last_e)
    blk_valid = jnp.sum(jnp.where(mine, jnp.clip((pstart + cnt)[None, :] - pos, 0, MOE_BLK), 0), axis=-1)
    blk_src = jnp.minimum(jnp.arange(n_blk, dtype=I32), jnp.maximum(n_act - 1, 0))
    i32 = lambda t: t.astype(I32)
    return i32(dst_tab).reshape(-1), n_chunks, off, i32(blk_e), i32(blk_valid), i32(blk_src)


def kernel(x_prompt, x_sample, state_ret, cache_swa_k, cache_swa_v, state_conv, w_in, w_out, ret_gn_g,
           conv_w, conv_b, conv_ln_g, conv_ln_b, sgu_ln_g, sgu_ln_b, sgu_w, sgu_b, ln1_g, ln1_b,
           ln2_g, ln2_b, router_g_w, router_g_b, router_e_w, router_e_b, moe_w1, moe_w3, moe_w2):
    depth = w_in.shape[0]
    n_seq, seq, d = x_prompt.shape
    n_dec, t_new, _ = x_sample.shape
    past = cache_swa_k.shape[2]
    rows_s = -(-t_new // SUBLANE) * SUBLANE
    n_p = n_seq * seq
    n_s = n_dec * rows_s
    n_tail = -(-n_s // TM) * TM
    nt = n_p + n_tail
    assert seq % SWA_SB == 0 and seq % TM == 0 and TM % TD == 0
    assert t_new % RET_CHUNK != 0 and t_new <= SGU_CHUNK and w_in.shape[2] == N_COLBLK * MIX
    assert n_dec % SAMPLE_SEQS == 0 and n_dec % SWA_SAMPLE_SEQS == 0
    assert n_p % (SAMPLE_SEQS * rows_s) == 0
    alpha = (2 * depth) ** 0.25
    keep = min(DILATED_CONFIGS[-1][0], seq)
    n_blk = -(-(nt * TOP_K + (nt // TD) * N_EXPERTS * (MOE_CHUNK - 1) + N_EXPERTS * (MOE_BLK - 1)) // MOE_BLK)
    cache_k = jnp.transpose(cache_swa_k, (0, 1, 3, 4, 2)).reshape(depth, n_dec, MIX, past)
    cache_v = jnp.transpose(cache_swa_v, (0, 1, 3, 4, 2)).reshape(depth, n_dec, MIX, past)

    x = jnp.concatenate([x_prompt.reshape(n_p, d),
                         jnp.pad(x_sample, ((0, 0), (0, rows_s - t_new), (0, 0))).reshape(n_s, d),
                         jnp.zeros((n_tail - n_s, d), F32)], axis=0)
    pos = jnp.concatenate([jnp.arange(seq, dtype=I32),
                           PAST_LEN + jnp.arange(n_tail, dtype=I32) % rows_s])
    rope = _rope_tables(pos, HEAD_DIM, RET_THETA) + _rope_tables(pos, ROPE_DIM, ROPE_THETA)
    ret_tabs_p = _retention_tables(RET_CHUNK, RET_CHUNK)
    ret_tabs_s = _retention_tables(t_new, rows_s)
    lane_head = jnp.arange(MIX) // HEAD_DIM
    avg = ((lane_head[:, None] == lane_head[None, :]).astype(F32) / HEAD_DIM).astype(BF16)
    row2 = lambda v: v.reshape(1, -1)
    mix = jnp.zeros((nt, 4 * MIX), F32)
    slots = jnp.zeros((n_blk * MOE_BLK, d // 2), U32)

    outs = [[] for _ in range(9)]
    for l in range(depth):
        h = _in_proj(x, w_in, l, rope, n_p // TM, seq // TM)
        gn = row2(ret_gn_g[l])
        cw = jnp.pad(conv_w[l], ((0, CONV_CTX - CONV_WIDTH), (0, 0)))
        cargs = (cw, row2(conv_b[l]), row2(conv_ln_g[l]), row2(conv_ln_b[l]))
        sbias = jnp.repeat(sgu_b[l].T, HEAD_DIM, axis=1)
        sargs = (row2(sgu_ln_g[l]), row2(sgu_ln_b[l]), sgu_w[l], sbias)
        mix, glu_p, s_p = _mixers_prompt(h, mix, ret_tabs_p, avg, gn, cargs, sargs, seq=seq, n_seq=n_seq)
        mix = _swa_prompt(h, mix, seq=seq, n_seq=n_seq)
        mix, s_s = _retention(h, mix, _block_diag(state_ret[l]), ret_tabs_s, avg, gn, rows=rows_s, n_seq=n_dec,
                              n_chunks=1, seqs=SAMPLE_SEQS, chunks=1, row0=n_p)
        mix = _swa_sample(h, mix, cache_k, cache_v, l, rows=rows_s, n_seq=n_dec, seqs=SWA_SAMPLE_SEQS, row0=n_p)
        ctx_s = jnp.pad(state_conv[l], ((0, 0), (CONV_CTX - (CONV_WIDTH - 1), 0), (0, 0)))
        mix, glu_s = _conv(h, mix, ctx_s, *cargs, rows=rows_s, n_seq=n_dec, n_tiles=1, row0=n_p)
        mix, vn_s = _sgu(h, mix, *sargs, avg, rows=SAMPLE_SEQS * rows_s, sub=rows_s, n_steps=n_dec // SAMPLE_SEQS,
                         row0=n_p, emit_vn=True)

        wr = jnp.concatenate([router_g_w[l].T, jnp.transpose(router_e_w[l], (0, 2, 1)).reshape(N_EXPERTS, d),
                              jnp.zeros((LANE - N_GROUPS - N_EXPERTS, d), F32)], axis=0)
        br = jnp.concatenate([router_g_b[l], router_e_b[l].reshape(-1),
                              jnp.zeros((LANE - N_GROUPS - N_EXPERTS,), F32)]).reshape(LANE, 1)
        x1, route, counts = _out_router(mix, w_out, l, x, row2(ln1_g[l]), row2(ln1_b[l]), wr, br, alpha=alpha)
        dst_tab, n_chunks, off, blk_e, blk_valid, blk_src = _routing_plan(counts, n_blk)
        slots = _dispatch(dst_tab, n_chunks, x1, route, off, slots)
        slots = _ffn(slots, moe_w1, moe_w3, moe_w2, l, blk_e, blk_valid, blk_src)
        x = _combine(dst_tab, n_chunks, x1, route, off, row2(ln2_g[l]), row2(ln2_b[l]), slots, alpha=alpha)

        def tail(col, n_keep):
            return jnp.stack([lax.slice(h, ((b + 1) * seq - n_keep, col * MIX), ((b + 1) * seq, (col + 1) * MIX))
                              for b in range(n_seq)])

        def new_rows(col):
            blk = lax.slice(h, (n_p, col * MIX), (n_p + n_s, (col + 1) * MIX))
            return blk.reshape(n_dec, rows_s, MIX)[:, :t_new]

        heads = lambda t: t.reshape(t.shape[0], t.shape[1], N_HEADS, HEAD_DIM)
        outs[0].append(_diag_blocks(s_p))
        outs[1].append(heads(tail(COL_SK, keep)))
        outs[2].append(heads(tail(COL_SV, keep)))
        outs[3].append(glu_p.reshape(n_seq, seq, MIX)[:, seq - (CONV_WIDTH - 1):])
        outs[4].append(_diag_blocks(s_s))
        outs[5].append(heads(new_rows(COL_SK)))
        outs[6].append(heads(new_rows(COL_SV)))
        outs[7].append(glu_s.reshape(n_dec, rows_s, MIX)[:, :t_new])
        outs[8].append(vn_s.reshape(n_dec, rows_s, MIX)[:, :t_new])

    y_prompt = x[:n_p].reshape(n_seq, seq, d)
    y_sample = x[n_p:n_p + n_s].reshape(n_dec, rows_s, d)[:, :t_new]
    return (y_prompt, y_sample) + tuple(jnp.stack(o) for o in outs)
```

```python
import functools
import math

import jax
import jax.numpy as jnp
from jax import lax
from jax.experimental import pallas as pl
from jax.experimental.pallas import tpu as pltpu

F32 = jnp.float32
BF16 = jnp.bfloat16
I32 = jnp.int32
U32 = jnp.uint32

HEAD_DIM = 64
N_HEADS = 4
MIX = N_HEADS * HEAD_DIM
(COL_RQ, COL_RK, COL_RV, COL_RG, COL_SQ, COL_SK, COL_SV,
 COL_CV, COL_CG, COL_GU, COL_GV) = range(11)
N_COLBLK = 11
RET_CHUNK = 128
RET_THETA = 10000.0
ROPE_THETA = 500000.0
ROPE_DIM = HEAD_DIM // 4
DILATED_CONFIGS = ((128, 1), (512, 4), (2048, 16))
DIL_BLOCK = 128
CONV_WIDTH = 31
SGU_CHUNK = 128
N_GROUPS = 4
EXPERTS_PER_GROUP = 8
N_EXPERTS = N_GROUPS * EXPERTS_PER_GROUP
TOP_K = 2
PAST_LEN = 8192
LN_EPS = 1e-5
NEG_INF = -1e30

LANE = 128
SUBLANE = 8
VMEM_CAP_BYTES = 60000 * 1024
COMPILER_TEMP_BYTES = 12 * 1024 * 1024

TM = 512
TD = 256
OUT_ROWS = 128
ROUTE_ROWS = -(-(N_GROUPS + N_EXPERTS) // SUBLANE) * SUBLANE
MOE_BLK = 512
MOE_CHUNK = SUBLANE
LOC_ROWS = -(-(TD * TOP_K + N_EXPERTS * (MOE_CHUNK - 1)) // LANE) * LANE
MAX_CHUNKS = LOC_ROWS // MOE_CHUNK
CHUNK_UNROLL = 4
MIX_ROWS = 1024
CONV_CTX = 32
SWA_SB = DIL_BLOCK * 16
SWA_TILE = 256
SWA_UNROLL = 4
SAMPLE_SEQS = 8
SWA_SAMPLE_SEQS = 4


def _vmem(nbytes):
    return int(min(VMEM_CAP_BYTES, nbytes + COMPILER_TEMP_BYTES))


def _params(sem, nbytes):
    return pltpu.CompilerParams(dimension_semantics=sem, vmem_limit_bytes=_vmem(nbytes))


def _dot(a, b):
    return jnp.dot(a, b, preferred_element_type=F32)


def _dot_nt(a, b):
    return lax.dot_general(a, b, (((1,), (1,)), ((), ())), preferred_element_type=F32)


def _dot_tn(a, b):
    return lax.dot_general(a, b, (((0,), (0,)), ((), ())), preferred_element_type=F32)


def _split_dot(x, m):
    hi = x.astype(BF16)
    lo = (x - hi.astype(F32)).astype(BF16)
    return _dot(hi, m) + _dot(lo, m)


def _sigmoid(x):
    return 1.0 / (1.0 + jnp.exp(-x))


def _layer_norm(z, g, b):
    mu = jnp.mean(z, axis=-1, keepdims=True)
    d = z - mu
    var = jnp.mean(d * d, axis=-1, keepdims=True)
    return d * lax.rsqrt(var + LN_EPS) * g + b


def _group_norm(z, avg):
    mu = _split_dot(z, avg)
    d = z - mu
    var = _split_dot(d * d, avg)
    return d * lax.rsqrt(var + LN_EPS)


def _lane_head(width):
    return lax.broadcasted_iota(I32, (1, width), 1) // HEAD_DIM


def _rotate(x, cos, sin_signed, half):
    first = (lax.broadcasted_iota(I32, (1, LANE), 1) % HEAD_DIM) < half
    parts = []
    for p in range(x.shape[1] // LANE):
        t = x[:, p * LANE:(p + 1) * LANE]
        up = pltpu.roll(t, half, 1)
        down = pltpu.roll(t, LANE - half, 1)
        parts.append(jnp.where(first, down, up))
    return x * cos + jnp.concatenate(parts, axis=1) * sin_signed


def _in_proj_body(x_ref, w_ref, cr_ref, sr_ref, cs_ref, ss_ref, h_ref, wbf_ref):
    @pl.when(pl.program_id(0) == 0)
    def _():
        for j in range(N_COLBLK):
            wbf_ref[:, j * MIX:(j + 1) * MIX] = w_ref[0, :, j * MIX:(j + 1) * MIX].astype(BF16)

    xb = x_ref[...].astype(BF16)
    for j in range(N_COLBLK):
        hj = _dot(xb, wbf_ref[:, j * MIX:(j + 1) * MIX])
        if j in (COL_RQ, COL_RK):
            hj = _rotate(hj, cr_ref[...], sr_ref[...], HEAD_DIM // 2)
        if j == COL_RK:
            hj = hj * (HEAD_DIM ** -0.5)
        if j in (COL_SQ, COL_SK):
            hj = _rotate(hj, cs_ref[...], ss_ref[...], ROPE_DIM // 2)
        h_ref[:, j * MIX:(j + 1) * MIX] = hj


def _in_proj(x, w, layer, tabs, n_prompt_tiles, tiles_per_seq):
    nt, d = x.shape
    width = w.shape[2]

    def tab_map(i):
        return (jnp.where(i < n_prompt_tiles, i % tiles_per_seq, tiles_per_seq + i - n_prompt_tiles), 0)

    tab_spec = pl.BlockSpec((TM, MIX), tab_map)
    nbytes = d * width * 4 + d * width * 2 + 2 * TM * d * 4 + 8 * TM * MIX * 4 + 2 * TM * width * 4
    return pl.pallas_call(
        _in_proj_body,
        grid=(nt // TM,),
        in_specs=[pl.BlockSpec((TM, d), lambda i: (i, 0)),
                  pl.BlockSpec((1, d, width), lambda i: (layer, 0, 0), pipeline_mode=pl.Buffered(1)),
                  tab_spec, tab_spec, tab_spec, tab_spec],
        out_specs=pl.BlockSpec((TM, width), lambda i: (i, 0)),
        out_shape=jax.ShapeDtypeStruct((nt, width), F32),
        scratch_shapes=[pltpu.VMEM((d, width), BF16)],
        compiler_params=_params(("arbitrary",), nbytes),
        name="in_proj",
    )(x, w, *tabs)


def _retention_chunk(q, k, v, g, state, tabs):
    dec_ref, qdec_ref, kdec_ref, cmat_ref, bdm_ref, avg_ref, gn_ref = tabs
    head = _lane_head(MIX)
    vb = v.astype(BF16)
    kb = k.astype(BF16)
    o = _dot(q.astype(BF16), state.astype(BF16)) * qdec_ref[...]
    for h in range(N_HEADS):
        mh = head == h
        a = _dot_nt(jnp.where(mh, q, 0.0).astype(BF16), kb) * dec_ref[h]
        o = o + jnp.where(mh, _dot(a.astype(BF16), vb), 0.0)
    upd = _dot_tn((k * kdec_ref[...]).astype(BF16), vb)
    state = state * cmat_ref[...] + upd * bdm_ref[...]
    return _group_norm(o, avg_ref[...]) * gn_ref[...] * (g * _sigmoid(g)), state


def _retention_tables(l_real, l_pad):
    hh = jnp.arange(N_HEADS, dtype=F32)
    log_g = jnp.log1p(-jnp.exp2(-5.0 - hh))
    i = jnp.arange(l_pad, dtype=F32)
    rel = i[:, None] - i[None, :]
    dec = jnp.where(rel >= 0, jnp.exp(log_g[:, None, None] * jnp.maximum(rel, 0.0)), 0.0)
    qd = jnp.exp(log_g[:, None] * (i + 1.0))
    kd = jnp.where(i < l_real, jnp.exp(log_g[:, None] * (l_real - 1.0 - i)), 0.0)
    cd = jnp.exp(log_g * l_real)
    expand = lambda t: jnp.repeat(t.T, HEAD_DIM, axis=1)
    row_head = jnp.arange(MIX) // HEAD_DIM
    bdm = (row_head[:, None] == row_head[None, :]).astype(F32)
    cmat = cd[row_head][:, None] * bdm
    return dec, expand(qd), expand(kd), cmat, bdm


def _swa_prompt_body(q_ref, k_ref, v_ref, mix_ref, y_ref, o0, o1, o2, l0, l1, l2, *, seq):
    del mix_ref
    o_scr = (o0, o1, o2)
    l_scr = (l0, l1, l2)
    head = _lane_head(LANE)
    qi = lax.broadcasted_iota(I32, (DIL_BLOCK, 1), 0)
    ki = lax.broadcasted_iota(I32, (1, 2 * DIL_BLOCK), 1)
    rel = DIL_BLOCK + qi - ki
    scale = HEAD_DIM ** -0.5
    for sb in range(seq // SWA_SB):
        for ci, (window, dil) in enumerate(DILATED_CONFIGS):
            span = window // dil
            band = (rel >= 0) & (rel <= span)

            def block(i, carry, sb=sb, ci=ci, dil=dil, band=band):
                nloc = i // dil
                off = nloc * (DIL_BLOCK * dil) + i % dil
                start = sb * SWA_SB + off
                if sb == 0:
                    first = nloc == 0
                    pstart = jnp.where(first, start, start - DIL_BLOCK * dil)
                    valid = band & (ki >= jnp.where(first, DIL_BLOCK, 0))
                else:
                    pstart = start - DIL_BLOCK * dil
                    valid = band
                stride = dil if dil > 1 else None
                cur = pl.ds(start, DIL_BLOCK, stride=stride)
                prev = pl.ds(pstart, DIL_BLOCK, stride=stride)
                qb = q_ref[cur, :] * scale
                kb = jnp.concatenate([k_ref[prev, :], k_ref[cur, :]], axis=0).astype(BF16)
                vb = jnp.concatenate([v_ref[prev, :], v_ref[cur, :]], axis=0).astype(BF16)
                o = jnp.zeros((DIL_BLOCK, LANE), F32)
                ls = jnp.zeros((DIL_BLOCK, LANE), F32)
                for hh in range(LANE // HEAD_DIM):
                    mh = head == hh
                    s = _dot_nt(jnp.where(mh, qb, 0.0).astype(BF16), kb)
                    s = jnp.where(valid, s, NEG_INF)
                    m = jnp.max(s, axis=-1, keepdims=True)
                    p = jnp.exp(s - m)
                    l = jnp.sum(p, axis=-1, keepdims=True)
                    o = jnp.where(mh, _dot(p.astype(BF16), vb) / l, o)
                    ls = jnp.where(mh, m + jnp.log(l), ls)
                dst = pl.ds(off, DIL_BLOCK, stride=stride)
                o_scr[ci][dst, :] = o
                l_scr[ci][dst, :] = ls
                return carry

            lax.fori_loop(0, SWA_SB // DIL_BLOCK, block, 0, unroll=SWA_UNROLL)

        def combine(t, carry, sb=sb):
            rows = pl.ds(t * SWA_TILE, SWA_TILE)
            la, lb, lc = l_scr[0][rows, :], l_scr[1][rows, :], l_scr[2][rows, :]
            mx = jnp.maximum(jnp.maximum(la, lb), lc)
            ea, eb, ec = jnp.exp(la - mx), jnp.exp(lb - mx), jnp.exp(lc - mx)
            num = ea * o_scr[0][rows, :] + eb * o_scr[1][rows, :] + ec * o_scr[2][rows, :]
            y_ref[pl.ds(sb * SWA_SB + t * SWA_TILE, SWA_TILE), :] = num / (ea + eb + ec)
            return carry

        lax.fori_loop(0, SWA_SB // SWA_TILE, combine, 0)


def _swa_prompt(h, mix, *, seq, n_seq):
    halves = MIX // LANE

    def hspec(col):
        return pl.BlockSpec((seq, LANE), lambda b, p: (b, col * halves + p))

    nbytes = 2 * 4 * seq * LANE * 4 + 6 * SWA_SB * LANE * 4
    return pl.pallas_call(
        functools.partial(_swa_prompt_body, seq=seq),
        grid=(n_seq, halves),
        in_specs=[hspec(COL_SQ), hspec(COL_SK), hspec(COL_SV), pl.BlockSpec(memory_space=pl.ANY)],
        out_specs=pl.BlockSpec((seq, LANE), lambda b, p: (b, halves + p)),
        out_shape=jax.ShapeDtypeStruct(mix.shape, F32),
        scratch_shapes=[pltpu.VMEM((SWA_SB, LANE), F32)] * 6,
        input_output_aliases={3: 0},
        compiler_params=_params(("arbitrary", "arbitrary"), nbytes),
        name="swa_prompt",
    )(h, h, h, mix)


def _swa_sample_body(q_ref, k_ref, v_ref, ck_ref, cv_ref, mix_ref, y_ref, *, past, rows, seqs):
    del mix_ref
    head = _lane_head(MIX)
    hr = N_HEADS * rows
    scale = HEAD_DIM ** -0.5
    t_row = lax.broadcasted_iota(I32, (hr, 1), 0) % rows
    rel_c = past + t_row - lax.broadcasted_iota(I32, (1, past), 1)
    rel_n = t_row - lax.broadcasted_iota(I32, (1, LANE), 1)
    valid_c, valid_n = [], []
    for window, dil in DILATED_CONFIGS:
        span = window // dil
        shift = int(math.log2(dil))
        for rel, dst in ((rel_c, valid_c), (rel_n, valid_n)):
            dst.append((rel >= 0) & ((rel & (dil - 1)) == 0) & ((rel >> shift) <= span))
    zpad = jnp.zeros((LANE - rows, MIX), F32)
    for s in range(seqs):
        rs = slice(s * rows, (s + 1) * rows)
        q = q_ref[rs, :]
        qm = jnp.concatenate([jnp.where(head == h, q, 0.0) for h in range(N_HEADS)], axis=0).astype(BF16)
        ck = ck_ref[0, s].astype(BF16)
        cv = cv_ref[0, s].astype(BF16)
        kn = jnp.concatenate([k_ref[rs, :], zpad], axis=0).astype(BF16)
        vn = jnp.concatenate([v_ref[rs, :], zpad], axis=0).astype(BF16)
        s_c = _dot(qm, ck) * scale
        s_n = _dot_nt(qm, kn) * scale
        pcs, pns, ls, lses = [], [], [], []
        for ci in range(len(DILATED_CONFIGS)):
            sc = jnp.where(valid_c[ci], s_c, NEG_INF)
            sn = jnp.where(valid_n[ci], s_n, NEG_INF)
            m = jnp.maximum(jnp.max(sc, axis=-1, keepdims=True), jnp.max(sn, axis=-1, keepdims=True))
            pc = jnp.exp(sc - m)
            pn = jnp.exp(sn - m)
            l = jnp.sum(pc, axis=-1, keepdims=True) + jnp.sum(pn, axis=-1, keepdims=True)
            pcs.append(pc.astype(BF16))
            pns.append(pn.astype(BF16))
            ls.append(l)
            lses.append(m + jnp.log(l))
        o_all = _dot_nt(jnp.concatenate(pcs, axis=0), cv) + _dot(jnp.concatenate(pns, axis=0), vn)
        outs = [o_all[ci * hr:(ci + 1) * hr, :] / ls[ci] for ci in range(len(DILATED_CONFIGS))]
        mx = jnp.maximum(jnp.maximum(lses[0], lses[1]), lses[2])
        es = [jnp.exp(le - mx) for le in lses]
        y_all = (es[0] * outs[0] + es[1] * outs[1] + es[2] * outs[2]) / (es[0] + es[1] + es[2])
        y = jnp.zeros((rows, MIX), F32)
        for h in range(N_HEADS):
            y = jnp.where(head == h, y_all[h * rows:(h + 1) * rows, :], y)
        y_ref[rs, :] = y


def _swa_sample(h, mix, cache_k, cache_v, layer, *, rows, n_seq, seqs, row0):
    past = cache_k.shape[3]
    blk = seqs * rows
    rb0 = row0 // blk

    def hspec(col):
        return pl.BlockSpec((blk, MIX), lambda b: (rb0 + b, col))

    cspec = pl.BlockSpec((1, seqs, MIX, past), lambda b: (layer, b, 0, 0))
    nbytes = 2 * (2 * seqs * past * MIX * 4 + 4 * blk * MIX * 4) + 16 * N_HEADS * rows * past * 4 + 2 * past * MIX * 2
    return pl.pallas_call(
        functools.partial(_swa_sample_body, past=past, rows=rows, seqs=seqs),
        grid=(n_seq // seqs,),
        in_specs=[hspec(COL_SQ), hspec(COL_SK), hspec(COL_SV), cspec, cspec,
                  pl.BlockSpec(memory_space=pl.ANY)],
        out_specs=pl.BlockSpec((blk, MIX), lambda b: (rb0 + b, 1)),
        out_shape=jax.ShapeDtypeStruct(mix.shape, F32),
        input_output_aliases={5: 0},
        compiler_params=_params(("arbitrary",), nbytes),
        name="swa_sample",
    )(h, h, h, cache_k, cache_v, mix)


def _conv_rows(ctx, base, sub, w_ref, cb_ref, g_ref, b_ref):
    lead = CONV_CTX - (CONV_WIDTH - 1)
    window = ctx[base:base + sub + CONV_CTX, :]
    acc = jnp.zeros((sub, MIX), F32)
    for r in range(SUBLANE):
        shifted = pltpu.roll(window, sub + CONV_CTX - (lead + r), 0)
        for j in range(r, CONV_WIDTH, SUBLANE):
            acc = acc + shifted[j - r:j - r + sub, :] * w_ref[j:j + 1, :]
    z = _layer_norm(acc + cb_ref[...], g_ref[...], b_ref[...])
    return z * _sigmoid(z)


def _gelu_tanh(x):
    c = math.sqrt(2.0 / math.pi)
    return x * (0.5 * (1.0 + jnp.tanh(c * (x + 0.044715 * (x * x * x)))))


def _sgu_weights(ws_ref):
    tri = (lax.broadcasted_iota(I32, (SGU_CHUNK, SGU_CHUNK), 0)
           >= lax.broadcasted_iota(I32, (SGU_CHUNK, SGU_CHUNK), 1))
    return [jnp.where(tri, ws_ref[gi], 0.0).astype(BF16) for gi in range(N_HEADS)]


def _sgu_chunk(u_raw, v_raw, wms, tabs):
    g_ref, b_ref, bias_ref, avg_ref = tabs
    sub = u_raw.shape[0]
    head = _lane_head(MIX)
    u = _gelu_tanh(u_raw)
    vn = _group_norm(_gelu_tanh(v_raw), avg_ref[...]) * g_ref[...] + b_ref[...]
    vp = vn if sub == SGU_CHUNK else jnp.concatenate([vn, jnp.zeros((SGU_CHUNK - sub, MIX), F32)], axis=0)
    vnb = vp.astype(BF16)
    z = jnp.zeros((SGU_CHUNK, MIX), F32)
    for gi in range(N_HEADS):
        z = jnp.where(head == gi, _dot(wms[gi], vnb), z)
    z = z + bias_ref[...]
    return u * z[0:sub, :], vn


def _mixers_body(q_ref, k_ref, v_ref, g_ref, cv_ref, cg_ref, gu_ref, gv_ref,
                 dec_ref, qdec_ref, kdec_ref, cmat_ref, bdm_ref, avg_ref, gn_ref,
                 cw_ref, cb_ref, clg_ref, clb_ref, sg_ref, sb_ref, ws_ref, bias_ref, mix_in,
                 mix_ref, glu_ref, sout_ref, s_scr, ctx, *, n_steps, rows):
    del mix_in
    c = pl.program_id(1)

    @pl.when(c == 0)
    def _():
        s_scr[...] = jnp.zeros_like(s_scr)
        ctx[0:CONV_CTX, :] = jnp.zeros((CONV_CTX, MIX), F32)

    ret_tabs = (dec_ref, qdec_ref, kdec_ref, cmat_ref, bdm_ref, avg_ref, gn_ref)
    sgu_tabs = (sg_ref, sb_ref, bias_ref, avg_ref)
    wms = _sgu_weights(ws_ref)
    glu = cv_ref[...] * _sigmoid(cg_ref[...])
    glu_ref[...] = glu
    ctx[CONV_CTX:CONV_CTX + rows, :] = glu
    state = s_scr[...]
    for cc in range(rows // RET_CHUNK):
        rs = slice(cc * RET_CHUNK, (cc + 1) * RET_CHUNK)
        y, state = _retention_chunk(q_ref[rs, :], k_ref[rs, :], v_ref[rs, :], g_ref[rs, :], state, ret_tabs)
        mix_ref[rs, 0:MIX] = y
        mix_ref[rs, 2 * MIX:3 * MIX] = _conv_rows(ctx, cc * RET_CHUNK, RET_CHUNK, cw_ref, cb_ref, clg_ref, clb_ref)
        mix_ref[rs, 3 * MIX:4 * MIX] = _sgu_chunk(gu_ref[rs, :], gv_ref[rs, :], wms, sgu_tabs)[0]
    mix_ref[:, MIX:2 * MIX] = jnp.zeros((rows, MIX), F32)
    s_scr[...] = state
    ctx[0:CONV_CTX, :] = ctx[rows:rows + CONV_CTX, :]

    @pl.when(c == n_steps - 1)
    def _():
        sout_ref[0] = s_scr[...]


def _mixers_prompt(h, mix, ret_tabs, avg, gn_g, conv_args, sgu_args, *, seq, n_seq):
    assert RET_CHUNK == SGU_CHUNK and seq % MIX_ROWS == 0
    rows = MIX_ROWS
    n_steps = seq // rows
    dec, qdec, kdec, cmat, bdm = ret_tabs
    cw, cb, clg, clb = conv_args
    sg, sb, ws, bias = sgu_args

    def hspec(col):
        return pl.BlockSpec((rows, MIX), lambda b, c: (b * n_steps + c, col))

    def const(arr):
        return pl.BlockSpec(arr.shape, lambda b, c: (0,) * arr.ndim)

    tables = [dec, qdec, kdec, cmat, bdm, avg, gn_g, cw, cb, clg, clb, sg, sb, ws, bias]
    cols = [COL_RQ, COL_RK, COL_RV, COL_RG, COL_CV, COL_CG, COL_GU, COL_GV]
    nbytes = (2 * (len(cols) + 5) * rows * MIX * 4 + 2 * sum(t.size * 4 for t in tables)
              + (CONV_CTX + rows) * MIX * 4 + MIX * MIX * 4)
    return pl.pallas_call(
        functools.partial(_mixers_body, n_steps=n_steps, rows=rows),
        grid=(n_seq, n_steps),
        in_specs=[hspec(col) for col in cols] + [const(t) for t in tables] + [pl.BlockSpec(memory_space=pl.ANY)],
        out_specs=[pl.BlockSpec((rows, 4 * MIX), lambda b, c: (b * n_steps + c, 0)),
                   pl.BlockSpec((rows, MIX), lambda b, c: (b * n_steps + c, 0)),
                   pl.BlockSpec((1, MIX, MIX), lambda b, c: (b, 0, 0))],
        out_shape=[jax.ShapeDtypeStruct(mix.shape, F32),
                   jax.ShapeDtypeStruct((n_seq * seq, MIX), F32),
                   jax.ShapeDtypeStruct((n_seq, MIX, MIX), F32)],
        scratch_shapes=[pltpu.VMEM((MIX, MIX), F32), pltpu.VMEM((CONV_CTX + rows, MIX), F32)],
        input_output_aliases={len(cols) + len(tables): 0},
        compiler_params=_params(("arbitrary", "arbitrary"), nbytes),
        name="mixers_prompt",
    )(*([h] * len(cols)), *tables, mix)


def _mixers_sample_body(q_ref, k_ref, v_ref, g_ref, cv_ref, cg_ref, gu_ref, gv_ref, s0_ref, buf_ref,
                        dec_ref, qdec_ref, kdec_ref, cmat_ref, bdm_ref, avg_ref, gn_ref,
                        cw_ref, cb_ref, clg_ref, clb_ref, sg_ref, sb_ref, ws_ref, bias_ref, mix_in,
                        mix_ref, glu_ref, vn_ref, sout_ref, ctx, *, seqs, rows):
    del mix_in
    ret_tabs = (dec_ref, qdec_ref, kdec_ref, cmat_ref, bdm_ref, avg_ref, gn_ref)
    sgu_tabs = (sg_ref, sb_ref, bias_ref, avg_ref)
    wms = _sgu_weights(ws_ref)
    glu = cv_ref[...] * _sigmoid(cg_ref[...])
    glu_ref[...] = glu
    for s in range(seqs):
        rs = slice(s * rows, (s + 1) * rows)
        y, sout_ref[s] = _retention_chunk(q_ref[rs, :], k_ref[rs, :], v_ref[rs, :], g_ref[rs, :], s0_ref[s], ret_tabs)
        mix_ref[rs, 0:MIX] = y
        ctx[s, 0:CONV_CTX, :] = buf_ref[s]
        ctx[s, CONV_CTX:CONV_CTX + rows, :] = glu[rs, :]
        mix_ref[rs, 2 * MIX:3 * MIX] = _conv_rows(ctx.at[s], 0, rows, cw_ref, cb_ref, clg_ref, clb_ref)
        mix_ref[rs, 3 * MIX:4 * MIX], vn_ref[rs, :] = _sgu_chunk(gu_ref[rs, :], gv_ref[rs, :], wms, sgu_tabs)
    mix_ref[:, MIX:2 * MIX] = jnp.zeros((seqs * rows, MIX), F32)


def _mixers_sample(h, mix, s0, buf, ret_tabs, avg, gn_g, conv_args, sgu_args, *, rows, n_seq, seqs, row0):
    blk = seqs * rows
    rb0 = row0 // blk
    dec, qdec, kdec, cmat, bdm = ret_tabs
    cw, cb, clg, clb = conv_args
    sg, sb, ws, bias = sgu_args

    def hspec(col):
        return pl.BlockSpec((blk, MIX), lambda b: (rb0 + b, col))

    def const(arr):
        return pl.BlockSpec(arr.shape, lambda b: (0,) * arr.ndim)

    per_seq = lambda arr: pl.BlockSpec((seqs,) + arr.shape[1:], lambda b: (b,) + (0,) * (arr.ndim - 1))
    tables = [dec, qdec, kdec, cmat, bdm, avg, gn_g, cw, cb, clg, clb, sg, sb, ws, bias]
    cols = [COL_RQ, COL_RK, COL_RV, COL_RG, COL_CV, COL_CG, COL_GU, COL_GV]
    rows_out = lambda width: pl.BlockSpec((blk, width), lambda b: (b, 0))
    nbytes = (2 * (len(cols) + 6) * blk * MIX * 4 + 2 * sum(t.size * 4 for t in tables)
              + 4 * seqs * MIX * MIX * 4 + 3 * seqs * (CONV_CTX + rows) * MIX * 4)
    return pl.pallas_call(
        functools.partial(_mixers_sample_body, seqs=seqs, rows=rows),
        grid=(n_seq // seqs,),
        in_specs=([hspec(col) for col in cols] + [per_seq(s0), per_seq(buf)] + [const(t) for t in tables]
                  + [pl.BlockSpec(memory_space=pl.ANY)]),
        out_specs=[pl.BlockSpec((blk, 4 * MIX), lambda b: (rb0 + b, 0)), rows_out(MIX), rows_out(MIX), per_seq(s0)],
        out_shape=[jax.ShapeDtypeStruct(mix.shape, F32),
                   jax.ShapeDtypeStruct((n_seq * rows, MIX), F32),
                   jax.ShapeDtypeStruct((n_seq * rows, MIX), F32),
                   jax.ShapeDtypeStruct(s0.shape, F32)],
        scratch_shapes=[pltpu.VMEM((seqs, CONV_CTX + rows, MIX), F32)],
        input_output_aliases={len(cols) + 2 + len(tables): 0},
        compiler_params=_params(("arbitrary",), nbytes),
        name="mixers_sample",
    )(*([h] * len(cols)), s0, buf, *tables, mix)


def _out_router_body(mix_ref, w_ref, x_ref, g_ref, b_ref, wr_ref, br_ref,
                     x1_ref, route_ref, counts_ref, wbf, wrbf, logit_scr, *, alpha):
    @pl.when(pl.program_id(0) == 0)
    def _():
        wbf[...] = w_ref[0].astype(BF16)
        wrbf[...] = wr_ref[...].astype(BF16)

    parts = [slice(r0, r0 + OUT_ROWS) for r0 in range(0, TM, OUT_ROWS)]
    ys = [_dot(mix_ref[rs, :].astype(BF16), wbf[...]) for rs in parts]
    for rs, y in zip(parts, ys):
        x1 = _layer_norm(alpha * x_ref[rs, :] + y, g_ref[...], b_ref[...])
        x1_ref[rs, :] = x1
        logit_scr[:, rs] = _dot_nt(wrbf[...], x1.astype(BF16)) + br_ref[...]
    logits = logit_scr[0:ROUTE_ROWS, :]
    row = lax.broadcasted_iota(I32, (ROUTE_ROWS, 1), 0).astype(F32)

    def first_max(mask, vals):
        masked = jnp.where(mask, vals, NEG_INF)
        top = jnp.max(masked, axis=0, keepdims=True)
        idx = jnp.min(jnp.where(mask & (masked == top), row, float(LANE)), axis=0, keepdims=True)
        return top, idx

    gmask = row < N_GROUPS
    gmax, gsel = first_max(gmask, logits)
    g_gate = 1.0 / jnp.sum(jnp.where(gmask, jnp.exp(logits - gmax), 0.0), axis=0, keepdims=True)
    lo = N_GROUPS + EXPERTS_PER_GROUP * gsel
    emask = (row >= lo) & (row < lo + EXPERTS_PER_GROUP)
    v1, i1 = first_max(emask, logits)
    v2, i2 = first_max(emask & (row != i1), logits)
    e12 = jnp.exp(v2 - v1)
    both = (row == i1).astype(F32) + (row == i2).astype(F32)

    earlier = (lax.broadcasted_iota(I32, (TD, TD), 0) < lax.broadcasted_iota(I32, (TD, TD), 1)).astype(BF16)
    rank_a, rank_b = [], []
    for td in range(TM // TD):
        cs = slice(td * TD, (td + 1) * TD)
        tile = both[:, cs]
        counts_ref[td] = jnp.sum(tile, axis=1, keepdims=True)
        before = _dot(tile.astype(BF16), earlier)
        rank_a.append(jnp.sum(jnp.where(row == i1[:, cs], before, 0.0), axis=0, keepdims=True))
        rank_b.append(jnp.sum(jnp.where(row == i2[:, cs], before, 0.0), axis=0, keepdims=True))
    values = (i1 - N_GROUPS, i2 - N_GROUPS, g_gate * (1.0 / (1.0 + e12)), g_gate * (e12 / (1.0 + e12)),
              jnp.concatenate(rank_a, axis=1), jnp.concatenate(rank_b, axis=1))
    out_row = lax.broadcasted_iota(I32, (SUBLANE, 1), 0)
    route = jnp.zeros((SUBLANE, TM), F32)
    for k, val in enumerate(values):
        route = jnp.where(out_row == k, val, route)
    route_ref[...] = route


def _out_router(mix, w_out, layer, x, g, b, wr, br, *, alpha):
    nt, d = x.shape
    kdim = mix.shape[1]
    const = lambda shape: pl.BlockSpec(shape, lambda i: (0, 0))
    row = lambda width: pl.BlockSpec((TM, width), lambda i: (i, 0))
    nbytes = 2 * (kdim * d * 4 + 2 * d * LANE * 4 + TM * kdim * 4 + 2 * TM * d * 4 + TM * LANE * 4) + kdim * d * 2
    return pl.pallas_call(
        functools.partial(_out_router_body, alpha=alpha),
        grid=(nt // TM,),
        in_specs=[row(kdim), pl.BlockSpec((1, kdim, d), lambda i: (layer, 0, 0)), row(d),
                  const((1, d)), const((1, d)), const((LANE, d)), const((LANE, 1))],
        out_specs=[row(d), pl.BlockSpec((SUBLANE, TM), lambda i: (0, i)),
                   pl.BlockSpec((TM // TD, ROUTE_ROWS, 1), lambda i: (i, 0, 0))],
        out_shape=[jax.ShapeDtypeStruct((nt, d), F32), jax.ShapeDtypeStruct((SUBLANE, nt), F32),
                   jax.ShapeDtypeStruct((nt // TD, ROUTE_ROWS, 1), F32)],
        scratch_shapes=[pltpu.VMEM((kdim, d), BF16), pltpu.VMEM((LANE, d), BF16), pltpu.VMEM((LANE, TM), F32)],
        compiler_params=_params(("arbitrary",), nbytes),
        name="out_router",
    )(mix, w_out, x, g, b, wr, br)


def _local_slots(route, off_col):
    expert = lax.broadcasted_iota(I32, (N_EXPERTS, 1), 0).astype(F32)
    slots = []
    for kk in range(TOP_K):
        start = jnp.sum(jnp.where(expert == route[kk:kk + 1, :], off_col, 0.0), axis=0, keepdims=True)
        slots.append(start + route[4 + kk:5 + kk, :])
    return slots


def _pack_rows(y):
    half = y.shape[1] // 2
    bits = lax.bitcast_convert_type(y.astype(BF16).astype(F32), U32)
    return (bits[:, :half] & jnp.uint32(0xFFFF0000)) | (bits[:, half:] >> 16)


def _unpack_rows(w):
    hi = lax.bitcast_convert_type(w & jnp.uint32(0xFFFF0000), F32)
    lo = lax.bitcast_convert_type(w << 16, F32)
    return jnp.concatenate([hi, lo], axis=1)


def _chunk_copy(hbm, hbm_row, loc, chunk, sem, *, to_hbm):
    vm = loc.at[pl.ds(pl.multiple_of(chunk * MOE_CHUNK, MOE_CHUNK), MOE_CHUNK), :]
    hb = hbm.at[pl.ds(pl.multiple_of(hbm_row, MOE_CHUNK), MOE_CHUNK), :]
    return pltpu.make_async_copy(vm, hb, sem) if to_hbm else pltpu.make_async_copy(hb, vm, sem)


def _for_each_chunk(count, fn):
    def group(t, carry):
        for u in range(CHUNK_UNROLL):
            fn(t * CHUNK_UNROLL + u)
        return carry

    def single(c, carry):
        fn(c)
        return carry

    whole = count // CHUNK_UNROLL
    lax.fori_loop(0, whole, group, 0)
    lax.fori_loop(whole * CHUNK_UNROLL, count, single, 0)


def _wait_chunks(hbm, buf, sem, count, *, to_hbm):
    _for_each_chunk(count, lambda c: _chunk_copy(hbm, 0, buf, 0, sem, to_hbm=to_hbm).wait())


def _dispatch_body(dst_tab, n_chunks, x_ref, route_ref, off_ref, xs_in, xs_hbm, loc, sem):
    del xs_in
    i = pl.program_id(0)
    last = pl.num_programs(0) - 1
    cur = i % 2
    slot_a, slot_b = _local_slots(route_ref[...], off_ref[0])
    buf_row = lax.broadcasted_iota(I32, (LOC_ROWS, 1), 0).astype(F32)
    onehot = ((buf_row == slot_a) | (buf_row == slot_b)).astype(BF16)
    loc[cur] = _pack_rows(_dot(onehot, x_ref[...].astype(BF16)))

    _for_each_chunk(n_chunks[i], lambda c: _chunk_copy(
        xs_hbm, dst_tab[i * MAX_CHUNKS + c], loc.at[cur], c, sem.at[cur], to_hbm=True).start())

    @pl.when(i > 0)
    def _():
        _wait_chunks(xs_hbm, loc.at[1 - cur], sem.at[1 - cur], n_chunks[jnp.maximum(i - 1, 0)], to_hbm=True)

    @pl.when(i == last)
    def _():
        _wait_chunks(xs_hbm, loc.at[cur], sem.at[cur], n_chunks[i], to_hbm=True)


def _dispatch(dst_tab, n_chunks, x1, route, off, xs_prev):
    nt, d = x1.shape
    row = lambda width: pl.BlockSpec((TD, width), lambda i, *_: (i, 0))
    nbytes = 2 * (TD * d * 4 + TD * LANE * 4) + 2 * LOC_ROWS * d * 4 + 4 * TD * LOC_ROWS * 4 + LOC_ROWS * d * 4
    return pl.pallas_call(
        _dispatch_body,
        grid_spec=pltpu.PrefetchScalarGridSpec(
            num_scalar_prefetch=2,
            grid=(nt // TD,),
            in_specs=[row(d), pl.BlockSpec((SUBLANE, TD), lambda i, *_: (0, i)),
                      pl.BlockSpec((1, N_EXPERTS, 1), lambda i, *_: (i, 0, 0)),
                      pl.BlockSpec(memory_space=pl.ANY)],
            out_specs=pl.BlockSpec(memory_space=pl.ANY),
            scratch_shapes=[pltpu.VMEM((2, LOC_ROWS, d // 2), U32), pltpu.SemaphoreType.DMA((2,))]),
        out_shape=jax.ShapeDtypeStruct(xs_prev.shape, U32),
        input_output_aliases={5: 0},
        compiler_params=_params(("arbitrary",), nbytes),
        name="moe_dispatch",
    )(dst_tab, n_chunks, x1, route, off, xs_prev)


def _ffn_body(blk_e, blk_valid, blk_src, x_ref, w1_ref, w3_ref, w2_ref, y_ref, w1b, w3b, w2b):
    del blk_src
    j = pl.program_id(0)

    @pl.when(blk_valid[j] > 0)
    def _():
        @pl.when((j == 0) | (blk_e[j] != blk_e[jnp.maximum(j - 1, 0)]))
        def _():
            w1b[...] = w1_ref[0, 0].astype(BF16)
            w3b[...] = w3_ref[0, 0].astype(BF16)
            w2b[...] = w2_ref[0, 0].astype(BF16)

        xb = _unpack_rows(x_ref[...]).astype(BF16)
        h1 = _dot(xb, w1b[...])
        h3 = _dot(xb, w3b[...])
        act = (h1 * _sigmoid(h1)) * h3
        y_ref[...] = _pack_rows(_dot(act.astype(BF16), w2b[...]))


def _ffn(xs, w1, w3, w2, layer, blk_e, blk_valid, blk_src):
    d = w1.shape[2]
    de = w1.shape[3]
    n_blk = xs.shape[0] // MOE_BLK
    wmap = lambda j, be, bv, bs: (layer, be[j], 0, 0)
    rows = pl.BlockSpec((MOE_BLK, d // 2), lambda j, be, bv, bs: (bs[j], 0))
    nbytes = 2 * (3 * d * de * 4 + MOE_BLK * d * 4) + 3 * d * de * 2 + 6 * MOE_BLK * de * 4 + 4 * MOE_BLK * d * 4
    wspec = lambda shape: pl.BlockSpec(shape, wmap)
    return pl.pallas_call(
        _ffn_body,
        grid_spec=pltpu.PrefetchScalarGridSpec(
            num_scalar_prefetch=3,
            grid=(n_blk,),
            in_specs=[rows, wspec((1, 1, d, de)), wspec((1, 1, d, de)), wspec((1, 1, de, d))],
            out_specs=rows,
            scratch_shapes=[pltpu.VMEM((d, de), BF16), pltpu.VMEM((d, de), BF16), pltpu.VMEM((de, d), BF16)]),
        out_shape=jax.ShapeDtypeStruct(xs.shape, U32),
        input_output_aliases={3: 0},
        compiler_params=_params(("arbitrary",), nbytes),
        name="expert_ffn",
    )(blk_e, blk_valid, blk_src, xs, w1, w3, w2)


def _combine_body(dst_tab, n_chunks, x_ref, route_ref, off_ref, g_ref, b_ref, ys_hbm, x2_ref,
                  *rest, alpha, head_tiles):
    *tail_ref, loc, sem = rest
    i = pl.program_id(0)
    last = pl.num_programs(0) - 1
    cur = i % 2

    def fetch(tile, buf):
        _for_each_chunk(n_chunks[tile], lambda c: _chunk_copy(
            ys_hbm, dst_tab[tile * MAX_CHUNKS + c], loc.at[buf], c, sem.at[buf], to_hbm=False).start())

    @pl.when(i == 0)
    def _():
        loc[...] = jnp.zeros_like(loc)
        fetch(0, 0)

    @pl.when(i < last)
    def _():
        fetch(jnp.minimum(i + 1, last), 1 - cur)

    route = route_ref[...]
    slot_a, slot_b = _local_slots(route, off_ref[0])
    buf_row = lax.broadcasted_iota(I32, (LOC_ROWS, 1), 0).astype(F32)
    gates = (jnp.where(buf_row == slot_a, route[2:3, :], 0.0)
             + jnp.where(buf_row == slot_b, route[3:4, :], 0.0))
    _wait_chunks(ys_hbm, loc.at[cur], sem.at[cur], n_chunks[i], to_hbm=False)
    f = _dot_tn(gates.astype(BF16), _unpack_rows(loc[cur]).astype(BF16))
    x2 = _layer_norm(alpha * x_ref[...] + f, g_ref[...], b_ref[...])
    if tail_ref:
        @pl.when(i < head_tiles)
        def _():
            x2_ref[...] = x2

        @pl.when(i >= head_tiles)
        def _():
            tail_ref[0][...] = x2
    else:
        x2_ref[...] = x2


def _combine(dst_tab, n_chunks, x1, route, off, g, b, ys, *, alpha, head_rows=None):
    nt, d = x1.shape
    row = lambda width: pl.BlockSpec((TD, width), lambda i, dt, nc: (i, 0))
    const = lambda shape: pl.BlockSpec(shape, lambda i, dt, nc: (0, 0))
    nbytes = 2 * (3 * TD * d * 4 + TD * LANE * 4) + 2 * LOC_ROWS * d * 4 + 4 * TD * LOC_ROWS * 4 + LOC_ROWS * d * 2
    if head_rows is None:
        head_tiles = nt // TD
        out_specs = row(d)
        out_shape = jax.ShapeDtypeStruct((nt, d), F32)
    else:
        head_tiles = head_rows // TD
        out_specs = [pl.BlockSpec((TD, d), lambda i, dt, nc: (jnp.minimum(i, head_tiles - 1), 0)),
                     pl.BlockSpec((TD, d), lambda i, dt, nc: (jnp.maximum(i - head_tiles, 0), 0))]
        out_shape = [jax.ShapeDtypeStruct((head_rows, d), F32), jax.ShapeDtypeStruct((nt - head_rows, d), F32)]
    scratch = [pltpu.VMEM((2, LOC_ROWS, d // 2), U32), pltpu.SemaphoreType.DMA((2,))]
    return pl.pallas_call(
        functools.partial(_combine_body, alpha=alpha, head_tiles=head_tiles),
        grid_spec=pltpu.PrefetchScalarGridSpec(
            num_scalar_prefetch=2,
            grid=(nt // TD,),
            in_specs=[row(d), pl.BlockSpec((SUBLANE, TD), lambda i, dt, nc: (0, i)),
                      pl.BlockSpec((1, N_EXPERTS, 1), lambda i, dt, nc: (i, 0, 0)),
                      const((1, d)), const((1, d)), pl.BlockSpec(memory_space=pl.ANY)],
            out_specs=out_specs,
            scratch_shapes=scratch),
        out_shape=out_shape,
        compiler_params=_params(("arbitrary",), nbytes),
        name="combine_ln",
    )(dst_tab, n_chunks, x1, route, off, g, b, ys)


def _rope_tables(pos, rot_dim, theta):
    half = rot_dim // 2
    inv = jnp.float32(theta) ** (-jnp.arange(half, dtype=F32) * (2.0 / rot_dim))
    ang = pos.astype(F32)[:, None] * inv[None, :]
    c, s = jnp.cos(ang), jnp.sin(ang)
    rest = HEAD_DIM - rot_dim
    ones = jnp.ones((pos.shape[0], rest), F32)
    zeros = jnp.zeros((pos.shape[0], rest), F32)
    cos_h = jnp.concatenate([c, c, ones], axis=1)
    sin_h = jnp.concatenate([-s, s, zeros], axis=1)
    return jnp.tile(cos_h, (1, N_HEADS)), jnp.tile(sin_h, (1, N_HEADS))


def _block_diag(s):
    rows = [jnp.pad(s[:, h], ((0, 0), (0, 0), (h * HEAD_DIM, MIX - (h + 1) * HEAD_DIM))) for h in range(N_HEADS)]
    return jnp.concatenate(rows, axis=1)


def _diag_blocks(s):
    return jnp.stack([s[:, h * HEAD_DIM:(h + 1) * HEAD_DIM, h * HEAD_DIM:(h + 1) * HEAD_DIM]
                      for h in range(N_HEADS)], axis=1)


def _routing_plan(tile_counts, n_blk):
    cnt_t = tile_counts[:, N_GROUPS:N_GROUPS + N_EXPERTS, 0].astype(I32)
    run_chunks = (cnt_t + MOE_CHUNK - 1) // MOE_CHUNK
    run_rows = run_chunks * MOE_CHUNK
    cnt = jnp.sum(run_rows, axis=0)
    pcnt = (cnt + MOE_BLK - 1) // MOE_BLK * MOE_BLK
    pstart = jnp.cumsum(pcnt) - pcnt
    n_act = jnp.sum(pcnt) // MOE_BLK
    gstart = pstart[None, :] + jnp.cumsum(run_rows, axis=0) - run_rows
    chunk0 = jnp.cumsum(run_chunks, axis=1) - run_chunks
    n_chunks = jnp.sum(run_chunks, axis=1).astype(I32)

    def spread(first, count, value, n):
        p = jnp.arange(n, dtype=I32)[:, None]
        f, c, v = first[..., None, :], count[..., None, :], value[..., None, :]
        return jnp.sum(jnp.where((f <= p) & (p < f + c), v + (p - f) * MOE_CHUNK, 0), axis=-1)

    dst_tab = spread(chunk0, run_chunks, gstart, MAX_CHUNKS)
    off = (chunk0 * MOE_CHUNK).astype(F32)[:, :, None]
    pos = jnp.arange(n_blk, dtype=I32)[:, None] * MOE_BLK
    mine = (pstart[None, :] <= pos) & (pos < (pstart + pcnt)[None, :])
    experts = jnp.arange(N_EXPERTS, dtype=I32)[None, :]
    last_e = jnp.max(jnp.where(cnt > 0, experts[0], 0))
    active = pos[:, 0] < n_act * MOE_BLK
    blk_e = jnp.where(active, jnp.sum(jnp.where(mine, experts, 0), axis=-1), last_e)
    blk_valid = jnp.sum(jnp.where(mine, jnp.clip((pstart + cnt)[None, :] - pos, 0, MOE_BLK), 0), axis=-1)
    blk_src = jnp.minimum(jnp.arange(n_blk, dtype=I32), jnp.maximum(n_act - 1, 0))
    i32 = lambda t: t.astype(I32)
    return i32(dst_tab).reshape(-1), n_chunks, off, i32(blk_e), i32(blk_valid), i32(blk_src)


def kernel(x_prompt, x_sample, state_ret, cache_swa_k, cache_swa_v, state_conv, w_in, w_out, ret_gn_g,
           conv_w, conv_b, conv_ln_g, conv_ln_b, sgu_ln_g, sgu_ln_b, sgu_w, sgu_b, ln1_g, ln1_b,
           ln2_g, ln2_b, router_g_w, router_g_b, router_e_w, router_e_b, moe_w1, moe_w3, moe_w2):
    depth = w_in.shape[0]
    n_seq, seq, d = x_prompt.shape
    n_dec, t_new, _ = x_sample.shape
    past = cache_swa_k.shape[2]
    rows_s = -(-t_new // SUBLANE) * SUBLANE
    n_p = n_seq * seq
    n_s = n_dec * rows_s
    n_tail = -(-n_s // TM) * TM
    nt = n_p + n_tail
    assert seq % SWA_SB == 0 and seq % TM == 0 and TM % TD == 0
    assert t_new % RET_CHUNK != 0 and t_new <= SGU_CHUNK and w_in.shape[2] == N_COLBLK * MIX
    assert n_dec % SAMPLE_SEQS == 0 and n_dec % SWA_SAMPLE_SEQS == 0
    assert n_p % (SAMPLE_SEQS * rows_s) == 0
    alpha = (2 * depth) ** 0.25
    keep = min(DILATED_CONFIGS[-1][0], seq)
    n_blk = -(-(nt * TOP_K + (nt // TD) * N_EXPERTS * (MOE_CHUNK - 1) + N_EXPERTS * (MOE_BLK - 1)) // MOE_BLK)
    cache_k = jnp.transpose(cache_swa_k, (0, 1, 3, 4, 2)).reshape(depth, n_dec, MIX, past)
    cache_v = jnp.transpose(cache_swa_v, (0, 1, 3, 4, 2)).reshape(depth, n_dec, MIX, past)

    x = jnp.concatenate([x_prompt.reshape(n_p, d),
                         jnp.pad(x_sample, ((0, 0), (0, rows_s - t_new), (0, 0))).reshape(n_s, d),
                         jnp.zeros((n_tail - n_s, d), F32)], axis=0)
    pos = jnp.concatenate([jnp.arange(seq, dtype=I32),
                           PAST_LEN + jnp.arange(n_tail, dtype=I32) % rows_s])
    rope = _rope_tables(pos, HEAD_DIM, RET_THETA) + _rope_tables(pos, ROPE_DIM, ROPE_THETA)
    ret_tabs_p = _retention_tables(RET_CHUNK, RET_CHUNK)
    ret_tabs_s = _retention_tables(t_new, rows_s)
    lane_head = jnp.arange(MIX) // HEAD_DIM
    avg = ((lane_head[:, None] == lane_head[None, :]).astype(F32) / HEAD_DIM).astype(BF16)
    row2 = lambda v: v.reshape(1, -1)
    mix = jnp.zeros((nt, 4 * MIX), F32)
    slots = jnp.zeros((n_blk * MOE_BLK, d // 2), U32)

    outs = [[] for _ in range(9)]
    for l in range(depth):
        h = _in_proj(x, w_in, l, rope, n_p // TM, seq // TM)
        gn = row2(ret_gn_g[l])
        cw = jnp.pad(conv_w[l], ((0, CONV_CTX - CONV_WIDTH), (0, 0)))
        cargs = (cw, row2(conv_b[l]), row2(conv_ln_g[l]), row2(conv_ln_b[l]))
        sbias = jnp.repeat(sgu_b[l].T, HEAD_DIM, axis=1)
        sargs = (row2(sgu_ln_g[l]), row2(sgu_ln_b[l]), sgu_w[l], sbias)
        mix, glu_p, s_p = _mixers_prompt(h, mix, ret_tabs_p, avg, gn, cargs, sargs, seq=seq, n_seq=n_seq)
        mix = _swa_prompt(h, mix, seq=seq, n_seq=n_seq)
        ctx_s = jnp.pad(state_conv[l], ((0, 0), (CONV_CTX - (CONV_WIDTH - 1), 0), (0, 0)))
        mix, glu_s, vn_s, s_s = _mixers_sample(h, mix, _block_diag(state_ret[l]), ctx_s, ret_tabs_s, avg, gn, cargs,
                                               sargs, rows=rows_s, n_seq=n_dec, seqs=SAMPLE_SEQS, row0=n_p)
        mix = _swa_sample(h, mix, cache_k, cache_v, l, rows=rows_s, n_seq=n_dec, seqs=SWA_SAMPLE_SEQS, row0=n_p)

        wr = jnp.concatenate([router_g_w[l].T, jnp.transpose(router_e_w[l], (0, 2, 1)).reshape(N_EXPERTS, d),
                              jnp.zeros((LANE - N_GROUPS - N_EXPERTS, d), F32)], axis=0)
        br = jnp.concatenate([router_g_b[l], router_e_b[l].reshape(-1),
                              jnp.zeros((LANE - N_GROUPS - N_EXPERTS,), F32)]).reshape(LANE, 1)
        x1, route, counts = _out_router(mix, w_out, l, x, row2(ln1_g[l]), row2(ln1_b[l]), wr, br, alpha=alpha)
        dst_tab, n_chunks, off, blk_e, blk_valid, blk_src = _routing_plan(counts, n_blk)
        slots = _dispatch(dst_tab, n_chunks, x1, route, off, slots)
        slots = _ffn(slots, moe_w1, moe_w3, moe_w2, l, blk_e, blk_valid, blk_src)
        x = _combine(dst_tab, n_chunks, x1, route, off, row2(ln2_g[l]), row2(ln2_b[l]), slots, alpha=alpha,
                     head_rows=n_p if l == depth - 1 else None)

        def tail(col, n_keep):
            return jnp.stack([lax.slice(h, ((b + 1) * seq - n_keep, col * MIX), ((b + 1) * seq, (col + 1) * MIX))
                              for b in range(n_seq)])

        def new_rows(col):
            blk = lax.slice(h, (n_p, col * MIX), (n_p + n_s, (col + 1) * MIX))
            return blk.reshape(n_dec, rows_s, MIX)[:, :t_new]

        heads = lambda t: t.reshape(t.shape[0], t.shape[1], N_HEADS, HEAD_DIM)
        outs[0].append(_diag_blocks(s_p))
        outs[1].append(heads(tail(COL_SK, keep)))
        outs[2].append(heads(tail(COL_SV, keep)))
        outs[3].append(glu_p.reshape(n_seq, seq, MIX)[:, seq - (CONV_WIDTH - 1):])
        outs[4].append(_diag_blocks(s_s))
        outs[5].append(heads(new_rows(COL_SK)))
        outs[6].append(heads(new_rows(COL_SV)))
        outs[7].append(glu_s.reshape(n_dec, rows_s, MIX)[:, :t_new])
        outs[8].append(vn_s.reshape(n_dec, rows_s, MIX)[:, :t_new])

    x_head, x_tail = x
    y_prompt = x_head.reshape(n_seq, seq, d)
    y_sample = x_tail[:n_s].reshape(n_dec, rows_s, d)[:, :t_new]
    return (y_prompt, y_sample) + tuple(jnp.stack(o) for o in outs)
```

```python
import functools
import math

import jax
import jax.numpy as jnp
from jax import lax
from jax.experimental import pallas as pl
from jax.experimental.pallas import tpu as pltpu

F32 = jnp.float32
BF16 = jnp.bfloat16
I32 = jnp.int32
U32 = jnp.uint32

HEAD_DIM = 64
N_HEADS = 4
MIX = N_HEADS * HEAD_DIM
(COL_RQ, COL_RK, COL_RV, COL_RG, COL_SQ, COL_SK, COL_SV,
 COL_CV, COL_CG, COL_GU, COL_GV) = range(11)
N_COLBLK = 11
RET_CHUNK = 128
RET_THETA = 10000.0
ROPE_THETA = 500000.0
ROPE_DIM = HEAD_DIM // 4
DILATED_CONFIGS = ((128, 1), (512, 4), (2048, 16))
DIL_BLOCK = 128
CONV_WIDTH = 31
SGU_CHUNK = 128
N_GROUPS = 4
EXPERTS_PER_GROUP = 8
N_EXPERTS = N_GROUPS * EXPERTS_PER_GROUP
TOP_K = 2
PAST_LEN = 8192
LN_EPS = 1e-5
NEG_INF = -1e30

LANE = 128
SUBLANE = 8
VMEM_CAP_BYTES = 60000 * 1024
COMPILER_TEMP_BYTES = 12 * 1024 * 1024

TM = 512
TD = 256
OUT_ROWS = 128
ROUTE_ROWS = -(-(N_GROUPS + N_EXPERTS) // SUBLANE) * SUBLANE
MOE_BLK = 512
MOE_CHUNK = SUBLANE
LOC_ROWS = -(-(TD * TOP_K + N_EXPERTS * (MOE_CHUNK - 1)) // LANE) * LANE
MAX_CHUNKS = LOC_ROWS // MOE_CHUNK
CHUNK_UNROLL = 4
MIX_ROWS = 1024
CONV_CTX = 32
SWA_SB = DIL_BLOCK * 16
SWA_TILE = 256
SWA_UNROLL = 4
SAMPLE_SEQS = 8
SWA_SAMPLE_SEQS = 4


def _vmem(nbytes):
    return int(min(VMEM_CAP_BYTES, nbytes + COMPILER_TEMP_BYTES))


def _params(sem, nbytes):
    return pltpu.CompilerParams(dimension_semantics=sem, vmem_limit_bytes=_vmem(nbytes))


def _dot(a, b):
    return jnp.dot(a, b, preferred_element_type=F32)


def _dot_nt(a, b):
    return lax.dot_general(a, b, (((1,), (1,)), ((), ())), preferred_element_type=F32)


def _dot_tn(a, b):
    return lax.dot_general(a, b, (((0,), (0,)), ((), ())), preferred_element_type=F32)


def _split_dot(x, m):
    hi = x.astype(BF16)
    lo = (x - hi.astype(F32)).astype(BF16)
    return _dot(hi, m) + _dot(lo, m)


def _sigmoid(x):
    return 1.0 / (1.0 + jnp.exp(-x))


def _layer_norm(z, g, b):
    mu = jnp.mean(z, axis=-1, keepdims=True)
    d = z - mu
    var = jnp.mean(d * d, axis=-1, keepdims=True)
    return d * lax.rsqrt(var + LN_EPS) * g + b


def _group_norm(z, avg):
    mu = _split_dot(z, avg)
    d = z - mu
    var = _split_dot(d * d, avg)
    return d * lax.rsqrt(var + LN_EPS)


def _lane_head(width):
    return lax.broadcasted_iota(I32, (1, width), 1) // HEAD_DIM


def _rotate(x, cos, sin_signed, half):
    first = (lax.broadcasted_iota(I32, (1, LANE), 1) % HEAD_DIM) < half
    parts = []
    for p in range(x.shape[1] // LANE):
        t = x[:, p * LANE:(p + 1) * LANE]
        up = pltpu.roll(t, half, 1)
        down = pltpu.roll(t, LANE - half, 1)
        parts.append(jnp.where(first, down, up))
    return x * cos + jnp.concatenate(parts, axis=1) * sin_signed


def _in_proj_body(x_ref, w_ref, cr_ref, sr_ref, cs_ref, ss_ref, h_ref, wbf_ref):
    @pl.when(pl.program_id(0) == 0)
    def _():
        for j in range(N_COLBLK):
            wbf_ref[:, j * MIX:(j + 1) * MIX] = w_ref[0, :, j * MIX:(j + 1) * MIX].astype(BF16)

    xb = x_ref[...].astype(BF16)
    for j in range(N_COLBLK):
        hj = _dot(xb, wbf_ref[:, j * MIX:(j + 1) * MIX])
        if j in (COL_RQ, COL_RK):
            hj = _rotate(hj, cr_ref[...], sr_ref[...], HEAD_DIM // 2)
        if j == COL_RK:
            hj = hj * (HEAD_DIM ** -0.5)
        if j in (COL_SQ, COL_SK):
            hj = _rotate(hj, cs_ref[...], ss_ref[...], ROPE_DIM // 2)
        h_ref[:, j * MIX:(j + 1) * MIX] = hj


def _in_proj(x, w, layer, tabs, n_prompt_tiles, tiles_per_seq):
    nt, d = x.shape
    width = w.shape[2]

    def tab_map(i):
        return (jnp.where(i < n_prompt_tiles, i % tiles_per_seq, tiles_per_seq + i - n_prompt_tiles), 0)

    tab_spec = pl.BlockSpec((TM, MIX), tab_map)
    nbytes = d * width * 4 + d * width * 2 + 2 * TM * d * 4 + 8 * TM * MIX * 4 + 2 * TM * width * 4
    return pl.pallas_call(
        _in_proj_body,
        grid=(nt // TM,),
        in_specs=[pl.BlockSpec((TM, d), lambda i: (i, 0)),
                  pl.BlockSpec((1, d, width), lambda i: (layer, 0, 0), pipeline_mode=pl.Buffered(1)),
                  tab_spec, tab_spec, tab_spec, tab_spec],
        out_specs=pl.BlockSpec((TM, width), lambda i: (i, 0)),
        out_shape=jax.ShapeDtypeStruct((nt, width), F32),
        scratch_shapes=[pltpu.VMEM((d, width), BF16)],
        compiler_params=_params(("arbitrary",), nbytes),
        name="in_proj",
    )(x, w, *tabs)


def _retention_chunk(q, k, v, g, state, tabs):
    dec_ref, qdec_ref, kdec_ref, cmat_ref, bdm_ref, avg_ref, gn_ref = tabs
    head = _lane_head(MIX)
    vb = v.astype(BF16)
    kb = k.astype(BF16)
    o = _dot(q.astype(BF16), state.astype(BF16)) * qdec_ref[...]
    for h in range(N_HEADS):
        mh = head == h
        a = _dot_nt(jnp.where(mh, q, 0.0).astype(BF16), kb) * dec_ref[h]
        o = o + jnp.where(mh, _dot(a.astype(BF16), vb), 0.0)
    upd = _dot_tn((k * kdec_ref[...]).astype(BF16), vb)
    state = state * cmat_ref[...] + upd * bdm_ref[...]
    return _group_norm(o, avg_ref[...]) * gn_ref[...] * (g * _sigmoid(g)), state


def _retention_tables(l_real, l_pad):
    hh = jnp.arange(N_HEADS, dtype=F32)
    log_g = jnp.log1p(-jnp.exp2(-5.0 - hh))
    i = jnp.arange(l_pad, dtype=F32)
    rel = i[:, None] - i[None, :]
    dec = jnp.where(rel >= 0, jnp.exp(log_g[:, None, None] * jnp.maximum(rel, 0.0)), 0.0)
    qd = jnp.exp(log_g[:, None] * (i + 1.0))
    kd = jnp.where(i < l_real, jnp.exp(log_g[:, None] * (l_real - 1.0 - i)), 0.0)
    cd = jnp.exp(log_g * l_real)
    expand = lambda t: jnp.repeat(t.T, HEAD_DIM, axis=1)
    row_head = jnp.arange(MIX) // HEAD_DIM
    bdm = (row_head[:, None] == row_head[None, :]).astype(F32)
    cmat = cd[row_head][:, None] * bdm
    return dec, expand(qd), expand(kd), cmat, bdm


def _swa_prompt_body(q_ref, k_ref, v_ref, mix_ref, y_ref, o0, o1, o2, l0, l1, l2, *, seq):
    del mix_ref
    o_scr = (o0, o1, o2)
    l_scr = (l0, l1, l2)
    head = _lane_head(LANE)
    qi = lax.broadcasted_iota(I32, (DIL_BLOCK, 1), 0)
    ki = lax.broadcasted_iota(I32, (1, 2 * DIL_BLOCK), 1)
    rel = DIL_BLOCK + qi - ki
    scale = HEAD_DIM ** -0.5
    for sb in range(seq // SWA_SB):
        for ci, (window, dil) in enumerate(DILATED_CONFIGS):
            span = window // dil
            band = (rel >= 0) & (rel <= span)

            def block(i, carry, sb=sb, ci=ci, dil=dil, band=band):
                nloc = i // dil
                off = nloc * (DIL_BLOCK * dil) + i % dil
                start = sb * SWA_SB + off
                if sb == 0:
                    first = nloc == 0
                    pstart = jnp.where(first, start, start - DIL_BLOCK * dil)
                    valid = band & (ki >= jnp.where(first, DIL_BLOCK, 0))
                else:
                    pstart = start - DIL_BLOCK * dil
                    valid = band
                stride = dil if dil > 1 else None
                cur = pl.ds(start, DIL_BLOCK, stride=stride)
                prev = pl.ds(pstart, DIL_BLOCK, stride=stride)
                qb = q_ref[cur, :] * scale
                kb = jnp.concatenate([k_ref[prev, :], k_ref[cur, :]], axis=0).astype(BF16)
                vb = jnp.concatenate([v_ref[prev, :], v_ref[cur, :]], axis=0).astype(BF16)
                o = jnp.zeros((DIL_BLOCK, LANE), F32)
                ls = jnp.zeros((DIL_BLOCK, LANE), F32)
                for hh in range(LANE // HEAD_DIM):
                    mh = head == hh
                    s = _dot_nt(jnp.where(mh, qb, 0.0).astype(BF16), kb)
                    s = jnp.where(valid, s, NEG_INF)
                    m = jnp.max(s, axis=-1, keepdims=True)
                    p = jnp.exp(s - m)
                    l = jnp.sum(p, axis=-1, keepdims=True)
                    o = jnp.where(mh, _dot(p.astype(BF16), vb) / l, o)
                    ls = jnp.where(mh, m + jnp.log(l), ls)
                dst = pl.ds(off, DIL_BLOCK, stride=stride)
                o_scr[ci][dst, :] = o
                l_scr[ci][dst, :] = ls
                return carry

            lax.fori_loop(0, SWA_SB // DIL_BLOCK, block, 0, unroll=SWA_UNROLL)

        def combine(t, carry, sb=sb):
            rows = pl.ds(t * SWA_TILE, SWA_TILE)
            la, lb, lc = l_scr[0][rows, :], l_scr[1][rows, :], l_scr[2][rows, :]
            mx = jnp.maximum(jnp.maximum(la, lb), lc)
            ea, eb, ec = jnp.exp(la - mx), jnp.exp(lb - mx), jnp.exp(lc - mx)
            num = ea * o_scr[0][rows, :] + eb * o_scr[1][rows, :] + ec * o_scr[2][rows, :]
            y_ref[pl.ds(sb * SWA_SB + t * SWA_TILE, SWA_TILE), :] = (num / (ea + eb + ec)).astype(y_ref.dtype)
            return carry

        lax.fori_loop(0, SWA_SB // SWA_TILE, combine, 0)


def _swa_prompt(h, mix, *, seq, n_seq):
    halves = MIX // LANE

    def hspec(col):
        return pl.BlockSpec((seq, LANE), lambda b, p: (b, col * halves + p))

    nbytes = 2 * 4 * seq * LANE * 4 + 6 * SWA_SB * LANE * 4
    return pl.pallas_call(
        functools.partial(_swa_prompt_body, seq=seq),
        grid=(n_seq, halves),
        in_specs=[hspec(COL_SQ), hspec(COL_SK), hspec(COL_SV), pl.BlockSpec(memory_space=pl.ANY)],
        out_specs=pl.BlockSpec((seq, LANE), lambda b, p: (b, halves + p)),
        out_shape=jax.ShapeDtypeStruct(mix.shape, mix.dtype),
        scratch_shapes=[pltpu.VMEM((SWA_SB, LANE), F32)] * 6,
        input_output_aliases={3: 0},
        compiler_params=_params(("arbitrary", "arbitrary"), nbytes),
        name="swa_prompt",
    )(h, h, h, mix)


def _swa_sample_body(q_ref, k_ref, v_ref, ck_ref, cv_ref, mix_ref, y_ref, *, past, rows, seqs):
    del mix_ref
    head = _lane_head(MIX)
    hr = N_HEADS * rows
    scale = HEAD_DIM ** -0.5
    t_row = lax.broadcasted_iota(I32, (hr, 1), 0) % rows
    rel_c = past + t_row - lax.broadcasted_iota(I32, (1, past), 1)
    rel_n = t_row - lax.broadcasted_iota(I32, (1, LANE), 1)
    valid_c, valid_n = [], []
    for window, dil in DILATED_CONFIGS:
        span = window // dil
        shift = int(math.log2(dil))
        for rel, dst in ((rel_c, valid_c), (rel_n, valid_n)):
            dst.append((rel >= 0) & ((rel & (dil - 1)) == 0) & ((rel >> shift) <= span))
    zpad = jnp.zeros((LANE - rows, MIX), F32)
    ys = []
    for s in range(seqs):
        rs = slice(s * rows, (s + 1) * rows)
        q = q_ref[rs, :]
        qm = jnp.concatenate([jnp.where(head == h, q, 0.0) for h in range(N_HEADS)], axis=0).astype(BF16)
        ck = ck_ref[0, s].astype(BF16)
        cv = cv_ref[0, s].astype(BF16)
        kn = jnp.concatenate([k_ref[rs, :], zpad], axis=0).astype(BF16)
        vn = jnp.concatenate([v_ref[rs, :], zpad], axis=0).astype(BF16)
        s_c = _dot(qm, ck) * scale
        s_n = _dot_nt(qm, kn) * scale
        pcs, pns, ls, lses = [], [], [], []
        for ci in range(len(DILATED_CONFIGS)):
            sc = jnp.where(valid_c[ci], s_c, NEG_INF)
            sn = jnp.where(valid_n[ci], s_n, NEG_INF)
            m = jnp.maximum(jnp.max(sc, axis=-1, keepdims=True), jnp.max(sn, axis=-1, keepdims=True))
            pc = jnp.exp(sc - m)
            pn = jnp.exp(sn - m)
            l = jnp.sum(pc, axis=-1, keepdims=True) + jnp.sum(pn, axis=-1, keepdims=True)
            pcs.append(pc.astype(BF16))
            pns.append(pn.astype(BF16))
            ls.append(l)
            lses.append(m + jnp.log(l))
        o_all = _dot_nt(jnp.concatenate(pcs, axis=0), cv) + _dot(jnp.concatenate(pns, axis=0), vn)
        outs = [o_all[ci * hr:(ci + 1) * hr, :] / ls[ci] for ci in range(len(DILATED_CONFIGS))]
        mx = jnp.maximum(jnp.maximum(lses[0], lses[1]), lses[2])
        es = [jnp.exp(le - mx) for le in lses]
        y_all = (es[0] * outs[0] + es[1] * outs[1] + es[2] * outs[2]) / (es[0] + es[1] + es[2])
        y = jnp.zeros((rows, MIX), F32)
        for h in range(N_HEADS):
            y = jnp.where(head == h, y_all[h * rows:(h + 1) * rows, :], y)
        ys.append(y)
    y_ref[...] = jnp.concatenate(ys, axis=0).astype(y_ref.dtype)


def _swa_sample(h, mix, cache_k, cache_v, layer, *, rows, n_seq, seqs, row0):
    past = cache_k.shape[3]
    blk = seqs * rows
    rb0 = row0 // blk

    def hspec(col):
        return pl.BlockSpec((blk, MIX), lambda b: (rb0 + b, col))

    cspec = pl.BlockSpec((1, seqs, MIX, past), lambda b: (layer, b, 0, 0))
    nbytes = 2 * (2 * seqs * past * MIX * 4 + 4 * blk * MIX * 4) + 16 * N_HEADS * rows * past * 4 + 2 * past * MIX * 2
    return pl.pallas_call(
        functools.partial(_swa_sample_body, past=past, rows=rows, seqs=seqs),
        grid=(n_seq // seqs,),
        in_specs=[hspec(COL_SQ), hspec(COL_SK), hspec(COL_SV), cspec, cspec,
                  pl.BlockSpec(memory_space=pl.ANY)],
        out_specs=pl.BlockSpec((blk, MIX), lambda b: (rb0 + b, 1)),
        out_shape=jax.ShapeDtypeStruct(mix.shape, mix.dtype),
        input_output_aliases={5: 0},
        compiler_params=_params(("arbitrary",), nbytes),
        name="swa_sample",
    )(h, h, h, cache_k, cache_v, mix)


def _conv_rows(ctx, base, sub, w_ref, cb_ref, g_ref, b_ref):
    lead = CONV_CTX - (CONV_WIDTH - 1)
    window = ctx[base:base + sub + CONV_CTX, :]
    acc = jnp.zeros((sub, MIX), F32)
    for r in range(SUBLANE):
        shifted = pltpu.roll(window, sub + CONV_CTX - (lead + r), 0)
        for j in range(r, CONV_WIDTH, SUBLANE):
            acc = acc + shifted[j - r:j - r + sub, :] * w_ref[j:j + 1, :]
    z = _layer_norm(acc + cb_ref[...], g_ref[...], b_ref[...])
    return z * _sigmoid(z)


def _gelu_tanh(x):
    c = math.sqrt(2.0 / math.pi)
    return x * (0.5 * (1.0 + jnp.tanh(c * (x + 0.044715 * (x * x * x)))))


def _sgu_weights(ws_ref):
    tri = (lax.broadcasted_iota(I32, (SGU_CHUNK, SGU_CHUNK), 0)
           >= lax.broadcasted_iota(I32, (SGU_CHUNK, SGU_CHUNK), 1))
    return [jnp.where(tri, ws_ref[gi], 0.0).astype(BF16) for gi in range(N_HEADS)]


def _sgu_chunk(u_raw, v_raw, wms, tabs):
    g_ref, b_ref, bias_ref, avg_ref = tabs
    sub = u_raw.shape[0]
    head = _lane_head(MIX)
    u = _gelu_tanh(u_raw)
    vn = _group_norm(_gelu_tanh(v_raw), avg_ref[...]) * g_ref[...] + b_ref[...]
    vp = vn if sub == SGU_CHUNK else jnp.concatenate([vn, jnp.zeros((SGU_CHUNK - sub, MIX), F32)], axis=0)
    vnb = vp.astype(BF16)
    z = jnp.zeros((SGU_CHUNK, MIX), F32)
    for gi in range(N_HEADS):
        z = jnp.where(head == gi, _dot(wms[gi], vnb), z)
    z = z + bias_ref[...]
    return u * z[0:sub, :], vn


def _mixers_body(q_ref, k_ref, v_ref, g_ref, cv_ref, cg_ref, gu_ref, gv_ref,
                 dec_ref, qdec_ref, kdec_ref, cmat_ref, bdm_ref, avg_ref, gn_ref,
                 cw_ref, cb_ref, clg_ref, clb_ref, sg_ref, sb_ref, ws_ref, bias_ref, mix_in,
                 mix_ref, glu_ref, sout_ref, s_scr, ctx, *, n_steps, rows):
    del mix_in
    c = pl.program_id(1)

    @pl.when(c == 0)
    def _():
        s_scr[...] = jnp.zeros_like(s_scr)
        ctx[0:CONV_CTX, :] = jnp.zeros((CONV_CTX, MIX), F32)

    ret_tabs = (dec_ref, qdec_ref, kdec_ref, cmat_ref, bdm_ref, avg_ref, gn_ref)
    sgu_tabs = (sg_ref, sb_ref, bias_ref, avg_ref)
    wms = _sgu_weights(ws_ref)
    glu = cv_ref[...] * _sigmoid(cg_ref[...])
    glu_ref[...] = glu
    ctx[CONV_CTX:CONV_CTX + rows, :] = glu
    state = s_scr[...]
    for cc in range(rows // RET_CHUNK):
        rs = slice(cc * RET_CHUNK, (cc + 1) * RET_CHUNK)
        y, state = _retention_chunk(q_ref[rs, :], k_ref[rs, :], v_ref[rs, :], g_ref[rs, :], state, ret_tabs)
        out = mix_ref.dtype
        mix_ref[rs, 0:MIX] = y.astype(out)
        mix_ref[rs, 2 * MIX:3 * MIX] = _conv_rows(ctx, cc * RET_CHUNK, RET_CHUNK, cw_ref, cb_ref, clg_ref, clb_ref).astype(out)
        mix_ref[rs, 3 * MIX:4 * MIX] = _sgu_chunk(gu_ref[rs, :], gv_ref[rs, :], wms, sgu_tabs)[0].astype(out)
    mix_ref[:, MIX:2 * MIX] = jnp.zeros((rows, MIX), mix_ref.dtype)
    s_scr[...] = state
    ctx[0:CONV_CTX, :] = ctx[rows:rows + CONV_CTX, :]

    @pl.when(c == n_steps - 1)
    def _():
        sout_ref[0] = s_scr[...]


def _mixers_prompt(h, mix, ret_tabs, avg, gn_g, conv_args, sgu_args, *, seq, n_seq):
    assert RET_CHUNK == SGU_CHUNK and seq % MIX_ROWS == 0
    rows = MIX_ROWS
    n_steps = seq // rows
    dec, qdec, kdec, cmat, bdm = ret_tabs
    cw, cb, clg, clb = conv_args
    sg, sb, ws, bias = sgu_args

    def hspec(col):
        return pl.BlockSpec((rows, MIX), lambda b, c: (b * n_steps + c, col))

    def const(arr):
        return pl.BlockSpec(arr.shape, lambda b, c: (0,) * arr.ndim)

    tables = [dec, qdec, kdec, cmat, bdm, avg, gn_g, cw, cb, clg, clb, sg, sb, ws, bias]
    cols = [COL_RQ, COL_RK, COL_RV, COL_RG, COL_CV, COL_CG, COL_GU, COL_GV]
    nbytes = (2 * (len(cols) + 5) * rows * MIX * 4 + 2 * sum(t.size * 4 for t in tables)
              + (CONV_CTX + rows) * MIX * 4 + MIX * MIX * 4)
    return pl.pallas_call(
        functools.partial(_mixers_body, n_steps=n_steps, rows=rows),
        grid=(n_seq, n_steps),
        in_specs=[hspec(col) for col in cols] + [const(t) for t in tables] + [pl.BlockSpec(memory_space=pl.ANY)],
        out_specs=[pl.BlockSpec((rows, 4 * MIX), lambda b, c: (b * n_steps + c, 0)),
                   pl.BlockSpec((rows, MIX), lambda b, c: (b * n_steps + c, 0)),
                   pl.BlockSpec((1, MIX, MIX), lambda b, c: (b, 0, 0))],
        out_shape=[jax.ShapeDtypeStruct(mix.shape, mix.dtype),
                   jax.ShapeDtypeStruct((n_seq * seq, MIX), F32),
                   jax.ShapeDtypeStruct((n_seq, MIX, MIX), F32)],
        scratch_shapes=[pltpu.VMEM((MIX, MIX), F32), pltpu.VMEM((CONV_CTX + rows, MIX), F32)],
        input_output_aliases={len(cols) + len(tables): 0},
        compiler_params=_params(("arbitrary", "arbitrary"), nbytes),
        name="mixers_prompt",
    )(*([h] * len(cols)), *tables, mix)


def _mixers_sample_body(q_ref, k_ref, v_ref, g_ref, cv_ref, cg_ref, gu_ref, gv_ref, s0_ref, buf_ref,
                        dec_ref, qdec_ref, kdec_ref, cmat_ref, bdm_ref, avg_ref, gn_ref,
                        cw_ref, cb_ref, clg_ref, clb_ref, sg_ref, sb_ref, ws_ref, bias_ref, mix_in,
                        mix_ref, glu_ref, vn_ref, sout_ref, ctx, *, seqs, rows):
    del mix_in
    ret_tabs = (dec_ref, qdec_ref, kdec_ref, cmat_ref, bdm_ref, avg_ref, gn_ref)
    sgu_tabs = (sg_ref, sb_ref, bias_ref, avg_ref)
    wms = _sgu_weights(ws_ref)
    glu = cv_ref[...] * _sigmoid(cg_ref[...])
    glu_ref[...] = glu
    ret, conv, gate = [], [], []
    for s in range(seqs):
        rs = slice(s * rows, (s + 1) * rows)
        y, sout_ref[s] = _retention_chunk(q_ref[rs, :], k_ref[rs, :], v_ref[rs, :], g_ref[rs, :], s0_ref[s], ret_tabs)
        ret.append(y)
        ctx[s, 0:CONV_CTX, :] = buf_ref[s]
        ctx[s, CONV_CTX:CONV_CTX + rows, :] = glu[rs, :]
        conv.append(_conv_rows(ctx.at[s], 0, rows, cw_ref, cb_ref, clg_ref, clb_ref))
        y, vn_ref[rs, :] = _sgu_chunk(gu_ref[rs, :], gv_ref[rs, :], wms, sgu_tabs)
        gate.append(y)
    blocks = (ret, [jnp.zeros((rows, MIX), F32)] * seqs, conv, gate)
    for col, parts in enumerate(blocks):
        mix_ref[:, col * MIX:(col + 1) * MIX] = jnp.concatenate(parts, axis=0).astype(mix_ref.dtype)


def _mixers_sample(h, mix, s0, buf, ret_tabs, avg, gn_g, conv_args, sgu_args, *, rows, n_seq, seqs, row0):
    blk = seqs * rows
    rb0 = row0 // blk
    dec, qdec, kdec, cmat, bdm = ret_tabs
    cw, cb, clg, clb = conv_args
    sg, sb, ws, bias = sgu_args

    def hspec(col):
        return pl.BlockSpec((blk, MIX), lambda b: (rb0 + b, col))

    def const(arr):
        return pl.BlockSpec(arr.shape, lambda b: (0,) * arr.ndim)

    per_seq = lambda arr: pl.BlockSpec((seqs,) + arr.shape[1:], lambda b: (b,) + (0,) * (arr.ndim - 1))
    tables = [dec, qdec, kdec, cmat, bdm, avg, gn_g, cw, cb, clg, clb, sg, sb, ws, bias]
    cols = [COL_RQ, COL_RK, COL_RV, COL_RG, COL_CV, COL_CG, COL_GU, COL_GV]
    rows_out = lambda width: pl.BlockSpec((blk, width), lambda b: (b, 0))
    nbytes = (2 * (len(cols) + 6) * blk * MIX * 4 + 2 * sum(t.size * 4 for t in tables)
              + 4 * seqs * MIX * MIX * 4 + 3 * seqs * (CONV_CTX + rows) * MIX * 4)
    return pl.pallas_call(
        functools.partial(_mixers_sample_body, seqs=seqs, rows=rows),
        grid=(n_seq // seqs,),
        in_specs=([hspec(col) for col in cols] + [per_seq(s0), per_seq(buf)] + [const(t) for t in tables]
                  + [pl.BlockSpec(memory_space=pl.ANY)]),
        out_specs=[pl.BlockSpec((blk, 4 * MIX), lambda b: (rb0 + b, 0)), rows_out(MIX), rows_out(MIX), per_seq(s0)],
        out_shape=[jax.ShapeDtypeStruct(mix.shape, mix.dtype),
                   jax.ShapeDtypeStruct((n_seq * rows, MIX), F32),
                   jax.ShapeDtypeStruct((n_seq * rows, MIX), F32),
                   jax.ShapeDtypeStruct(s0.shape, F32)],
        scratch_shapes=[pltpu.VMEM((seqs, CONV_CTX + rows, MIX), F32)],
        input_output_aliases={len(cols) + 2 + len(tables): 0},
        compiler_params=_params(("arbitrary",), nbytes),
        name="mixers_sample",
    )(*([h] * len(cols)), s0, buf, *tables, mix)


def _out_router_body(mix_ref, w_ref, x_ref, g_ref, b_ref, wr_ref, br_ref,
                     x1_ref, route_ref, counts_ref, wbf, wrbf, logit_scr, *, alpha):
    @pl.when(pl.program_id(0) == 0)
    def _():
        wbf[...] = w_ref[0].astype(BF16)
        wrbf[...] = wr_ref[...].astype(BF16)

    parts = [slice(r0, r0 + OUT_ROWS) for r0 in range(0, TM, OUT_ROWS)]
    ys = [_dot(mix_ref[rs, :], wbf[...]) for rs in parts]
    for rs, y in zip(parts, ys):
        x1 = _layer_norm(alpha * x_ref[rs, :] + y, g_ref[...], b_ref[...])
        x1_ref[rs, :] = x1
        logit_scr[:, rs] = _dot_nt(wrbf[...], x1.astype(BF16)) + br_ref[...]
    logits = logit_scr[0:ROUTE_ROWS, :]
    row = lax.broadcasted_iota(I32, (ROUTE_ROWS, 1), 0).astype(F32)

    def first_max(mask, vals):
        masked = jnp.where(mask, vals, NEG_INF)
        top = jnp.max(masked, axis=0, keepdims=True)
        idx = jnp.min(jnp.where(mask & (masked == top), row, float(LANE)), axis=0, keepdims=True)
        return top, idx

    gmask = row < N_GROUPS
    gmax, gsel = first_max(gmask, logits)
    g_gate = 1.0 / jnp.sum(jnp.where(gmask, jnp.exp(logits - gmax), 0.0), axis=0, keepdims=True)
    lo = N_GROUPS + EXPERTS_PER_GROUP * gsel
    emask = (row >= lo) & (row < lo + EXPERTS_PER_GROUP)
    v1, i1 = first_max(emask, logits)
    v2, i2 = first_max(emask & (row != i1), logits)
    e12 = jnp.exp(v2 - v1)
    both = (row == i1).astype(F32) + (row == i2).astype(F32)

    earlier = (lax.broadcasted_iota(I32, (TD, TD), 0) < lax.broadcasted_iota(I32, (TD, TD), 1)).astype(BF16)
    rank_a, rank_b = [], []
    for td in range(TM // TD):
        cs = slice(td * TD, (td + 1) * TD)
        tile = both[:, cs]
        counts_ref[td] = jnp.sum(tile, axis=1, keepdims=True)
        before = _dot(tile.astype(BF16), earlier)
        rank_a.append(jnp.sum(jnp.where(row == i1[:, cs], before, 0.0), axis=0, keepdims=True))
        rank_b.append(jnp.sum(jnp.where(row == i2[:, cs], before, 0.0), axis=0, keepdims=True))
    values = (i1 - N_GROUPS, i2 - N_GROUPS, g_gate * (1.0 / (1.0 + e12)), g_gate * (e12 / (1.0 + e12)),
              jnp.concatenate(rank_a, axis=1), jnp.concatenate(rank_b, axis=1))
    out_row = lax.broadcasted_iota(I32, (SUBLANE, 1), 0)
    route = jnp.zeros((SUBLANE, TM), F32)
    for k, val in enumerate(values):
        route = jnp.where(out_row == k, val, route)
    route_ref[...] = route


def _out_router(mix, w_out, layer, x, g, b, wr, br, *, alpha):
    nt, d = x.shape
    kdim = mix.shape[1]
    const = lambda shape: pl.BlockSpec(shape, lambda i: (0, 0))
    row = lambda width: pl.BlockSpec((TM, width), lambda i: (i, 0))
    nbytes = 2 * (kdim * d * 4 + 2 * d * LANE * 4 + TM * kdim * 4 + 2 * TM * d * 4 + TM * LANE * 4) + kdim * d * 2
    return pl.pallas_call(
        functools.partial(_out_router_body, alpha=alpha),
        grid=(nt // TM,),
        in_specs=[row(kdim), pl.BlockSpec((1, kdim, d), lambda i: (layer, 0, 0)), row(d),
                  const((1, d)), const((1, d)), const((LANE, d)), const((LANE, 1))],
        out_specs=[row(d), pl.BlockSpec((SUBLANE, TM), lambda i: (0, i)),
                   pl.BlockSpec((TM // TD, ROUTE_ROWS, 1), lambda i: (i, 0, 0))],
        out_shape=[jax.ShapeDtypeStruct((nt, d), F32), jax.ShapeDtypeStruct((SUBLANE, nt), F32),
                   jax.ShapeDtypeStruct((nt // TD, ROUTE_ROWS, 1), F32)],
        scratch_shapes=[pltpu.VMEM((kdim, d), BF16), pltpu.VMEM((LANE, d), BF16), pltpu.VMEM((LANE, TM), F32)],
        compiler_params=_params(("arbitrary",), nbytes),
        name="out_router",
    )(mix, w_out, x, g, b, wr, br)


def _local_slots(route, off_col):
    expert = lax.broadcasted_iota(I32, (N_EXPERTS, 1), 0).astype(F32)
    slots = []
    for kk in range(TOP_K):
        start = jnp.sum(jnp.where(expert == route[kk:kk + 1, :], off_col, 0.0), axis=0, keepdims=True)
        slots.append(start + route[4 + kk:5 + kk, :])
    return slots


def _pack_rows(y):
    half = y.shape[1] // 2
    bits = lax.bitcast_convert_type(y.astype(BF16).astype(F32), U32)
    return (bits[:, :half] & jnp.uint32(0xFFFF0000)) | (bits[:, half:] >> 16)


def _unpack_rows(w):
    hi = lax.bitcast_convert_type(w & jnp.uint32(0xFFFF0000), F32)
    lo = lax.bitcast_convert_type(w << 16, F32)
    return jnp.concatenate([hi, lo], axis=1)


def _chunk_copy(hbm, hbm_row, loc, chunk, sem, *, to_hbm):
    vm = loc.at[pl.ds(pl.multiple_of(chunk * MOE_CHUNK, MOE_CHUNK), MOE_CHUNK), :]
    hb = hbm.at[pl.ds(pl.multiple_of(hbm_row, MOE_CHUNK), MOE_CHUNK), :]
    return pltpu.make_async_copy(vm, hb, sem) if to_hbm else pltpu.make_async_copy(hb, vm, sem)


def _for_each_chunk(count, fn):
    def group(t, carry):
        for u in range(CHUNK_UNROLL):
            fn(t * CHUNK_UNROLL + u)
        return carry

    def single(c, carry):
        fn(c)
        return carry

    whole = count // CHUNK_UNROLL
    lax.fori_loop(0, whole, group, 0)
    lax.fori_loop(whole * CHUNK_UNROLL, count, single, 0)


def _wait_chunks(hbm, buf, sem, count, *, to_hbm):
    _for_each_chunk(count, lambda c: _chunk_copy(hbm, 0, buf, 0, sem, to_hbm=to_hbm).wait())


def _dispatch_body(dst_tab, n_chunks, x_ref, route_ref, off_ref, xs_in, xs_hbm, loc, sem):
    del xs_in
    i = pl.program_id(0)
    last = pl.num_programs(0) - 1
    cur = i % 2
    slot_a, slot_b = _local_slots(route_ref[...], off_ref[0])
    buf_row = lax.broadcasted_iota(I32, (LOC_ROWS, 1), 0).astype(F32)
    onehot = ((buf_row == slot_a) | (buf_row == slot_b)).astype(BF16)
    loc[cur] = _pack_rows(_dot(onehot, x_ref[...].astype(BF16)))

    _for_each_chunk(n_chunks[i], lambda c: _chunk_copy(
        xs_hbm, dst_tab[i * MAX_CHUNKS + c], loc.at[cur], c, sem.at[cur], to_hbm=True).start())

    @pl.when(i > 0)
    def _():
        _wait_chunks(xs_hbm, loc.at[1 - cur], sem.at[1 - cur], n_chunks[jnp.maximum(i - 1, 0)], to_hbm=True)

    @pl.when(i == last)
    def _():
        _wait_chunks(xs_hbm, loc.at[cur], sem.at[cur], n_chunks[i], to_hbm=True)


def _dispatch(dst_tab, n_chunks, x1, route, off, xs_prev):
    nt, d = x1.shape
    row = lambda width: pl.BlockSpec((TD, width), lambda i, *_: (i, 0))
    nbytes = 2 * (TD * d * 4 + TD * LANE * 4) + 2 * LOC_ROWS * d * 4 + 4 * TD * LOC_ROWS * 4 + LOC_ROWS * d * 4
    return pl.pallas_call(
        _dispatch_body,
        grid_spec=pltpu.PrefetchScalarGridSpec(
            num_scalar_prefetch=2,
            grid=(nt // TD,),
            in_specs=[row(d), pl.BlockSpec((SUBLANE, TD), lambda i, *_: (0, i)),
                      pl.BlockSpec((1, N_EXPERTS, 1), lambda i, *_: (i, 0, 0)),
                      pl.BlockSpec(memory_space=pl.ANY)],
            out_specs=pl.BlockSpec(memory_space=pl.ANY),
            scratch_shapes=[pltpu.VMEM((2, LOC_ROWS, d // 2), U32), pltpu.SemaphoreType.DMA((2,))]),
        out_shape=jax.ShapeDtypeStruct(xs_prev.shape, U32),
        input_output_aliases={5: 0},
        compiler_params=_params(("arbitrary",), nbytes),
        name="moe_dispatch",
    )(dst_tab, n_chunks, x1, route, off, xs_prev)


def _ffn_body(blk_e, blk_valid, blk_src, x_ref, w1_ref, w3_ref, w2_ref, y_ref, w1b, w3b, w2b):
    del blk_src
    j = pl.program_id(0)

    n_valid = blk_valid[j]
    half = MOE_BLK // 2

    @pl.when((n_valid > 0) & ((j == 0) | (blk_e[j] != blk_e[jnp.maximum(j - 1, 0)])))
    def _():
        w1b[...] = w1_ref[0, 0].astype(BF16)
        w3b[...] = w3_ref[0, 0].astype(BF16)
        w2b[...] = w2_ref[0, 0].astype(BF16)

    def ffn(rows):
        xb = _unpack_rows(x_ref[rows, :]).astype(BF16)
        h1 = _dot(xb, w1b[...])
        h3 = _dot(xb, w3b[...])
        act = (h1 * _sigmoid(h1)) * h3
        y_ref[rows, :] = _pack_rows(_dot(act.astype(BF16), w2b[...]))

    @pl.when(n_valid > half)
    def _():
        ffn(slice(0, MOE_BLK))

    @pl.when((n_valid > 0) & (n_valid <= half))
    def _():
        ffn(slice(0, half))
        y_ref[half:MOE_BLK, :] = jnp.zeros((half, y_ref.shape[1]), U32)


def _ffn(xs, w1, w3, w2, layer, blk_e, blk_valid, blk_src):
    d = w1.shape[2]
    de = w1.shape[3]
    n_blk = xs.shape[0] // MOE_BLK
    wmap = lambda j, be, bv, bs: (layer, be[j], 0, 0)
    rows = pl.BlockSpec((MOE_BLK, d // 2), lambda j, be, bv, bs: (bs[j], 0))
    nbytes = 2 * (3 * d * de * 4 + MOE_BLK * d * 4) + 3 * d * de * 2 + 6 * MOE_BLK * de * 4 + 4 * MOE_BLK * d * 4
    wspec = lambda shape: pl.BlockSpec(shape, wmap)
    return pl.pallas_call(
        _ffn_body,
        grid_spec=pltpu.PrefetchScalarGridSpec(
            num_scalar_prefetch=3,
            grid=(n_blk,),
            in_specs=[rows, wspec((1, 1, d, de)), wspec((1, 1, d, de)), wspec((1, 1, de, d))],
            out_specs=rows,
            scratch_shapes=[pltpu.VMEM((d, de), BF16), pltpu.VMEM((d, de), BF16), pltpu.VMEM((de, d), BF16)]),
        out_shape=jax.ShapeDtypeStruct(xs.shape, U32),
        input_output_aliases={3: 0},
        compiler_params=_params(("arbitrary",), nbytes),
        name="expert_ffn",
    )(blk_e, blk_valid, blk_src, xs, w1, w3, w2)


def _combine_body(dst_tab, n_chunks, x_ref, route_ref, off_ref, g_ref, b_ref, ys_hbm, x2_ref,
                  *rest, alpha, head_tiles):
    *tail_ref, loc, sem = rest
    i = pl.program_id(0)
    last = pl.num_programs(0) - 1
    cur = i % 2

    def fetch(tile, buf):
        _for_each_chunk(n_chunks[tile], lambda c: _chunk_copy(
            ys_hbm, dst_tab[tile * MAX_CHUNKS + c], loc.at[buf], c, sem.at[buf], to_hbm=False).start())

    @pl.when(i == 0)
    def _():
        loc[...] = jnp.zeros_like(loc)
        fetch(0, 0)

    @pl.when(i < last)
    def _():
        fetch(jnp.minimum(i + 1, last), 1 - cur)

    route = route_ref[...]
    slot_a, slot_b = _local_slots(route, off_ref[0])
    buf_row = lax.broadcasted_iota(I32, (LOC_ROWS, 1), 0).astype(F32)
    gates = (jnp.where(buf_row == slot_a, route[2:3, :], 0.0)
             + jnp.where(buf_row == slot_b, route[3:4, :], 0.0))
    _wait_chunks(ys_hbm, loc.at[cur], sem.at[cur], n_chunks[i], to_hbm=False)
    f = _dot_tn(gates.astype(BF16), _unpack_rows(loc[cur]).astype(BF16))
    x2 = _layer_norm(alpha * x_ref[...] + f, g_ref[...], b_ref[...])
    if tail_ref:
        @pl.when(i < head_tiles)
        def _():
            x2_ref[...] = x2

        @pl.when(i >= head_tiles)
        def _():
            tail_ref[0][...] = x2
    else:
        x2_ref[...] = x2


def _combine(dst_tab, n_chunks, x1, route, off, g, b, ys, *, alpha, head_rows=None):
    nt, d = x1.shape
    row = lambda width: pl.BlockSpec((TD, width), lambda i, dt, nc: (i, 0))
    const = lambda shape: pl.BlockSpec(shape, lambda i, dt, nc: (0, 0))
    nbytes = 2 * (3 * TD * d * 4 + TD * LANE * 4) + 2 * LOC_ROWS * d * 4 + 4 * TD * LOC_ROWS * 4 + LOC_ROWS * d * 2
    if head_rows is None:
        head_tiles = nt // TD
        out_specs = row(d)
        out_shape = jax.ShapeDtypeStruct((nt, d), F32)
    else:
        head_tiles = head_rows // TD
        out_specs = [pl.BlockSpec((TD, d), lambda i, dt, nc: (jnp.minimum(i, head_tiles - 1), 0)),
                     pl.BlockSpec((TD, d), lambda i, dt, nc: (jnp.maximum(i - head_tiles, 0), 0))]
        out_shape = [jax.ShapeDtypeStruct((head_rows, d), F32), jax.ShapeDtypeStruct((nt - head_rows, d), F32)]
    scratch = [pltpu.VMEM((2, LOC_ROWS, d // 2), U32), pltpu.SemaphoreType.DMA((2,))]
    return pl.pallas_call(
        functools.partial(_combine_body, alpha=alpha, head_tiles=head_tiles),
        grid_spec=pltpu.PrefetchScalarGridSpec(
            num_scalar_prefetch=2,
            grid=(nt // TD,),
            in_specs=[row(d), pl.BlockSpec((SUBLANE, TD), lambda i, dt, nc: (0, i)),
                      pl.BlockSpec((1, N_EXPERTS, 1), lambda i, dt, nc: (i, 0, 0)),
                      const((1, d)), const((1, d)), pl.BlockSpec(memory_space=pl.ANY)],
            out_specs=out_specs,
            scratch_shapes=scratch),
        out_shape=out_shape,
        compiler_params=_params(("arbitrary",), nbytes),
        name="combine_ln",
    )(dst_tab, n_chunks, x1, route, off, g, b, ys)


def _rope_tables(pos, rot_dim, theta):
    half = rot_dim // 2
    inv = jnp.float32(theta) ** (-jnp.arange(half, dtype=F32) * (2.0 / rot_dim))
    ang = pos.astype(F32)[:, None] * inv[None, :]
    c, s = jnp.cos(ang), jnp.sin(ang)
    rest = HEAD_DIM - rot_dim
    ones = jnp.ones((pos.shape[0], rest), F32)
    zeros = jnp.zeros((pos.shape[0], rest), F32)
    cos_h = jnp.concatenate([c, c, ones], axis=1)
    sin_h = jnp.concatenate([-s, s, zeros], axis=1)
    return jnp.tile(cos_h, (1, N_HEADS)), jnp.tile(sin_h, (1, N_HEADS))


def _block_diag(s):
    rows = [jnp.pad(s[:, h], ((0, 0), (0, 0), (h * HEAD_DIM, MIX - (h + 1) * HEAD_DIM))) for h in range(N_HEADS)]
    return jnp.concatenate(rows, axis=1)


def _diag_blocks(s):
    return jnp.stack([s[:, h * HEAD_DIM:(h + 1) * HEAD_DIM, h * HEAD_DIM:(h + 1) * HEAD_DIM]
                      for h in range(N_HEADS)], axis=1)


def _routing_plan(tile_counts, n_blk):
    cnt_t = tile_counts[:, N_GROUPS:N_GROUPS + N_EXPERTS, 0].astype(I32)
    run_chunks = (cnt_t + MOE_CHUNK - 1) // MOE_CHUNK
    run_rows = run_chunks * MOE_CHUNK
    cnt = jnp.sum(run_rows, axis=0)
    pcnt = (cnt + MOE_BLK - 1) // MOE_BLK * MOE_BLK
    pstart = jnp.cumsum(pcnt) - pcnt
    n_act = jnp.sum(pcnt) // MOE_BLK
    gstart = pstart[None, :] + jnp.cumsum(run_rows, axis=0) - run_rows
    chunk0 = jnp.cumsum(run_chunks, axis=1) - run_chunks
    n_chunks = jnp.sum(run_chunks, axis=1).astype(I32)

    def spread(first, count, value, n):
        p = jnp.arange(n, dtype=I32)[:, None]
        f, c, v = first[..., None, :], count[..., None, :], value[..., None, :]
        return jnp.sum(jnp.where((f <= p) & (p < f + c), v + (p - f) * MOE_CHUNK, 0), axis=-1)

    dst_tab = spread(chunk0, run_chunks, gstart, MAX_CHUNKS)
    off = (chunk0 * MOE_CHUNK).astype(F32)[:, :, None]
    pos = jnp.arange(n_blk, dtype=I32)[:, None] * MOE_BLK
    mine = (pstart[None, :] <= pos) & (pos < (pstart + pcnt)[None, :])
    experts = jnp.arange(N_EXPERTS, dtype=I32)[None, :]
    last_e = jnp.max(jnp.where(cnt > 0, experts[0], 0))
    active = pos[:, 0] < n_act * MOE_BLK
    blk_e = jnp.where(active, jnp.sum(jnp.where(mine, experts, 0), axis=-1), last_e)
    blk_valid = jnp.sum(jnp.where(mine, jnp.clip((pstart + cnt)[None, :] - pos, 0, MOE_BLK), 0), axis=-1)
    blk_src = jnp.minimum(jnp.arange(n_blk, dtype=I32), jnp.maximum(n_act - 1, 0))
    i32 = lambda t: t.astype(I32)
    return i32(dst_tab).reshape(-1), n_chunks, off, i32(blk_e), i32(blk_valid), i32(blk_src)


def kernel(x_prompt, x_sample, state_ret, cache_swa_k, cache_swa_v, state_conv, w_in, w_out, ret_gn_g,
           conv_w, conv_b, conv_ln_g, conv_ln_b, sgu_ln_g, sgu_ln_b, sgu_w, sgu_b, ln1_g, ln1_b,
           ln2_g, ln2_b, router_g_w, router_g_b, router_e_w, router_e_b, moe_w1, moe_w3, moe_w2):
    depth = w_in.shape[0]
    n_seq, seq, d = x_prompt.shape
    n_dec, t_new, _ = x_sample.shape
    past = cache_swa_k.shape[2]
    rows_s = -(-t_new // SUBLANE) * SUBLANE
    n_p = n_seq * seq
    n_s = n_dec * rows_s
    n_tail = -(-n_s // TM) * TM
    nt = n_p + n_tail
    assert seq % SWA_SB == 0 and seq % TM == 0 and TM % TD == 0
    assert t_new % RET_CHUNK != 0 and t_new <= SGU_CHUNK and w_in.shape[2] == N_COLBLK * MIX
    assert n_dec % SAMPLE_SEQS == 0 and n_dec % SWA_SAMPLE_SEQS == 0
    assert n_p % (SAMPLE_SEQS * rows_s) == 0
    alpha = (2 * depth) ** 0.25
    keep = min(DILATED_CONFIGS[-1][0], seq)
    n_blk = -(-(nt * TOP_K + (nt // TD) * N_EXPERTS * (MOE_CHUNK - 1) + N_EXPERTS * (MOE_BLK - 1)) // MOE_BLK)
    cache_k = jnp.transpose(cache_swa_k, (0, 1, 3, 4, 2)).reshape(depth, n_dec, MIX, past)
    cache_v = jnp.transpose(cache_swa_v, (0, 1, 3, 4, 2)).reshape(depth, n_dec, MIX, past)

    x = jnp.concatenate([x_prompt.reshape(n_p, d),
                         jnp.pad(x_sample, ((0, 0), (0, rows_s - t_new), (0, 0))).reshape(n_s, d),
                         jnp.zeros((n_tail - n_s, d), F32)], axis=0)
    pos = jnp.concatenate([jnp.arange(seq, dtype=I32),
                           PAST_LEN + jnp.arange(n_tail, dtype=I32) % rows_s])
    rope = _rope_tables(pos, HEAD_DIM, RET_THETA) + _rope_tables(pos, ROPE_DIM, ROPE_THETA)
    ret_tabs_p = _retention_tables(RET_CHUNK, RET_CHUNK)
    ret_tabs_s = _retention_tables(t_new, rows_s)
    lane_head = jnp.arange(MIX) // HEAD_DIM
    avg = ((lane_head[:, None] == lane_head[None, :]).astype(F32) / HEAD_DIM).astype(BF16)
    row2 = lambda v: v.reshape(1, -1)
    mix = jnp.zeros((nt, 4 * MIX), BF16)
    slots = jnp.zeros((n_blk * MOE_BLK, d // 2), U32)

    outs = [[] for _ in range(9)]
    for l in range(depth):
        h = _in_proj(x, w_in, l, rope, n_p // TM, seq // TM)
        gn = row2(ret_gn_g[l])
        cw = jnp.pad(conv_w[l], ((0, CONV_CTX - CONV_WIDTH), (0, 0)))
        cargs = (cw, row2(conv_b[l]), row2(conv_ln_g[l]), row2(conv_ln_b[l]))
        sbias = jnp.repeat(sgu_b[l].T, HEAD_DIM, axis=1)
        sargs = (row2(sgu_ln_g[l]), row2(sgu_ln_b[l]), sgu_w[l], sbias)
        mix, glu_p, s_p = _mixers_prompt(h, mix, ret_tabs_p, avg, gn, cargs, sargs, seq=seq, n_seq=n_seq)
        mix = _swa_prompt(h, mix, seq=seq, n_seq=n_seq)
        ctx_s = jnp.pad(state_conv[l], ((0, 0), (CONV_CTX - (CONV_WIDTH - 1), 0), (0, 0)))
        mix, glu_s, vn_s, s_s = _mixers_sample(h, mix, _block_diag(state_ret[l]), ctx_s, ret_tabs_s, avg, gn, cargs,
                                               sargs, rows=rows_s, n_seq=n_dec, seqs=SAMPLE_SEQS, row0=n_p)
        mix = _swa_sample(h, mix, cache_k, cache_v, l, rows=rows_s, n_seq=n_dec, seqs=SWA_SAMPLE_SEQS, row0=n_p)

        wr = jnp.concatenate([router_g_w[l].T, jnp.transpose(router_e_w[l], (0, 2, 1)).reshape(N_EXPERTS, d),
                              jnp.zeros((LANE - N_GROUPS - N_EXPERTS, d), F32)], axis=0)
        br = jnp.concatenate([router_g_b[l], router_e_b[l].reshape(-1),
                              jnp.zeros((LANE - N_GROUPS - N_EXPERTS,), F32)]).reshape(LANE, 1)
        x1, route, counts = _out_router(mix, w_out, l, x, row2(ln1_g[l]), row2(ln1_b[l]), wr, br, alpha=alpha)
        dst_tab, n_chunks, off, blk_e, blk_valid, blk_src = _routing_plan(counts, n_blk)
        slots = _dispatch(dst_tab, n_chunks, x1, route, off, slots)
        slots = _ffn(slots, moe_w1, moe_w3, moe_w2, l, blk_e, blk_valid, blk_src)
        x = _combine(dst_tab, n_chunks, x1, route, off, row2(ln2_g[l]), row2(ln2_b[l]), slots, alpha=alpha,
                     head_rows=n_p if l == depth - 1 else None)

        def tail(col, n_keep):
            return jnp.stack([lax.slice(h, ((b + 1) * seq - n_keep, col * MIX), ((b + 1) * seq, (col + 1) * MIX))
                              for b in range(n_seq)])

        def new_rows(col):
            blk = lax.slice(h, (n_p, col * MIX), (n_p + n_s, (col + 1) * MIX))
            return blk.reshape(n_dec, rows_s, MIX)[:, :t_new]

        heads = lambda t: t.reshape(t.shape[0], t.shape[1], N_HEADS, HEAD_DIM)
        outs[0].append(_diag_blocks(s_p))
        outs[1].append(heads(tail(COL_SK, keep)))
        outs[2].append(heads(tail(COL_SV, keep)))
        outs[3].append(glu_p.reshape(n_seq, seq, MIX)[:, seq - (CONV_WIDTH - 1):])
        outs[4].append(_diag_blocks(s_s))
        outs[5].append(heads(new_rows(COL_SK)))
        outs[6].append(heads(new_rows(COL_SV)))
        outs[7].append(glu_s.reshape(n_dec, rows_s, MIX)[:, :t_new])
        outs[8].append(vn_s.reshape(n_dec, rows_s, MIX)[:, :t_new])

    x_head, x_tail = x
    y_prompt = x_head.reshape(n_seq, seq, d)
    y_sample = x_tail[:n_s].reshape(n_dec, rows_s, d)[:, :t_new]
    return (y_prompt, y_sample) + tuple(jnp.stack(o) for o in outs)
```

```python
import functools
import math

import jax
import jax.numpy as jnp
from jax import lax
from jax.experimental import pallas as pl
from jax.experimental.pallas import tpu as pltpu

F32 = jnp.float32
BF16 = jnp.bfloat16
I32 = jnp.int32
U32 = jnp.uint32

HEAD_DIM = 64
N_HEADS = 4
MIX = N_HEADS * HEAD_DIM
(COL_RQ, COL_RK, COL_RV, COL_RG, COL_SQ, COL_SK, COL_SV,
 COL_CV, COL_CG, COL_GU, COL_GV) = range(11)
N_COLBLK = 11
RET_CHUNK = 128
RET_THETA = 10000.0
ROPE_THETA = 500000.0
ROPE_DIM = HEAD_DIM // 4
DILATED_CONFIGS = ((128, 1), (512, 4), (2048, 16))
DIL_BLOCK = 128
CONV_WIDTH = 31
SGU_CHUNK = 128
N_GROUPS = 4
EXPERTS_PER_GROUP = 8
N_EXPERTS = N_GROUPS * EXPERTS_PER_GROUP
TOP_K = 2
PAST_LEN = 8192
LN_EPS = 1e-5
NEG_INF = -1e30

LANE = 128
SUBLANE = 8
VMEM_CAP_BYTES = 60000 * 1024
COMPILER_TEMP_BYTES = 12 * 1024 * 1024

TM = 512
TD = 256
OUT_ROWS = 128
ROUTE_ROWS = -(-(N_GROUPS + N_EXPERTS) // SUBLANE) * SUBLANE
MOE_BLK = 512
MOE_CHUNK = SUBLANE
LOC_ROWS = -(-(TD * TOP_K + N_EXPERTS * (MOE_CHUNK - 1)) // LANE) * LANE
MAX_CHUNKS = LOC_ROWS // MOE_CHUNK
CHUNK_UNROLL = 4
MIX_ROWS = 1024
CONV_CTX = 32
SWA_SB = DIL_BLOCK * 16
SWA_TILE = 256
SWA_UNROLL = 4
SAMPLE_SEQS = 8
SWA_SAMPLE_SEQS = 4


def _vmem(nbytes):
    return int(min(VMEM_CAP_BYTES, nbytes + COMPILER_TEMP_BYTES))


def _params(sem, nbytes):
    return pltpu.CompilerParams(dimension_semantics=sem, vmem_limit_bytes=_vmem(nbytes))


def _dot(a, b):
    return jnp.dot(a, b, preferred_element_type=F32)


def _dot_nt(a, b):
    return lax.dot_general(a, b, (((1,), (1,)), ((), ())), preferred_element_type=F32)


def _dot_tn(a, b):
    return lax.dot_general(a, b, (((0,), (0,)), ((), ())), preferred_element_type=F32)


def _split_dot(x, m):
    hi = x.astype(BF16)
    lo = (x - hi.astype(F32)).astype(BF16)
    return _dot(hi, m) + _dot(lo, m)


def _sigmoid(x):
    return 1.0 / (1.0 + jnp.exp(-x))


def _layer_norm(z, g, b):
    mu = jnp.mean(z, axis=-1, keepdims=True)
    d = z - mu
    var = jnp.mean(d * d, axis=-1, keepdims=True)
    return d * lax.rsqrt(var + LN_EPS) * g + b


def _group_norm(z, avg):
    mu = _split_dot(z, avg)
    d = z - mu
    var = _split_dot(d * d, avg)
    return d * lax.rsqrt(var + LN_EPS)


def _lane_head(width):
    return lax.broadcasted_iota(I32, (1, width), 1) // HEAD_DIM


def _rotate(x, cos, sin_signed, half):
    first = (lax.broadcasted_iota(I32, (1, LANE), 1) % HEAD_DIM) < half
    parts = []
    for p in range(x.shape[1] // LANE):
        t = x[:, p * LANE:(p + 1) * LANE]
        up = pltpu.roll(t, half, 1)
        down = pltpu.roll(t, LANE - half, 1)
        parts.append(jnp.where(first, down, up))
    return x * cos + jnp.concatenate(parts, axis=1) * sin_signed


def _in_proj_body(x_ref, w_ref, cr_ref, sr_ref, cs_ref, ss_ref, h_ref, wbf_ref):
    @pl.when(pl.program_id(0) == 0)
    def _():
        for j in range(N_COLBLK):
            wbf_ref[:, j * MIX:(j + 1) * MIX] = w_ref[0, :, j * MIX:(j + 1) * MIX].astype(BF16)

    xb = x_ref[...].astype(BF16)
    for j in range(N_COLBLK):
        hj = _dot(xb, wbf_ref[:, j * MIX:(j + 1) * MIX])
        if j in (COL_RQ, COL_RK):
            hj = _rotate(hj, cr_ref[...], sr_ref[...], HEAD_DIM // 2)
        if j == COL_RK:
            hj = hj * (HEAD_DIM ** -0.5)
        if j in (COL_SQ, COL_SK):
            hj = _rotate(hj, cs_ref[...], ss_ref[...], ROPE_DIM // 2)
        h_ref[:, j * MIX:(j + 1) * MIX] = hj


def _in_proj(x, w, layer, tabs, n_prompt_tiles, tiles_per_seq):
    nt, d = x.shape
    width = w.shape[2]

    def tab_map(i):
        return (jnp.where(i < n_prompt_tiles, i % tiles_per_seq, tiles_per_seq + i - n_prompt_tiles), 0)

    tab_spec = pl.BlockSpec((TM, MIX), tab_map)
    nbytes = d * width * 4 + d * width * 2 + 2 * TM * d * 4 + 8 * TM * MIX * 4 + 2 * TM * width * 4
    return pl.pallas_call(
        _in_proj_body,
        grid=(nt // TM,),
        in_specs=[pl.BlockSpec((TM, d), lambda i: (i, 0)),
                  pl.BlockSpec((1, d, width), lambda i: (layer, 0, 0), pipeline_mode=pl.Buffered(1)),
                  tab_spec, tab_spec, tab_spec, tab_spec],
        out_specs=pl.BlockSpec((TM, width), lambda i: (i, 0)),
        out_shape=jax.ShapeDtypeStruct((nt, width), F32),
        scratch_shapes=[pltpu.VMEM((d, width), BF16)],
        compiler_params=_params(("arbitrary",), nbytes),
        name="in_proj",
    )(x, w, *tabs)


def _retention_chunk(q, k, v, g, state, tabs):
    dec_ref, qdec_ref, kdec_ref, cmat_ref, bdm_ref, avg_ref, gn_ref = tabs
    head = _lane_head(MIX)
    vb = v.astype(BF16)
    kb = k.astype(BF16)
    o = _dot(q.astype(BF16), state.astype(BF16)) * qdec_ref[...]
    for h in range(N_HEADS):
        mh = head == h
        a = _dot_nt(jnp.where(mh, q, 0.0).astype(BF16), kb) * dec_ref[h]
        o = o + jnp.where(mh, _dot(a.astype(BF16), vb), 0.0)
    upd = _dot_tn((k * kdec_ref[...]).astype(BF16), vb)
    state = state * cmat_ref[...] + upd * bdm_ref[...]
    return _group_norm(o, avg_ref[...]) * gn_ref[...] * (g * _sigmoid(g)), state


def _retention_tables(l_real, l_pad):
    hh = jnp.arange(N_HEADS, dtype=F32)
    log_g = jnp.log1p(-jnp.exp2(-5.0 - hh))
    i = jnp.arange(l_pad, dtype=F32)
    rel = i[:, None] - i[None, :]
    dec = jnp.where(rel >= 0, jnp.exp(log_g[:, None, None] * jnp.maximum(rel, 0.0)), 0.0)
    qd = jnp.exp(log_g[:, None] * (i + 1.0))
    kd = jnp.where(i < l_real, jnp.exp(log_g[:, None] * (l_real - 1.0 - i)), 0.0)
    cd = jnp.exp(log_g * l_real)
    expand = lambda t: jnp.repeat(t.T, HEAD_DIM, axis=1)
    row_head = jnp.arange(MIX) // HEAD_DIM
    bdm = (row_head[:, None] == row_head[None, :]).astype(F32)
    cmat = cd[row_head][:, None] * bdm
    return dec, expand(qd), expand(kd), cmat, bdm


def _swa_prompt_body(q_ref, k_ref, v_ref, mix_ref, y_ref, o0, o1, o2, l0, l1, l2, *, seq):
    del mix_ref
    o_scr = (o0, o1, o2)
    l_scr = (l0, l1, l2)
    head = _lane_head(LANE)
    qi = lax.broadcasted_iota(I32, (DIL_BLOCK, 1), 0)
    ki = lax.broadcasted_iota(I32, (1, 2 * DIL_BLOCK), 1)
    rel = DIL_BLOCK + qi - ki
    scale = HEAD_DIM ** -0.5
    for sb in range(seq // SWA_SB):
        for ci, (window, dil) in enumerate(DILATED_CONFIGS):
            span = window // dil
            band = (rel >= 0) & (rel <= span)

            def block(i, carry, sb=sb, ci=ci, dil=dil, band=band):
                nloc = i // dil
                off = nloc * (DIL_BLOCK * dil) + i % dil
                start = sb * SWA_SB + off
                if sb == 0:
                    first = nloc == 0
                    pstart = jnp.where(first, start, start - DIL_BLOCK * dil)
                    valid = band & (ki >= jnp.where(first, DIL_BLOCK, 0))
                else:
                    pstart = start - DIL_BLOCK * dil
                    valid = band
                stride = dil if dil > 1 else None
                cur = pl.ds(start, DIL_BLOCK, stride=stride)
                prev = pl.ds(pstart, DIL_BLOCK, stride=stride)
                qb = q_ref[cur, :] * scale
                kb = jnp.concatenate([k_ref[prev, :], k_ref[cur, :]], axis=0).astype(BF16)
                vb = jnp.concatenate([v_ref[prev, :], v_ref[cur, :]], axis=0).astype(BF16)
                o = jnp.zeros((DIL_BLOCK, LANE), F32)
                ls = jnp.zeros((DIL_BLOCK, LANE), F32)
                for hh in range(LANE // HEAD_DIM):
                    mh = head == hh
                    s = _dot_nt(jnp.where(mh, qb, 0.0).astype(BF16), kb)
                    s = jnp.where(valid, s, NEG_INF)
                    m = jnp.max(s, axis=-1, keepdims=True)
                    p = jnp.exp(s - m)
                    l = jnp.sum(p, axis=-1, keepdims=True)
                    o = jnp.where(mh, _dot(p.astype(BF16), vb) / l, o)
                    ls = jnp.where(mh, m + jnp.log(l), ls)
                dst = pl.ds(off, DIL_BLOCK, stride=stride)
                o_scr[ci][dst, :] = o
                l_scr[ci][dst, :] = ls
                return carry

            lax.fori_loop(0, SWA_SB // DIL_BLOCK, block, 0, unroll=SWA_UNROLL)

        def combine(t, carry, sb=sb):
            rows = pl.ds(t * SWA_TILE, SWA_TILE)
            la, lb, lc = l_scr[0][rows, :], l_scr[1][rows, :], l_scr[2][rows, :]
            mx = jnp.maximum(jnp.maximum(la, lb), lc)
            ea, eb, ec = jnp.exp(la - mx), jnp.exp(lb - mx), jnp.exp(lc - mx)
            num = ea * o_scr[0][rows, :] + eb * o_scr[1][rows, :] + ec * o_scr[2][rows, :]
            y_ref[pl.ds(sb * SWA_SB + t * SWA_TILE, SWA_TILE), :] = (num / (ea + eb + ec)).astype(y_ref.dtype)
            return carry

        lax.fori_loop(0, SWA_SB // SWA_TILE, combine, 0)


def _swa_prompt(h, mix, *, seq, n_seq):
    halves = MIX // LANE

    def hspec(col):
        return pl.BlockSpec((seq, LANE), lambda b, p: (b, col * halves + p))

    nbytes = 2 * 4 * seq * LANE * 4 + 6 * SWA_SB * LANE * 4
    return pl.pallas_call(
        functools.partial(_swa_prompt_body, seq=seq),
        grid=(n_seq, halves),
        in_specs=[hspec(COL_SQ), hspec(COL_SK), hspec(COL_SV), pl.BlockSpec(memory_space=pl.ANY)],
        out_specs=pl.BlockSpec((seq, LANE), lambda b, p: (b, halves + p)),
        out_shape=jax.ShapeDtypeStruct(mix.shape, mix.dtype),
        scratch_shapes=[pltpu.VMEM((SWA_SB, LANE), F32)] * 6,
        input_output_aliases={3: 0},
        compiler_params=_params(("arbitrary", "arbitrary"), nbytes),
        name="swa_prompt",
    )(h, h, h, mix)


def _swa_sample_body(q_ref, k_ref, v_ref, ck_ref, cv_ref, mix_ref, y_ref, *, past, rows, seqs):
    del mix_ref
    head = _lane_head(MIX)
    hr = N_HEADS * rows
    scale = HEAD_DIM ** -0.5
    t_row = lax.broadcasted_iota(I32, (hr, 1), 0) % rows
    rel_c = past + t_row - lax.broadcasted_iota(I32, (1, past), 1)
    rel_n = t_row - lax.broadcasted_iota(I32, (1, LANE), 1)
    valid_c, valid_n = [], []
    for window, dil in DILATED_CONFIGS:
        span = window // dil
        shift = int(math.log2(dil))
        for rel, dst in ((rel_c, valid_c), (rel_n, valid_n)):
            dst.append((rel >= 0) & ((rel & (dil - 1)) == 0) & ((rel >> shift) <= span))
    zpad = jnp.zeros((LANE - rows, MIX), F32)
    ys = []
    for s in range(seqs):
        rs = slice(s * rows, (s + 1) * rows)
        q = q_ref[rs, :]
        qm = jnp.concatenate([jnp.where(head == h, q, 0.0) for h in range(N_HEADS)], axis=0).astype(BF16)
        ck = ck_ref[0, s].astype(BF16)
        cv = cv_ref[0, s].astype(BF16)
        kn = jnp.concatenate([k_ref[rs, :], zpad], axis=0).astype(BF16)
        vn = jnp.concatenate([v_ref[rs, :], zpad], axis=0).astype(BF16)
        s_c = _dot(qm, ck) * scale
        s_n = _dot_nt(qm, kn) * scale
        pcs, pns, ls, lses = [], [], [], []
        for ci in range(len(DILATED_CONFIGS)):
            sc = jnp.where(valid_c[ci], s_c, NEG_INF)
            sn = jnp.where(valid_n[ci], s_n, NEG_INF)
            m = jnp.maximum(jnp.max(sc, axis=-1, keepdims=True), jnp.max(sn, axis=-1, keepdims=True))
            pc = jnp.exp(sc - m)
            pn = jnp.exp(sn - m)
            l = jnp.sum(pc, axis=-1, keepdims=True) + jnp.sum(pn, axis=-1, keepdims=True)
            pcs.append(pc.astype(BF16))
            pns.append(pn.astype(BF16))
            ls.append(l)
            lses.append(m + jnp.log(l))
        o_all = _dot_nt(jnp.concatenate(pcs, axis=0), cv) + _dot(jnp.concatenate(pns, axis=0), vn)
        outs = [o_all[ci * hr:(ci + 1) * hr, :] / ls[ci] for ci in range(len(DILATED_CONFIGS))]
        mx = jnp.maximum(jnp.maximum(lses[0], lses[1]), lses[2])
        es = [jnp.exp(le - mx) for le in lses]
        y_all = (es[0] * outs[0] + es[1] * outs[1] + es[2] * outs[2]) / (es[0] + es[1] + es[2])
        y = jnp.zeros((rows, MIX), F32)
        for h in range(N_HEADS):
            y = jnp.where(head == h, y_all[h * rows:(h + 1) * rows, :], y)
        ys.append(y)
    y_ref[...] = jnp.concatenate(ys, axis=0).astype(y_ref.dtype)


def _swa_sample(h, mix, cache_k, cache_v, layer, *, rows, n_seq, seqs, row0):
    past = cache_k.shape[3]
    blk = seqs * rows
    rb0 = row0 // blk

    def hspec(col):
        return pl.BlockSpec((blk, MIX), lambda b: (rb0 + b, col))

    cspec = pl.BlockSpec((1, seqs, MIX, past), lambda b: (layer, b, 0, 0))
    nbytes = 2 * (2 * seqs * past * MIX * 4 + 4 * blk * MIX * 4) + 16 * N_HEADS * rows * past * 4 + 2 * past * MIX * 2
    return pl.pallas_call(
        functools.partial(_swa_sample_body, past=past, rows=rows, seqs=seqs),
        grid=(n_seq // seqs,),
        in_specs=[hspec(COL_SQ), hspec(COL_SK), hspec(COL_SV), cspec, cspec,
                  pl.BlockSpec(memory_space=pl.ANY)],
        out_specs=pl.BlockSpec((blk, MIX), lambda b: (rb0 + b, 1)),
        out_shape=jax.ShapeDtypeStruct(mix.shape, mix.dtype),
        input_output_aliases={5: 0},
        compiler_params=_params(("arbitrary",), nbytes),
        name="swa_sample",
    )(h, h, h, cache_k, cache_v, mix)


def _conv_rows(ctx, base, sub, w_ref, cb_ref, g_ref, b_ref):
    lead = CONV_CTX - (CONV_WIDTH - 1)
    window = ctx[base:base + sub + CONV_CTX, :]
    acc = jnp.zeros((sub, MIX), F32)
    for r in range(SUBLANE):
        shifted = pltpu.roll(window, sub + CONV_CTX - (lead + r), 0)
        for j in range(r, CONV_WIDTH, SUBLANE):
            acc = acc + shifted[j - r:j - r + sub, :] * w_ref[j:j + 1, :]
    z = _layer_norm(acc + cb_ref[...], g_ref[...], b_ref[...])
    return z * _sigmoid(z)


def _gelu_tanh(x):
    c = math.sqrt(2.0 / math.pi)
    return x * (0.5 * (1.0 + jnp.tanh(c * (x + 0.044715 * (x * x * x)))))


def _sgu_weights(ws_ref):
    tri = (lax.broadcasted_iota(I32, (SGU_CHUNK, SGU_CHUNK), 0)
           >= lax.broadcasted_iota(I32, (SGU_CHUNK, SGU_CHUNK), 1))
    return [jnp.where(tri, ws_ref[gi], 0.0).astype(BF16) for gi in range(N_HEADS)]


def _sgu_chunk(u_raw, v_raw, wms, tabs):
    g_ref, b_ref, bias_ref, avg_ref = tabs
    sub = u_raw.shape[0]
    head = _lane_head(MIX)
    u = _gelu_tanh(u_raw)
    vn = _group_norm(_gelu_tanh(v_raw), avg_ref[...]) * g_ref[...] + b_ref[...]
    vp = vn if sub == SGU_CHUNK else jnp.concatenate([vn, jnp.zeros((SGU_CHUNK - sub, MIX), F32)], axis=0)
    vnb = vp.astype(BF16)
    z = jnp.zeros((SGU_CHUNK, MIX), F32)
    for gi in range(N_HEADS):
        z = jnp.where(head == gi, _dot(wms[gi], vnb), z)
    z = z + bias_ref[...]
    return u * z[0:sub, :], vn


def _mixers_body(q_ref, k_ref, v_ref, g_ref, cv_ref, cg_ref, gu_ref, gv_ref,
                 dec_ref, qdec_ref, kdec_ref, cmat_ref, bdm_ref, avg_ref, gn_ref,
                 cw_ref, cb_ref, clg_ref, clb_ref, sg_ref, sb_ref, ws_ref, bias_ref, mix_in,
                 mix_ref, glu_ref, sout_ref, s_scr, ctx, *, n_steps, rows):
    del mix_in
    c = pl.program_id(1)

    @pl.when(c == 0)
    def _():
        s_scr[...] = jnp.zeros_like(s_scr)
        ctx[0:CONV_CTX, :] = jnp.zeros((CONV_CTX, MIX), F32)

    ret_tabs = (dec_ref, qdec_ref, kdec_ref, cmat_ref, bdm_ref, avg_ref, gn_ref)
    sgu_tabs = (sg_ref, sb_ref, bias_ref, avg_ref)
    wms = _sgu_weights(ws_ref)
    glu = cv_ref[...] * _sigmoid(cg_ref[...])
    glu_ref[...] = glu
    ctx[CONV_CTX:CONV_CTX + rows, :] = glu
    state = s_scr[...]
    for cc in range(rows // RET_CHUNK):
        rs = slice(cc * RET_CHUNK, (cc + 1) * RET_CHUNK)
        y, state = _retention_chunk(q_ref[rs, :], k_ref[rs, :], v_ref[rs, :], g_ref[rs, :], state, ret_tabs)
        out = mix_ref.dtype
        mix_ref[rs, 0:MIX] = y.astype(out)
        mix_ref[rs, 2 * MIX:3 * MIX] = _conv_rows(ctx, cc * RET_CHUNK, RET_CHUNK, cw_ref, cb_ref, clg_ref, clb_ref).astype(out)
        mix_ref[rs, 3 * MIX:4 * MIX] = _sgu_chunk(gu_ref[rs, :], gv_ref[rs, :], wms, sgu_tabs)[0].astype(out)
    mix_ref[:, MIX:2 * MIX] = jnp.zeros((rows, MIX), mix_ref.dtype)
    s_scr[...] = state
    ctx[0:CONV_CTX, :] = ctx[rows:rows + CONV_CTX, :]

    @pl.when(c == n_steps - 1)
    def _():
        sout_ref[0] = s_scr[...]


def _mixers_prompt(h, mix, ret_tabs, avg, gn_g, conv_args, sgu_args, *, seq, n_seq):
    assert RET_CHUNK == SGU_CHUNK and seq % MIX_ROWS == 0
    rows = MIX_ROWS
    n_steps = seq // rows
    dec, qdec, kdec, cmat, bdm = ret_tabs
    cw, cb, clg, clb = conv_args
    sg, sb, ws, bias = sgu_args

    def hspec(col):
        return pl.BlockSpec((rows, MIX), lambda b, c: (b * n_steps + c, col))

    def const(arr):
        return pl.BlockSpec(arr.shape, lambda b, c: (0,) * arr.ndim)

    tables = [dec, qdec, kdec, cmat, bdm, avg, gn_g, cw, cb, clg, clb, sg, sb, ws, bias]
    cols = [COL_RQ, COL_RK, COL_RV, COL_RG, COL_CV, COL_CG, COL_GU, COL_GV]
    nbytes = (2 * (len(cols) + 5) * rows * MIX * 4 + 2 * sum(t.size * 4 for t in tables)
              + (CONV_CTX + rows) * MIX * 4 + MIX * MIX * 4)
    return pl.pallas_call(
        functools.partial(_mixers_body, n_steps=n_steps, rows=rows),
        grid=(n_seq, n_steps),
        in_specs=[hspec(col) for col in cols] + [const(t) for t in tables] + [pl.BlockSpec(memory_space=pl.ANY)],
        out_specs=[pl.BlockSpec((rows, 4 * MIX), lambda b, c: (b * n_steps + c, 0)),
                   pl.BlockSpec((rows, MIX), lambda b, c: (b * n_steps + c, 0)),
                   pl.BlockSpec((1, MIX, MIX), lambda b, c: (b, 0, 0))],
        out_shape=[jax.ShapeDtypeStruct(mix.shape, mix.dtype),
                   jax.ShapeDtypeStruct((n_seq * seq, MIX), F32),
                   jax.ShapeDtypeStruct((n_seq, MIX, MIX), F32)],
        scratch_shapes=[pltpu.VMEM((MIX, MIX), F32), pltpu.VMEM((CONV_CTX + rows, MIX), F32)],
        input_output_aliases={len(cols) + len(tables): 0},
        compiler_params=_params(("arbitrary", "arbitrary"), nbytes),
        name="mixers_prompt",
    )(*([h] * len(cols)), *tables, mix)


def _mixers_sample_body(q_ref, k_ref, v_ref, g_ref, cv_ref, cg_ref, gu_ref, gv_ref, s0_ref, buf_ref,
                        dec_ref, qdec_ref, kdec_ref, cmat_ref, bdm_ref, avg_ref, gn_ref,
                        cw_ref, cb_ref, clg_ref, clb_ref, sg_ref, sb_ref, ws_ref, bias_ref, mix_in,
                        mix_ref, glu_ref, vn_ref, sout_ref, ctx, *, seqs, rows):
    del mix_in
    ret_tabs = (dec_ref, qdec_ref, kdec_ref, cmat_ref, bdm_ref, avg_ref, gn_ref)
    sgu_tabs = (sg_ref, sb_ref, bias_ref, avg_ref)
    wms = _sgu_weights(ws_ref)
    glu = cv_ref[...] * _sigmoid(cg_ref[...])
    glu_ref[...] = glu
    ret, conv, gate = [], [], []
    for s in range(seqs):
        rs = slice(s * rows, (s + 1) * rows)
        y, sout_ref[s] = _retention_chunk(q_ref[rs, :], k_ref[rs, :], v_ref[rs, :], g_ref[rs, :], s0_ref[s], ret_tabs)
        ret.append(y)
        ctx[s, 0:CONV_CTX, :] = buf_ref[s]
        ctx[s, CONV_CTX:CONV_CTX + rows, :] = glu[rs, :]
        conv.append(_conv_rows(ctx.at[s], 0, rows, cw_ref, cb_ref, clg_ref, clb_ref))
        y, vn_ref[rs, :] = _sgu_chunk(gu_ref[rs, :], gv_ref[rs, :], wms, sgu_tabs)
        gate.append(y)
    blocks = (ret, [jnp.zeros((rows, MIX), F32)] * seqs, conv, gate)
    for col, parts in enumerate(blocks):
        mix_ref[:, col * MIX:(col + 1) * MIX] = jnp.concatenate(parts, axis=0).astype(mix_ref.dtype)


def _mixers_sample(h, mix, s0, buf, ret_tabs, avg, gn_g, conv_args, sgu_args, *, rows, n_seq, seqs, row0):
    blk = seqs * rows
    rb0 = row0 // blk
    dec, qdec, kdec, cmat, bdm = ret_tabs
    cw, cb, clg, clb = conv_args
    sg, sb, ws, bias = sgu_args

    def hspec(col):
        return pl.BlockSpec((blk, MIX), lambda b: (rb0 + b, col))

    def const(arr):
        return pl.BlockSpec(arr.shape, lambda b: (0,) * arr.ndim)

    per_seq = lambda arr: pl.BlockSpec((seqs,) + arr.shape[1:], lambda b: (b,) + (0,) * (arr.ndim - 1))
    tables = [dec, qdec, kdec, cmat, bdm, avg, gn_g, cw, cb, clg, clb, sg, sb, ws, bias]
    cols = [COL_RQ, COL_RK, COL_RV, COL_RG, COL_CV, COL_CG, COL_GU, COL_GV]
    rows_out = lambda width: pl.BlockSpec((blk, width), lambda b: (b, 0))
    nbytes = (2 * (len(cols) + 6) * blk * MIX * 4 + 2 * sum(t.size * 4 for t in tables)
              + 4 * seqs * MIX * MIX * 4 + 3 * seqs * (CONV_CTX + rows) * MIX * 4)
    return pl.pallas_call(
        functools.partial(_mixers_sample_body, seqs=seqs, rows=rows),
        grid=(n_seq // seqs,),
        in_specs=([hspec(col) for col in cols] + [per_seq(s0), per_seq(buf)] + [const(t) for t in tables]
                  + [pl.BlockSpec(memory_space=pl.ANY)]),
        out_specs=[pl.BlockSpec((blk, 4 * MIX), lambda b: (rb0 + b, 0)), rows_out(MIX), rows_out(MIX), per_seq(s0)],
        out_shape=[jax.ShapeDtypeStruct(mix.shape, mix.dtype),
                   jax.ShapeDtypeStruct((n_seq * rows, MIX), F32),
                   jax.ShapeDtypeStruct((n_seq * rows, MIX), F32),
                   jax.ShapeDtypeStruct(s0.shape, F32)],
        scratch_shapes=[pltpu.VMEM((seqs, CONV_CTX + rows, MIX), F32)],
        input_output_aliases={len(cols) + 2 + len(tables): 0},
        compiler_params=_params(("arbitrary",), nbytes),
        name="mixers_sample",
    )(*([h] * len(cols)), s0, buf, *tables, mix)


def _out_router_body(mix_ref, w_ref, x_ref, g_ref, b_ref, wr_ref, br_ref,
                     x1_ref, route_ref, counts_ref, wbf, wrbf, logit_scr, *, alpha):
    @pl.when(pl.program_id(0) == 0)
    def _():
        wbf[...] = w_ref[0].astype(BF16)
        wrbf[...] = wr_ref[...].astype(BF16)

    parts = [slice(r0, r0 + OUT_ROWS) for r0 in range(0, TM, OUT_ROWS)]
    ys = [_dot(mix_ref[rs, :], wbf[...]) for rs in parts]
    for rs, y in zip(parts, ys):
        x1 = _layer_norm(alpha * x_ref[rs, :] + y, g_ref[...], b_ref[...])
        x1_ref[rs, :] = x1
        logit_scr[:, rs] = _dot_nt(wrbf[...], x1.astype(BF16)) + br_ref[...]
    logits = logit_scr[0:ROUTE_ROWS, :]
    row = lax.broadcasted_iota(I32, (ROUTE_ROWS, 1), 0).astype(F32)

    def first_max(mask, vals):
        masked = jnp.where(mask, vals, NEG_INF)
        top = jnp.max(masked, axis=0, keepdims=True)
        idx = jnp.min(jnp.where(mask & (masked == top), row, float(LANE)), axis=0, keepdims=True)
        return top, idx

    gmask = row < N_GROUPS
    gmax, gsel = first_max(gmask, logits)
    g_gate = 1.0 / jnp.sum(jnp.where(gmask, jnp.exp(logits - gmax), 0.0), axis=0, keepdims=True)
    lo = N_GROUPS + EXPERTS_PER_GROUP * gsel
    emask = (row >= lo) & (row < lo + EXPERTS_PER_GROUP)
    v1, i1 = first_max(emask, logits)
    v2, i2 = first_max(emask & (row != i1), logits)
    e12 = jnp.exp(v2 - v1)
    both = (row == i1).astype(F32) + (row == i2).astype(F32)

    earlier = (lax.broadcasted_iota(I32, (TD, TD), 0) < lax.broadcasted_iota(I32, (TD, TD), 1)).astype(BF16)
    rank_a, rank_b = [], []
    for td in range(TM // TD):
        cs = slice(td * TD, (td + 1) * TD)
        tile = both[:, cs]
        counts_ref[td] = jnp.sum(tile, axis=1, keepdims=True)
        before = _dot(tile.astype(BF16), earlier)
        rank_a.append(jnp.sum(jnp.where(row == i1[:, cs], before, 0.0), axis=0, keepdims=True))
        rank_b.append(jnp.sum(jnp.where(row == i2[:, cs], before, 0.0), axis=0, keepdims=True))
    values = (i1 - N_GROUPS, i2 - N_GROUPS, g_gate * (1.0 / (1.0 + e12)), g_gate * (e12 / (1.0 + e12)),
              jnp.concatenate(rank_a, axis=1), jnp.concatenate(rank_b, axis=1))
    out_row = lax.broadcasted_iota(I32, (SUBLANE, 1), 0)
    route = jnp.zeros((SUBLANE, TM), F32)
    for k, val in enumerate(values):
        route = jnp.where(out_row == k, val, route)
    route_ref[...] = route


def _out_router(mix, w_out, layer, x, g, b, wr, br, *, alpha):
    nt, d = x.shape
    kdim = mix.shape[1]
    const = lambda shape: pl.BlockSpec(shape, lambda i: (0, 0))
    row = lambda width: pl.BlockSpec((TM, width), lambda i: (i, 0))
    nbytes = 2 * (kdim * d * 4 + 2 * d * LANE * 4 + TM * kdim * 4 + 2 * TM * d * 4 + TM * LANE * 4) + kdim * d * 2
    return pl.pallas_call(
        functools.partial(_out_router_body, alpha=alpha),
        grid=(nt // TM,),
        in_specs=[row(kdim), pl.BlockSpec((1, kdim, d), lambda i: (layer, 0, 0)), row(d),
                  const((1, d)), const((1, d)), const((LANE, d)), const((LANE, 1))],
        out_specs=[row(d), pl.BlockSpec((SUBLANE, TM), lambda i: (0, i)),
                   pl.BlockSpec((TM // TD, ROUTE_ROWS, 1), lambda i: (i, 0, 0))],
        out_shape=[jax.ShapeDtypeStruct((nt, d), F32), jax.ShapeDtypeStruct((SUBLANE, nt), F32),
                   jax.ShapeDtypeStruct((nt // TD, ROUTE_ROWS, 1), F32)],
        scratch_shapes=[pltpu.VMEM((kdim, d), BF16), pltpu.VMEM((LANE, d), BF16), pltpu.VMEM((LANE, TM), F32)],
        compiler_params=_params(("arbitrary",), nbytes),
        name="out_router",
    )(mix, w_out, x, g, b, wr, br)


def _local_slots(route, off_col):
    expert = lax.broadcasted_iota(I32, (N_EXPERTS, 1), 0).astype(F32)
    slots = []
    for kk in range(TOP_K):
        start = jnp.sum(jnp.where(expert == route[kk:kk + 1, :], off_col, 0.0), axis=0, keepdims=True)
        slots.append(start + route[4 + kk:5 + kk, :])
    return slots


def _pack_rows(y):
    half = y.shape[1] // 2
    bits = lax.bitcast_convert_type(y.astype(BF16).astype(F32), U32)
    return (bits[:, :half] & jnp.uint32(0xFFFF0000)) | (bits[:, half:] >> 16)


def _unpack_rows(w):
    hi = lax.bitcast_convert_type(w & jnp.uint32(0xFFFF0000), F32)
    lo = lax.bitcast_convert_type(w << 16, F32)
    return jnp.concatenate([hi, lo], axis=1)


def _chunk_copy(hbm, hbm_row, loc, chunk, sem, *, to_hbm):
    vm = loc.at[pl.ds(pl.multiple_of(chunk * MOE_CHUNK, MOE_CHUNK), MOE_CHUNK), :]
    hb = hbm.at[pl.ds(pl.multiple_of(hbm_row, MOE_CHUNK), MOE_CHUNK), :]
    return pltpu.make_async_copy(vm, hb, sem) if to_hbm else pltpu.make_async_copy(hb, vm, sem)


def _for_each_chunk(count, fn):
    def group(t, carry):
        for u in range(CHUNK_UNROLL):
            fn(t * CHUNK_UNROLL + u)
        return carry

    def single(c, carry):
        fn(c)
        return carry

    whole = count // CHUNK_UNROLL
    lax.fori_loop(0, whole, group, 0)
    lax.fori_loop(whole * CHUNK_UNROLL, count, single, 0)


def _wait_chunks(hbm, buf, sem, count, *, to_hbm):
    _for_each_chunk(count, lambda c: _chunk_copy(hbm, 0, buf, 0, sem, to_hbm=to_hbm).wait())


def _dispatch_body(dst_tab, n_chunks, x_ref, route_ref, off_ref, xs_in, xs_hbm, loc, sem):
    del xs_in
    i = pl.program_id(0)
    last = pl.num_programs(0) - 1
    cur = i % 2
    slot_a, slot_b = _local_slots(route_ref[...], off_ref[0])
    buf_row = lax.broadcasted_iota(I32, (LOC_ROWS, 1), 0).astype(F32)
    onehot = ((buf_row == slot_a) | (buf_row == slot_b)).astype(BF16)
    loc[cur] = _pack_rows(_dot(onehot, x_ref[...].astype(BF16)))

    _for_each_chunk(n_chunks[i], lambda c: _chunk_copy(
        xs_hbm, dst_tab[i * MAX_CHUNKS + c], loc.at[cur], c, sem.at[cur], to_hbm=True).start())

    @pl.when(i > 0)
    def _():
        _wait_chunks(xs_hbm, loc.at[1 - cur], sem.at[1 - cur], n_chunks[jnp.maximum(i - 1, 0)], to_hbm=True)

    @pl.when(i == last)
    def _():
        _wait_chunks(xs_hbm, loc.at[cur], sem.at[cur], n_chunks[i], to_hbm=True)


def _dispatch(dst_tab, n_chunks, x1, route, off, xs_prev):
    nt, d = x1.shape
    row = lambda width: pl.BlockSpec((TD, width), lambda i, *_: (i, 0))
    nbytes = 2 * (TD * d * 4 + TD * LANE * 4) + 2 * LOC_ROWS * d * 4 + 4 * TD * LOC_ROWS * 4 + LOC_ROWS * d * 4
    return pl.pallas_call(
        _dispatch_body,
        grid_spec=pltpu.PrefetchScalarGridSpec(
            num_scalar_prefetch=2,
            grid=(nt // TD,),
            in_specs=[row(d), pl.BlockSpec((SUBLANE, TD), lambda i, *_: (0, i)),
                      pl.BlockSpec((1, N_EXPERTS, 1), lambda i, *_: (i, 0, 0)),
                      pl.BlockSpec(memory_space=pl.ANY)],
            out_specs=pl.BlockSpec(memory_space=pl.ANY),
            scratch_shapes=[pltpu.VMEM((2, LOC_ROWS, d // 2), U32), pltpu.SemaphoreType.DMA((2,))]),
        out_shape=jax.ShapeDtypeStruct(xs_prev.shape, U32),
        input_output_aliases={5: 0},
        compiler_params=_params(("arbitrary",), nbytes),
        name="moe_dispatch",
    )(dst_tab, n_chunks, x1, route, off, xs_prev)


def _ffn_body(blk_e, blk_valid, blk_src, blk_slot, blk_next, x_ref, w1_hbm, w3_hbm, w2_hbm, y_ref,
              w1f, w3f, w2f, w1b, w3b, w2b, sem, *, layer):
    del blk_src
    j = pl.program_id(0)
    n_valid = blk_valid[j]
    half = MOE_BLK // 2
    slot = blk_slot[j]

    def weight_copies(expert, buf):
        return [pltpu.make_async_copy(hbm.at[layer, expert], vm.at[buf], sem.at[buf])
                for hbm, vm in ((w1_hbm, w1f), (w3_hbm, w3f), (w2_hbm, w2f))]

    @pl.when(j == 0)
    def _():
        for cp in weight_copies(blk_e[0], slot):
            cp.start()

    @pl.when((n_valid > 0) & ((j == 0) | (blk_e[j] != blk_e[jnp.maximum(j - 1, 0)])))
    def _():
        for cp in weight_copies(blk_e[j], slot):
            cp.wait()
        w1b[...] = w1f[slot].astype(BF16)
        w3b[...] = w3f[slot].astype(BF16)
        w2b[...] = w2f[slot].astype(BF16)

        @pl.when(blk_next[j] >= 0)
        def _():
            for cp in weight_copies(jnp.maximum(blk_next[j], 0), 1 - slot):
                cp.start()

    def ffn(rows):
        xb = _unpack_rows(x_ref[rows, :]).astype(BF16)
        h1 = _dot(xb, w1b[...])
        h3 = _dot(xb, w3b[...])
        act = (h1 * _sigmoid(h1)) * h3
        y_ref[rows, :] = _pack_rows(_dot(act.astype(BF16), w2b[...]))

    @pl.when(n_valid > half)
    def _():
        ffn(slice(0, MOE_BLK))

    @pl.when((n_valid > 0) & (n_valid <= half))
    def _():
        ffn(slice(0, half))
        y_ref[half:MOE_BLK, :] = jnp.zeros((half, y_ref.shape[1]), U32)


def _ffn(xs, w1, w3, w2, layer, blk_e, blk_valid, blk_src, blk_slot, blk_next):
    d = w1.shape[2]
    de = w1.shape[3]
    n_blk = xs.shape[0] // MOE_BLK
    rows = pl.BlockSpec((MOE_BLK, d // 2), lambda j, be, bv, bs, sl, nx: (bs[j], 0))
    nbytes = 2 * (3 * d * de * 4 + MOE_BLK * d * 4) + 3 * d * de * 2 + 6 * MOE_BLK * de * 4 + 4 * MOE_BLK * d * 4
    anywhere = pl.BlockSpec(memory_space=pl.ANY)
    return pl.pallas_call(
        functools.partial(_ffn_body, layer=layer),
        grid_spec=pltpu.PrefetchScalarGridSpec(
            num_scalar_prefetch=5,
            grid=(n_blk,),
            in_specs=[rows, anywhere, anywhere, anywhere],
            out_specs=rows,
            scratch_shapes=[pltpu.VMEM((2, d, de), F32), pltpu.VMEM((2, d, de), F32), pltpu.VMEM((2, de, d), F32),
                            pltpu.VMEM((d, de), BF16), pltpu.VMEM((d, de), BF16), pltpu.VMEM((de, d), BF16),
                            pltpu.SemaphoreType.DMA((2,))]),
        out_shape=jax.ShapeDtypeStruct(xs.shape, U32),
        input_output_aliases={5: 0},
        compiler_params=_params(("arbitrary",), nbytes),
        name="expert_ffn",
    )(blk_e, blk_valid, blk_src, blk_slot, blk_next, xs, w1, w3, w2)


def _combine_body(dst_tab, n_chunks, x_ref, route_ref, off_ref, g_ref, b_ref, ys_hbm, x2_ref,
                  *rest, alpha, head_tiles):
    *tail_ref, loc, sem = rest
    i = pl.program_id(0)
    last = pl.num_programs(0) - 1
    cur = i % 2

    def fetch(tile, buf):
        _for_each_chunk(n_chunks[tile], lambda c: _chunk_copy(
            ys_hbm, dst_tab[tile * MAX_CHUNKS + c], loc.at[buf], c, sem.at[buf], to_hbm=False).start())

    @pl.when(i == 0)
    def _():
        loc[...] = jnp.zeros_like(loc)
        fetch(0, 0)

    @pl.when(i < last)
    def _():
        fetch(jnp.minimum(i + 1, last), 1 - cur)

    route = route_ref[...]
    slot_a, slot_b = _local_slots(route, off_ref[0])
    buf_row = lax.broadcasted_iota(I32, (LOC_ROWS, 1), 0).astype(F32)
    gates = (jnp.where(buf_row == slot_a, route[2:3, :], 0.0)
             + jnp.where(buf_row == slot_b, route[3:4, :], 0.0))
    _wait_chunks(ys_hbm, loc.at[cur], sem.at[cur], n_chunks[i], to_hbm=False)
    f = _dot_tn(gates.astype(BF16), _unpack_rows(loc[cur]).astype(BF16))
    x2 = _layer_norm(alpha * x_ref[...] + f, g_ref[...], b_ref[...])
    if tail_ref:
        @pl.when(i < head_tiles)
        def _():
            x2_ref[...] = x2

        @pl.when(i >= head_tiles)
        def _():
            tail_ref[0][...] = x2
    else:
        x2_ref[...] = x2


def _combine(dst_tab, n_chunks, x1, route, off, g, b, ys, *, alpha, head_rows=None):
    nt, d = x1.shape
    row = lambda width: pl.BlockSpec((TD, width), lambda i, dt, nc: (i, 0))
    const = lambda shape: pl.BlockSpec(shape, lambda i, dt, nc: (0, 0))
    nbytes = 2 * (3 * TD * d * 4 + TD * LANE * 4) + 2 * LOC_ROWS * d * 4 + 4 * TD * LOC_ROWS * 4 + LOC_ROWS * d * 2
    if head_rows is None:
        head_tiles = nt // TD
        out_specs = row(d)
        out_shape = jax.ShapeDtypeStruct((nt, d), F32)
    else:
        head_tiles = head_rows // TD
        out_specs = [pl.BlockSpec((TD, d), lambda i, dt, nc: (jnp.minimum(i, head_tiles - 1), 0)),
                     pl.BlockSpec((TD, d), lambda i, dt, nc: (jnp.maximum(i - head_tiles, 0), 0))]
        out_shape = [jax.ShapeDtypeStruct((head_rows, d), F32), jax.ShapeDtypeStruct((nt - head_rows, d), F32)]
    scratch = [pltpu.VMEM((2, LOC_ROWS, d // 2), U32), pltpu.SemaphoreType.DMA((2,))]
    return pl.pallas_call(
        functools.partial(_combine_body, alpha=alpha, head_tiles=head_tiles),
        grid_spec=pltpu.PrefetchScalarGridSpec(
            num_scalar_prefetch=2,
            grid=(nt // TD,),
            in_specs=[row(d), pl.BlockSpec((SUBLANE, TD), lambda i, dt, nc: (0, i)),
                      pl.BlockSpec((1, N_EXPERTS, 1), lambda i, dt, nc: (i, 0, 0)),
                      const((1, d)), const((1, d)), pl.BlockSpec(memory_space=pl.ANY)],
            out_specs=out_specs,
            scratch_shapes=scratch),
        out_shape=out_shape,
        compiler_params=_params(("arbitrary",), nbytes),
        name="combine_ln",
    )(dst_tab, n_chunks, x1, route, off, g, b, ys)


def _rope_tables(pos, rot_dim, theta):
    half = rot_dim // 2
    inv = jnp.float32(theta) ** (-jnp.arange(half, dtype=F32) * (2.0 / rot_dim))
    ang = pos.astype(F32)[:, None] * inv[None, :]
    c, s = jnp.cos(ang), jnp.sin(ang)
    rest = HEAD_DIM - rot_dim
    ones = jnp.ones((pos.shape[0], rest), F32)
    zeros = jnp.zeros((pos.shape[0], rest), F32)
    cos_h = jnp.concatenate([c, c, ones], axis=1)
    sin_h = jnp.concatenate([-s, s, zeros], axis=1)
    return jnp.tile(cos_h, (1, N_HEADS)), jnp.tile(sin_h, (1, N_HEADS))


def _block_diag(s):
    rows = [jnp.pad(s[:, h], ((0, 0), (0, 0), (h * HEAD_DIM, MIX - (h + 1) * HEAD_DIM))) for h in range(N_HEADS)]
    return jnp.concatenate(rows, axis=1)


def _diag_blocks(s):
    return jnp.stack([s[:, h * HEAD_DIM:(h + 1) * HEAD_DIM, h * HEAD_DIM:(h + 1) * HEAD_DIM]
                      for h in range(N_HEADS)], axis=1)


def _routing_plan(tile_counts, n_blk):
    cnt_t = tile_counts[:, N_GROUPS:N_GROUPS + N_EXPERTS, 0].astype(I32)
    run_chunks = (cnt_t + MOE_CHUNK - 1) // MOE_CHUNK
    run_rows = run_chunks * MOE_CHUNK
    cnt = jnp.sum(run_rows, axis=0)
    pcnt = (cnt + MOE_BLK - 1) // MOE_BLK * MOE_BLK
    pstart = jnp.cumsum(pcnt) - pcnt
    n_act = jnp.sum(pcnt) // MOE_BLK
    gstart = pstart[None, :] + jnp.cumsum(run_rows, axis=0) - run_rows
    chunk0 = jnp.cumsum(run_chunks, axis=1) - run_chunks
    n_chunks = jnp.sum(run_chunks, axis=1).astype(I32)

    def spread(first, count, value, n):
        p = jnp.arange(n, dtype=I32)[:, None]
        f, c, v = first[..., None, :], count[..., None, :], value[..., None, :]
        return jnp.sum(jnp.where((f <= p) & (p < f + c), v + (p - f) * MOE_CHUNK, 0), axis=-1)

    dst_tab = spread(chunk0, run_chunks, gstart, MAX_CHUNKS)
    off = (chunk0 * MOE_CHUNK).astype(F32)[:, :, None]
    pos = jnp.arange(n_blk, dtype=I32)[:, None] * MOE_BLK
    mine = (pstart[None, :] <= pos) & (pos < (pstart + pcnt)[None, :])
    experts = jnp.arange(N_EXPERTS, dtype=I32)[None, :]
    last_e = jnp.max(jnp.where(cnt > 0, experts[0], 0))
    active = pos[:, 0] < n_act * MOE_BLK
    blk_e = jnp.where(active, jnp.sum(jnp.where(mine, experts, 0), axis=-1), last_e)
    blk_valid = jnp.sum(jnp.where(mine, jnp.clip((pstart + cnt)[None, :] - pos, 0, MOE_BLK), 0), axis=-1)
    blk_src = jnp.minimum(jnp.arange(n_blk, dtype=I32), jnp.maximum(n_act - 1, 0))
    i32 = lambda t: t.astype(I32)
    used = cnt > 0
    slot_e = (jnp.cumsum(used.astype(I32)) - 1) % 2
    later = (experts > experts.T) & used[None, :]
    next_e = jnp.min(jnp.where(later, experts, N_EXPERTS), axis=-1)
    next_e = jnp.where(next_e < N_EXPERTS, next_e, -1)
    return (i32(dst_tab).reshape(-1), n_chunks, off, i32(blk_e), i32(blk_valid), i32(blk_src),
            i32(slot_e[blk_e]), i32(next_e[blk_e]))


def kernel(x_prompt, x_sample, state_ret, cache_swa_k, cache_swa_v, state_conv, w_in, w_out, ret_gn_g,
           conv_w, conv_b, conv_ln_g, conv_ln_b, sgu_ln_g, sgu_ln_b, sgu_w, sgu_b, ln1_g, ln1_b,
           ln2_g, ln2_b, router_g_w, router_g_b, router_e_w, router_e_b, moe_w1, moe_w3, moe_w2):
    depth = w_in.shape[0]
    n_seq, seq, d = x_prompt.shape
    n_dec, t_new, _ = x_sample.shape
    past = cache_swa_k.shape[2]
    rows_s = -(-t_new // SUBLANE) * SUBLANE
    n_p = n_seq * seq
    n_s = n_dec * rows_s
    n_tail = -(-n_s // TM) * TM
    nt = n_p + n_tail
    assert seq % SWA_SB == 0 and seq % TM == 0 and TM % TD == 0
    assert t_new % RET_CHUNK != 0 and t_new <= SGU_CHUNK and w_in.shape[2] == N_COLBLK * MIX
    assert n_dec % SAMPLE_SEQS == 0 and n_dec % SWA_SAMPLE_SEQS == 0
    assert n_p % (SAMPLE_SEQS * rows_s) == 0
    alpha = (2 * depth) ** 0.25
    keep = min(DILATED_CONFIGS[-1][0], seq)
    n_blk = -(-(nt * TOP_K + (nt // TD) * N_EXPERTS * (MOE_CHUNK - 1) + N_EXPERTS * (MOE_BLK - 1)) // MOE_BLK)
    cache_k = jnp.transpose(cache_swa_k, (0, 1, 3, 4, 2)).reshape(depth, n_dec, MIX, past)
    cache_v = jnp.transpose(cache_swa_v, (0, 1, 3, 4, 2)).reshape(depth, n_dec, MIX, past)

    x = jnp.concatenate([x_prompt.reshape(n_p, d),
                         jnp.pad(x_sample, ((0, 0), (0, rows_s - t_new), (0, 0))).reshape(n_s, d),
                         jnp.zeros((n_tail - n_s, d), F32)], axis=0)
    pos = jnp.concatenate([jnp.arange(seq, dtype=I32),
                           PAST_LEN + jnp.arange(n_tail, dtype=I32) % rows_s])
    rope = _rope_tables(pos, HEAD_DIM, RET_THETA) + _rope_tables(pos, ROPE_DIM, ROPE_THETA)
    ret_tabs_p = _retention_tables(RET_CHUNK, RET_CHUNK)
    ret_tabs_s = _retention_tables(t_new, rows_s)
    lane_head = jnp.arange(MIX) // HEAD_DIM
    avg = ((lane_head[:, None] == lane_head[None, :]).astype(F32) / HEAD_DIM).astype(BF16)
    row2 = lambda v: v.reshape(1, -1)
    mix = jnp.zeros((nt, 4 * MIX), BF16)
    slots = jnp.zeros((n_blk * MOE_BLK, d // 2), U32)

    outs = [[] for _ in range(9)]
    for l in range(depth):
        h = _in_proj(x, w_in, l, rope, n_p // TM, seq // TM)
        gn = row2(ret_gn_g[l])
        cw = jnp.pad(conv_w[l], ((0, CONV_CTX - CONV_WIDTH), (0, 0)))
        cargs = (cw, row2(conv_b[l]), row2(conv_ln_g[l]), row2(conv_ln_b[l]))
        sbias = jnp.repeat(sgu_b[l].T, HEAD_DIM, axis=1)
        sargs = (row2(sgu_ln_g[l]), row2(sgu_ln_b[l]), sgu_w[l], sbias)
        mix, glu_p, s_p = _mixers_prompt(h, mix, ret_tabs_p, avg, gn, cargs, sargs, seq=seq, n_seq=n_seq)
        mix = _swa_prompt(h, mix, seq=seq, n_seq=n_seq)
        ctx_s = jnp.pad(state_conv[l], ((0, 0), (CONV_CTX - (CONV_WIDTH - 1), 0), (0, 0)))
        mix, glu_s, vn_s, s_s = _mixers_sample(h, mix, _block_diag(state_ret[l]), ctx_s, ret_tabs_s, avg, gn, cargs,
                                               sargs, rows=rows_s, n_seq=n_dec, seqs=SAMPLE_SEQS, row0=n_p)
        mix = _swa_sample(h, mix, cache_k, cache_v, l, rows=rows_s, n_seq=n_dec, seqs=SWA_SAMPLE_SEQS, row0=n_p)

        wr = jnp.concatenate([router_g_w[l].T, jnp.transpose(router_e_w[l], (0, 2, 1)).reshape(N_EXPERTS, d),
                              jnp.zeros((LANE - N_GROUPS - N_EXPERTS, d), F32)], axis=0)
        br = jnp.concatenate([router_g_b[l], router_e_b[l].reshape(-1),
                              jnp.zeros((LANE - N_GROUPS - N_EXPERTS,), F32)]).reshape(LANE, 1)
        x1, route, counts = _out_router(mix, w_out, l, x, row2(ln1_g[l]), row2(ln1_b[l]), wr, br, alpha=alpha)
        dst_tab, n_chunks, off, blk_e, blk_valid, blk_src, blk_slot, blk_next = _routing_plan(counts, n_blk)
        slots = _dispatch(dst_tab, n_chunks, x1, route, off, slots)
        slots = _ffn(slots, moe_w1, moe_w3, moe_w2, l, blk_e, blk_valid, blk_src, blk_slot, blk_next)
        x = _combine(dst_tab, n_chunks, x1, route, off, row2(ln2_g[l]), row2(ln2_b[l]), slots, alpha=alpha,
                     head_rows=n_p if l == depth - 1 else None)

        def tail(col, n_keep):
            return jnp.stack([lax.slice(h, ((b + 1) * seq - n_keep, col * MIX), ((b + 1) * seq, (col + 1) * MIX))
                              for b in range(n_seq)])

        def new_rows(col):
            blk = lax.slice(h, (n_p, col * MIX), (n_p + n_s, (col + 1) * MIX))
            return blk.reshape(n_dec, rows_s, MIX)[:, :t_new]

        heads = lambda t: t.reshape(t.shape[0], t.shape[1], N_HEADS, HEAD_DIM)
        outs[0].append(_diag_blocks(s_p))
        outs[1].append(heads(tail(COL_SK, keep)))
        outs[2].append(heads(tail(COL_SV, keep)))
        outs[3].append(glu_p.reshape(n_seq, seq, MIX)[:, seq - (CONV_WIDTH - 1):])
        outs[4].append(_diag_blocks(s_s))
        outs[5].append(heads(new_rows(COL_SK)))
        outs[6].append(heads(new_rows(COL_SV)))
        outs[7].append(glu_s.reshape(n_dec, rows_s, MIX)[:, :t_new])
        outs[8].append(vn_s.reshape(n_dec, rows_s, MIX)[:, :t_new])

    x_head, x_tail = x
    y_prompt = x_head.reshape(n_seq, seq, d)
    y_sample = x_tail[:n_s].reshape(n_dec, rows_s, d)[:, :t_new]
    return (y_prompt, y_sample) + tuple(jnp.stack(o) for o in outs)
```

```python
import functools
import math

import jax
import jax.numpy as jnp
from jax import lax
from jax.experimental import pallas as pl
from jax.experimental.pallas import tpu as pltpu

F32 = jnp.float32
BF16 = jnp.bfloat16
I32 = jnp.int32
U32 = jnp.uint32

HEAD_DIM = 64
N_HEADS = 4
MIX = N_HEADS * HEAD_DIM
(COL_RQ, COL_RK, COL_RV, COL_RG, COL_SQ, COL_SK, COL_SV,
 COL_CV, COL_CG, COL_GU, COL_GV) = range(11)
N_COLBLK = 11
RET_CHUNK = 128
RET_THETA = 10000.0
ROPE_THETA = 500000.0
ROPE_DIM = HEAD_DIM // 4
DILATED_CONFIGS = ((128, 1), (512, 4), (2048, 16))
DIL_BLOCK = 128
CONV_WIDTH = 31
SGU_CHUNK = 128
N_GROUPS = 4
EXPERTS_PER_GROUP = 8
N_EXPERTS = N_GROUPS * EXPERTS_PER_GROUP
TOP_K = 2
PAST_LEN = 8192
LN_EPS = 1e-5
NEG_INF = -1e30

LANE = 128
SUBLANE = 8
VMEM_CAP_BYTES = 60000 * 1024
COMPILER_TEMP_BYTES = 12 * 1024 * 1024

TM = 512
TD = 256
OUT_ROWS = 128
ROUTE_ROWS = -(-(N_GROUPS + N_EXPERTS) // SUBLANE) * SUBLANE
MOE_BLK = 512
MOE_CHUNK = SUBLANE
LOC_ROWS = -(-(TD * TOP_K + N_EXPERTS * (MOE_CHUNK - 1)) // LANE) * LANE
MAX_CHUNKS = LOC_ROWS // MOE_CHUNK
CHUNK_UNROLL = 4
MIX_ROWS = 1024
CONV_CTX = 32
SWA_SB = DIL_BLOCK * 16
SWA_TILE = 256
SWA_UNROLL = 8
SAMPLE_SEQS = 8
SWA_SAMPLE_SEQS = 4


def _vmem(nbytes):
    return int(min(VMEM_CAP_BYTES, nbytes + COMPILER_TEMP_BYTES))


def _params(sem, nbytes):
    return pltpu.CompilerParams(dimension_semantics=sem, vmem_limit_bytes=_vmem(nbytes))


def _dot(a, b):
    return jnp.dot(a, b, preferred_element_type=F32)


def _dot_nt(a, b):
    return lax.dot_general(a, b, (((1,), (1,)), ((), ())), preferred_element_type=F32)


def _dot_tn(a, b):
    return lax.dot_general(a, b, (((0,), (0,)), ((), ())), preferred_element_type=F32)


def _split_dot(x, m):
    hi = x.astype(BF16)
    lo = (x - hi.astype(F32)).astype(BF16)
    return _dot(hi, m) + _dot(lo, m)


def _sigmoid(x):
    return 1.0 / (1.0 + jnp.exp(-x))


def _layer_norm(z, g, b):
    mu = jnp.mean(z, axis=-1, keepdims=True)
    d = z - mu
    var = jnp.mean(d * d, axis=-1, keepdims=True)
    return d * lax.rsqrt(var + LN_EPS) * g + b


def _group_norm(z, avg):
    mu = _split_dot(z, avg)
    d = z - mu
    var = _split_dot(d * d, avg)
    return d * lax.rsqrt(var + LN_EPS)


def _lane_head(width):
    return lax.broadcasted_iota(I32, (1, width), 1) // HEAD_DIM


def _rotate(x, cos, sin_signed, half):
    first = (lax.broadcasted_iota(I32, (1, LANE), 1) % HEAD_DIM) < half
    parts = []
    for p in range(x.shape[1] // LANE):
        t = x[:, p * LANE:(p + 1) * LANE]
        up = pltpu.roll(t, half, 1)
        down = pltpu.roll(t, LANE - half, 1)
        parts.append(jnp.where(first, down, up))
    return x * cos + jnp.concatenate(parts, axis=1) * sin_signed


def _in_proj_body(x_ref, w_ref, cr_ref, sr_ref, cs_ref, ss_ref, h_ref, wbf_ref):
    @pl.when(pl.program_id(0) == 0)
    def _():
        for j in range(N_COLBLK):
            wbf_ref[:, j * MIX:(j + 1) * MIX] = w_ref[0, :, j * MIX:(j + 1) * MIX].astype(BF16)

    xb = x_ref[...].astype(BF16)
    for j in range(N_COLBLK):
        hj = _dot(xb, wbf_ref[:, j * MIX:(j + 1) * MIX])
        if j in (COL_RQ, COL_RK):
            hj = _rotate(hj, cr_ref[...], sr_ref[...], HEAD_DIM // 2)
        if j == COL_RK:
            hj = hj * (HEAD_DIM ** -0.5)
        if j in (COL_SQ, COL_SK):
            hj = _rotate(hj, cs_ref[...], ss_ref[...], ROPE_DIM // 2)
        h_ref[:, j * MIX:(j + 1) * MIX] = hj


def _in_proj(x, w, layer, tabs, n_prompt_tiles, tiles_per_seq):
    nt, d = x.shape
    width = w.shape[2]

    def tab_map(i):
        return (jnp.where(i < n_prompt_tiles, i % tiles_per_seq, tiles_per_seq + i - n_prompt_tiles), 0)

    tab_spec = pl.BlockSpec((TM, MIX), tab_map)
    nbytes = d * width * 4 + d * width * 2 + 2 * TM * d * 4 + 8 * TM * MIX * 4 + 2 * TM * width * 4
    return pl.pallas_call(
        _in_proj_body,
        grid=(nt // TM,),
        in_specs=[pl.BlockSpec((TM, d), lambda i: (i, 0)),
                  pl.BlockSpec((1, d, width), lambda i: (layer, 0, 0), pipeline_mode=pl.Buffered(1)),
                  tab_spec, tab_spec, tab_spec, tab_spec],
        out_specs=pl.BlockSpec((TM, width), lambda i: (i, 0)),
        out_shape=jax.ShapeDtypeStruct((nt, width), F32),
        scratch_shapes=[pltpu.VMEM((d, width), BF16)],
        compiler_params=_params(("arbitrary",), nbytes),
        name="in_proj",
    )(x, w, *tabs)


def _retention_chunk(q, k, v, g, state, tabs):
    dec_ref, qdec_ref, kdec_ref, cmat_ref, bdm_ref, avg_ref, gn_ref = tabs
    head = _lane_head(MIX)
    vb = v.astype(BF16)
    kb = k.astype(BF16)
    o = _dot(q.astype(BF16), state.astype(BF16)) * qdec_ref[...]
    for h in range(N_HEADS):
        mh = head == h
        a = _dot_nt(jnp.where(mh, q, 0.0).astype(BF16), kb) * dec_ref[h]
        o = o + jnp.where(mh, _dot(a.astype(BF16), vb), 0.0)
    upd = _dot_tn((k * kdec_ref[...]).astype(BF16), vb)
    state = state * cmat_ref[...] + upd * bdm_ref[...]
    return _group_norm(o, avg_ref[...]) * gn_ref[...] * (g * _sigmoid(g)), state


def _retention_tables(l_real, l_pad):
    hh = jnp.arange(N_HEADS, dtype=F32)
    log_g = jnp.log1p(-jnp.exp2(-5.0 - hh))
    i = jnp.arange(l_pad, dtype=F32)
    rel = i[:, None] - i[None, :]
    dec = jnp.where(rel >= 0, jnp.exp(log_g[:, None, None] * jnp.maximum(rel, 0.0)), 0.0)
    qd = jnp.exp(log_g[:, None] * (i + 1.0))
    kd = jnp.where(i < l_real, jnp.exp(log_g[:, None] * (l_real - 1.0 - i)), 0.0)
    cd = jnp.exp(log_g * l_real)
    expand = lambda t: jnp.repeat(t.T, HEAD_DIM, axis=1)
    row_head = jnp.arange(MIX) // HEAD_DIM
    bdm = (row_head[:, None] == row_head[None, :]).astype(F32)
    cmat = cd[row_head][:, None] * bdm
    return dec, expand(qd), expand(kd), cmat, bdm


def _swa_prompt_body(q_ref, k_ref, v_ref, mix_ref, y_ref, o0, o1, o2, l0, l1, l2, *, seq):
    del mix_ref
    o_scr = (o0, o1, o2)
    l_scr = (l0, l1, l2)
    head = _lane_head(LANE)
    qi = lax.broadcasted_iota(I32, (DIL_BLOCK, 1), 0)
    ki = lax.broadcasted_iota(I32, (1, 2 * DIL_BLOCK), 1)
    rel = DIL_BLOCK + qi - ki
    scale = HEAD_DIM ** -0.5
    for sb in range(seq // SWA_SB):
        for ci, (window, dil) in enumerate(DILATED_CONFIGS):
            span = window // dil
            band = (rel >= 0) & (rel <= span)

            def block(i, carry, sb=sb, ci=ci, dil=dil, band=band):
                nloc = i // dil
                off = nloc * (DIL_BLOCK * dil) + i % dil
                start = sb * SWA_SB + off
                if sb == 0:
                    first = nloc == 0
                    pstart = jnp.where(first, start, start - DIL_BLOCK * dil)
                    valid = band & (ki >= jnp.where(first, DIL_BLOCK, 0))
                else:
                    pstart = start - DIL_BLOCK * dil
                    valid = band
                stride = dil if dil > 1 else None
                cur = pl.ds(start, DIL_BLOCK, stride=stride)
                prev = pl.ds(pstart, DIL_BLOCK, stride=stride)
                qb = q_ref[cur, :] * scale
                kb = jnp.concatenate([k_ref[prev, :], k_ref[cur, :]], axis=0).astype(BF16)
                vb = jnp.concatenate([v_ref[prev, :], v_ref[cur, :]], axis=0).astype(BF16)
                o = jnp.zeros((DIL_BLOCK, LANE), F32)
                ls = jnp.zeros((DIL_BLOCK, LANE), F32)
                for hh in range(LANE // HEAD_DIM):
                    mh = head == hh
                    s = _dot_nt(jnp.where(mh, qb, 0.0).astype(BF16), kb)
                    s = jnp.where(valid, s, NEG_INF)
                    m = jnp.max(s, axis=-1, keepdims=True)
                    p = jnp.exp(s - m)
                    l = jnp.sum(p, axis=-1, keepdims=True)
                    o = jnp.where(mh, _dot(p.astype(BF16), vb) / l, o)
                    ls = jnp.where(mh, m + jnp.log(l), ls)
                dst = pl.ds(off, DIL_BLOCK, stride=stride)
                o_scr[ci][dst, :] = o
                l_scr[ci][dst, :] = ls
                return carry

            lax.fori_loop(0, SWA_SB // DIL_BLOCK, block, 0, unroll=SWA_UNROLL)

        def combine(t, carry, sb=sb):
            rows = pl.ds(t * SWA_TILE, SWA_TILE)
            la, lb, lc = l_scr[0][rows, :], l_scr[1][rows, :], l_scr[2][rows, :]
            mx = jnp.maximum(jnp.maximum(la, lb), lc)
            ea, eb, ec = jnp.exp(la - mx), jnp.exp(lb - mx), jnp.exp(lc - mx)
            num = ea * o_scr[0][rows, :] + eb * o_scr[1][rows, :] + ec * o_scr[2][rows, :]
            y_ref[pl.ds(sb * SWA_SB + t * SWA_TILE, SWA_TILE), :] = (num / (ea + eb + ec)).astype(y_ref.dtype)
            return carry

        lax.fori_loop(0, SWA_SB // SWA_TILE, combine, 0)


def _swa_prompt(h, mix, *, seq, n_seq):
    halves = MIX // LANE

    def hspec(col):
        return pl.BlockSpec((seq, LANE), lambda b, p: (b, col * halves + p))

    nbytes = 2 * 4 * seq * LANE * 4 + 6 * SWA_SB * LANE * 4
    return pl.pallas_call(
        functools.partial(_swa_prompt_body, seq=seq),
        grid=(n_seq, halves),
        in_specs=[hspec(COL_SQ), hspec(COL_SK), hspec(COL_SV), pl.BlockSpec(memory_space=pl.ANY)],
        out_specs=pl.BlockSpec((seq, LANE), lambda b, p: (b, halves + p)),
        out_shape=jax.ShapeDtypeStruct(mix.shape, mix.dtype),
        scratch_shapes=[pltpu.VMEM((SWA_SB, LANE), F32)] * 6,
        input_output_aliases={3: 0},
        compiler_params=_params(("arbitrary", "arbitrary"), nbytes),
        name="swa_prompt",
    )(h, h, h, mix)


def _swa_sample_body(q_ref, k_ref, v_ref, ck_ref, cv_ref, mix_ref, y_ref, *, past, rows, seqs):
    del mix_ref
    head = _lane_head(MIX)
    hr = N_HEADS * rows
    scale = HEAD_DIM ** -0.5
    t_row = lax.broadcasted_iota(I32, (hr, 1), 0) % rows
    rel_c = past + t_row - lax.broadcasted_iota(I32, (1, past), 1)
    rel_n = t_row - lax.broadcasted_iota(I32, (1, LANE), 1)
    valid_c, valid_n = [], []
    for window, dil in DILATED_CONFIGS:
        span = window // dil
        shift = int(math.log2(dil))
        for rel, dst in ((rel_c, valid_c), (rel_n, valid_n)):
            dst.append((rel >= 0) & ((rel & (dil - 1)) == 0) & ((rel >> shift) <= span))
    zpad = jnp.zeros((LANE - rows, MIX), F32)
    ys = []
    for s in range(seqs):
        rs = slice(s * rows, (s + 1) * rows)
        q = q_ref[rs, :]
        qm = jnp.concatenate([jnp.where(head == h, q, 0.0) for h in range(N_HEADS)], axis=0).astype(BF16)
        ck = ck_ref[0, s].astype(BF16)
        cv = cv_ref[0, s].astype(BF16)
        kn = jnp.concatenate([k_ref[rs, :], zpad], axis=0).astype(BF16)
        vn = jnp.concatenate([v_ref[rs, :], zpad], axis=0).astype(BF16)
        s_c = _dot(qm, ck) * scale
        s_n = _dot_nt(qm, kn) * scale
        pcs, pns, ls, lses = [], [], [], []
        for ci in range(len(DILATED_CONFIGS)):
            sc = jnp.where(valid_c[ci], s_c, NEG_INF)
            sn = jnp.where(valid_n[ci], s_n, NEG_INF)
            m = jnp.maximum(jnp.max(sc, axis=-1, keepdims=True), jnp.max(sn, axis=-1, keepdims=True))
            pc = jnp.exp(sc - m)
            pn = jnp.exp(sn - m)
            l = jnp.sum(pc, axis=-1, keepdims=True) + jnp.sum(pn, axis=-1, keepdims=True)
            pcs.append(pc.astype(BF16))
            pns.append(pn.astype(BF16))
            ls.append(l)
            lses.append(m + jnp.log(l))
        o_all = _dot_nt(jnp.concatenate(pcs, axis=0), cv) + _dot(jnp.concatenate(pns, axis=0), vn)
        outs = [o_all[ci * hr:(ci + 1) * hr, :] / ls[ci] for ci in range(len(DILATED_CONFIGS))]
        mx = jnp.maximum(jnp.maximum(lses[0], lses[1]), lses[2])
        es = [jnp.exp(le - mx) for le in lses]
        y_all = (es[0] * outs[0] + es[1] * outs[1] + es[2] * outs[2]) / (es[0] + es[1] + es[2])
        y = jnp.zeros((rows, MIX), F32)
        for h in range(N_HEADS):
            y = jnp.where(head == h, y_all[h * rows:(h + 1) * rows, :], y)
        ys.append(y)
    y_ref[...] = jnp.concatenate(ys, axis=0).astype(y_ref.dtype)


def _swa_sample(h, mix, cache_k, cache_v, layer, *, rows, n_seq, seqs, row0):
    past = cache_k.shape[3]
    blk = seqs * rows
    rb0 = row0 // blk

    def hspec(col):
        return pl.BlockSpec((blk, MIX), lambda b: (rb0 + b, col))

    cspec = pl.BlockSpec((1, seqs, MIX, past), lambda b: (layer, b, 0, 0))
    nbytes = 2 * (2 * seqs * past * MIX * 4 + 4 * blk * MIX * 4) + 16 * N_HEADS * rows * past * 4 + 2 * past * MIX * 2
    return pl.pallas_call(
        functools.partial(_swa_sample_body, past=past, rows=rows, seqs=seqs),
        grid=(n_seq // seqs,),
        in_specs=[hspec(COL_SQ), hspec(COL_SK), hspec(COL_SV), cspec, cspec,
                  pl.BlockSpec(memory_space=pl.ANY)],
        out_specs=pl.BlockSpec((blk, MIX), lambda b: (rb0 + b, 1)),
        out_shape=jax.ShapeDtypeStruct(mix.shape, mix.dtype),
        input_output_aliases={5: 0},
        compiler_params=_params(("arbitrary",), nbytes),
        name="swa_sample",
    )(h, h, h, cache_k, cache_v, mix)


def _conv_rows(ctx, base, sub, w_ref, cb_ref, g_ref, b_ref):
    lead = CONV_CTX - (CONV_WIDTH - 1)
    window = ctx[base:base + sub + CONV_CTX, :]
    acc = jnp.zeros((sub, MIX), F32)
    for r in range(SUBLANE):
        shifted = pltpu.roll(window, sub + CONV_CTX - (lead + r), 0)
        for j in range(r, CONV_WIDTH, SUBLANE):
            acc = acc + shifted[j - r:j - r + sub, :] * w_ref[j:j + 1, :]
    z = _layer_norm(acc + cb_ref[...], g_ref[...], b_ref[...])
    return z * _sigmoid(z)


def _gelu_tanh(x):
    c = math.sqrt(2.0 / math.pi)
    return x * (0.5 * (1.0 + jnp.tanh(c * (x + 0.044715 * (x * x * x)))))


def _sgu_weights(ws_ref):
    tri = (lax.broadcasted_iota(I32, (SGU_CHUNK, SGU_CHUNK), 0)
           >= lax.broadcasted_iota(I32, (SGU_CHUNK, SGU_CHUNK), 1))
    return [jnp.where(tri, ws_ref[gi], 0.0).astype(BF16) for gi in range(N_HEADS)]


def _sgu_chunk(u_raw, v_raw, wms, tabs):
    g_ref, b_ref, bias_ref, avg_ref = tabs
    sub = u_raw.shape[0]
    head = _lane_head(MIX)
    u = _gelu_tanh(u_raw)
    vn = _group_norm(_gelu_tanh(v_raw), avg_ref[...]) * g_ref[...] + b_ref[...]
    vp = vn if sub == SGU_CHUNK else jnp.concatenate([vn, jnp.zeros((SGU_CHUNK - sub, MIX), F32)], axis=0)
    vnb = vp.astype(BF16)
    z = jnp.zeros((SGU_CHUNK, MIX), F32)
    for gi in range(N_HEADS):
        z = jnp.where(head == gi, _dot(wms[gi], vnb), z)
    z = z + bias_ref[...]
    return u * z[0:sub, :], vn


def _mixers_body(q_ref, k_ref, v_ref, g_ref, cv_ref, cg_ref, gu_ref, gv_ref,
                 dec_ref, qdec_ref, kdec_ref, cmat_ref, bdm_ref, avg_ref, gn_ref,
                 cw_ref, cb_ref, clg_ref, clb_ref, sg_ref, sb_ref, ws_ref, bias_ref, mix_in,
                 mix_ref, glu_ref, sout_ref, s_scr, ctx, *, n_steps, rows):
    del mix_in
    c = pl.program_id(1)

    @pl.when(c == 0)
    def _():
        s_scr[...] = jnp.zeros_like(s_scr)
        ctx[0:CONV_CTX, :] = jnp.zeros((CONV_CTX, MIX), F32)

    ret_tabs = (dec_ref, qdec_ref, kdec_ref, cmat_ref, bdm_ref, avg_ref, gn_ref)
    sgu_tabs = (sg_ref, sb_ref, bias_ref, avg_ref)
    wms = _sgu_weights(ws_ref)
    glu = cv_ref[...] * _sigmoid(cg_ref[...])
    glu_ref[...] = glu
    ctx[CONV_CTX:CONV_CTX + rows, :] = glu
    state = s_scr[...]
    for cc in range(rows // RET_CHUNK):
        rs = slice(cc * RET_CHUNK, (cc + 1) * RET_CHUNK)
        y, state = _retention_chunk(q_ref[rs, :], k_ref[rs, :], v_ref[rs, :], g_ref[rs, :], state, ret_tabs)
        out = mix_ref.dtype
        mix_ref[rs, 0:MIX] = y.astype(out)
        mix_ref[rs, 2 * MIX:3 * MIX] = _conv_rows(ctx, cc * RET_CHUNK, RET_CHUNK, cw_ref, cb_ref, clg_ref, clb_ref).astype(out)
        mix_ref[rs, 3 * MIX:4 * MIX] = _sgu_chunk(gu_ref[rs, :], gv_ref[rs, :], wms, sgu_tabs)[0].astype(out)
    mix_ref[:, MIX:2 * MIX] = jnp.zeros((rows, MIX), mix_ref.dtype)
    s_scr[...] = state
    ctx[0:CONV_CTX, :] = ctx[rows:rows + CONV_CTX, :]

    @pl.when(c == n_steps - 1)
    def _():
        sout_ref[0] = s_scr[...]


def _mixers_prompt(h, mix, ret_tabs, avg, gn_g, conv_args, sgu_args, *, seq, n_seq):
    assert RET_CHUNK == SGU_CHUNK and seq % MIX_ROWS == 0
    rows = MIX_ROWS
    n_steps = seq // rows
    dec, qdec, kdec, cmat, bdm = ret_tabs
    cw, cb, clg, clb = conv_args
    sg, sb, ws, bias = sgu_args

    def hspec(col):
        return pl.BlockSpec((rows, MIX), lambda b, c: (b * n_steps + c, col))

    def const(arr):
        return pl.BlockSpec(arr.shape, lambda b, c: (0,) * arr.ndim)

    tables = [dec, qdec, kdec, cmat, bdm, avg, gn_g, cw, cb, clg, clb, sg, sb, ws, bias]
    cols = [COL_RQ, COL_RK, COL_RV, COL_RG, COL_CV, COL_CG, COL_GU, COL_GV]
    nbytes = (2 * (len(cols) + 5) * rows * MIX * 4 + 2 * sum(t.size * 4 for t in tables)
              + (CONV_CTX + rows) * MIX * 4 + MIX * MIX * 4)
    return pl.pallas_call(
        functools.partial(_mixers_body, n_steps=n_steps, rows=rows),
        grid=(n_seq, n_steps),
        in_specs=[hspec(col) for col in cols] + [const(t) for t in tables] + [pl.BlockSpec(memory_space=pl.ANY)],
        out_specs=[pl.BlockSpec((rows, 4 * MIX), lambda b, c: (b * n_steps + c, 0)),
                   pl.BlockSpec((rows, MIX), lambda b, c: (b * n_steps + c, 0)),
                   pl.BlockSpec((1, MIX, MIX), lambda b, c: (b, 0, 0))],
        out_shape=[jax.ShapeDtypeStruct(mix.shape, mix.dtype),
                   jax.ShapeDtypeStruct((n_seq * seq, MIX), F32),
                   jax.ShapeDtypeStruct((n_seq, MIX, MIX), F32)],
        scratch_shapes=[pltpu.VMEM((MIX, MIX), F32), pltpu.VMEM((CONV_CTX + rows, MIX), F32)],
        input_output_aliases={len(cols) + len(tables): 0},
        compiler_params=_params(("arbitrary", "arbitrary"), nbytes),
        name="mixers_prompt",
    )(*([h] * len(cols)), *tables, mix)


def _mixers_sample_body(q_ref, k_ref, v_ref, g_ref, cv_ref, cg_ref, gu_ref, gv_ref, s0_ref, buf_ref,
                        dec_ref, qdec_ref, kdec_ref, cmat_ref, bdm_ref, avg_ref, gn_ref,
                        cw_ref, cb_ref, clg_ref, clb_ref, sg_ref, sb_ref, ws_ref, bias_ref, mix_in,
                        mix_ref, glu_ref, vn_ref, sout_ref, ctx, *, seqs, rows):
    del mix_in
    ret_tabs = (dec_ref, qdec_ref, kdec_ref, cmat_ref, bdm_ref, avg_ref, gn_ref)
    sgu_tabs = (sg_ref, sb_ref, bias_ref, avg_ref)
    wms = _sgu_weights(ws_ref)
    glu = cv_ref[...] * _sigmoid(cg_ref[...])
    glu_ref[...] = glu
    ret, conv, gate = [], [], []
    for s in range(seqs):
        rs = slice(s * rows, (s + 1) * rows)
        y, sout_ref[s] = _retention_chunk(q_ref[rs, :], k_ref[rs, :], v_ref[rs, :], g_ref[rs, :], s0_ref[s], ret_tabs)
        ret.append(y)
        ctx[s, 0:CONV_CTX, :] = buf_ref[s]
        ctx[s, CONV_CTX:CONV_CTX + rows, :] = glu[rs, :]
        conv.append(_conv_rows(ctx.at[s], 0, rows, cw_ref, cb_ref, clg_ref, clb_ref))
        y, vn_ref[rs, :] = _sgu_chunk(gu_ref[rs, :], gv_ref[rs, :], wms, sgu_tabs)
        gate.append(y)
    blocks = (ret, [jnp.zeros((rows, MIX), F32)] * seqs, conv, gate)
    for col, parts in enumerate(blocks):
        mix_ref[:, col * MIX:(col + 1) * MIX] = jnp.concatenate(parts, axis=0).astype(mix_ref.dtype)


def _mixers_sample(h, mix, s0, buf, ret_tabs, avg, gn_g, conv_args, sgu_args, *, rows, n_seq, seqs, row0):
    blk = seqs * rows
    rb0 = row0 // blk
    dec, qdec, kdec, cmat, bdm = ret_tabs
    cw, cb, clg, clb = conv_args
    sg, sb, ws, bias = sgu_args

    def hspec(col):
        return pl.BlockSpec((blk, MIX), lambda b: (rb0 + b, col))

    def const(arr):
        return pl.BlockSpec(arr.shape, lambda b: (0,) * arr.ndim)

    per_seq = lambda arr: pl.BlockSpec((seqs,) + arr.shape[1:], lambda b: (b,) + (0,) * (arr.ndim - 1))
    tables = [dec, qdec, kdec, cmat, bdm, avg, gn_g, cw, cb, clg, clb, sg, sb, ws, bias]
    cols = [COL_RQ, COL_RK, COL_RV, COL_RG, COL_CV, COL_CG, COL_GU, COL_GV]
    rows_out = lambda width: pl.BlockSpec((blk, width), lambda b: (b, 0))
    nbytes = (2 * (len(cols) + 6) * blk * MIX * 4 + 2 * sum(t.size * 4 for t in tables)
              + 4 * seqs * MIX * MIX * 4 + 3 * seqs * (CONV_CTX + rows) * MIX * 4)
    return pl.pallas_call(
        functools.partial(_mixers_sample_body, seqs=seqs, rows=rows),
        grid=(n_seq // seqs,),
        in_specs=([hspec(col) for col in cols] + [per_seq(s0), per_seq(buf)] + [const(t) for t in tables]
                  + [pl.BlockSpec(memory_space=pl.ANY)]),
        out_specs=[pl.BlockSpec((blk, 4 * MIX), lambda b: (rb0 + b, 0)), rows_out(MIX), rows_out(MIX), per_seq(s0)],
        out_shape=[jax.ShapeDtypeStruct(mix.shape, mix.dtype),
                   jax.ShapeDtypeStruct((n_seq * rows, MIX), F32),
                   jax.ShapeDtypeStruct((n_seq * rows, MIX), F32),
                   jax.ShapeDtypeStruct(s0.shape, F32)],
        scratch_shapes=[pltpu.VMEM((seqs, CONV_CTX + rows, MIX), F32)],
        input_output_aliases={len(cols) + 2 + len(tables): 0},
        compiler_params=_params(("arbitrary",), nbytes),
        name="mixers_sample",
    )(*([h] * len(cols)), s0, buf, *tables, mix)


def _out_router_body(mix_ref, w_ref, x_ref, g_ref, b_ref, wr_ref, br_ref,
                     x1_ref, route_ref, counts_ref, wbf, wrbf, logit_scr, *, alpha):
    @pl.when(pl.program_id(0) == 0)
    def _():
        wbf[...] = w_ref[0].astype(BF16)
        wrbf[...] = wr_ref[...].astype(BF16)

    parts = [slice(r0, r0 + OUT_ROWS) for r0 in range(0, TM, OUT_ROWS)]
    ys = [_dot(mix_ref[rs, :], wbf[...]) for rs in parts]
    for rs, y in zip(parts, ys):
        x1 = _layer_norm(alpha * x_ref[rs, :] + y, g_ref[...], b_ref[...])
        x1_ref[rs, :] = x1
        logit_scr[:, rs] = _dot_nt(wrbf[...], x1.astype(BF16)) + br_ref[...]
    logits = logit_scr[0:ROUTE_ROWS, :]
    row = lax.broadcasted_iota(I32, (ROUTE_ROWS, 1), 0).astype(F32)

    def first_max(mask, vals):
        masked = jnp.where(mask, vals, NEG_INF)
        top = jnp.max(masked, axis=0, keepdims=True)
        idx = jnp.min(jnp.where(mask & (masked == top), row, float(LANE)), axis=0, keepdims=True)
        return top, idx

    gmask = row < N_GROUPS
    gmax, gsel = first_max(gmask, logits)
    g_gate = 1.0 / jnp.sum(jnp.where(gmask, jnp.exp(logits - gmax), 0.0), axis=0, keepdims=True)
    lo = N_GROUPS + EXPERTS_PER_GROUP * gsel
    emask = (row >= lo) & (row < lo + EXPERTS_PER_GROUP)
    v1, i1 = first_max(emask, logits)
    v2, i2 = first_max(emask & (row != i1), logits)
    e12 = jnp.exp(v2 - v1)
    both = (row == i1).astype(F32) + (row == i2).astype(F32)

    earlier = (lax.broadcasted_iota(I32, (TD, TD), 0) < lax.broadcasted_iota(I32, (TD, TD), 1)).astype(BF16)
    rank_a, rank_b = [], []
    for td in range(TM // TD):
        cs = slice(td * TD, (td + 1) * TD)
        tile = both[:, cs]
        counts_ref[td] = jnp.sum(tile, axis=1, keepdims=True)
        before = _dot(tile.astype(BF16), earlier)
        rank_a.append(jnp.sum(jnp.where(row == i1[:, cs], before, 0.0), axis=0, keepdims=True))
        rank_b.append(jnp.sum(jnp.where(row == i2[:, cs], before, 0.0), axis=0, keepdims=True))
    values = (i1 - N_GROUPS, i2 - N_GROUPS, g_gate * (1.0 / (1.0 + e12)), g_gate * (e12 / (1.0 + e12)),
              jnp.concatenate(rank_a, axis=1), jnp.concatenate(rank_b, axis=1))
    out_row = lax.broadcasted_iota(I32, (SUBLANE, 1), 0)
    route = jnp.zeros((SUBLANE, TM), F32)
    for k, val in enumerate(values):
        route = jnp.where(out_row == k, val, route)
    route_ref[...] = route


def _out_router(mix, w_out, layer, x, g, b, wr, br, *, alpha):
    nt, d = x.shape
    kdim = mix.shape[1]
    const = lambda shape: pl.BlockSpec(shape, lambda i: (0, 0))
    row = lambda width: pl.BlockSpec((TM, width), lambda i: (i, 0))
    nbytes = 2 * (kdim * d * 4 + 2 * d * LANE * 4 + TM * kdim * 4 + 2 * TM * d * 4 + TM * LANE * 4) + kdim * d * 2
    return pl.pallas_call(
        functools.partial(_out_router_body, alpha=alpha),
        grid=(nt // TM,),
        in_specs=[row(kdim), pl.BlockSpec((1, kdim, d), lambda i: (layer, 0, 0)), row(d),
                  const((1, d)), const((1, d)), const((LANE, d)), const((LANE, 1))],
        out_specs=[row(d), pl.BlockSpec((SUBLANE, TM), lambda i: (0, i)),
                   pl.BlockSpec((TM // TD, ROUTE_ROWS, 1), lambda i: (i, 0, 0))],
        out_shape=[jax.ShapeDtypeStruct((nt, d), F32), jax.ShapeDtypeStruct((SUBLANE, nt), F32),
                   jax.ShapeDtypeStruct((nt // TD, ROUTE_ROWS, 1), F32)],
        scratch_shapes=[pltpu.VMEM((kdim, d), BF16), pltpu.VMEM((LANE, d), BF16), pltpu.VMEM((LANE, TM), F32)],
        compiler_params=_params(("arbitrary",), nbytes),
        name="out_router",
    )(mix, w_out, x, g, b, wr, br)


def _local_slots(route, off_col):
    expert = lax.broadcasted_iota(I32, (N_EXPERTS, 1), 0).astype(F32)
    slots = []
    for kk in range(TOP_K):
        start = jnp.sum(jnp.where(expert == route[kk:kk + 1, :], off_col, 0.0), axis=0, keepdims=True)
        slots.append(start + route[4 + kk:5 + kk, :])
    return slots


def _pack_rows(y):
    half = y.shape[1] // 2
    bits = lax.bitcast_convert_type(y.astype(BF16).astype(F32), U32)
    return (bits[:, :half] & jnp.uint32(0xFFFF0000)) | (bits[:, half:] >> 16)


def _unpack_rows(w):
    hi = lax.bitcast_convert_type(w & jnp.uint32(0xFFFF0000), F32)
    lo = lax.bitcast_convert_type(w << 16, F32)
    return jnp.concatenate([hi, lo], axis=1)


def _chunk_copy(hbm, hbm_row, loc, chunk, sem, *, to_hbm):
    vm = loc.at[pl.ds(pl.multiple_of(chunk * MOE_CHUNK, MOE_CHUNK), MOE_CHUNK), :]
    hb = hbm.at[pl.ds(pl.multiple_of(hbm_row, MOE_CHUNK), MOE_CHUNK), :]
    return pltpu.make_async_copy(vm, hb, sem) if to_hbm else pltpu.make_async_copy(hb, vm, sem)


def _for_each_chunk(count, fn):
    def group(t, carry):
        for u in range(CHUNK_UNROLL):
            fn(t * CHUNK_UNROLL + u)
        return carry

    def single(c, carry):
        fn(c)
        return carry

    whole = count // CHUNK_UNROLL
    lax.fori_loop(0, whole, group, 0)
    lax.fori_loop(whole * CHUNK_UNROLL, count, single, 0)


def _wait_chunks(hbm, buf, sem, count, *, to_hbm):
    _for_each_chunk(count, lambda c: _chunk_copy(hbm, 0, buf, 0, sem, to_hbm=to_hbm).wait())


def _dispatch_body(dst_tab, n_chunks, x_ref, route_ref, off_ref, xs_in, xs_hbm, loc, sem):
    del xs_in
    i = pl.program_id(0)
    last = pl.num_programs(0) - 1
    cur = i % 2
    slot_a, slot_b = _local_slots(route_ref[...], off_ref[0])
    buf_row = lax.broadcasted_iota(I32, (LOC_ROWS, 1), 0).astype(F32)
    onehot = ((buf_row == slot_a) | (buf_row == slot_b)).astype(BF16)
    loc[cur] = _pack_rows(_dot(onehot, x_ref[...].astype(BF16)))

    _for_each_chunk(n_chunks[i], lambda c: _chunk_copy(
        xs_hbm, dst_tab[i * MAX_CHUNKS + c], loc.at[cur], c, sem.at[cur], to_hbm=True).start())

    @pl.when(i > 0)
    def _():
        _wait_chunks(xs_hbm, loc.at[1 - cur], sem.at[1 - cur], n_chunks[jnp.maximum(i - 1, 0)], to_hbm=True)

    @pl.when(i == last)
    def _():
        _wait_chunks(xs_hbm, loc.at[cur], sem.at[cur], n_chunks[i], to_hbm=True)


def _dispatch(dst_tab, n_chunks, x1, route, off, xs_prev):
    nt, d = x1.shape
    row = lambda width: pl.BlockSpec((TD, width), lambda i, *_: (i, 0))
    nbytes = 2 * (TD * d * 4 + TD * LANE * 4) + 2 * LOC_ROWS * d * 4 + 4 * TD * LOC_ROWS * 4 + LOC_ROWS * d * 4
    return pl.pallas_call(
        _dispatch_body,
        grid_spec=pltpu.PrefetchScalarGridSpec(
            num_scalar_prefetch=2,
            grid=(nt // TD,),
            in_specs=[row(d), pl.BlockSpec((SUBLANE, TD), lambda i, *_: (0, i)),
                      pl.BlockSpec((1, N_EXPERTS, 1), lambda i, *_: (i, 0, 0)),
                      pl.BlockSpec(memory_space=pl.ANY)],
            out_specs=pl.BlockSpec(memory_space=pl.ANY),
            scratch_shapes=[pltpu.VMEM((2, LOC_ROWS, d // 2), U32), pltpu.SemaphoreType.DMA((2,))]),
        out_shape=jax.ShapeDtypeStruct(xs_prev.shape, U32),
        input_output_aliases={5: 0},
        compiler_params=_params(("arbitrary",), nbytes),
        name="moe_dispatch",
    )(dst_tab, n_chunks, x1, route, off, xs_prev)


def _ffn_body(blk_e, blk_valid, blk_src, blk_slot, blk_next, x_ref, w1_hbm, w3_hbm, w2_hbm, y_ref,
              w1f, w3f, w2f, w1b, w3b, w2b, sem, *, layer):
    del blk_src
    j = pl.program_id(0)
    n_valid = blk_valid[j]
    half = MOE_BLK // 2
    slot = blk_slot[j]

    def weight_copies(expert, buf):
        return [pltpu.make_async_copy(hbm.at[layer, expert], vm.at[buf], sem.at[buf])
                for hbm, vm in ((w1_hbm, w1f), (w3_hbm, w3f), (w2_hbm, w2f))]

    @pl.when(j == 0)
    def _():
        for cp in weight_copies(blk_e[0], slot):
            cp.start()

    @pl.when((n_valid > 0) & ((j == 0) | (blk_e[j] != blk_e[jnp.maximum(j - 1, 0)])))
    def _():
        for cp in weight_copies(blk_e[j], slot):
            cp.wait()
        w1b[...] = w1f[slot].astype(BF16)
        w3b[...] = w3f[slot].astype(BF16)
        w2b[...] = w2f[slot].astype(BF16)

        @pl.when(blk_next[j] >= 0)
        def _():
            for cp in weight_copies(jnp.maximum(blk_next[j], 0), 1 - slot):
                cp.start()

    def ffn(rows):
        xb = _unpack_rows(x_ref[rows, :]).astype(BF16)
        h1 = _dot(xb, w1b[...])
        h3 = _dot(xb, w3b[...])
        act = (h1 * _sigmoid(h1)) * h3
        y_ref[rows, :] = _pack_rows(_dot(act.astype(BF16), w2b[...]))

    @pl.when(n_valid > half)
    def _():
        ffn(slice(0, MOE_BLK))

    @pl.when((n_valid > 0) & (n_valid <= half))
    def _():
        ffn(slice(0, half))
        y_ref[half:MOE_BLK, :] = jnp.zeros((half, y_ref.shape[1]), U32)


def _ffn(xs, w1, w3, w2, layer, blk_e, blk_valid, blk_src, blk_slot, blk_next):
    d = w1.shape[2]
    de = w1.shape[3]
    n_blk = xs.shape[0] // MOE_BLK
    rows = pl.BlockSpec((MOE_BLK, d // 2), lambda j, be, bv, bs, sl, nx: (bs[j], 0))
    nbytes = 2 * (3 * d * de * 4 + MOE_BLK * d * 4) + 3 * d * de * 2 + 6 * MOE_BLK * de * 4 + 4 * MOE_BLK * d * 4
    anywhere = pl.BlockSpec(memory_space=pl.ANY)
    return pl.pallas_call(
        functools.partial(_ffn_body, layer=layer),
        grid_spec=pltpu.PrefetchScalarGridSpec(
            num_scalar_prefetch=5,
            grid=(n_blk,),
            in_specs=[rows, anywhere, anywhere, anywhere],
            out_specs=rows,
            scratch_shapes=[pltpu.VMEM((2, d, de), F32), pltpu.VMEM((2, d, de), F32), pltpu.VMEM((2, de, d), F32),
                            pltpu.VMEM((d, de), BF16), pltpu.VMEM((d, de), BF16), pltpu.VMEM((de, d), BF16),
                            pltpu.SemaphoreType.DMA((2,))]),
        out_shape=jax.ShapeDtypeStruct(xs.shape, U32),
        input_output_aliases={5: 0},
        compiler_params=_params(("arbitrary",), nbytes),
        name="expert_ffn",
    )(blk_e, blk_valid, blk_src, blk_slot, blk_next, xs, w1, w3, w2)


def _combine_body(dst_tab, n_chunks, x_ref, route_ref, off_ref, g_ref, b_ref, ys_hbm, x2_ref,
                  *rest, alpha, head_tiles):
    *tail_ref, loc, sem = rest
    i = pl.program_id(0)
    last = pl.num_programs(0) - 1
    cur = i % 2

    def fetch(tile, buf):
        _for_each_chunk(n_chunks[tile], lambda c: _chunk_copy(
            ys_hbm, dst_tab[tile * MAX_CHUNKS + c], loc.at[buf], c, sem.at[buf], to_hbm=False).start())

    @pl.when(i == 0)
    def _():
        loc[...] = jnp.zeros_like(loc)
        fetch(0, 0)

    @pl.when(i < last)
    def _():
        fetch(jnp.minimum(i + 1, last), 1 - cur)

    route = route_ref[...]
    slot_a, slot_b = _local_slots(route, off_ref[0])
    buf_row = lax.broadcasted_iota(I32, (LOC_ROWS, 1), 0).astype(F32)
    gates = (jnp.where(buf_row == slot_a, route[2:3, :], 0.0)
             + jnp.where(buf_row == slot_b, route[3:4, :], 0.0))
    _wait_chunks(ys_hbm, loc.at[cur], sem.at[cur], n_chunks[i], to_hbm=False)
    f = _dot_tn(gates.astype(BF16), _unpack_rows(loc[cur]).astype(BF16))
    x2 = _layer_norm(alpha * x_ref[...] + f, g_ref[...], b_ref[...])
    if tail_ref:
        @pl.when(i < head_tiles)
        def _():
            x2_ref[...] = x2

        @pl.when(i >= head_tiles)
        def _():
            tail_ref[0][...] = x2
    else:
        x2_ref[...] = x2


def _combine(dst_tab, n_chunks, x1, route, off, g, b, ys, *, alpha, head_rows=None):
    nt, d = x1.shape
    row = lambda width: pl.BlockSpec((TD, width), lambda i, dt, nc: (i, 0))
    const = lambda shape: pl.BlockSpec(shape, lambda i, dt, nc: (0, 0))
    nbytes = 2 * (3 * TD * d * 4 + TD * LANE * 4) + 2 * LOC_ROWS * d * 4 + 4 * TD * LOC_ROWS * 4 + LOC_ROWS * d * 2
    if head_rows is None:
        head_tiles = nt // TD
        out_specs = row(d)
        out_shape = jax.ShapeDtypeStruct((nt, d), F32)
    else:
        head_tiles = head_rows // TD
        out_specs = [pl.BlockSpec((TD, d), lambda i, dt, nc: (jnp.minimum(i, head_tiles - 1), 0)),
                     pl.BlockSpec((TD, d), lambda i, dt, nc: (jnp.maximum(i - head_tiles, 0), 0))]
        out_shape = [jax.ShapeDtypeStruct((head_rows, d), F32), jax.ShapeDtypeStruct((nt - head_rows, d), F32)]
    scratch = [pltpu.VMEM((2, LOC_ROWS, d // 2), U32), pltpu.SemaphoreType.DMA((2,))]
    return pl.pallas_call(
        functools.partial(_combine_body, alpha=alpha, head_tiles=head_tiles),
        grid_spec=pltpu.PrefetchScalarGridSpec(
            num_scalar_prefetch=2,
            grid=(nt // TD,),
            in_specs=[row(d), pl.BlockSpec((SUBLANE, TD), lambda i, dt, nc: (0, i)),
                      pl.BlockSpec((1, N_EXPERTS, 1), lambda i, dt, nc: (i, 0, 0)),
                      const((1, d)), const((1, d)), pl.BlockSpec(memory_space=pl.ANY)],
            out_specs=out_specs,
            scratch_shapes=scratch),
        out_shape=out_shape,
        compiler_params=_params(("arbitrary",), nbytes),
        name="combine_ln",
    )(dst_tab, n_chunks, x1, route, off, g, b, ys)


def _rope_tables(pos, rot_dim, theta):
    half = rot_dim // 2
    inv = jnp.float32(theta) ** (-jnp.arange(half, dtype=F32) * (2.0 / rot_dim))
    ang = pos.astype(F32)[:, None] * inv[None, :]
    c, s = jnp.cos(ang), jnp.sin(ang)
    rest = HEAD_DIM - rot_dim
    ones = jnp.ones((pos.shape[0], rest), F32)
    zeros = jnp.zeros((pos.shape[0], rest), F32)
    cos_h = jnp.concatenate([c, c, ones], axis=1)
    sin_h = jnp.concatenate([-s, s, zeros], axis=1)
    return jnp.tile(cos_h, (1, N_HEADS)), jnp.tile(sin_h, (1, N_HEADS))


def _block_diag(s):
    rows = [jnp.pad(s[:, h], ((0, 0), (0, 0), (h * HEAD_DIM, MIX - (h + 1) * HEAD_DIM))) for h in range(N_HEADS)]
    return jnp.concatenate(rows, axis=1)


def _diag_blocks(s):
    return jnp.stack([s[:, h * HEAD_DIM:(h + 1) * HEAD_DIM, h * HEAD_DIM:(h + 1) * HEAD_DIM]
                      for h in range(N_HEADS)], axis=1)


def _routing_plan(tile_counts, n_blk):
    cnt_t = tile_counts[:, N_GROUPS:N_GROUPS + N_EXPERTS, 0].astype(I32)
    run_chunks = (cnt_t + MOE_CHUNK - 1) // MOE_CHUNK
    run_rows = run_chunks * MOE_CHUNK
    cnt = jnp.sum(run_rows, axis=0)
    pcnt = (cnt + MOE_BLK - 1) // MOE_BLK * MOE_BLK
    pstart = jnp.cumsum(pcnt) - pcnt
    n_act = jnp.sum(pcnt) // MOE_BLK
    gstart = pstart[None, :] + jnp.cumsum(run_rows, axis=0) - run_rows
    chunk0 = jnp.cumsum(run_chunks, axis=1) - run_chunks
    n_chunks = jnp.sum(run_chunks, axis=1).astype(I32)

    def spread(first, count, value, n):
        p = jnp.arange(n, dtype=I32)[:, None]
        f, c, v = first[..., None, :], count[..., None, :], value[..., None, :]
        return jnp.sum(jnp.where((f <= p) & (p < f + c), v + (p - f) * MOE_CHUNK, 0), axis=-1)

    dst_tab = spread(chunk0, run_chunks, gstart, MAX_CHUNKS)
    off = (chunk0 * MOE_CHUNK).astype(F32)[:, :, None]
    pos = jnp.arange(n_blk, dtype=I32)[:, None] * MOE_BLK
    mine = (pstart[None, :] <= pos) & (pos < (pstart + pcnt)[None, :])
    experts = jnp.arange(N_EXPERTS, dtype=I32)[None, :]
    last_e = jnp.max(jnp.where(cnt > 0, experts[0], 0))
    active = pos[:, 0] < n_act * MOE_BLK
    blk_e = jnp.where(active, jnp.sum(jnp.where(mine, experts, 0), axis=-1), last_e)
    blk_valid = jnp.sum(jnp.where(mine, jnp.clip((pstart + cnt)[None, :] - pos, 0, MOE_BLK), 0), axis=-1)
    blk_src = jnp.minimum(jnp.arange(n_blk, dtype=I32), jnp.maximum(n_act - 1, 0))
    i32 = lambda t: t.astype(I32)
    used = cnt > 0
    slot_e = (jnp.cumsum(used.astype(I32)) - 1) % 2
    later = (experts > experts.T) & used[None, :]
    next_e = jnp.min(jnp.where(later, experts, N_EXPERTS), axis=-1)
    next_e = jnp.where(next_e < N_EXPERTS, next_e, -1)
    return (i32(dst_tab).reshape(-1), n_chunks, off, i32(blk_e), i32(blk_valid), i32(blk_src),
            i32(slot_e[blk_e]), i32(next_e[blk_e]))


def kernel(x_prompt, x_sample, state_ret, cache_swa_k, cache_swa_v, state_conv, w_in, w_out, ret_gn_g,
           conv_w, conv_b, conv_ln_g, conv_ln_b, sgu_ln_g, sgu_ln_b, sgu_w, sgu_b, ln1_g, ln1_b,
           ln2_g, ln2_b, router_g_w, router_g_b, router_e_w, router_e_b, moe_w1, moe_w3, moe_w2):
    depth = w_in.shape[0]
    n_seq, seq, d = x_prompt.shape
    n_dec, t_new, _ = x_sample.shape
    past = cache_swa_k.shape[2]
    rows_s = -(-t_new // SUBLANE) * SUBLANE
    n_p = n_seq * seq
    n_s = n_dec * rows_s
    n_tail = -(-n_s // TM) * TM
    nt = n_p + n_tail
    assert seq % SWA_SB == 0 and seq % TM == 0 and TM % TD == 0
    assert t_new % RET_CHUNK != 0 and t_new <= SGU_CHUNK and w_in.shape[2] == N_COLBLK * MIX
    assert n_dec % SAMPLE_SEQS == 0 and n_dec % SWA_SAMPLE_SEQS == 0
    assert n_p % (SAMPLE_SEQS * rows_s) == 0
    alpha = (2 * depth) ** 0.25
    keep = min(DILATED_CONFIGS[-1][0], seq)
    n_blk = -(-(nt * TOP_K + (nt // TD) * N_EXPERTS * (MOE_CHUNK - 1) + N_EXPERTS * (MOE_BLK - 1)) // MOE_BLK)
    cache_k = jnp.transpose(cache_swa_k, (0, 1, 3, 4, 2)).reshape(depth, n_dec, MIX, past)
    cache_v = jnp.transpose(cache_swa_v, (0, 1, 3, 4, 2)).reshape(depth, n_dec, MIX, past)

    x = jnp.concatenate([x_prompt.reshape(n_p, d),
                         jnp.pad(x_sample, ((0, 0), (0, rows_s - t_new), (0, 0))).reshape(n_s, d),
                         jnp.zeros((n_tail - n_s, d), F32)], axis=0)
    pos = jnp.concatenate([jnp.arange(seq, dtype=I32),
                           PAST_LEN + jnp.arange(n_tail, dtype=I32) % rows_s])
    rope = _rope_tables(pos, HEAD_DIM, RET_THETA) + _rope_tables(pos, ROPE_DIM, ROPE_THETA)
    ret_tabs_p = _retention_tables(RET_CHUNK, RET_CHUNK)
    ret_tabs_s = _retention_tables(t_new, rows_s)
    lane_head = jnp.arange(MIX) // HEAD_DIM
    avg = ((lane_head[:, None] == lane_head[None, :]).astype(F32) / HEAD_DIM).astype(BF16)
    row2 = lambda v: v.reshape(1, -1)
    mix = jnp.zeros((nt, 4 * MIX), BF16)
    slots = jnp.zeros((n_blk * MOE_BLK, d // 2), U32)

    outs = [[] for _ in range(9)]
    for l in range(depth):
        h = _in_proj(x, w_in, l, rope, n_p // TM, seq // TM)
        gn = row2(ret_gn_g[l])
        cw = jnp.pad(conv_w[l], ((0, CONV_CTX - CONV_WIDTH), (0, 0)))
        cargs = (cw, row2(conv_b[l]), row2(conv_ln_g[l]), row2(conv_ln_b[l]))
        sbias = jnp.repeat(sgu_b[l].T, HEAD_DIM, axis=1)
        sargs = (row2(sgu_ln_g[l]), row2(sgu_ln_b[l]), sgu_w[l], sbias)
        mix, glu_p, s_p = _mixers_prompt(h, mix, ret_tabs_p, avg, gn, cargs, sargs, seq=seq, n_seq=n_seq)
        mix = _swa_prompt(h, mix, seq=seq, n_seq=n_seq)
        ctx_s = jnp.pad(state_conv[l], ((0, 0), (CONV_CTX - (CONV_WIDTH - 1), 0), (0, 0)))
        mix, glu_s, vn_s, s_s = _mixers_sample(h, mix, _block_diag(state_ret[l]), ctx_s, ret_tabs_s, avg, gn, cargs,
                                               sargs, rows=rows_s, n_seq=n_dec, seqs=SAMPLE_SEQS, row0=n_p)
        mix = _swa_sample(h, mix, cache_k, cache_v, l, rows=rows_s, n_seq=n_dec, seqs=SWA_SAMPLE_SEQS, row0=n_p)

        wr = jnp.concatenate([router_g_w[l].T, jnp.transpose(router_e_w[l], (0, 2, 1)).reshape(N_EXPERTS, d),
                              jnp.zeros((LANE - N_GROUPS - N_EXPERTS, d), F32)], axis=0)
        br = jnp.concatenate([router_g_b[l], router_e_b[l].reshape(-1),
                              jnp.zeros((LANE - N_GROUPS - N_EXPERTS,), F32)]).reshape(LANE, 1)
        x1, route, counts = _out_router(mix, w_out, l, x, row2(ln1_g[l]), row2(ln1_b[l]), wr, br, alpha=alpha)
        dst_tab, n_chunks, off, blk_e, blk_valid, blk_src, blk_slot, blk_next = _routing_plan(counts, n_blk)
        slots = _dispatch(dst_tab, n_chunks, x1, route, off, slots)
        slots = _ffn(slots, moe_w1, moe_w3, moe_w2, l, blk_e, blk_valid, blk_src, blk_slot, blk_next)
        x = _combine(dst_tab, n_chunks, x1, route, off, row2(ln2_g[l]), row2(ln2_b[l]), slots, alpha=alpha,
                     head_rows=n_p if l == depth - 1 else None)

        def tail(col, n_keep):
            return jnp.stack([lax.slice(h, ((b + 1) * seq - n_keep, col * MIX), ((b + 1) * seq, (col + 1) * MIX))
                              for b in range(n_seq)])

        def new_rows(col):
            blk = lax.slice(h, (n_p, col * MIX), (n_p + n_s, (col + 1) * MIX))
            return blk.reshape(n_dec, rows_s, MIX)[:, :t_new]

        heads = lambda t: t.reshape(t.shape[0], t.shape[1], N_HEADS, HEAD_DIM)
        outs[0].append(_diag_blocks(s_p))
        outs[1].append(heads(tail(COL_SK, keep)))
        outs[2].append(heads(tail(COL_SV, keep)))
        outs[3].append(glu_p.reshape(n_seq, seq, MIX)[:, seq - (CONV_WIDTH - 1):])
        outs[4].append(_diag_blocks(s_s))
        outs[5].append(heads(new_rows(COL_SK)))
        outs[6].append(heads(new_rows(COL_SV)))
        outs[7].append(glu_s.reshape(n_dec, rows_s, MIX)[:, :t_new])
        outs[8].append(vn_s.reshape(n_dec, rows_s, MIX)[:, :t_new])

    x_head, x_tail = x
    y_prompt = x_head.reshape(n_seq, seq, d)
    y_sample = x_tail[:n_s].reshape(n_dec, rows_s, d)[:, :t_new]
    return (y_prompt, y_sample) + tuple(jnp.stack(o) for o in outs)
```

```python
import functools
import math

import jax
import jax.numpy as jnp
from jax import lax
from jax.experimental import pallas as pl
from jax.experimental.pallas import tpu as pltpu

F32 = jnp.float32
BF16 = jnp.bfloat16
I32 = jnp.int32
U32 = jnp.uint32

HEAD_DIM = 64
N_HEADS = 4
MIX = N_HEADS * HEAD_DIM
(COL_RQ, COL_RK, COL_RV, COL_RG, COL_SQ, COL_SK, COL_SV,
 COL_CV, COL_CG, COL_GU, COL_GV) = range(11)
N_COLBLK = 11
RET_CHUNK = 128
RET_THETA = 10000.0
ROPE_THETA = 500000.0
ROPE_DIM = HEAD_DIM // 4
DILATED_CONFIGS = ((128, 1), (512, 4), (2048, 16))
DIL_BLOCK = 128
CONV_WIDTH = 31
SGU_CHUNK = 128
N_GROUPS = 4
EXPERTS_PER_GROUP = 8
N_EXPERTS = N_GROUPS * EXPERTS_PER_GROUP
TOP_K = 2
PAST_LEN = 8192
LN_EPS = 1e-5
NEG_INF = -1e30

LANE = 128
SUBLANE = 8
VMEM_CAP_BYTES = 60000 * 1024
COMPILER_TEMP_BYTES = 12 * 1024 * 1024

TM = 512
TD = 256
OUT_ROWS = 128
ROUTE_ROWS = -(-(N_GROUPS + N_EXPERTS) // SUBLANE) * SUBLANE
MOE_BLK = 512
MOE_CHUNK = SUBLANE
LOC_ROWS = -(-(TD * TOP_K + N_EXPERTS * (MOE_CHUNK - 1)) // LANE) * LANE
MAX_CHUNKS = LOC_ROWS // MOE_CHUNK
CHUNK_UNROLL = 4
MIX_ROWS = 1024
CONV_CTX = 32
SWA_SB = DIL_BLOCK * 16
SWA_TILE = 256
SWA_UNROLL = 16
SAMPLE_SEQS = 8
SWA_SAMPLE_SEQS = 4


def _vmem(nbytes):
    return int(min(VMEM_CAP_BYTES, nbytes + COMPILER_TEMP_BYTES))


def _params(sem, nbytes):
    return pltpu.CompilerParams(dimension_semantics=sem, vmem_limit_bytes=_vmem(nbytes))


def _dot(a, b):
    return jnp.dot(a, b, preferred_element_type=F32)


def _dot_nt(a, b):
    return lax.dot_general(a, b, (((1,), (1,)), ((), ())), preferred_element_type=F32)


def _dot_tn(a, b):
    return lax.dot_general(a, b, (((0,), (0,)), ((), ())), preferred_element_type=F32)


def _split_dot(x, m):
    hi = x.astype(BF16)
    lo = (x - hi.astype(F32)).astype(BF16)
    return _dot(hi, m) + _dot(lo, m)


def _sigmoid(x):
    return 1.0 / (1.0 + jnp.exp(-x))


def _layer_norm(z, g, b):
    mu = jnp.mean(z, axis=-1, keepdims=True)
    d = z - mu
    var = jnp.mean(d * d, axis=-1, keepdims=True)
    return d * lax.rsqrt(var + LN_EPS) * g + b


def _group_norm(z, avg):
    mu = _split_dot(z, avg)
    d = z - mu
    var = _split_dot(d * d, avg)
    return d * lax.rsqrt(var + LN_EPS)


def _lane_head(width):
    return lax.broadcasted_iota(I32, (1, width), 1) // HEAD_DIM


def _rotate(x, cos, sin_signed, half):
    first = (lax.broadcasted_iota(I32, (1, LANE), 1) % HEAD_DIM) < half
    parts = []
    for p in range(x.shape[1] // LANE):
        t = x[:, p * LANE:(p + 1) * LANE]
        up = pltpu.roll(t, half, 1)
        down = pltpu.roll(t, LANE - half, 1)
        parts.append(jnp.where(first, down, up))
    return x * cos + jnp.concatenate(parts, axis=1) * sin_signed


def _in_proj_body(x_ref, w_ref, cr_ref, sr_ref, cs_ref, ss_ref, h_ref, wbf_ref):
    @pl.when(pl.program_id(0) == 0)
    def _():
        for j in range(N_COLBLK):
            wbf_ref[:, j * MIX:(j + 1) * MIX] = w_ref[0, :, j * MIX:(j + 1) * MIX].astype(BF16)

    xb = x_ref[...].astype(BF16)
    for j in range(N_COLBLK):
        hj = _dot(xb, wbf_ref[:, j * MIX:(j + 1) * MIX])
        if j in (COL_RQ, COL_RK):
            hj = _rotate(hj, cr_ref[...], sr_ref[...], HEAD_DIM // 2)
        if j == COL_RK:
            hj = hj * (HEAD_DIM ** -0.5)
        if j in (COL_SQ, COL_SK):
            hj = _rotate(hj, cs_ref[...], ss_ref[...], ROPE_DIM // 2)
        h_ref[:, j * MIX:(j + 1) * MIX] = hj


def _in_proj(x, w, layer, tabs, n_prompt_tiles, tiles_per_seq):
    nt, d = x.shape
    width = w.shape[2]

    def tab_map(i):
        return (jnp.where(i < n_prompt_tiles, i % tiles_per_seq, tiles_per_seq + i - n_prompt_tiles), 0)

    tab_spec = pl.BlockSpec((TM, MIX), tab_map)
    nbytes = d * width * 4 + d * width * 2 + 2 * TM * d * 4 + 8 * TM * MIX * 4 + 2 * TM * width * 4
    return pl.pallas_call(
        _in_proj_body,
        grid=(nt // TM,),
        in_specs=[pl.BlockSpec((TM, d), lambda i: (i, 0)),
                  pl.BlockSpec((1, d, width), lambda i: (layer, 0, 0), pipeline_mode=pl.Buffered(1)),
                  tab_spec, tab_spec, tab_spec, tab_spec],
        out_specs=pl.BlockSpec((TM, width), lambda i: (i, 0)),
        out_shape=jax.ShapeDtypeStruct((nt, width), F32),
        scratch_shapes=[pltpu.VMEM((d, width), BF16)],
        compiler_params=_params(("arbitrary",), nbytes),
        name="in_proj",
    )(x, w, *tabs)


def _retention_chunk(q, k, v, g, state, tabs):
    dec_ref, qdec_ref, kdec_ref, cmat_ref, bdm_ref, avg_ref, gn_ref = tabs
    head = _lane_head(MIX)
    vb = v.astype(BF16)
    kb = k.astype(BF16)
    o = _dot(q.astype(BF16), state.astype(BF16)) * qdec_ref[...]
    for h in range(N_HEADS):
        mh = head == h
        a = _dot_nt(jnp.where(mh, q, 0.0).astype(BF16), kb) * dec_ref[h]
        o = o + jnp.where(mh, _dot(a.astype(BF16), vb), 0.0)
    upd = _dot_tn((k * kdec_ref[...]).astype(BF16), vb)
    state = state * cmat_ref[...] + upd * bdm_ref[...]
    return _group_norm(o, avg_ref[...]) * gn_ref[...] * (g * _sigmoid(g)), state


def _retention_tables(l_real, l_pad):
    hh = jnp.arange(N_HEADS, dtype=F32)
    log_g = jnp.log1p(-jnp.exp2(-5.0 - hh))
    i = jnp.arange(l_pad, dtype=F32)
    rel = i[:, None] - i[None, :]
    dec = jnp.where(rel >= 0, jnp.exp(log_g[:, None, None] * jnp.maximum(rel, 0.0)), 0.0)
    qd = jnp.exp(log_g[:, None] * (i + 1.0))
    kd = jnp.where(i < l_real, jnp.exp(log_g[:, None] * (l_real - 1.0 - i)), 0.0)
    cd = jnp.exp(log_g * l_real)
    expand = lambda t: jnp.repeat(t.T, HEAD_DIM, axis=1)
    row_head = jnp.arange(MIX) // HEAD_DIM
    bdm = (row_head[:, None] == row_head[None, :]).astype(F32)
    cmat = cd[row_head][:, None] * bdm
    return dec, expand(qd), expand(kd), cmat, bdm


def _swa_prompt_body(q_ref, k_ref, v_ref, mix_ref, y_ref, o0, o1, o2, l0, l1, l2, *, seq):
    del mix_ref
    o_scr = (o0, o1, o2)
    l_scr = (l0, l1, l2)
    head = _lane_head(LANE)
    qi = lax.broadcasted_iota(I32, (DIL_BLOCK, 1), 0)
    ki = lax.broadcasted_iota(I32, (1, 2 * DIL_BLOCK), 1)
    rel = DIL_BLOCK + qi - ki
    scale = HEAD_DIM ** -0.5
    for sb in range(seq // SWA_SB):
        for ci, (window, dil) in enumerate(DILATED_CONFIGS):
            span = window // dil
            band = (rel >= 0) & (rel <= span)

            def block(i, carry, sb=sb, ci=ci, dil=dil, band=band):
                nloc = i // dil
                off = nloc * (DIL_BLOCK * dil) + i % dil
                start = sb * SWA_SB + off
                if sb == 0:
                    first = nloc == 0
                    pstart = jnp.where(first, start, start - DIL_BLOCK * dil)
                    valid = band & (ki >= jnp.where(first, DIL_BLOCK, 0))
                else:
                    pstart = start - DIL_BLOCK * dil
                    valid = band
                stride = dil if dil > 1 else None
                cur = pl.ds(start, DIL_BLOCK, stride=stride)
                prev = pl.ds(pstart, DIL_BLOCK, stride=stride)
                qb = q_ref[cur, :] * scale
                kb = jnp.concatenate([k_ref[prev, :], k_ref[cur, :]], axis=0).astype(BF16)
                vb = jnp.concatenate([v_ref[prev, :], v_ref[cur, :]], axis=0).astype(BF16)
                o = jnp.zeros((DIL_BLOCK, LANE), F32)
                ls = jnp.zeros((DIL_BLOCK, LANE), F32)
                for hh in range(LANE // HEAD_DIM):
                    mh = head == hh
                    s = _dot_nt(jnp.where(mh, qb, 0.0).astype(BF16), kb)
                    s = jnp.where(valid, s, NEG_INF)
                    m = jnp.max(s, axis=-1, keepdims=True)
                    p = jnp.exp(s - m)
                    l = jnp.sum(p, axis=-1, keepdims=True)
                    o = jnp.where(mh, _dot(p.astype(BF16), vb) / l, o)
                    ls = jnp.where(mh, m + jnp.log(l), ls)
                dst = pl.ds(off, DIL_BLOCK, stride=stride)
                o_scr[ci][dst, :] = o
                l_scr[ci][dst, :] = ls
                return carry

            lax.fori_loop(0, SWA_SB // DIL_BLOCK, block, 0, unroll=SWA_UNROLL)

        def combine(t, carry, sb=sb):
            rows = pl.ds(t * SWA_TILE, SWA_TILE)
            la, lb, lc = l_scr[0][rows, :], l_scr[1][rows, :], l_scr[2][rows, :]
            mx = jnp.maximum(jnp.maximum(la, lb), lc)
            ea, eb, ec = jnp.exp(la - mx), jnp.exp(lb - mx), jnp.exp(lc - mx)
            num = ea * o_scr[0][rows, :] + eb * o_scr[1][rows, :] + ec * o_scr[2][rows, :]
            y_ref[pl.ds(sb * SWA_SB + t * SWA_TILE, SWA_TILE), :] = (num / (ea + eb + ec)).astype(y_ref.dtype)
            return carry

        lax.fori_loop(0, SWA_SB // SWA_TILE, combine, 0)


def _swa_prompt(h, mix, *, seq, n_seq):
    halves = MIX // LANE

    def hspec(col):
        return pl.BlockSpec((seq, LANE), lambda b, p: (b, col * halves + p))

    nbytes = 2 * 4 * seq * LANE * 4 + 6 * SWA_SB * LANE * 4
    return pl.pallas_call(
        functools.partial(_swa_prompt_body, seq=seq),
        grid=(n_seq, halves),
        in_specs=[hspec(COL_SQ), hspec(COL_SK), hspec(COL_SV), pl.BlockSpec(memory_space=pl.ANY)],
        out_specs=pl.BlockSpec((seq, LANE), lambda b, p: (b, halves + p)),
        out_shape=jax.ShapeDtypeStruct(mix.shape, mix.dtype),
        scratch_shapes=[pltpu.VMEM((SWA_SB, LANE), F32)] * 6,
        input_output_aliases={3: 0},
        compiler_params=_params(("arbitrary", "arbitrary"), nbytes),
        name="swa_prompt",
    )(h, h, h, mix)


def _swa_sample_body(q_ref, k_ref, v_ref, ck_ref, cv_ref, mix_ref, y_ref, *, past, rows, seqs):
    del mix_ref
    head = _lane_head(MIX)
    hr = N_HEADS * rows
    scale = HEAD_DIM ** -0.5
    t_row = lax.broadcasted_iota(I32, (hr, 1), 0) % rows
    rel_c = past + t_row - lax.broadcasted_iota(I32, (1, past), 1)
    rel_n = t_row - lax.broadcasted_iota(I32, (1, LANE), 1)
    valid_c, valid_n = [], []
    for window, dil in DILATED_CONFIGS:
        span = window // dil
        shift = int(math.log2(dil))
        for rel, dst in ((rel_c, valid_c), (rel_n, valid_n)):
            dst.append((rel >= 0) & ((rel & (dil - 1)) == 0) & ((rel >> shift) <= span))
    zpad = jnp.zeros((LANE - rows, MIX), F32)
    ys = []
    for s in range(seqs):
        rs = slice(s * rows, (s + 1) * rows)
        q = q_ref[rs, :]
        qm = jnp.concatenate([jnp.where(head == h, q, 0.0) for h in range(N_HEADS)], axis=0).astype(BF16)
        ck = ck_ref[0, s].astype(BF16)
        cv = cv_ref[0, s].astype(BF16)
        kn = jnp.concatenate([k_ref[rs, :], zpad], axis=0).astype(BF16)
        vn = jnp.concatenate([v_ref[rs, :], zpad], axis=0).astype(BF16)
        s_c = _dot(qm, ck) * scale
        s_n = _dot_nt(qm, kn) * scale
        pcs, pns, ls, lses = [], [], [], []
        for ci in range(len(DILATED_CONFIGS)):
            sc = jnp.where(valid_c[ci], s_c, NEG_INF)
            sn = jnp.where(valid_n[ci], s_n, NEG_INF)
            m = jnp.maximum(jnp.max(sc, axis=-1, keepdims=True), jnp.max(sn, axis=-1, keepdims=True))
            pc = jnp.exp(sc - m)
            pn = jnp.exp(sn - m)
            l = jnp.sum(pc, axis=-1, keepdims=True) + jnp.sum(pn, axis=-1, keepdims=True)
            pcs.append(pc.astype(BF16))
            pns.append(pn.astype(BF16))
            ls.append(l)
            lses.append(m + jnp.log(l))
        o_all = _dot_nt(jnp.concatenate(pcs, axis=0), cv) + _dot(jnp.concatenate(pns, axis=0), vn)
        outs = [o_all[ci * hr:(ci + 1) * hr, :] / ls[ci] for ci in range(len(DILATED_CONFIGS))]
        mx = jnp.maximum(jnp.maximum(lses[0], lses[1]), lses[2])
        es = [jnp.exp(le - mx) for le in lses]
        y_all = (es[0] * outs[0] + es[1] * outs[1] + es[2] * outs[2]) / (es[0] + es[1] + es[2])
        y = jnp.zeros((rows, MIX), F32)
        for h in range(N_HEADS):
            y = jnp.where(head == h, y_all[h * rows:(h + 1) * rows, :], y)
        ys.append(y)
    y_ref[...] = jnp.concatenate(ys, axis=0).astype(y_ref.dtype)


def _swa_sample(h, mix, cache_k, cache_v, layer, *, rows, n_seq, seqs, row0):
    past = cache_k.shape[3]
    blk = seqs * rows
    rb0 = row0 // blk

    def hspec(col):
        return pl.BlockSpec((blk, MIX), lambda b: (rb0 + b, col))

    cspec = pl.BlockSpec((1, seqs, MIX, past), lambda b: (layer, b, 0, 0))
    nbytes = 2 * (2 * seqs * past * MIX * 4 + 4 * blk * MIX * 4) + 16 * N_HEADS * rows * past * 4 + 2 * past * MIX * 2
    return pl.pallas_call(
        functools.partial(_swa_sample_body, past=past, rows=rows, seqs=seqs),
        grid=(n_seq // seqs,),
        in_specs=[hspec(COL_SQ), hspec(COL_SK), hspec(COL_SV), cspec, cspec,
                  pl.BlockSpec(memory_space=pl.ANY)],
        out_specs=pl.BlockSpec((blk, MIX), lambda b: (rb0 + b, 1)),
        out_shape=jax.ShapeDtypeStruct(mix.shape, mix.dtype),
        input_output_aliases={5: 0},
        compiler_params=_params(("arbitrary",), nbytes),
        name="swa_sample",
    )(h, h, h, cache_k, cache_v, mix)


def _conv_rows(ctx, base, sub, w_ref, cb_ref, g_ref, b_ref):
    lead = CONV_CTX - (CONV_WIDTH - 1)
    window = ctx[base:base + sub + CONV_CTX, :]
    acc = jnp.zeros((sub, MIX), F32)
    for r in range(SUBLANE):
        shifted = pltpu.roll(window, sub + CONV_CTX - (lead + r), 0)
        for j in range(r, CONV_WIDTH, SUBLANE):
            acc = acc + shifted[j - r:j - r + sub, :] * w_ref[j:j + 1, :]
    z = _layer_norm(acc + cb_ref[...], g_ref[...], b_ref[...])
    return z * _sigmoid(z)


def _gelu_tanh(x):
    c = math.sqrt(2.0 / math.pi)
    return x * (0.5 * (1.0 + jnp.tanh(c * (x + 0.044715 * (x * x * x)))))


def _sgu_weights(ws_ref):
    tri = (lax.broadcasted_iota(I32, (SGU_CHUNK, SGU_CHUNK), 0)
           >= lax.broadcasted_iota(I32, (SGU_CHUNK, SGU_CHUNK), 1))
    return [jnp.where(tri, ws_ref[gi], 0.0).astype(BF16) for gi in range(N_HEADS)]


def _sgu_chunk(u_raw, v_raw, wms, tabs):
    g_ref, b_ref, bias_ref, avg_ref = tabs
    sub = u_raw.shape[0]
    head = _lane_head(MIX)
    u = _gelu_tanh(u_raw)
    vn = _group_norm(_gelu_tanh(v_raw), avg_ref[...]) * g_ref[...] + b_ref[...]
    vp = vn if sub == SGU_CHUNK else jnp.concatenate([vn, jnp.zeros((SGU_CHUNK - sub, MIX), F32)], axis=0)
    vnb = vp.astype(BF16)
    z = jnp.zeros((SGU_CHUNK, MIX), F32)
    for gi in range(N_HEADS):
        z = jnp.where(head == gi, _dot(wms[gi], vnb), z)
    z = z + bias_ref[...]
    return u * z[0:sub, :], vn


def _mixers_body(q_ref, k_ref, v_ref, g_ref, cv_ref, cg_ref, gu_ref, gv_ref,
                 dec_ref, qdec_ref, kdec_ref, cmat_ref, bdm_ref, avg_ref, gn_ref,
                 cw_ref, cb_ref, clg_ref, clb_ref, sg_ref, sb_ref, ws_ref, bias_ref, mix_in,
                 mix_ref, glu_ref, sout_ref, s_scr, ctx, *, n_steps, rows):
    del mix_in
    c = pl.program_id(1)

    @pl.when(c == 0)
    def _():
        s_scr[...] = jnp.zeros_like(s_scr)
        ctx[0:CONV_CTX, :] = jnp.zeros((CONV_CTX, MIX), F32)

    ret_tabs = (dec_ref, qdec_ref, kdec_ref, cmat_ref, bdm_ref, avg_ref, gn_ref)
    sgu_tabs = (sg_ref, sb_ref, bias_ref, avg_ref)
    wms = _sgu_weights(ws_ref)
    glu = cv_ref[...] * _sigmoid(cg_ref[...])
    glu_ref[...] = glu
    ctx[CONV_CTX:CONV_CTX + rows, :] = glu
    state = s_scr[...]
    for cc in range(rows // RET_CHUNK):
        rs = slice(cc * RET_CHUNK, (cc + 1) * RET_CHUNK)
        y, state = _retention_chunk(q_ref[rs, :], k_ref[rs, :], v_ref[rs, :], g_ref[rs, :], state, ret_tabs)
        out = mix_ref.dtype
        mix_ref[rs, 0:MIX] = y.astype(out)
        mix_ref[rs, 2 * MIX:3 * MIX] = _conv_rows(ctx, cc * RET_CHUNK, RET_CHUNK, cw_ref, cb_ref, clg_ref, clb_ref).astype(out)
        mix_ref[rs, 3 * MIX:4 * MIX] = _sgu_chunk(gu_ref[rs, :], gv_ref[rs, :], wms, sgu_tabs)[0].astype(out)
    mix_ref[:, MIX:2 * MIX] = jnp.zeros((rows, MIX), mix_ref.dtype)
    s_scr[...] = state
    ctx[0:CONV_CTX, :] = ctx[rows:rows + CONV_CTX, :]

    @pl.when(c == n_steps - 1)
    def _():
        sout_ref[0] = s_scr[...]


def _mixers_prompt(h, mix, ret_tabs, avg, gn_g, conv_args, sgu_args, *, seq, n_seq):
    assert RET_CHUNK == SGU_CHUNK and seq % MIX_ROWS == 0
    rows = MIX_ROWS
    n_steps = seq // rows
    dec, qdec, kdec, cmat, bdm = ret_tabs
    cw, cb, clg, clb = conv_args
    sg, sb, ws, bias = sgu_args

    def hspec(col):
        return pl.BlockSpec((rows, MIX), lambda b, c: (b * n_steps + c, col))

    def const(arr):
        return pl.BlockSpec(arr.shape, lambda b, c: (0,) * arr.ndim)

    tables = [dec, qdec, kdec, cmat, bdm, avg, gn_g, cw, cb, clg, clb, sg, sb, ws, bias]
    cols = [COL_RQ, COL_RK, COL_RV, COL_RG, COL_CV, COL_CG, COL_GU, COL_GV]
    nbytes = (2 * (len(cols) + 5) * rows * MIX * 4 + 2 * sum(t.size * 4 for t in tables)
              + (CONV_CTX + rows) * MIX * 4 + MIX * MIX * 4)
    return pl.pallas_call(
        functools.partial(_mixers_body, n_steps=n_steps, rows=rows),
        grid=(n_seq, n_steps),
        in_specs=[hspec(col) for col in cols] + [const(t) for t in tables] + [pl.BlockSpec(memory_space=pl.ANY)],
        out_specs=[pl.BlockSpec((rows, 4 * MIX), lambda b, c: (b * n_steps + c, 0)),
                   pl.BlockSpec((rows, MIX), lambda b, c: (b * n_steps + c, 0)),
                   pl.BlockSpec((1, MIX, MIX), lambda b, c: (b, 0, 0))],
        out_shape=[jax.ShapeDtypeStruct(mix.shape, mix.dtype),
                   jax.ShapeDtypeStruct((n_seq * seq, MIX), F32),
                   jax.ShapeDtypeStruct((n_seq, MIX, MIX), F32)],
        scratch_shapes=[pltpu.VMEM((MIX, MIX), F32), pltpu.VMEM((CONV_CTX + rows, MIX), F32)],
        input_output_aliases={len(cols) + len(tables): 0},
        compiler_params=_params(("arbitrary", "arbitrary"), nbytes),
        name="mixers_prompt",
    )(*([h] * len(cols)), *tables, mix)


def _mixers_sample_body(q_ref, k_ref, v_ref, g_ref, cv_ref, cg_ref, gu_ref, gv_ref, s0_ref, buf_ref,
                        dec_ref, qdec_ref, kdec_ref, cmat_ref, bdm_ref, avg_ref, gn_ref,
                        cw_ref, cb_ref, clg_ref, clb_ref, sg_ref, sb_ref, ws_ref, bias_ref, mix_in,
                        mix_ref, glu_ref, vn_ref, sout_ref, ctx, *, seqs, rows):
    del mix_in
    ret_tabs = (dec_ref, qdec_ref, kdec_ref, cmat_ref, bdm_ref, avg_ref, gn_ref)
    sgu_tabs = (sg_ref, sb_ref, bias_ref, avg_ref)
    wms = _sgu_weights(ws_ref)
    glu = cv_ref[...] * _sigmoid(cg_ref[...])
    glu_ref[...] = glu
    ret, conv, gate = [], [], []
    for s in range(seqs):
        rs = slice(s * rows, (s + 1) * rows)
        y, sout_ref[s] = _retention_chunk(q_ref[rs, :], k_ref[rs, :], v_ref[rs, :], g_ref[rs, :], s0_ref[s], ret_tabs)
        ret.append(y)
        ctx[s, 0:CONV_CTX, :] = buf_ref[s]
        ctx[s, CONV_CTX:CONV_CTX + rows, :] = glu[rs, :]
        conv.append(_conv_rows(ctx.at[s], 0, rows, cw_ref, cb_ref, clg_ref, clb_ref))
        y, vn_ref[rs, :] = _sgu_chunk(gu_ref[rs, :], gv_ref[rs, :], wms, sgu_tabs)
        gate.append(y)
    blocks = (ret, [jnp.zeros((rows, MIX), F32)] * seqs, conv, gate)
    for col, parts in enumerate(blocks):
        mix_ref[:, col * MIX:(col + 1) * MIX] = jnp.concatenate(parts, axis=0).astype(mix_ref.dtype)


def _mixers_sample(h, mix, s0, buf, ret_tabs, avg, gn_g, conv_args, sgu_args, *, rows, n_seq, seqs, row0):
    blk = seqs * rows
    rb0 = row0 // blk
    dec, qdec, kdec, cmat, bdm = ret_tabs
    cw, cb, clg, clb = conv_args
    sg, sb, ws, bias = sgu_args

    def hspec(col):
        return pl.BlockSpec((blk, MIX), lambda b: (rb0 + b, col))

    def const(arr):
        return pl.BlockSpec(arr.shape, lambda b: (0,) * arr.ndim)

    per_seq = lambda arr: pl.BlockSpec((seqs,) + arr.shape[1:], lambda b: (b,) + (0,) * (arr.ndim - 1))
    tables = [dec, qdec, kdec, cmat, bdm, avg, gn_g, cw, cb, clg, clb, sg, sb, ws, bias]
    cols = [COL_RQ, COL_RK, COL_RV, COL_RG, COL_CV, COL_CG, COL_GU, COL_GV]
    rows_out = lambda width: pl.BlockSpec((blk, width), lambda b: (b, 0))
    nbytes = (2 * (len(cols) + 6) * blk * MIX * 4 + 2 * sum(t.size * 4 for t in tables)
              + 4 * seqs * MIX * MIX * 4 + 3 * seqs * (CONV_CTX + rows) * MIX * 4)
    return pl.pallas_call(
        functools.partial(_mixers_sample_body, seqs=seqs, rows=rows),
        grid=(n_seq // seqs,),
        in_specs=([hspec(col) for col in cols] + [per_seq(s0), per_seq(buf)] + [const(t) for t in tables]
                  + [pl.BlockSpec(memory_space=pl.ANY)]),
        out_specs=[pl.BlockSpec((blk, 4 * MIX), lambda b: (rb0 + b, 0)), rows_out(MIX), rows_out(MIX), per_seq(s0)],
        out_shape=[jax.ShapeDtypeStruct(mix.shape, mix.dtype),
                   jax.ShapeDtypeStruct((n_seq * rows, MIX), F32),
                   jax.ShapeDtypeStruct((n_seq * rows, MIX), F32),
                   jax.ShapeDtypeStruct(s0.shape, F32)],
        scratch_shapes=[pltpu.VMEM((seqs, CONV_CTX + rows, MIX), F32)],
        input_output_aliases={len(cols) + 2 + len(tables): 0},
        compiler_params=_params(("arbitrary",), nbytes),
        name="mixers_sample",
    )(*([h] * len(cols)), s0, buf, *tables, mix)


def _out_router_body(mix_ref, w_ref, x_ref, g_ref, b_ref, wr_ref, br_ref,
                     x1_ref, route_ref, counts_ref, wbf, wrbf, logit_scr, *, alpha):
    @pl.when(pl.program_id(0) == 0)
    def _():
        wbf[...] = w_ref[0].astype(BF16)
        wrbf[...] = wr_ref[...].astype(BF16)

    parts = [slice(r0, r0 + OUT_ROWS) for r0 in range(0, TM, OUT_ROWS)]
    ys = [_dot(mix_ref[rs, :], wbf[...]) for rs in parts]
    for rs, y in zip(parts, ys):
        x1 = _layer_norm(alpha * x_ref[rs, :] + y, g_ref[...], b_ref[...])
        x1_ref[rs, :] = x1
        logit_scr[:, rs] = _dot_nt(wrbf[...], x1.astype(BF16)) + br_ref[...]
    logits = logit_scr[0:ROUTE_ROWS, :]
    row = lax.broadcasted_iota(I32, (ROUTE_ROWS, 1), 0).astype(F32)

    def first_max(mask, vals):
        masked = jnp.where(mask, vals, NEG_INF)
        top = jnp.max(masked, axis=0, keepdims=True)
        idx = jnp.min(jnp.where(mask & (masked == top), row, float(LANE)), axis=0, keepdims=True)
        return top, idx

    gmask = row < N_GROUPS
    gmax, gsel = first_max(gmask, logits)
    g_gate = 1.0 / jnp.sum(jnp.where(gmask, jnp.exp(logits - gmax), 0.0), axis=0, keepdims=True)
    lo = N_GROUPS + EXPERTS_PER_GROUP * gsel
    emask = (row >= lo) & (row < lo + EXPERTS_PER_GROUP)
    v1, i1 = first_max(emask, logits)
    v2, i2 = first_max(emask & (row != i1), logits)
    e12 = jnp.exp(v2 - v1)
    both = (row == i1).astype(F32) + (row == i2).astype(F32)

    earlier = (lax.broadcasted_iota(I32, (TD, TD), 0) < lax.broadcasted_iota(I32, (TD, TD), 1)).astype(BF16)
    rank_a, rank_b = [], []
    for td in range(TM // TD):
        cs = slice(td * TD, (td + 1) * TD)
        tile = both[:, cs]
        counts_ref[td] = jnp.sum(tile, axis=1, keepdims=True)
        before = _dot(tile.astype(BF16), earlier)
        rank_a.append(jnp.sum(jnp.where(row == i1[:, cs], before, 0.0), axis=0, keepdims=True))
        rank_b.append(jnp.sum(jnp.where(row == i2[:, cs], before, 0.0), axis=0, keepdims=True))
    values = (i1 - N_GROUPS, i2 - N_GROUPS, g_gate * (1.0 / (1.0 + e12)), g_gate * (e12 / (1.0 + e12)),
              jnp.concatenate(rank_a, axis=1), jnp.concatenate(rank_b, axis=1))
    out_row = lax.broadcasted_iota(I32, (SUBLANE, 1), 0)
    route = jnp.zeros((SUBLANE, TM), F32)
    for k, val in enumerate(values):
        route = jnp.where(out_row == k, val, route)
    route_ref[...] = route


def _out_router(mix, w_out, layer, x, g, b, wr, br, *, alpha):
    nt, d = x.shape
    kdim = mix.shape[1]
    const = lambda shape: pl.BlockSpec(shape, lambda i: (0, 0))
    row = lambda width: pl.BlockSpec((TM, width), lambda i: (i, 0))
    nbytes = 2 * (kdim * d * 4 + 2 * d * LANE * 4 + TM * kdim * 4 + 2 * TM * d * 4 + TM * LANE * 4) + kdim * d * 2
    return pl.pallas_call(
        functools.partial(_out_router_body, alpha=alpha),
        grid=(nt // TM,),
        in_specs=[row(kdim), pl.BlockSpec((1, kdim, d), lambda i: (layer, 0, 0)), row(d),
                  const((1, d)), const((1, d)), const((LANE, d)), const((LANE, 1))],
        out_specs=[row(d), pl.BlockSpec((SUBLANE, TM), lambda i: (0, i)),
                   pl.BlockSpec((TM // TD, ROUTE_ROWS, 1), lambda i: (i, 0, 0))],
        out_shape=[jax.ShapeDtypeStruct((nt, d), F32), jax.ShapeDtypeStruct((SUBLANE, nt), F32),
                   jax.ShapeDtypeStruct((nt // TD, ROUTE_ROWS, 1), F32)],
        scratch_shapes=[pltpu.VMEM((kdim, d), BF16), pltpu.VMEM((LANE, d), BF16), pltpu.VMEM((LANE, TM), F32)],
        compiler_params=_params(("arbitrary",), nbytes),
        name="out_router",
    )(mix, w_out, x, g, b, wr, br)


def _local_slots(route, off_col):
    expert = lax.broadcasted_iota(I32, (N_EXPERTS, 1), 0).astype(F32)
    slots = []
    for kk in range(TOP_K):
        start = jnp.sum(jnp.where(expert == route[kk:kk + 1, :], off_col, 0.0), axis=0, keepdims=True)
        slots.append(start + route[4 + kk:5 + kk, :])
    return slots


def _pack_rows(y):
    half = y.shape[1] // 2
    bits = lax.bitcast_convert_type(y.astype(BF16).astype(F32), U32)
    return (bits[:, :half] & jnp.uint32(0xFFFF0000)) | (bits[:, half:] >> 16)


def _unpack_rows(w):
    hi = lax.bitcast_convert_type(w & jnp.uint32(0xFFFF0000), F32)
    lo = lax.bitcast_convert_type(w << 16, F32)
    return jnp.concatenate([hi, lo], axis=1)


def _chunk_copy(hbm, hbm_row, loc, chunk, sem, *, to_hbm):
    vm = loc.at[pl.ds(pl.multiple_of(chunk * MOE_CHUNK, MOE_CHUNK), MOE_CHUNK), :]
    hb = hbm.at[pl.ds(pl.multiple_of(hbm_row, MOE_CHUNK), MOE_CHUNK), :]
    return pltpu.make_async_copy(vm, hb, sem) if to_hbm else pltpu.make_async_copy(hb, vm, sem)


def _for_each_chunk(count, fn):
    def group(t, carry):
        for u in range(CHUNK_UNROLL):
            fn(t * CHUNK_UNROLL + u)
        return carry

    def single(c, carry):
        fn(c)
        return carry

    whole = count // CHUNK_UNROLL
    lax.fori_loop(0, whole, group, 0)
    lax.fori_loop(whole * CHUNK_UNROLL, count, single, 0)


def _wait_chunks(hbm, buf, sem, count, *, to_hbm):
    _for_each_chunk(count, lambda c: _chunk_copy(hbm, 0, buf, 0, sem, to_hbm=to_hbm).wait())


def _dispatch_body(dst_tab, n_chunks, x_ref, route_ref, off_ref, xs_in, xs_hbm, loc, sem):
    del xs_in
    i = pl.program_id(0)
    last = pl.num_programs(0) - 1
    cur = i % 2
    slot_a, slot_b = _local_slots(route_ref[...], off_ref[0])
    buf_row = lax.broadcasted_iota(I32, (LOC_ROWS, 1), 0).astype(F32)
    onehot = ((buf_row == slot_a) | (buf_row == slot_b)).astype(BF16)
    loc[cur] = _pack_rows(_dot(onehot, x_ref[...].astype(BF16)))

    _for_each_chunk(n_chunks[i], lambda c: _chunk_copy(
        xs_hbm, dst_tab[i * MAX_CHUNKS + c], loc.at[cur], c, sem.at[cur], to_hbm=True).start())

    @pl.when(i > 0)
    def _():
        _wait_chunks(xs_hbm, loc.at[1 - cur], sem.at[1 - cur], n_chunks[jnp.maximum(i - 1, 0)], to_hbm=True)

    @pl.when(i == last)
    def _():
        _wait_chunks(xs_hbm, loc.at[cur], sem.at[cur], n_chunks[i], to_hbm=True)


def _dispatch(dst_tab, n_chunks, x1, route, off, xs_prev):
    nt, d = x1.shape
    row = lambda width: pl.BlockSpec((TD, width), lambda i, *_: (i, 0))
    nbytes = 2 * (TD * d * 4 + TD * LANE * 4) + 2 * LOC_ROWS * d * 4 + 4 * TD * LOC_ROWS * 4 + LOC_ROWS * d * 4
    return pl.pallas_call(
        _dispatch_body,
        grid_spec=pltpu.PrefetchScalarGridSpec(
            num_scalar_prefetch=2,
            grid=(nt // TD,),
            in_specs=[row(d), pl.BlockSpec((SUBLANE, TD), lambda i, *_: (0, i)),
                      pl.BlockSpec((1, N_EXPERTS, 1), lambda i, *_: (i, 0, 0)),
                      pl.BlockSpec(memory_space=pl.ANY)],
            out_specs=pl.BlockSpec(memory_space=pl.ANY),
            scratch_shapes=[pltpu.VMEM((2, LOC_ROWS, d // 2), U32), pltpu.SemaphoreType.DMA((2,))]),
        out_shape=jax.ShapeDtypeStruct(xs_prev.shape, U32),
        input_output_aliases={5: 0},
        compiler_params=_params(("arbitrary",), nbytes),
        name="moe_dispatch",
    )(dst_tab, n_chunks, x1, route, off, xs_prev)


def _ffn_body(blk_e, blk_valid, blk_src, blk_slot, blk_next, x_ref, w1_hbm, w3_hbm, w2_hbm, y_ref,
              w1f, w3f, w2f, w1b, w3b, w2b, sem, *, layer):
    del blk_src
    j = pl.program_id(0)
    n_valid = blk_valid[j]
    half = MOE_BLK // 2
    slot = blk_slot[j]

    def weight_copies(expert, buf):
        return [pltpu.make_async_copy(hbm.at[layer, expert], vm.at[buf], sem.at[buf])
                for hbm, vm in ((w1_hbm, w1f), (w3_hbm, w3f), (w2_hbm, w2f))]

    @pl.when(j == 0)
    def _():
        for cp in weight_copies(blk_e[0], slot):
            cp.start()

    @pl.when((n_valid > 0) & ((j == 0) | (blk_e[j] != blk_e[jnp.maximum(j - 1, 0)])))
    def _():
        for cp in weight_copies(blk_e[j], slot):
            cp.wait()
        w1b[...] = w1f[slot].astype(BF16)
        w3b[...] = w3f[slot].astype(BF16)
        w2b[...] = w2f[slot].astype(BF16)

        @pl.when(blk_next[j] >= 0)
        def _():
            for cp in weight_copies(jnp.maximum(blk_next[j], 0), 1 - slot):
                cp.start()

    def ffn(rows):
        xb = _unpack_rows(x_ref[rows, :]).astype(BF16)
        h1 = _dot(xb, w1b[...])
        h3 = _dot(xb, w3b[...])
        act = (h1 * _sigmoid(h1)) * h3
        y_ref[rows, :] = _pack_rows(_dot(act.astype(BF16), w2b[...]))

    @pl.when(n_valid > half)
    def _():
        ffn(slice(0, MOE_BLK))

    @pl.when((n_valid > 0) & (n_valid <= half))
    def _():
        ffn(slice(0, half))
        y_ref[half:MOE_BLK, :] = jnp.zeros((half, y_ref.shape[1]), U32)


def _ffn(xs, w1, w3, w2, layer, blk_e, blk_valid, blk_src, blk_slot, blk_next):
    d = w1.shape[2]
    de = w1.shape[3]
    n_blk = xs.shape[0] // MOE_BLK
    rows = pl.BlockSpec((MOE_BLK, d // 2), lambda j, be, bv, bs, sl, nx: (bs[j], 0))
    nbytes = 2 * (3 * d * de * 4 + MOE_BLK * d * 4) + 3 * d * de * 2 + 6 * MOE_BLK * de * 4 + 4 * MOE_BLK * d * 4
    anywhere = pl.BlockSpec(memory_space=pl.ANY)
    return pl.pallas_call(
        functools.partial(_ffn_body, layer=layer),
        grid_spec=pltpu.PrefetchScalarGridSpec(
            num_scalar_prefetch=5,
            grid=(n_blk,),
            in_specs=[rows, anywhere, anywhere, anywhere],
            out_specs=rows,
            scratch_shapes=[pltpu.VMEM((2, d, de), F32), pltpu.VMEM((2, d, de), F32), pltpu.VMEM((2, de, d), F32),
                            pltpu.VMEM((d, de), BF16), pltpu.VMEM((d, de), BF16), pltpu.VMEM((de, d), BF16),
                            pltpu.SemaphoreType.DMA((2,))]),
        out_shape=jax.ShapeDtypeStruct(xs.shape, U32),
        input_output_aliases={5: 0},
        compiler_params=_params(("arbitrary",), nbytes),
        name="expert_ffn",
    )(blk_e, blk_valid, blk_src, blk_slot, blk_next, xs, w1, w3, w2)


def _combine_body(dst_tab, n_chunks, x_ref, route_ref, off_ref, g_ref, b_ref, ys_hbm, x2_ref,
                  *rest, alpha, head_tiles):
    *tail_ref, loc, sem = rest
    i = pl.program_id(0)
    last = pl.num_programs(0) - 1
    cur = i % 2

    def fetch(tile, buf):
        _for_each_chunk(n_chunks[tile], lambda c: _chunk_copy(
            ys_hbm, dst_tab[tile * MAX_CHUNKS + c], loc.at[buf], c, sem.at[buf], to_hbm=False).start())

    @pl.when(i == 0)
    def _():
        loc[...] = jnp.zeros_like(loc)
        fetch(0, 0)

    @pl.when(i < last)
    def _():
        fetch(jnp.minimum(i + 1, last), 1 - cur)

    route = route_ref[...]
    slot_a, slot_b = _local_slots(route, off_ref[0])
    buf_row = lax.broadcasted_iota(I32, (LOC_ROWS, 1), 0).astype(F32)
    gates = (jnp.where(buf_row == slot_a, route[2:3, :], 0.0)
             + jnp.where(buf_row == slot_b, route[3:4, :], 0.0))
    _wait_chunks(ys_hbm, loc.at[cur], sem.at[cur], n_chunks[i], to_hbm=False)
    f = _dot_tn(gates.astype(BF16), _unpack_rows(loc[cur]).astype(BF16))
    x2 = _layer_norm(alpha * x_ref[...] + f, g_ref[...], b_ref[...])
    if tail_ref:
        @pl.when(i < head_tiles)
        def _():
            x2_ref[...] = x2

        @pl.when(i >= head_tiles)
        def _():
            tail_ref[0][...] = x2
    else:
        x2_ref[...] = x2


def _combine(dst_tab, n_chunks, x1, route, off, g, b, ys, *, alpha, head_rows=None):
    nt, d = x1.shape
    row = lambda width: pl.BlockSpec((TD, width), lambda i, dt, nc: (i, 0))
    const = lambda shape: pl.BlockSpec(shape, lambda i, dt, nc: (0, 0))
    nbytes = 2 * (3 * TD * d * 4 + TD * LANE * 4) + 2 * LOC_ROWS * d * 4 + 4 * TD * LOC_ROWS * 4 + LOC_ROWS * d * 2
    if head_rows is None:
        head_tiles = nt // TD
        out_specs = row(d)
        out_shape = jax.ShapeDtypeStruct((nt, d), F32)
    else:
        head_tiles = head_rows // TD
        out_specs = [pl.BlockSpec((TD, d), lambda i, dt, nc: (jnp.minimum(i, head_tiles - 1), 0)),
                     pl.BlockSpec((TD, d), lambda i, dt, nc: (jnp.maximum(i - head_tiles, 0), 0))]
        out_shape = [jax.ShapeDtypeStruct((head_rows, d), F32), jax.ShapeDtypeStruct((nt - head_rows, d), F32)]
    scratch = [pltpu.VMEM((2, LOC_ROWS, d // 2), U32), pltpu.SemaphoreType.DMA((2,))]
    return pl.pallas_call(
        functools.partial(_combine_body, alpha=alpha, head_tiles=head_tiles),
        grid_spec=pltpu.PrefetchScalarGridSpec(
            num_scalar_prefetch=2,
            grid=(nt // TD,),
            in_specs=[row(d), pl.BlockSpec((SUBLANE, TD), lambda i, dt, nc: (0, i)),
                      pl.BlockSpec((1, N_EXPERTS, 1), lambda i, dt, nc: (i, 0, 0)),
                      const((1, d)), const((1, d)), pl.BlockSpec(memory_space=pl.ANY)],
            out_specs=out_specs,
            scratch_shapes=scratch),
        out_shape=out_shape,
        compiler_params=_params(("arbitrary",), nbytes),
        name="combine_ln",
    )(dst_tab, n_chunks, x1, route, off, g, b, ys)


def _rope_tables(pos, rot_dim, theta):
    half = rot_dim // 2
    inv = jnp.float32(theta) ** (-jnp.arange(half, dtype=F32) * (2.0 / rot_dim))
    ang = pos.astype(F32)[:, None] * inv[None, :]
    c, s = jnp.cos(ang), jnp.sin(ang)
    rest = HEAD_DIM - rot_dim
    ones = jnp.ones((pos.shape[0], rest), F32)
    zeros = jnp.zeros((pos.shape[0], rest), F32)
    cos_h = jnp.concatenate([c, c, ones], axis=1)
    sin_h = jnp.concatenate([-s, s, zeros], axis=1)
    return jnp.tile(cos_h, (1, N_HEADS)), jnp.tile(sin_h, (1, N_HEADS))


def _block_diag(s):
    rows = [jnp.pad(s[:, h], ((0, 0), (0, 0), (h * HEAD_DIM, MIX - (h + 1) * HEAD_DIM))) for h in range(N_HEADS)]
    return jnp.concatenate(rows, axis=1)


def _diag_blocks(s):
    return jnp.stack([s[:, h * HEAD_DIM:(h + 1) * HEAD_DIM, h * HEAD_DIM:(h + 1) * HEAD_DIM]
                      for h in range(N_HEADS)], axis=1)


def _routing_plan(tile_counts, n_blk):
    cnt_t = tile_counts[:, N_GROUPS:N_GROUPS + N_EXPERTS, 0].astype(I32)
    run_chunks = (cnt_t + MOE_CHUNK - 1) // MOE_CHUNK
    run_rows = run_chunks * MOE_CHUNK
    cnt = jnp.sum(run_rows, axis=0)
    pcnt = (cnt + MOE_BLK - 1) // MOE_BLK * MOE_BLK
    pstart = jnp.cumsum(pcnt) - pcnt
    n_act = jnp.sum(pcnt) // MOE_BLK
    gstart = pstart[None, :] + jnp.cumsum(run_rows, axis=0) - run_rows
    chunk0 = jnp.cumsum(run_chunks, axis=1) - run_chunks
    n_chunks = jnp.sum(run_chunks, axis=1).astype(I32)

    def spread(first, count, value, n):
        p = jnp.arange(n, dtype=I32)[:, None]
        f, c, v = first[..., None, :], count[..., None, :], value[..., None, :]
        return jnp.sum(jnp.where((f <= p) & (p < f + c), v + (p - f) * MOE_CHUNK, 0), axis=-1)

    dst_tab = spread(chunk0, run_chunks, gstart, MAX_CHUNKS)
    off = (chunk0 * MOE_CHUNK).astype(F32)[:, :, None]
    pos = jnp.arange(n_blk, dtype=I32)[:, None] * MOE_BLK
    mine = (pstart[None, :] <= pos) & (pos < (pstart + pcnt)[None, :])
    experts = jnp.arange(N_EXPERTS, dtype=I32)[None, :]
    last_e = jnp.max(jnp.where(cnt > 0, experts[0], 0))
    active = pos[:, 0] < n_act * MOE_BLK
    blk_e = jnp.where(active, jnp.sum(jnp.where(mine, experts, 0), axis=-1), last_e)
    blk_valid = jnp.sum(jnp.where(mine, jnp.clip((pstart + cnt)[None, :] - pos, 0, MOE_BLK), 0), axis=-1)
    blk_src = jnp.minimum(jnp.arange(n_blk, dtype=I32), jnp.maximum(n_act - 1, 0))
    i32 = lambda t: t.astype(I32)
    used = cnt > 0
    slot_e = (jnp.cumsum(used.astype(I32)) - 1) % 2
    later = (experts > experts.T) & used[None, :]
    next_e = jnp.min(jnp.where(later, experts, N_EXPERTS), axis=-1)
    next_e = jnp.where(next_e < N_EXPERTS, next_e, -1)
    return (i32(dst_tab).reshape(-1), n_chunks, off, i32(blk_e), i32(blk_valid), i32(blk_src),
            i32(slot_e[blk_e]), i32(next_e[blk_e]))


def kernel(x_prompt, x_sample, state_ret, cache_swa_k, cache_swa_v, state_conv, w_in, w_out, ret_gn_g,
           conv_w, conv_b, conv_ln_g, conv_ln_b, sgu_ln_g, sgu_ln_b, sgu_w, sgu_b, ln1_g, ln1_b,
           ln2_g, ln2_b, router_g_w, router_g_b, router_e_w, router_e_b, moe_w1, moe_w3, moe_w2):
    depth = w_in.shape[0]
    n_seq, seq, d = x_prompt.shape
    n_dec, t_new, _ = x_sample.shape
    past = cache_swa_k.shape[2]
    rows_s = -(-t_new // SUBLANE) * SUBLANE
    n_p = n_seq * seq
    n_s = n_dec * rows_s
    n_tail = -(-n_s // TM) * TM
    nt = n_p + n_tail
    assert seq % SWA_SB == 0 and seq % TM == 0 and TM % TD == 0
    assert t_new % RET_CHUNK != 0 and t_new <= SGU_CHUNK and w_in.shape[2] == N_COLBLK * MIX
    assert n_dec % SAMPLE_SEQS == 0 and n_dec % SWA_SAMPLE_SEQS == 0
    assert n_p % (SAMPLE_SEQS * rows_s) == 0
    alpha = (2 * depth) ** 0.25
    keep = min(DILATED_CONFIGS[-1][0], seq)
    n_blk = -(-(nt * TOP_K + (nt // TD) * N_EXPERTS * (MOE_CHUNK - 1) + N_EXPERTS * (MOE_BLK - 1)) // MOE_BLK)
    cache_k = jnp.transpose(cache_swa_k, (0, 1, 3, 4, 2)).reshape(depth, n_dec, MIX, past)
    cache_v = jnp.transpose(cache_swa_v, (0, 1, 3, 4, 2)).reshape(depth, n_dec, MIX, past)

    x = jnp.concatenate([x_prompt.reshape(n_p, d),
                         jnp.pad(x_sample, ((0, 0), (0, rows_s - t_new), (0, 0))).reshape(n_s, d),
                         jnp.zeros((n_tail - n_s, d), F32)], axis=0)
    pos = jnp.concatenate([jnp.arange(seq, dtype=I32),
                           PAST_LEN + jnp.arange(n_tail, dtype=I32) % rows_s])
    rope = _rope_tables(pos, HEAD_DIM, RET_THETA) + _rope_tables(pos, ROPE_DIM, ROPE_THETA)
    ret_tabs_p = _retention_tables(RET_CHUNK, RET_CHUNK)
    ret_tabs_s = _retention_tables(t_new, rows_s)
    lane_head = jnp.arange(MIX) // HEAD_DIM
    avg = ((lane_head[:, None] == lane_head[None, :]).astype(F32) / HEAD_DIM).astype(BF16)
    row2 = lambda v: v.reshape(1, -1)
    mix = jnp.zeros((nt, 4 * MIX), BF16)
    slots = jnp.zeros((n_blk * MOE_BLK, d // 2), U32)

    outs = [[] for _ in range(9)]
    for l in range(depth):
        h = _in_proj(x, w_in, l, rope, n_p // TM, seq // TM)
        gn = row2(ret_gn_g[l])
        cw = jnp.pad(conv_w[l], ((0, CONV_CTX - CONV_WIDTH), (0, 0)))
        cargs = (cw, row2(conv_b[l]), row2(conv_ln_g[l]), row2(conv_ln_b[l]))
        sbias = jnp.repeat(sgu_b[l].T, HEAD_DIM, axis=1)
        sargs = (row2(sgu_ln_g[l]), row2(sgu_ln_b[l]), sgu_w[l], sbias)
        mix, glu_p, s_p = _mixers_prompt(h, mix, ret_tabs_p, avg, gn, cargs, sargs, seq=seq, n_seq=n_seq)
        mix = _swa_prompt(h, mix, seq=seq, n_seq=n_seq)
        ctx_s = jnp.pad(state_conv[l], ((0, 0), (CONV_CTX - (CONV_WIDTH - 1), 0), (0, 0)))
        mix, glu_s, vn_s, s_s = _mixers_sample(h, mix, _block_diag(state_ret[l]), ctx_s, ret_tabs_s, avg, gn, cargs,
                                               sargs, rows=rows_s, n_seq=n_dec, seqs=SAMPLE_SEQS, row0=n_p)
        mix = _swa_sample(h, mix, cache_k, cache_v, l, rows=rows_s, n_seq=n_dec, seqs=SWA_SAMPLE_SEQS, row0=n_p)

        wr = jnp.concatenate([router_g_w[l].T, jnp.transpose(router_e_w[l], (0, 2, 1)).reshape(N_EXPERTS, d),
                              jnp.zeros((LANE - N_GROUPS - N_EXPERTS, d), F32)], axis=0)
        br = jnp.concatenate([router_g_b[l], router_e_b[l].reshape(-1),
                              jnp.zeros((LANE - N_GROUPS - N_EXPERTS,), F32)]).reshape(LANE, 1)
        x1, route, counts = _out_router(mix, w_out, l, x, row2(ln1_g[l]), row2(ln1_b[l]), wr, br, alpha=alpha)
        dst_tab, n_chunks, off, blk_e, blk_valid, blk_src, blk_slot, blk_next = _routing_plan(counts, n_blk)
        slots = _dispatch(dst_tab, n_chunks, x1, route, off, slots)
        slots = _ffn(slots, moe_w1, moe_w3, moe_w2, l, blk_e, blk_valid, blk_src, blk_slot, blk_next)
        x = _combine(dst_tab, n_chunks, x1, route, off, row2(ln2_g[l]), row2(ln2_b[l]), slots, alpha=alpha,
                     head_rows=n_p if l == depth - 1 else None)

        def tail(col, n_keep):
            return jnp.stack([lax.slice(h, ((b + 1) * seq - n_keep, col * MIX), ((b + 1) * seq, (col + 1) * MIX))
                              for b in range(n_seq)])

        def new_rows(col):
            blk = lax.slice(h, (n_p, col * MIX), (n_p + n_s, (col + 1) * MIX))
            return blk.reshape(n_dec, rows_s, MIX)[:, :t_new]

        heads = lambda t: t.reshape(t.shape[0], t.shape[1], N_HEADS, HEAD_DIM)
        outs[0].append(_diag_blocks(s_p))
        outs[1].append(heads(tail(COL_SK, keep)))
        outs[2].append(heads(tail(COL_SV, keep)))
        outs[3].append(glu_p.reshape(n_seq, seq, MIX)[:, seq - (CONV_WIDTH - 1):])
        outs[4].append(_diag_blocks(s_s))
        outs[5].append(heads(new_rows(COL_SK)))
        outs[6].append(heads(new_rows(COL_SV)))
        outs[7].append(glu_s.reshape(n_dec, rows_s, MIX)[:, :t_new])
        outs[8].append(vn_s.reshape(n_dec, rows_s, MIX)[:, :t_new])

    x_head, x_tail = x
    y_prompt = x_head.reshape(n_seq, seq, d)
    y_sample = x_tail[:n_s].reshape(n_dec, rows_s, d)[:, :t_new]
    return (y_prompt, y_sample) + tuple(jnp.stack(o) for o in outs)
```

```python
import functools
import math

import jax
import jax.numpy as jnp
from jax import lax
from jax.experimental import pallas as pl
from jax.experimental.pallas import tpu as pltpu

F32 = jnp.float32
BF16 = jnp.bfloat16
I32 = jnp.int32
U32 = jnp.uint32

HEAD_DIM = 64
N_HEADS = 4
MIX = N_HEADS * HEAD_DIM
(COL_RQ, COL_RK, COL_RV, COL_RG, COL_SQ, COL_SK, COL_SV,
 COL_CV, COL_CG, COL_GU, COL_GV) = range(11)
N_COLBLK = 11
RET_CHUNK = 128
RET_THETA = 10000.0
ROPE_THETA = 500000.0
ROPE_DIM = HEAD_DIM // 4
DILATED_CONFIGS = ((128, 1), (512, 4), (2048, 16))
DIL_BLOCK = 128
CONV_WIDTH = 31
SGU_CHUNK = 128
N_GROUPS = 4
EXPERTS_PER_GROUP = 8
N_EXPERTS = N_GROUPS * EXPERTS_PER_GROUP
TOP_K = 2
PAST_LEN = 8192
LN_EPS = 1e-5
NEG_INF = -1e30

LANE = 128
SUBLANE = 8
VMEM_CAP_BYTES = 60000 * 1024
COMPILER_TEMP_BYTES = 12 * 1024 * 1024

TM = 512
TD = 256
OUT_ROWS = 128
ROUTE_ROWS = -(-(N_GROUPS + N_EXPERTS) // SUBLANE) * SUBLANE
MOE_BLK = 512
MOE_CHUNK = SUBLANE
LOC_ROWS = -(-(TD * TOP_K + N_EXPERTS * (MOE_CHUNK - 1)) // LANE) * LANE
MAX_CHUNKS = LOC_ROWS // MOE_CHUNK
CHUNK_UNROLL = 4
MIX_ROWS = 1024
CONV_CTX = 32
SWA_SB = DIL_BLOCK * 16
SWA_TILE = 256
SWA_UNROLL = 16
SAMPLE_SEQS = 8
SWA_SAMPLE_SEQS = 4


def _vmem(nbytes):
    return int(min(VMEM_CAP_BYTES, nbytes + COMPILER_TEMP_BYTES))


def _params(sem, nbytes):
    return pltpu.CompilerParams(dimension_semantics=sem, vmem_limit_bytes=_vmem(nbytes))


def _dot(a, b):
    return jnp.dot(a, b, preferred_element_type=F32)


def _dot_nt(a, b):
    return lax.dot_general(a, b, (((1,), (1,)), ((), ())), preferred_element_type=F32)


def _dot_tn(a, b):
    return lax.dot_general(a, b, (((0,), (0,)), ((), ())), preferred_element_type=F32)


def _split_dot(x, m):
    hi = x.astype(BF16)
    lo = (x - hi.astype(F32)).astype(BF16)
    return _dot(hi, m) + _dot(lo, m)


def _sigmoid(x):
    return 1.0 / (1.0 + jnp.exp(-x))


def _layer_norm(z, g, b):
    mu = jnp.mean(z, axis=-1, keepdims=True)
    d = z - mu
    var = jnp.mean(d * d, axis=-1, keepdims=True)
    return d * lax.rsqrt(var + LN_EPS) * g + b


def _group_norm(z, avg):
    mu = _split_dot(z, avg)
    d = z - mu
    var = _split_dot(d * d, avg)
    return d * lax.rsqrt(var + LN_EPS)


def _lane_head(width):
    return lax.broadcasted_iota(I32, (1, width), 1) // HEAD_DIM


def _rotate(x, cos, sin_signed, half):
    first = (lax.broadcasted_iota(I32, (1, LANE), 1) % HEAD_DIM) < half
    parts = []
    for p in range(x.shape[1] // LANE):
        t = x[:, p * LANE:(p + 1) * LANE]
        up = pltpu.roll(t, half, 1)
        down = pltpu.roll(t, LANE - half, 1)
        parts.append(jnp.where(first, down, up))
    return x * cos + jnp.concatenate(parts, axis=1) * sin_signed


def _in_proj_body(x_ref, w_ref, cr_ref, sr_ref, cs_ref, ss_ref, h_ref, wbf_ref):
    @pl.when(pl.program_id(0) == 0)
    def _():
        for j in range(N_COLBLK):
            wbf_ref[:, j * MIX:(j + 1) * MIX] = w_ref[0, :, j * MIX:(j + 1) * MIX].astype(BF16)

    xb = x_ref[...].astype(BF16)
    for j in range(N_COLBLK):
        hj = _dot(xb, wbf_ref[:, j * MIX:(j + 1) * MIX])
        if j in (COL_RQ, COL_RK):
            hj = _rotate(hj, cr_ref[...], sr_ref[...], HEAD_DIM // 2)
        if j == COL_RK:
            hj = hj * (HEAD_DIM ** -0.5)
        if j in (COL_SQ, COL_SK):
            hj = _rotate(hj, cs_ref[...], ss_ref[...], ROPE_DIM // 2)
        h_ref[:, j * MIX:(j + 1) * MIX] = hj


def _in_proj(x, w, layer, tabs, n_prompt_tiles, tiles_per_seq):
    nt, d = x.shape
    width = w.shape[2]

    def tab_map(i):
        return (jnp.where(i < n_prompt_tiles, i % tiles_per_seq, tiles_per_seq + i - n_prompt_tiles), 0)

    tab_spec = pl.BlockSpec((TM, MIX), tab_map)
    nbytes = d * width * 4 + d * width * 2 + 2 * TM * d * 4 + 8 * TM * MIX * 4 + 2 * TM * width * 4
    return pl.pallas_call(
        _in_proj_body,
        grid=(nt // TM,),
        in_specs=[pl.BlockSpec((TM, d), lambda i: (i, 0)),
                  pl.BlockSpec((1, d, width), lambda i: (layer, 0, 0), pipeline_mode=pl.Buffered(1)),
                  tab_spec, tab_spec, tab_spec, tab_spec],
        out_specs=pl.BlockSpec((TM, width), lambda i: (i, 0)),
        out_shape=jax.ShapeDtypeStruct((nt, width), F32),
        scratch_shapes=[pltpu.VMEM((d, width), BF16)],
        compiler_params=_params(("arbitrary",), nbytes),
        name="in_proj",
    )(x, w, *tabs)


def _retention_chunk(q, k, v, g, state, tabs):
    dec_ref, qdec_ref, kdec_ref, cmat_ref, bdm_ref, avg_ref, gn_ref = tabs
    head = _lane_head(MIX)
    vb = v.astype(BF16)
    kb = k.astype(BF16)
    o = _dot(q.astype(BF16), state.astype(BF16)) * qdec_ref[...]
    for h in range(N_HEADS):
        mh = head == h
        a = _dot_nt(jnp.where(mh, q, 0.0).astype(BF16), kb) * dec_ref[h]
        o = o + jnp.where(mh, _dot(a.astype(BF16), vb), 0.0)
    upd = _dot_tn((k * kdec_ref[...]).astype(BF16), vb)
    state = state * cmat_ref[...] + upd * bdm_ref[...]
    return _group_norm(o, avg_ref[...]) * gn_ref[...] * (g * _sigmoid(g)), state


def _retention_tables(l_real, l_pad):
    hh = jnp.arange(N_HEADS, dtype=F32)
    log_g = jnp.log1p(-jnp.exp2(-5.0 - hh))
    i = jnp.arange(l_pad, dtype=F32)
    rel = i[:, None] - i[None, :]
    dec = jnp.where(rel >= 0, jnp.exp(log_g[:, None, None] * jnp.maximum(rel, 0.0)), 0.0)
    qd = jnp.exp(log_g[:, None] * (i + 1.0))
    kd = jnp.where(i < l_real, jnp.exp(log_g[:, None] * (l_real - 1.0 - i)), 0.0)
    cd = jnp.exp(log_g * l_real)
    expand = lambda t: jnp.repeat(t.T, HEAD_DIM, axis=1)
    row_head = jnp.arange(MIX) // HEAD_DIM
    bdm = (row_head[:, None] == row_head[None, :]).astype(F32)
    cmat = cd[row_head][:, None] * bdm
    return dec, expand(qd), expand(kd), cmat, bdm


def _swa_prompt_body(q_ref, k_ref, v_ref, mix_ref, y_ref, o0, o1, o2, l0, l1, l2, *, seq):
    del mix_ref
    o_scr = (o0, o1, o2)
    l_scr = (l0, l1, l2)
    head = _lane_head(LANE)
    qi = lax.broadcasted_iota(I32, (DIL_BLOCK, 1), 0)
    ki = lax.broadcasted_iota(I32, (1, 2 * DIL_BLOCK), 1)
    rel = DIL_BLOCK + qi - ki
    scale = HEAD_DIM ** -0.5
    for sb in range(seq // SWA_SB):
        for ci, (window, dil) in enumerate(DILATED_CONFIGS):
            span = window // dil
            band = (rel >= 0) & (rel <= span)

            def block(i, carry, sb=sb, ci=ci, dil=dil, band=band):
                nloc = i // dil
                off = nloc * (DIL_BLOCK * dil) + i % dil
                start = sb * SWA_SB + off
                if sb == 0:
                    first = nloc == 0
                    pstart = jnp.where(first, start, start - DIL_BLOCK * dil)
                    valid = band & (ki >= jnp.where(first, DIL_BLOCK, 0))
                else:
                    pstart = start - DIL_BLOCK * dil
                    valid = band
                stride = dil if dil > 1 else None
                cur = pl.ds(start, DIL_BLOCK, stride=stride)
                prev = pl.ds(pstart, DIL_BLOCK, stride=stride)
                qb = q_ref[cur, :] * scale
                kb = jnp.concatenate([k_ref[prev, :], k_ref[cur, :]], axis=0).astype(BF16)
                vb = jnp.concatenate([v_ref[prev, :], v_ref[cur, :]], axis=0).astype(BF16)
                o = jnp.zeros((DIL_BLOCK, LANE), F32)
                ls = jnp.zeros((DIL_BLOCK, LANE), F32)
                for hh in range(LANE // HEAD_DIM):
                    mh = head == hh
                    s = _dot_nt(jnp.where(mh, qb, 0.0).astype(BF16), kb)
                    s = jnp.where(valid, s, NEG_INF)
                    m = jnp.max(s, axis=-1, keepdims=True)
                    p = jnp.exp(s - m)
                    l = jnp.sum(p, axis=-1, keepdims=True)
                    o = jnp.where(mh, _dot(p.astype(BF16), vb) / l, o)
                    ls = jnp.where(mh, m + jnp.log(l), ls)
                dst = pl.ds(off, DIL_BLOCK, stride=stride)
                o_scr[ci][dst, :] = o
                l_scr[ci][dst, :] = ls
                return carry

            lax.fori_loop(0, SWA_SB // DIL_BLOCK, block, 0, unroll=SWA_UNROLL)

        def combine(t, carry, sb=sb):
            rows = pl.ds(t * SWA_TILE, SWA_TILE)
            la, lb, lc = l_scr[0][rows, :], l_scr[1][rows, :], l_scr[2][rows, :]
            mx = jnp.maximum(jnp.maximum(la, lb), lc)
            ea, eb, ec = jnp.exp(la - mx), jnp.exp(lb - mx), jnp.exp(lc - mx)
            num = ea * o_scr[0][rows, :] + eb * o_scr[1][rows, :] + ec * o_scr[2][rows, :]
            y_ref[pl.ds(sb * SWA_SB + t * SWA_TILE, SWA_TILE), :] = (num / (ea + eb + ec)).astype(y_ref.dtype)
            return carry

        lax.fori_loop(0, SWA_SB // SWA_TILE, combine, 0)


def _swa_prompt(h, mix, *, seq, n_seq):
    halves = MIX // LANE

    def hspec(col):
        return pl.BlockSpec((seq, LANE), lambda b, p: (b, col * halves + p))

    nbytes = 2 * 4 * seq * LANE * 4 + 6 * SWA_SB * LANE * 4
    return pl.pallas_call(
        functools.partial(_swa_prompt_body, seq=seq),
        grid=(n_seq, halves),
        in_specs=[hspec(COL_SQ), hspec(COL_SK), hspec(COL_SV), pl.BlockSpec(memory_space=pl.ANY)],
        out_specs=pl.BlockSpec((seq, LANE), lambda b, p: (b, halves + p)),
        out_shape=jax.ShapeDtypeStruct(mix.shape, mix.dtype),
        scratch_shapes=[pltpu.VMEM((SWA_SB, LANE), F32)] * 6,
        input_output_aliases={3: 0},
        compiler_params=_params(("arbitrary", "arbitrary"), nbytes),
        name="swa_prompt",
    )(h, h, h, mix)


def _swa_sample_body(q_ref, k_ref, v_ref, ck_ref, cv_ref, mix_ref, y_ref, *, past, rows, seqs):
    del mix_ref
    head = _lane_head(MIX)
    hr = N_HEADS * rows
    scale = HEAD_DIM ** -0.5
    t_row = lax.broadcasted_iota(I32, (hr, 1), 0) % rows
    rel_c = past + t_row - lax.broadcasted_iota(I32, (1, past), 1)
    rel_n = t_row - lax.broadcasted_iota(I32, (1, LANE), 1)
    valid_c, valid_n = [], []
    for window, dil in DILATED_CONFIGS:
        span = window // dil
        shift = int(math.log2(dil))
        for rel, dst in ((rel_c, valid_c), (rel_n, valid_n)):
            dst.append((rel >= 0) & ((rel & (dil - 1)) == 0) & ((rel >> shift) <= span))
    zpad = jnp.zeros((LANE - rows, MIX), F32)
    ys = []
    for s in range(seqs):
        rs = slice(s * rows, (s + 1) * rows)
        q = q_ref[rs, :]
        qm = jnp.concatenate([jnp.where(head == h, q, 0.0) for h in range(N_HEADS)], axis=0).astype(BF16)
        ck = ck_ref[0, s].astype(BF16)
        cv = cv_ref[0, s].astype(BF16)
        kn = jnp.concatenate([k_ref[rs, :], zpad], axis=0).astype(BF16)
        vn = jnp.concatenate([v_ref[rs, :], zpad], axis=0).astype(BF16)
        s_c = _dot(qm, ck) * scale
        s_n = _dot_nt(qm, kn) * scale
        pcs, pns, ls, lses = [], [], [], []
        for ci in range(len(DILATED_CONFIGS)):
            sc = jnp.where(valid_c[ci], s_c, NEG_INF)
            sn = jnp.where(valid_n[ci], s_n, NEG_INF)
            m = jnp.maximum(jnp.max(sc, axis=-1, keepdims=True), jnp.max(sn, axis=-1, keepdims=True))
            pc = jnp.exp(sc - m)
            pn = jnp.exp(sn - m)
            l = jnp.sum(pc, axis=-1, keepdims=True) + jnp.sum(pn, axis=-1, keepdims=True)
            pcs.append(pc.astype(BF16))
            pns.append(pn.astype(BF16))
            ls.append(l)
            lses.append(m + jnp.log(l))
        o_all = _dot_nt(jnp.concatenate(pcs, axis=0), cv) + _dot(jnp.concatenate(pns, axis=0), vn)
        outs = [o_all[ci * hr:(ci + 1) * hr, :] / ls[ci] for ci in range(len(DILATED_CONFIGS))]
        mx = jnp.maximum(jnp.maximum(lses[0], lses[1]), lses[2])
        es = [jnp.exp(le - mx) for le in lses]
        y_all = (es[0] * outs[0] + es[1] * outs[1] + es[2] * outs[2]) / (es[0] + es[1] + es[2])
        y = jnp.zeros((rows, MIX), F32)
        for h in range(N_HEADS):
            y = jnp.where(head == h, y_all[h * rows:(h + 1) * rows, :], y)
        ys.append(y)
    y_ref[...] = jnp.concatenate(ys, axis=0).astype(y_ref.dtype)


def _swa_sample(h, mix, cache_k, cache_v, layer, *, rows, n_seq, seqs, row0):
    past = cache_k.shape[3]
    blk = seqs * rows
    rb0 = row0 // blk

    def hspec(col):
        return pl.BlockSpec((blk, MIX), lambda b: (rb0 + b, col))

    cspec = pl.BlockSpec((1, seqs, MIX, past), lambda b: (layer, b, 0, 0))
    nbytes = 2 * (2 * seqs * past * MIX * 4 + 4 * blk * MIX * 4) + 16 * N_HEADS * rows * past * 4 + 2 * past * MIX * 2
    return pl.pallas_call(
        functools.partial(_swa_sample_body, past=past, rows=rows, seqs=seqs),
        grid=(n_seq // seqs,),
        in_specs=[hspec(COL_SQ), hspec(COL_SK), hspec(COL_SV), cspec, cspec,
                  pl.BlockSpec(memory_space=pl.ANY)],
        out_specs=pl.BlockSpec((blk, MIX), lambda b: (rb0 + b, 1)),
        out_shape=jax.ShapeDtypeStruct(mix.shape, mix.dtype),
        input_output_aliases={5: 0},
        compiler_params=_params(("arbitrary",), nbytes),
        name="swa_sample",
    )(h, h, h, cache_k, cache_v, mix)


def _conv_rows(ctx, base, sub, w_ref, cb_ref, g_ref, b_ref):
    lead = CONV_CTX - (CONV_WIDTH - 1)
    window = ctx[base:base + sub + CONV_CTX, :]
    acc = jnp.zeros((sub, MIX), F32)
    for r in range(SUBLANE):
        shifted = pltpu.roll(window, sub + CONV_CTX - (lead + r), 0)
        for j in range(r, CONV_WIDTH, SUBLANE):
            acc = acc + shifted[j - r:j - r + sub, :] * w_ref[j:j + 1, :]
    z = _layer_norm(acc + cb_ref[...], g_ref[...], b_ref[...])
    return z * _sigmoid(z)


def _gelu_tanh(x):
    c = math.sqrt(2.0 / math.pi)
    return x * (0.5 * (1.0 + jnp.tanh(c * (x + 0.044715 * (x * x * x)))))


def _sgu_weights(ws_ref):
    tri = (lax.broadcasted_iota(I32, (SGU_CHUNK, SGU_CHUNK), 0)
           >= lax.broadcasted_iota(I32, (SGU_CHUNK, SGU_CHUNK), 1))
    return [jnp.where(tri, ws_ref[gi], 0.0).astype(BF16) for gi in range(N_HEADS)]


def _sgu_chunk(u_raw, v_raw, wms, tabs):
    g_ref, b_ref, bias_ref, avg_ref = tabs
    sub = u_raw.shape[0]
    head = _lane_head(MIX)
    u = _gelu_tanh(u_raw)
    vn = _group_norm(_gelu_tanh(v_raw), avg_ref[...]) * g_ref[...] + b_ref[...]
    vp = vn if sub == SGU_CHUNK else jnp.concatenate([vn, jnp.zeros((SGU_CHUNK - sub, MIX), F32)], axis=0)
    vnb = vp.astype(BF16)
    z = jnp.zeros((SGU_CHUNK, MIX), F32)
    for gi in range(N_HEADS):
        z = jnp.where(head == gi, _dot(wms[gi], vnb), z)
    z = z + bias_ref[...]
    return u * z[0:sub, :], vn


def _mixers_body(q_ref, k_ref, v_ref, g_ref, cv_ref, cg_ref, gu_ref, gv_ref,
                 dec_ref, qdec_ref, kdec_ref, cmat_ref, bdm_ref, avg_ref, gn_ref,
                 cw_ref, cb_ref, clg_ref, clb_ref, sg_ref, sb_ref, ws_ref, bias_ref, mix_in,
                 mix_ref, glu_ref, sout_ref, s_scr, ctx, *, n_steps, rows):
    del mix_in
    c = pl.program_id(1)

    @pl.when(c == 0)
    def _():
        s_scr[...] = jnp.zeros_like(s_scr)
        ctx[0:CONV_CTX, :] = jnp.zeros((CONV_CTX, MIX), F32)

    ret_tabs = (dec_ref, qdec_ref, kdec_ref, cmat_ref, bdm_ref, avg_ref, gn_ref)
    sgu_tabs = (sg_ref, sb_ref, bias_ref, avg_ref)
    wms = _sgu_weights(ws_ref)
    glu = cv_ref[...] * _sigmoid(cg_ref[...])
    glu_ref[...] = glu
    ctx[CONV_CTX:CONV_CTX + rows, :] = glu
    state = s_scr[...]
    for cc in range(rows // RET_CHUNK):
        rs = slice(cc * RET_CHUNK, (cc + 1) * RET_CHUNK)
        y, state = _retention_chunk(q_ref[rs, :], k_ref[rs, :], v_ref[rs, :], g_ref[rs, :], state, ret_tabs)
        out = mix_ref.dtype
        mix_ref[rs, 0:MIX] = y.astype(out)
        mix_ref[rs, 2 * MIX:3 * MIX] = _conv_rows(ctx, cc * RET_CHUNK, RET_CHUNK, cw_ref, cb_ref, clg_ref, clb_ref).astype(out)
        mix_ref[rs, 3 * MIX:4 * MIX] = _sgu_chunk(gu_ref[rs, :], gv_ref[rs, :], wms, sgu_tabs)[0].astype(out)
    mix_ref[:, MIX:2 * MIX] = jnp.zeros((rows, MIX), mix_ref.dtype)
    s_scr[...] = state
    ctx[0:CONV_CTX, :] = ctx[rows:rows + CONV_CTX, :]

    @pl.when(c == n_steps - 1)
    def _():
        sout_ref[0] = s_scr[...]


def _mixers_prompt(h, mix, ret_tabs, avg, gn_g, conv_args, sgu_args, *, seq, n_seq):
    assert RET_CHUNK == SGU_CHUNK and seq % MIX_ROWS == 0
    rows = MIX_ROWS
    n_steps = seq // rows
    dec, qdec, kdec, cmat, bdm = ret_tabs
    cw, cb, clg, clb = conv_args
    sg, sb, ws, bias = sgu_args

    def hspec(col):
        return pl.BlockSpec((rows, MIX), lambda b, c: (b * n_steps + c, col))

    def const(arr):
        return pl.BlockSpec(arr.shape, lambda b, c: (0,) * arr.ndim)

    tables = [dec, qdec, kdec, cmat, bdm, avg, gn_g, cw, cb, clg, clb, sg, sb, ws, bias]
    cols = [COL_RQ, COL_RK, COL_RV, COL_RG, COL_CV, COL_CG, COL_GU, COL_GV]
    nbytes = (2 * (len(cols) + 5) * rows * MIX * 4 + 2 * sum(t.size * 4 for t in tables)
              + (CONV_CTX + rows) * MIX * 4 + MIX * MIX * 4)
    return pl.pallas_call(
        functools.partial(_mixers_body, n_steps=n_steps, rows=rows),
        grid=(n_seq, n_steps),
        in_specs=[hspec(col) for col in cols] + [const(t) for t in tables] + [pl.BlockSpec(memory_space=pl.ANY)],
        out_specs=[pl.BlockSpec((rows, 4 * MIX), lambda b, c: (b * n_steps + c, 0)),
                   pl.BlockSpec((rows, MIX), lambda b, c: (b * n_steps + c, 0)),
                   pl.BlockSpec((1, MIX, MIX), lambda b, c: (b, 0, 0))],
        out_shape=[jax.ShapeDtypeStruct(mix.shape, mix.dtype),
                   jax.ShapeDtypeStruct((n_seq * seq, MIX), F32),
                   jax.ShapeDtypeStruct((n_seq, MIX, MIX), F32)],
        scratch_shapes=[pltpu.VMEM((MIX, MIX), F32), pltpu.VMEM((CONV_CTX + rows, MIX), F32)],
        input_output_aliases={len(cols) + len(tables): 0},
        compiler_params=_params(("arbitrary", "arbitrary"), nbytes),
        name="mixers_prompt",
    )(*([h] * len(cols)), *tables, mix)


def _mixers_sample_body(q_ref, k_ref, v_ref, g_ref, cv_ref, cg_ref, gu_ref, gv_ref, s0_ref, buf_ref,
                        dec_ref, qdec_ref, kdec_ref, cmat_ref, bdm_ref, avg_ref, gn_ref,
                        cw_ref, cb_ref, clg_ref, clb_ref, sg_ref, sb_ref, ws_ref, bias_ref, mix_in,
                        mix_ref, glu_ref, vn_ref, sout_ref, ctx, *, seqs, rows):
    del mix_in
    ret_tabs = (dec_ref, qdec_ref, kdec_ref, cmat_ref, bdm_ref, avg_ref, gn_ref)
    sgu_tabs = (sg_ref, sb_ref, bias_ref, avg_ref)
    wms = _sgu_weights(ws_ref)
    glu = cv_ref[...] * _sigmoid(cg_ref[...])
    glu_ref[...] = glu
    ret, conv, gate = [], [], []
    for s in range(seqs):
        rs = slice(s * rows, (s + 1) * rows)
        y, sout_ref[s] = _retention_chunk(q_ref[rs, :], k_ref[rs, :], v_ref[rs, :], g_ref[rs, :], s0_ref[s], ret_tabs)
        ret.append(y)
        ctx[s, 0:CONV_CTX, :] = buf_ref[s]
        ctx[s, CONV_CTX:CONV_CTX + rows, :] = glu[rs, :]
        conv.append(_conv_rows(ctx.at[s], 0, rows, cw_ref, cb_ref, clg_ref, clb_ref))
        y, vn_ref[rs, :] = _sgu_chunk(gu_ref[rs, :], gv_ref[rs, :], wms, sgu_tabs)
        gate.append(y)
    blocks = (ret, [jnp.zeros((rows, MIX), F32)] * seqs, conv, gate)
    for col, parts in enumerate(blocks):
        mix_ref[:, col * MIX:(col + 1) * MIX] = jnp.concatenate(parts, axis=0).astype(mix_ref.dtype)


def _mixers_sample(h, mix, s0, buf, ret_tabs, avg, gn_g, conv_args, sgu_args, *, rows, n_seq, seqs, row0):
    blk = seqs * rows
    rb0 = row0 // blk
    dec, qdec, kdec, cmat, bdm = ret_tabs
    cw, cb, clg, clb = conv_args
    sg, sb, ws, bias = sgu_args

    def hspec(col):
        return pl.BlockSpec((blk, MIX), lambda b: (rb0 + b, col))

    def const(arr):
        return pl.BlockSpec(arr.shape, lambda b: (0,) * arr.ndim)

    per_seq = lambda arr: pl.BlockSpec((seqs,) + arr.shape[1:], lambda b: (b,) + (0,) * (arr.ndim - 1))
    tables = [dec, qdec, kdec, cmat, bdm, avg, gn_g, cw, cb, clg, clb, sg, sb, ws, bias]
    cols = [COL_RQ, COL_RK, COL_RV, COL_RG, COL_CV, COL_CG, COL_GU, COL_GV]
    rows_out = lambda width: pl.BlockSpec((blk, width), lambda b: (b, 0))
    nbytes = (2 * (len(cols) + 6) * blk * MIX * 4 + 2 * sum(t.size * 4 for t in tables)
              + 4 * seqs * MIX * MIX * 4 + 3 * seqs * (CONV_CTX + rows) * MIX * 4)
    return pl.pallas_call(
        functools.partial(_mixers_sample_body, seqs=seqs, rows=rows),
        grid=(n_seq // seqs,),
        in_specs=([hspec(col) for col in cols] + [per_seq(s0), per_seq(buf)] + [const(t) for t in tables]
                  + [pl.BlockSpec(memory_space=pl.ANY)]),
        out_specs=[pl.BlockSpec((blk, 4 * MIX), lambda b: (rb0 + b, 0)), rows_out(MIX), rows_out(MIX), per_seq(s0)],
        out_shape=[jax.ShapeDtypeStruct(mix.shape, mix.dtype),
                   jax.ShapeDtypeStruct((n_seq * rows, MIX), F32),
                   jax.ShapeDtypeStruct((n_seq * rows, MIX), F32),
                   jax.ShapeDtypeStruct(s0.shape, F32)],
        scratch_shapes=[pltpu.VMEM((seqs, CONV_CTX + rows, MIX), F32)],
        input_output_aliases={len(cols) + 2 + len(tables): 0},
        compiler_params=_params(("arbitrary",), nbytes),
        name="mixers_sample",
    )(*([h] * len(cols)), s0, buf, *tables, mix)


def _out_router_body(mix_ref, w_ref, x_ref, g_ref, b_ref, wr_ref, br_ref,
                     x1_ref, route_ref, counts_ref, wbf, wrbf, logit_scr, *, alpha):
    @pl.when(pl.program_id(0) == 0)
    def _():
        wbf[...] = w_ref[0].astype(BF16)
        wrbf[...] = wr_ref[...].astype(BF16)

    parts = [slice(r0, r0 + OUT_ROWS) for r0 in range(0, TM, OUT_ROWS)]
    ys = [_dot(mix_ref[rs, :], wbf[...]) for rs in parts]
    for rs, y in zip(parts, ys):
        x1 = _layer_norm(alpha * x_ref[rs, :] + y, g_ref[...], b_ref[...])
        x1_ref[rs, :] = x1
        logit_scr[:, rs] = _dot_nt(wrbf[...], x1.astype(BF16)) + br_ref[...]
    logits = logit_scr[0:ROUTE_ROWS, :]
    row = lax.broadcasted_iota(I32, (ROUTE_ROWS, 1), 0).astype(F32)

    def first_max(mask, vals):
        masked = jnp.where(mask, vals, NEG_INF)
        top = jnp.max(masked, axis=0, keepdims=True)
        idx = jnp.min(jnp.where(mask & (masked == top), row, float(LANE)), axis=0, keepdims=True)
        return top, idx

    gmask = row < N_GROUPS
    gmax, gsel = first_max(gmask, logits)
    g_gate = 1.0 / jnp.sum(jnp.where(gmask, jnp.exp(logits - gmax), 0.0), axis=0, keepdims=True)
    lo = N_GROUPS + EXPERTS_PER_GROUP * gsel
    emask = (row >= lo) & (row < lo + EXPERTS_PER_GROUP)
    v1, i1 = first_max(emask, logits)
    v2, i2 = first_max(emask & (row != i1), logits)
    e12 = jnp.exp(v2 - v1)
    both = (row == i1).astype(F32) + (row == i2).astype(F32)

    earlier = (lax.broadcasted_iota(I32, (TD, TD), 0) < lax.broadcasted_iota(I32, (TD, TD), 1)).astype(BF16)
    rank_a, rank_b = [], []
    for td in range(TM // TD):
        cs = slice(td * TD, (td + 1) * TD)
        tile = both[:, cs]
        counts_ref[td] = jnp.sum(tile, axis=1, keepdims=True)
        before = _dot(tile.astype(BF16), earlier)
        rank_a.append(jnp.sum(jnp.where(row == i1[:, cs], before, 0.0), axis=0, keepdims=True))
        rank_b.append(jnp.sum(jnp.where(row == i2[:, cs], before, 0.0), axis=0, keepdims=True))
    values = (i1 - N_GROUPS, i2 - N_GROUPS, g_gate * (1.0 / (1.0 + e12)), g_gate * (e12 / (1.0 + e12)),
              jnp.concatenate(rank_a, axis=1), jnp.concatenate(rank_b, axis=1))
    out_row = lax.broadcasted_iota(I32, (SUBLANE, 1), 0)
    route = jnp.zeros((SUBLANE, TM), F32)
    for k, val in enumerate(values):
        route = jnp.where(out_row == k, val, route)
    route_ref[...] = route


def _out_router(mix, w_out, layer, x, g, b, wr, br, *, alpha):
    nt, d = x.shape
    kdim = mix.shape[1]
    const = lambda shape: pl.BlockSpec(shape, lambda i: (0, 0))
    row = lambda width: pl.BlockSpec((TM, width), lambda i: (i, 0))
    nbytes = 2 * (kdim * d * 4 + 2 * d * LANE * 4 + TM * kdim * 4 + 2 * TM * d * 4 + TM * LANE * 4) + kdim * d * 2
    return pl.pallas_call(
        functools.partial(_out_router_body, alpha=alpha),
        grid=(nt // TM,),
        in_specs=[row(kdim), pl.BlockSpec((1, kdim, d), lambda i: (layer, 0, 0)), row(d),
                  const((1, d)), const((1, d)), const((LANE, d)), const((LANE, 1))],
        out_specs=[row(d), pl.BlockSpec((SUBLANE, TM), lambda i: (0, i)),
                   pl.BlockSpec((TM // TD, ROUTE_ROWS, 1), lambda i: (i, 0, 0))],
        out_shape=[jax.ShapeDtypeStruct((nt, d), F32), jax.ShapeDtypeStruct((SUBLANE, nt), F32),
                   jax.ShapeDtypeStruct((nt // TD, ROUTE_ROWS, 1), F32)],
        scratch_shapes=[pltpu.VMEM((kdim, d), BF16), pltpu.VMEM((LANE, d), BF16), pltpu.VMEM((LANE, TM), F32)],
        compiler_params=_params(("arbitrary",), nbytes),
        name="out_router",
    )(mix, w_out, x, g, b, wr, br)


def _local_slots(route, off_col):
    expert = lax.broadcasted_iota(I32, (N_EXPERTS, 1), 0).astype(F32)
    slots = []
    for kk in range(TOP_K):
        start = jnp.sum(jnp.where(expert == route[kk:kk + 1, :], off_col, 0.0), axis=0, keepdims=True)
        slots.append(start + route[4 + kk:5 + kk, :])
    return slots


def _pack_rows(y):
    half = y.shape[1] // 2
    bits = lax.bitcast_convert_type(y.astype(BF16).astype(F32), U32)
    return (bits[:, :half] & jnp.uint32(0xFFFF0000)) | (bits[:, half:] >> 16)


def _unpack_rows(w):
    hi = lax.bitcast_convert_type(w & jnp.uint32(0xFFFF0000), F32)
    lo = lax.bitcast_convert_type(w << 16, F32)
    return jnp.concatenate([hi, lo], axis=1)


def _chunk_copy(hbm, hbm_row, loc, chunk, sem, *, to_hbm):
    vm = loc.at[pl.ds(pl.multiple_of(chunk * MOE_CHUNK, MOE_CHUNK), MOE_CHUNK), :]
    hb = hbm.at[pl.ds(pl.multiple_of(hbm_row, MOE_CHUNK), MOE_CHUNK), :]
    return pltpu.make_async_copy(vm, hb, sem) if to_hbm else pltpu.make_async_copy(hb, vm, sem)


def _for_each_chunk(count, fn):
    def group(t, carry):
        for u in range(CHUNK_UNROLL):
            fn(t * CHUNK_UNROLL + u, u % 2)
        return carry

    def single(c, carry):
        fn(c, 0)
        return carry

    whole = count // CHUNK_UNROLL
    lax.fori_loop(0, whole, group, 0)
    lax.fori_loop(whole * CHUNK_UNROLL, count, single, 0)


def _wait_chunks(hbm, buf, sem, count, *, to_hbm):
    _for_each_chunk(count, lambda c, parity: _chunk_copy(hbm, 0, buf, 0, sem, to_hbm=to_hbm).wait())


def _dispatch_body(dst_tab, n_chunks, x_ref, route_ref, off_ref, xs_in, xs_hbm, loc, sem):
    del xs_in
    i = pl.program_id(0)
    last = pl.num_programs(0) - 1
    cur = i % 2
    slot_a, slot_b = _local_slots(route_ref[...], off_ref[0])
    buf_row = lax.broadcasted_iota(I32, (LOC_ROWS, 1), 0).astype(F32)
    onehot = ((buf_row == slot_a) | (buf_row == slot_b)).astype(BF16)
    loc[cur] = _pack_rows(_dot(onehot, x_ref[...].astype(BF16)))

    _for_each_chunk(n_chunks[i], lambda c, parity: _chunk_copy(
        xs_hbm, dst_tab[i * MAX_CHUNKS + c], loc.at[cur], c, sem.at[cur], to_hbm=True).start(priority=parity))

    @pl.when(i > 0)
    def _():
        _wait_chunks(xs_hbm, loc.at[1 - cur], sem.at[1 - cur], n_chunks[jnp.maximum(i - 1, 0)], to_hbm=True)

    @pl.when(i == last)
    def _():
        _wait_chunks(xs_hbm, loc.at[cur], sem.at[cur], n_chunks[i], to_hbm=True)


def _dispatch(dst_tab, n_chunks, x1, route, off, xs_prev):
    nt, d = x1.shape
    row = lambda width: pl.BlockSpec((TD, width), lambda i, *_: (i, 0))
    nbytes = 2 * (TD * d * 4 + TD * LANE * 4) + 2 * LOC_ROWS * d * 4 + 4 * TD * LOC_ROWS * 4 + LOC_ROWS * d * 4
    return pl.pallas_call(
        _dispatch_body,
        grid_spec=pltpu.PrefetchScalarGridSpec(
            num_scalar_prefetch=2,
            grid=(nt // TD,),
            in_specs=[row(d), pl.BlockSpec((SUBLANE, TD), lambda i, *_: (0, i)),
                      pl.BlockSpec((1, N_EXPERTS, 1), lambda i, *_: (i, 0, 0)),
                      pl.BlockSpec(memory_space=pl.ANY)],
            out_specs=pl.BlockSpec(memory_space=pl.ANY),
            scratch_shapes=[pltpu.VMEM((2, LOC_ROWS, d // 2), U32), pltpu.SemaphoreType.DMA((2,))]),
        out_shape=jax.ShapeDtypeStruct(xs_prev.shape, U32),
        input_output_aliases={5: 0},
        compiler_params=_params(("arbitrary",), nbytes),
        name="moe_dispatch",
    )(dst_tab, n_chunks, x1, route, off, xs_prev)


def _ffn_body(blk_e, blk_valid, blk_src, blk_slot, blk_next, x_ref, w1_hbm, w3_hbm, w2_hbm, y_ref,
              w1f, w3f, w2f, w1b, w3b, w2b, sem, *, layer):
    del blk_src
    j = pl.program_id(0)
    n_valid = blk_valid[j]
    half = MOE_BLK // 2
    slot = blk_slot[j]

    def weight_copies(expert, buf):
        return [pltpu.make_async_copy(hbm.at[layer, expert], vm.at[buf], sem.at[buf])
                for hbm, vm in ((w1_hbm, w1f), (w3_hbm, w3f), (w2_hbm, w2f))]

    @pl.when(j == 0)
    def _():
        for cp in weight_copies(blk_e[0], slot):
            cp.start()

    @pl.when((n_valid > 0) & ((j == 0) | (blk_e[j] != blk_e[jnp.maximum(j - 1, 0)])))
    def _():
        for cp in weight_copies(blk_e[j], slot):
            cp.wait()
        w1b[...] = w1f[slot].astype(BF16)
        w3b[...] = w3f[slot].astype(BF16)
        w2b[...] = w2f[slot].astype(BF16)

        @pl.when(blk_next[j] >= 0)
        def _():
            for cp in weight_copies(jnp.maximum(blk_next[j], 0), 1 - slot):
                cp.start()

    def ffn(rows):
        xb = _unpack_rows(x_ref[rows, :]).astype(BF16)
        h1 = _dot(xb, w1b[...])
        h3 = _dot(xb, w3b[...])
        act = (h1 * _sigmoid(h1)) * h3
        y_ref[rows, :] = _pack_rows(_dot(act.astype(BF16), w2b[...]))

    @pl.when(n_valid > half)
    def _():
        ffn(slice(0, MOE_BLK))

    @pl.when((n_valid > 0) & (n_valid <= half))
    def _():
        ffn(slice(0, half))
        y_ref[half:MOE_BLK, :] = jnp.zeros((half, y_ref.shape[1]), U32)


def _ffn(xs, w1, w3, w2, layer, blk_e, blk_valid, blk_src, blk_slot, blk_next):
    d = w1.shape[2]
    de = w1.shape[3]
    n_blk = xs.shape[0] // MOE_BLK
    rows = pl.BlockSpec((MOE_BLK, d // 2), lambda j, be, bv, bs, sl, nx: (bs[j], 0))
    nbytes = 2 * (3 * d * de * 4 + MOE_BLK * d * 4) + 3 * d * de * 2 + 6 * MOE_BLK * de * 4 + 4 * MOE_BLK * d * 4
    anywhere = pl.BlockSpec(memory_space=pl.ANY)
    return pl.pallas_call(
        functools.partial(_ffn_body, layer=layer),
        grid_spec=pltpu.PrefetchScalarGridSpec(
            num_scalar_prefetch=5,
            grid=(n_blk,),
            in_specs=[rows, anywhere, anywhere, anywhere],
            out_specs=rows,
            scratch_shapes=[pltpu.VMEM((2, d, de), F32), pltpu.VMEM((2, d, de), F32), pltpu.VMEM((2, de, d), F32),
                            pltpu.VMEM((d, de), BF16), pltpu.VMEM((d, de), BF16), pltpu.VMEM((de, d), BF16),
                            pltpu.SemaphoreType.DMA((2,))]),
        out_shape=jax.ShapeDtypeStruct(xs.shape, U32),
        input_output_aliases={5: 0},
        compiler_params=_params(("arbitrary",), nbytes),
        name="expert_ffn",
    )(blk_e, blk_valid, blk_src, blk_slot, blk_next, xs, w1, w3, w2)


def _combine_body(dst_tab, n_chunks, x_ref, route_ref, off_ref, g_ref, b_ref, ys_hbm, x2_ref,
                  *rest, alpha, head_tiles):
    *tail_ref, loc, sem = rest
    i = pl.program_id(0)
    last = pl.num_programs(0) - 1
    cur = i % 2

    def fetch(tile, buf):
        _for_each_chunk(n_chunks[tile], lambda c, parity: _chunk_copy(
            ys_hbm, dst_tab[tile * MAX_CHUNKS + c], loc.at[buf], c, sem.at[buf], to_hbm=False).start(priority=parity))

    @pl.when(i == 0)
    def _():
        loc[...] = jnp.zeros_like(loc)
        fetch(0, 0)

    @pl.when(i < last)
    def _():
        fetch(jnp.minimum(i + 1, last), 1 - cur)

    route = route_ref[...]
    slot_a, slot_b = _local_slots(route, off_ref[0])
    buf_row = lax.broadcasted_iota(I32, (LOC_ROWS, 1), 0).astype(F32)
    gates = (jnp.where(buf_row == slot_a, route[2:3, :], 0.0)
             + jnp.where(buf_row == slot_b, route[3:4, :], 0.0))
    _wait_chunks(ys_hbm, loc.at[cur], sem.at[cur], n_chunks[i], to_hbm=False)
    f = _dot_tn(gates.astype(BF16), _unpack_rows(loc[cur]).astype(BF16))
    x2 = _layer_norm(alpha * x_ref[...] + f, g_ref[...], b_ref[...])
    if tail_ref:
        @pl.when(i < head_tiles)
        def _():
            x2_ref[...] = x2

        @pl.when(i >= head_tiles)
        def _():
            tail_ref[0][...] = x2
    else:
        x2_ref[...] = x2


def _combine(dst_tab, n_chunks, x1, route, off, g, b, ys, *, alpha, head_rows=None):
    nt, d = x1.shape
    row = lambda width: pl.BlockSpec((TD, width), lambda i, dt, nc: (i, 0))
    const = lambda shape: pl.BlockSpec(shape, lambda i, dt, nc: (0, 0))
    nbytes = 2 * (3 * TD * d * 4 + TD * LANE * 4) + 2 * LOC_ROWS * d * 4 + 4 * TD * LOC_ROWS * 4 + LOC_ROWS * d * 2
    if head_rows is None:
        head_tiles = nt // TD
        out_specs = row(d)
        out_shape = jax.ShapeDtypeStruct((nt, d), F32)
    else:
        head_tiles = head_rows // TD
        out_specs = [pl.BlockSpec((TD, d), lambda i, dt, nc: (jnp.minimum(i, head_tiles - 1), 0)),
                     pl.BlockSpec((TD, d), lambda i, dt, nc: (jnp.maximum(i - head_tiles, 0), 0))]
        out_shape = [jax.ShapeDtypeStruct((head_rows, d), F32), jax.ShapeDtypeStruct((nt - head_rows, d), F32)]
    scratch = [pltpu.VMEM((2, LOC_ROWS, d // 2), U32), pltpu.SemaphoreType.DMA((2,))]
    return pl.pallas_call(
        functools.partial(_combine_body, alpha=alpha, head_tiles=head_tiles),
        grid_spec=pltpu.PrefetchScalarGridSpec(
            num_scalar_prefetch=2,
            grid=(nt // TD,),
            in_specs=[row(d), pl.BlockSpec((SUBLANE, TD), lambda i, dt, nc: (0, i)),
                      pl.BlockSpec((1, N_EXPERTS, 1), lambda i, dt, nc: (i, 0, 0)),
                      const((1, d)), const((1, d)), pl.BlockSpec(memory_space=pl.ANY)],
            out_specs=out_specs,
            scratch_shapes=scratch),
        out_shape=out_shape,
        compiler_params=_params(("arbitrary",), nbytes),
        name="combine_ln",
    )(dst_tab, n_chunks, x1, route, off, g, b, ys)


def _rope_tables(pos, rot_dim, theta):
    half = rot_dim // 2
    inv = jnp.float32(theta) ** (-jnp.arange(half, dtype=F32) * (2.0 / rot_dim))
    ang = pos.astype(F32)[:, None] * inv[None, :]
    c, s = jnp.cos(ang), jnp.sin(ang)
    rest = HEAD_DIM - rot_dim
    ones = jnp.ones((pos.shape[0], rest), F32)
    zeros = jnp.zeros((pos.shape[0], rest), F32)
    cos_h = jnp.concatenate([c, c, ones], axis=1)
    sin_h = jnp.concatenate([-s, s, zeros], axis=1)
    return jnp.tile(cos_h, (1, N_HEADS)), jnp.tile(sin_h, (1, N_HEADS))


def _block_diag(s):
    rows = [jnp.pad(s[:, h], ((0, 0), (0, 0), (h * HEAD_DIM, MIX - (h + 1) * HEAD_DIM))) for h in range(N_HEADS)]
    return jnp.concatenate(rows, axis=1)


def _diag_blocks(s):
    return jnp.stack([s[:, h * HEAD_DIM:(h + 1) * HEAD_DIM, h * HEAD_DIM:(h + 1) * HEAD_DIM]
                      for h in range(N_HEADS)], axis=1)


def _routing_plan(tile_counts, n_blk):
    cnt_t = tile_counts[:, N_GROUPS:N_GROUPS + N_EXPERTS, 0].astype(I32)
    run_chunks = (cnt_t + MOE_CHUNK - 1) // MOE_CHUNK
    run_rows = run_chunks * MOE_CHUNK
    cnt = jnp.sum(run_rows, axis=0)
    pcnt = (cnt + MOE_BLK - 1) // MOE_BLK * MOE_BLK
    pstart = jnp.cumsum(pcnt) - pcnt
    n_act = jnp.sum(pcnt) // MOE_BLK
    gstart = pstart[None, :] + jnp.cumsum(run_rows, axis=0) - run_rows
    chunk0 = jnp.cumsum(run_chunks, axis=1) - run_chunks
    n_chunks = jnp.sum(run_chunks, axis=1).astype(I32)

    def spread(first, count, value, n):
        p = jnp.arange(n, dtype=I32)[:, None]
        f, c, v = first[..., None, :], count[..., None, :], value[..., None, :]
        return jnp.sum(jnp.where((f <= p) & (p < f + c), v + (p - f) * MOE_CHUNK, 0), axis=-1)

    dst_tab = spread(chunk0, run_chunks, gstart, MAX_CHUNKS)
    off = (chunk0 * MOE_CHUNK).astype(F32)[:, :, None]
    pos = jnp.arange(n_blk, dtype=I32)[:, None] * MOE_BLK
    mine = (pstart[None, :] <= pos) & (pos < (pstart + pcnt)[None, :])
    experts = jnp.arange(N_EXPERTS, dtype=I32)[None, :]
    last_e = jnp.max(jnp.where(cnt > 0, experts[0], 0))
    active = pos[:, 0] < n_act * MOE_BLK
    blk_e = jnp.where(active, jnp.sum(jnp.where(mine, experts, 0), axis=-1), last_e)
    blk_valid = jnp.sum(jnp.where(mine, jnp.clip((pstart + cnt)[None, :] - pos, 0, MOE_BLK), 0), axis=-1)
    blk_src = jnp.minimum(jnp.arange(n_blk, dtype=I32), jnp.maximum(n_act - 1, 0))
    i32 = lambda t: t.astype(I32)
    used = cnt > 0
    slot_e = (jnp.cumsum(used.astype(I32)) - 1) % 2
    later = (experts > experts.T) & used[None, :]
    next_e = jnp.min(jnp.where(later, experts, N_EXPERTS), axis=-1)
    next_e = jnp.where(next_e < N_EXPERTS, next_e, -1)
    return (i32(dst_tab).reshape(-1), n_chunks, off, i32(blk_e), i32(blk_valid), i32(blk_src),
            i32(slot_e[blk_e]), i32(next_e[blk_e]))


def kernel(x_prompt, x_sample, state_ret, cache_swa_k, cache_swa_v, state_conv, w_in, w_out, ret_gn_g,
           conv_w, conv_b, conv_ln_g, conv_ln_b, sgu_ln_g, sgu_ln_b, sgu_w, sgu_b, ln1_g, ln1_b,
           ln2_g, ln2_b, router_g_w, router_g_b, router_e_w, router_e_b, moe_w1, moe_w3, moe_w2):
    depth = w_in.shape[0]
    n_seq, seq, d = x_prompt.shape
    n_dec, t_new, _ = x_sample.shape
    past = cache_swa_k.shape[2]
    rows_s = -(-t_new // SUBLANE) * SUBLANE
    n_p = n_seq * seq
    n_s = n_dec * rows_s
    n_tail = -(-n_s // TM) * TM
    nt = n_p + n_tail
    assert seq % SWA_SB == 0 and seq % TM == 0 and TM % TD == 0
    assert t_new % RET_CHUNK != 0 and t_new <= SGU_CHUNK and w_in.shape[2] == N_COLBLK * MIX
    assert n_dec % SAMPLE_SEQS == 0 and n_dec % SWA_SAMPLE_SEQS == 0
    assert n_p % (SAMPLE_SEQS * rows_s) == 0
    alpha = (2 * depth) ** 0.25
    keep = min(DILATED_CONFIGS[-1][0], seq)
    n_blk = -(-(nt * TOP_K + (nt // TD) * N_EXPERTS * (MOE_CHUNK - 1) + N_EXPERTS * (MOE_BLK - 1)) // MOE_BLK)
    cache_k = jnp.transpose(cache_swa_k, (0, 1, 3, 4, 2)).reshape(depth, n_dec, MIX, past)
    cache_v = jnp.transpose(cache_swa_v, (0, 1, 3, 4, 2)).reshape(depth, n_dec, MIX, past)

    x = jnp.concatenate([x_prompt.reshape(n_p, d),
                         jnp.pad(x_sample, ((0, 0), (0, rows_s - t_new), (0, 0))).reshape(n_s, d),
                         jnp.zeros((n_tail - n_s, d), F32)], axis=0)
    pos = jnp.concatenate([jnp.arange(seq, dtype=I32),
                           PAST_LEN + jnp.arange(n_tail, dtype=I32) % rows_s])
    rope = _rope_tables(pos, HEAD_DIM, RET_THETA) + _rope_tables(pos, ROPE_DIM, ROPE_THETA)
    ret_tabs_p = _retention_tables(RET_CHUNK, RET_CHUNK)
    ret_tabs_s = _retention_tables(t_new, rows_s)
    lane_head = jnp.arange(MIX) // HEAD_DIM
    avg = ((lane_head[:, None] == lane_head[None, :]).astype(F32) / HEAD_DIM).astype(BF16)
    row2 = lambda v: v.reshape(1, -1)
    mix = jnp.zeros((nt, 4 * MIX), BF16)
    slots = jnp.zeros((n_blk * MOE_BLK, d // 2), U32)

    outs = [[] for _ in range(9)]
    for l in range(depth):
        h = _in_proj(x, w_in, l, rope, n_p // TM, seq // TM)
        gn = row2(ret_gn_g[l])
        cw = jnp.pad(conv_w[l], ((0, CONV_CTX - CONV_WIDTH), (0, 0)))
        cargs = (cw, row2(conv_b[l]), row2(conv_ln_g[l]), row2(conv_ln_b[l]))
        sbias = jnp.repeat(sgu_b[l].T, HEAD_DIM, axis=1)
        sargs = (row2(sgu_ln_g[l]), row2(sgu_ln_b[l]), sgu_w[l], sbias)
        mix, glu_p, s_p = _mixers_prompt(h, mix, ret_tabs_p, avg, gn, cargs, sargs, seq=seq, n_seq=n_seq)
        mix = _swa_prompt(h, mix, seq=seq, n_seq=n_seq)
        ctx_s = jnp.pad(state_conv[l], ((0, 0), (CONV_CTX - (CONV_WIDTH - 1), 0), (0, 0)))
        mix, glu_s, vn_s, s_s = _mixers_sample(h, mix, _block_diag(state_ret[l]), ctx_s, ret_tabs_s, avg, gn, cargs,
                                               sargs, rows=rows_s, n_seq=n_dec, seqs=SAMPLE_SEQS, row0=n_p)
        mix = _swa_sample(h, mix, cache_k, cache_v, l, rows=rows_s, n_seq=n_dec, seqs=SWA_SAMPLE_SEQS, row0=n_p)

        wr = jnp.concatenate([router_g_w[l].T, jnp.transpose(router_e_w[l], (0, 2, 1)).reshape(N_EXPERTS, d),
                              jnp.zeros((LANE - N_GROUPS - N_EXPERTS, d), F32)], axis=0)
        br = jnp.concatenate([router_g_b[l], router_e_b[l].reshape(-1),
                              jnp.zeros((LANE - N_GROUPS - N_EXPERTS,), F32)]).reshape(LANE, 1)
        x1, route, counts = _out_router(mix, w_out, l, x, row2(ln1_g[l]), row2(ln1_b[l]), wr, br, alpha=alpha)
        dst_tab, n_chunks, off, blk_e, blk_valid, blk_src, blk_slot, blk_next = _routing_plan(counts, n_blk)
        slots = _dispatch(dst_tab, n_chunks, x1, route, off, slots)
        slots = _ffn(slots, moe_w1, moe_w3, moe_w2, l, blk_e, blk_valid, blk_src, blk_slot, blk_next)
        x = _combine(dst_tab, n_chunks, x1, route, off, row2(ln2_g[l]), row2(ln2_b[l]), slots, alpha=alpha,
                     head_rows=n_p if l == depth - 1 else None)

        def tail(col, n_keep):
            return jnp.stack([lax.slice(h, ((b + 1) * seq - n_keep, col * MIX), ((b + 1) * seq, (col + 1) * MIX))
                              for b in range(n_seq)])

        def new_rows(col):
            blk = lax.slice(h, (n_p, col * MIX), (n_p + n_s, (col + 1) * MIX))
            return blk.reshape(n_dec, rows_s, MIX)[:, :t_new]

        heads = lambda t: t.reshape(t.shape[0], t.shape[1], N_HEADS, HEAD_DIM)
        outs[0].append(_diag_blocks(s_p))
        outs[1].append(heads(tail(COL_SK, keep)))
        outs[2].append(heads(tail(COL_SV, keep)))
        outs[3].append(glu_p.reshape(n_seq, seq, MIX)[:, seq - (CONV_WIDTH - 1):])
        outs[4].append(_diag_blocks(s_s))
        outs[5].append(heads(new_rows(COL_SK)))
        outs[6].append(heads(new_rows(COL_SV)))
        outs[7].append(glu_s.reshape(n_dec, rows_s, MIX)[:, :t_new])
        outs[8].append(vn_s.reshape(n_dec, rows_s, MIX)[:, :t_new])

    x_head, x_tail = x
    y_prompt = x_head.reshape(n_seq, seq, d)
    y_sample = x_tail[:n_s].reshape(n_dec, rows_s, d)[:, :t_new]
    return (y_prompt, y_sample) + tuple(jnp.stack(o) for o in outs)
```
